```python
import math
import jax, jax.numpy as jnp
from jax import lax
import numpy as np

D_MODEL = 1024
BATCH = 16
SEQ = 256
DEPTH = 2
DEC_BATCH = 8
DEC_SEQ = 1024
PAST_LEN = 256

GRID_W = 64
D_HYENA = 512
HYENA_ORDER = 2
HYENA_POS_BANDS = 16
HYENA_POS_DIM = 1 + 2 * HYENA_POS_BANDS
HYENA_FILTER_HIDDEN = 64
HYENA_WINDOW_SHIFT = 0.05
N_HEADS = 8
N_KV_HEADS = 2
HEAD_DIM = 64
ATT_Q = N_HEADS * HEAD_DIM
ATT_KV = N_KV_HEADS * HEAD_DIM
Q_BLOCK = 128
ROPE_THETA = 10000.0
D_SCONV = 512
N_BRANCH = 3
N_EXPERTS = 16
EXPERT_FF = 512
CAPACITY_FACTOR = 2
NORM_EPS = 1e-6
HY_COLS = (HYENA_ORDER + 1) * D_HYENA
ATT_COLS = ATT_Q + 2 * ATT_KV
SC_COLS = 3 * D_SCONV
GATE_COLS = N_BRANCH * D_MODEL
D_IN = HY_COLS + ATT_COLS + SC_COLS + GATE_COLS
F32 = jnp.float32

kernel_name = 'hybrid_diffusion_hyena_gqa_shortconv_ecmoe_step'


def rms_norm(x, w):
    x32 = x.astype(F32)
    y = x32 * lax.rsqrt(jnp.mean(x32 * x32, axis=-1, keepdims=True) + NORM_EPS)
    return (y * w.astype(F32)).astype(x.dtype)


def conv3_centred(u, w, b):
    up = jnp.pad(u, ((0, 0), (1, 1), (0, 0)))
    return up[:, :-2] * w[0] + up[:, 1:-1] * w[1] + up[:, 2:] * w[2] + b


def hyena_filters(L, w1, b1, w2, b2, w3, b3, decay):
    n = jnp.arange(L, dtype=F32)
    t = n / max(L - 1, 1)
    bands = jnp.linspace(1e-4, HYENA_POS_BANDS - 1, HYENA_POS_BANDS, dtype=F32)
    ang = 2.0 * math.pi * n[:, None] * bands[None, :] / L
    z = jnp.concatenate([t[:, None], jnp.cos(ang), jnp.sin(ang)], axis=-1).astype(w1.dtype)
    h = jnp.sin(z @ w1 + b1)
    h = jnp.sin(h @ w2 + b2)
    h = (h @ w3 + b3).astype(F32)
    window = jnp.exp(-t[:, None] * jnp.abs(decay.astype(F32))[None, :]) + HYENA_WINDOW_SHIFT
    return (h * window).reshape(L, 2, HYENA_ORDER, D_HYENA)


def bidir_fftconv(u, h_fwd, h_bwd, bias):
    L = u.shape[1]
    C = h_fwd.shape[1]
    k = jnp.concatenate([h_fwd, jnp.zeros((1, C), F32), h_bwd[:0:-1]], axis=0)
    u32 = u.astype(F32)
    U = jnp.fft.rfft(u32, n=2 * L, axis=1)
    K = jnp.fft.rfft(k, n=2 * L, axis=0)
    y = jnp.fft.irfft(U * K[None], n=2 * L, axis=1)[:, :L]
    return (y + u32 * bias.astype(F32)).astype(u.dtype)


def hyena_mixer(hy, short_w, short_b, filt, bias):
    hy = conv3_centred(hy, short_w, short_b)
    parts = jnp.split(hy, HYENA_ORDER + 1, axis=-1)
    z = parts[0]
    for o in range(HYENA_ORDER):
        z = parts[o + 1] * bidir_fftconv(z, filt[:, 0, o], filt[:, 1, o], bias[o])
    return z


def axial_rope_tables(rows):
    row = jnp.repeat(jnp.arange(rows, dtype=F32), GRID_W)
    col = jnp.tile(jnp.arange(GRID_W, dtype=F32), rows)
    axis_dim = HEAD_DIM // 2
    inv_freq = ROPE_THETA ** (-jnp.arange(0, axis_dim, 2, dtype=F32) / axis_dim)
    ar = row[:, None] * inv_freq[None, :]
    ac = col[:, None] * inv_freq[None, :]
    return (jnp.cos(ar), jnp.sin(ar), jnp.cos(ac), jnp.sin(ac))


def rotate_half_pairs(x, cos, sin):
    x1, x2 = jnp.split(x, 2, axis=-1)
    return jnp.concatenate([x1 * cos - x2 * sin, x2 * cos + x1 * sin], axis=-1)


def apply_axial_rope(x, rope):
    cos_r, sin_r, cos_c, sin_c = rope
    xr, xc = jnp.split(x.astype(F32), 2, axis=-1)
    out = jnp.concatenate([
        rotate_half_pairs(xr, cos_r[None, :, None, :], sin_r[None, :, None, :]),
        rotate_half_pairs(xc, cos_c[None, :, None, :], sin_c[None, :, None, :])], axis=-1)
    return out.astype(x.dtype)


def blocked_gqa(q, k, v):
    B, Lq = q.shape[0], q.shape[1]
    nb = Lq // Q_BLOCK
    G = N_HEADS // N_KV_HEADS
    qb = q.reshape(B, nb, Q_BLOCK, N_KV_HEADS, G, HEAD_DIM).transpose(1, 0, 2, 3, 4, 5)
    scale = HEAD_DIM ** -0.5

    def one_block(qblk):
        s = jnp.einsum('bqkgd,bskd->bkgqs', qblk, k).astype(F32) * scale
        p = jax.nn.softmax(s, axis=-1).astype(v.dtype)
        return jnp.einsum('bkgqs,bskd->bqkgd', p, v)

    o = lax.map(one_block, qb)
    return o.transpose(1, 0, 2, 3, 4, 5).reshape(B, Lq, ATT_Q)


def expert_choice_moe(h, router_w, w_gate, w_up, w_down):
    B, N, D = h.shape
    cap = CAPACITY_FACTOR * N // N_EXPERTS
    aff = jax.nn.softmax((h @ router_w).astype(F32), axis=-1)
    gate, idx = lax.top_k(jnp.swapaxes(aff, 1, 2), cap)
    xs = jax.vmap(lambda hb, ib: hb[ib])(h, idx)
    a = jnp.einsum('becd,edf->becf', xs, w_gate)
    u = jnp.einsum('becd,edf->becf', xs, w_up)
    y = jnp.einsum('becf,efd->becd', jax.nn.silu(a) * u, w_down)
    y = y * gate[..., None].astype(y.dtype)
    return jax.vmap(lambda yb, ib: jnp.zeros((N, D), yb.dtype).at[ib.reshape(-1)].add(yb.reshape(-1, D)))(y, idx)


def trunk_layer(x, cond, p, rope, ctx_k, ctx_v):
    B, L, D = x.shape
    mod = (jax.nn.silu(cond) @ p['mod_w'] + p['mod_b']).reshape(-1, 1, 6 * D)
    sh1, sc1, g1, sh2, sc2, g2 = jnp.split(mod, 6, axis=-1)
    h = rms_norm(x, p['norm1_w']) * (1 + sc1) + sh1
    proj = h @ p['w_in']
    hy, qkv, scv, gates = jnp.split(proj, [HY_COLS, HY_COLS + ATT_COLS, HY_COLS + ATT_COLS + SC_COLS], axis=-1)
    filt = hyena_filters(L, p['hy_w1'], p['hy_b1'], p['hy_w2'], p['hy_b2'], p['hy_w3'], p['hy_b3'], p['hy_decay'])
    ya = hyena_mixer(hy, p['hy_short_w'], p['hy_short_b'], filt, p['hy_bias'])
    q, k, v = jnp.split(qkv, [ATT_Q, ATT_Q + ATT_KV], axis=-1)
    q = rms_norm(q.reshape(B, L, N_HEADS, HEAD_DIM), p['q_norm_w'])
    k = rms_norm(k.reshape(B, L, N_KV_HEADS, HEAD_DIM), p['k_norm_w'])
    v = v.reshape(B, L, N_KV_HEADS, HEAD_DIM)
    if rope is None:
        keys, vals = k, v
    else:
        q = apply_axial_rope(q, rope)
        keys = jnp.concatenate([ctx_k, apply_axial_rope(k, rope)], axis=1)
        vals = jnp.concatenate([ctx_v, v], axis=1)
    yb = blocked_gqa(q, keys, vals)
    u, bgate, cgate = jnp.split(scv, 3, axis=-1)
    yc = bgate * conv3_centred(cgate * u, p['sc_w'], p['sc_b'])
    ga, gb, gc = jnp.split(gates, 3, axis=-1)
    merged = (jax.nn.sigmoid(ga) * (ya @ p['w_br_a'])
              + jax.nn.sigmoid(gb) * (yb @ p['w_br_b'])
              + jax.nn.sigmoid(gc) * (yc @ p['w_br_c']))
    x = x + g1 * (merged @ p['w_o'])
    h2 = rms_norm(x, p['norm2_w']) * (1 + sc2) + sh2
    x = x + g2 * expert_choice_moe(h2, p['router_w'], p['exp_w_gate'], p['exp_w_up'], p['exp_w_down'])
    return x, k, v


def setup_inputs(seed: int = 0) -> dict:
    key = jax.random.key(seed)
    ks = iter(jax.random.split(key, 64))

    def nrm(shape, scale):
        return jax.random.normal(next(ks), shape, F32) * scale

    return {
        'x_prompt': nrm((BATCH, SEQ, D_MODEL), 1.0),
        'x_sample': nrm((DEC_BATCH, DEC_SEQ, D_MODEL), 1.0),
        'cache_k': nrm((DEC_BATCH, DEPTH, PAST_LEN, N_KV_HEADS, HEAD_DIM), 1.0),
        'cache_v': nrm((DEC_BATCH, DEPTH, PAST_LEN, N_KV_HEADS, HEAD_DIM), 1.0),
        'c': nrm((DEC_BATCH, D_MODEL), 1.0),
        'c_ctx': nrm((D_MODEL,), 1.0),
        'mod_w': nrm((DEPTH, D_MODEL, 6 * D_MODEL), 0.5 * D_MODEL ** -0.5),
        'mod_b': nrm((DEPTH, 6 * D_MODEL), 0.02),
        'norm1_w': 1.0 + nrm((DEPTH, D_MODEL), 0.05),
        'norm2_w': 1.0 + nrm((DEPTH, D_MODEL), 0.05),
        'w_in': nrm((DEPTH, D_MODEL, D_IN), D_MODEL ** -0.5),
        'hy_short_w': nrm((DEPTH, 3, HY_COLS), 0.5),
        'hy_short_b': nrm((DEPTH, HY_COLS), 0.01),
        'hy_w1': nrm((DEPTH, HYENA_POS_DIM, HYENA_FILTER_HIDDEN), HYENA_POS_DIM ** -0.5),
        'hy_b1': nrm((DEPTH, HYENA_FILTER_HIDDEN), 0.1),
        'hy_w2': nrm((DEPTH, HYENA_FILTER_HIDDEN, HYENA_FILTER_HIDDEN), HYENA_FILTER_HIDDEN ** -0.5),
        'hy_b2': nrm((DEPTH, HYENA_FILTER_HIDDEN), 0.1),
        'hy_w3': nrm((DEPTH, HYENA_FILTER_HIDDEN, 2 * HYENA_ORDER * D_HYENA), 0.02 * HYENA_FILTER_HIDDEN ** -0.5),
        'hy_b3': nrm((DEPTH, 2 * HYENA_ORDER * D_HYENA), 0.01),
        'hy_decay': jax.random.uniform(next(ks), (DEPTH, 2 * HYENA_ORDER * D_HYENA), F32, 3.0, 15.0),
        'hy_bias': nrm((DEPTH, HYENA_ORDER, D_HYENA), 0.1),
        'q_norm_w': 1.0 + nrm((DEPTH, HEAD_DIM), 0.05),
        'k_norm_w': 1.0 + nrm((DEPTH, HEAD_DIM), 0.05),
        'sc_w': nrm((DEPTH, 3, D_SCONV), 0.5),
        'sc_b': nrm((DEPTH, D_SCONV), 0.01),
        'w_br_a': nrm((DEPTH, D_HYENA, D_MODEL), D_HYENA ** -0.5),
        'w_br_b': nrm((DEPTH, ATT_Q, D_MODEL), ATT_Q ** -0.5),
        'w_br_c': nrm((DEPTH, D_SCONV, D_MODEL), D_SCONV ** -0.5),
        'w_o': nrm((DEPTH, D_MODEL, D_MODEL), D_MODEL ** -0.5),
        'router_w': nrm((DEPTH, D_MODEL, N_EXPERTS), D_MODEL ** -0.5),
        'exp_w_gate': nrm((DEPTH, N_EXPERTS, D_MODEL, EXPERT_FF), D_MODEL ** -0.5),
        'exp_w_up': nrm((DEPTH, N_EXPERTS, D_MODEL, EXPERT_FF), D_MODEL ** -0.5),
        'exp_w_down': nrm((DEPTH, N_EXPERTS, EXPERT_FF, D_MODEL), EXPERT_FF ** -0.5),
    }


def reference(x_prompt, x_sample, cache_k, cache_v, c, c_ctx, mod_w, mod_b, norm1_w, norm2_w, w_in,
              hy_short_w, hy_short_b, hy_w1, hy_b1, hy_w2, hy_b2, hy_w3, hy_b3, hy_decay, hy_bias,
              q_norm_w, k_norm_w, sc_w, sc_b, w_br_a, w_br_b, w_br_c, w_o, router_w,
              exp_w_gate, exp_w_up, exp_w_down):
    def layer_params(l):
        return {
            'mod_w': mod_w[l], 'mod_b': mod_b[l], 'norm1_w': norm1_w[l], 'norm2_w': norm2_w[l],
            'w_in': w_in[l], 'hy_short_w': hy_short_w[l], 'hy_short_b': hy_short_b[l],
            'hy_w1': hy_w1[l], 'hy_b1': hy_b1[l], 'hy_w2': hy_w2[l], 'hy_b2': hy_b2[l],
            'hy_w3': hy_w3[l], 'hy_b3': hy_b3[l], 'hy_decay': hy_decay[l], 'hy_bias': hy_bias[l],
            'q_norm_w': q_norm_w[l], 'k_norm_w': k_norm_w[l], 'sc_w': sc_w[l], 'sc_b': sc_b[l],
            'w_br_a': w_br_a[l], 'w_br_b': w_br_b[l], 'w_br_c': w_br_c[l], 'w_o': w_o[l],
            'router_w': router_w[l], 'exp_w_gate': exp_w_gate[l], 'exp_w_up': exp_w_up[l],
            'exp_w_down': exp_w_down[l],
        }

    xp = x_prompt
    ks_new, vs_new = [], []
    for l in range(DEPTH):
        xp, k_l, v_l = trunk_layer(xp, c_ctx, layer_params(l), None, None, None)
        ks_new.append(k_l)
        vs_new.append(v_l)
    new_cache_k = jnp.stack(ks_new, axis=1)
    new_cache_v = jnp.stack(vs_new, axis=1)

    rows = x_sample.shape[1] // GRID_W
    rope = axial_rope_tables(rows)
    xs = x_sample
    for l in range(DEPTH):
        xs, _, _ = trunk_layer(xs, c, layer_params(l), rope, cache_k[:, l], cache_v[:, l])

    return (xp, xs, new_cache_k, new_cache_v)
```

```python
import functools
import math

import numpy as np
import jax
import jax.numpy as jnp
from jax import lax
from jax.experimental import pallas as pl
from jax.experimental.pallas import tpu as pltpu

D_MODEL = 1024
DEPTH = 2
GRID_W = 64
D_HYENA = 512
HYENA_ORDER = 2
HYENA_POS_BANDS = 16
HYENA_POS_DIM = 1 + 2 * HYENA_POS_BANDS
HYENA_FILTER_HIDDEN = 64
HYENA_WINDOW_SHIFT = 0.05
N_HEADS = 8
N_KV_HEADS = 2
HEAD_DIM = 64
HEADS_PER_KV = N_HEADS // N_KV_HEADS
ATT_Q = N_HEADS * HEAD_DIM
ATT_KV = N_KV_HEADS * HEAD_DIM
ROPE_THETA = 10000.0
D_SCONV = 512
N_EXPERTS = 16
EXPERT_FF = 512
CAPACITY_FACTOR = 2
NORM_EPS = 1e-6
HY_COLS = (HYENA_ORDER + 1) * D_HYENA
ATT_COLS = ATT_Q + 2 * ATT_KV
SC_COLS = 3 * D_SCONV
GATE_COLS = 3 * D_MODEL
MAIN_COLS = HY_COLS + SC_COLS + GATE_COLS

F32 = jnp.float32
BF16 = jnp.bfloat16

V7X_VMEM_BYTES = 64 * 1024 * 1024
VMEM_LIMIT = V7X_VMEM_BYTES - 8 * 1024 * 1024
LANES = 128
MLP_PAD = 128


def _params(*sem):
    return pltpu.CompilerParams(dimension_semantics=sem, vmem_limit_bytes=VMEM_LIMIT)


def _const_spec(shape):
    nd = len(shape)
    return pl.BlockSpec(shape, lambda *_: (0,) * nd, pipeline_mode=pl.Buffered(1))


def _dot(a, b):
    return jnp.dot(a, b, preferred_element_type=F32)


def _silu(x):
    return x * (1.0 / (1.0 + jnp.exp(-x)))


def _sigmoid(x):
    return 1.0 / (1.0 + jnp.exp(-x))


@functools.lru_cache(maxsize=None)
def _dft_tables(L):
    n = 2 * L
    k = np.arange(L, dtype=np.int64)
    idx = (k[:, None] * k[None, :]) % n
    ang = idx.astype(np.float64) * (2.0 * np.pi / n)
    c = np.cos(ang)
    s = np.sin(ang)
    s[0, :] = 1.0 - 2.0 * (k % 2)
    fwd = np.concatenate([c, s], axis=0).astype(np.float32)
    inv = np.concatenate([c, s.T], axis=1).astype(np.float32)
    return fwd, inv


@functools.lru_cache(maxsize=None)
def _pos_features(L):
    n = np.arange(L, dtype=np.float64)
    t = n / max(L - 1, 1)
    bands = np.linspace(1e-4, HYENA_POS_BANDS - 1, HYENA_POS_BANDS)
    ang = 2.0 * math.pi * n[:, None] * bands[None, :] / L
    z = np.concatenate([t[:, None], np.cos(ang), np.sin(ang)], axis=-1)
    zp = np.zeros((L, MLP_PAD), np.float32)
    zp[:, :HYENA_POS_DIM] = z
    return zp


@functools.lru_cache(maxsize=None)
def _rope_tables(L):
    rows = L // GRID_W
    row = np.repeat(np.arange(rows, dtype=np.float64), GRID_W)
    col = np.tile(np.arange(GRID_W, dtype=np.float64), rows)
    axis_dim = HEAD_DIM // 2
    inv_freq = ROPE_THETA ** (-np.arange(0, axis_dim, 2, dtype=np.float64) / axis_dim)
    ar = row[:, None] * inv_freq[None, :]
    ac = col[:, None] * inv_freq[None, :]
    ang = np.concatenate([ar, ar, ac, ac], axis=-1)
    cos = np.cos(ang).astype(np.float32)
    sin = np.sin(ang).astype(np.float32)
    return np.tile(cos, (1, N_HEADS)), np.tile(sin, (1, N_HEADS))


@functools.lru_cache(maxsize=None)
def _head_tables():
    lane = np.arange(ATT_Q)
    block_ones = (lane[:, None] // HEAD_DIM == lane[None, :] // HEAD_DIM).astype(np.float32)
    src = np.arange(ATT_KV)
    dst = np.arange(HEADS_PER_KV * HEAD_DIM)
    rep = np.stack([(src[:, None] == g * HEAD_DIM + dst[None, :] % HEAD_DIM) for g in range(N_KV_HEADS)])
    return block_ones, rep.astype(np.float32)


def _mod_kernel(cond_ref, w_ref, b_ref, o_ref):
    a = _silu(cond_ref[...]).astype(BF16)
    o_ref[...] = _dot(a, w_ref[...].astype(BF16)) + b_ref[...]


def _modulation(cond, mod_w, mod_b):
    rows = cond.shape[0]
    tn = 1024
    ncols = mod_w.shape[-1]
    return pl.pallas_call(
        _mod_kernel,
        grid=(DEPTH, ncols // tn),
        in_specs=[
            pl.BlockSpec((rows, D_MODEL), lambda l, j: (0, 0)),
            pl.BlockSpec((None, D_MODEL, tn), lambda l, j: (l, 0, j)),
            pl.BlockSpec((None, 1, tn), lambda l, j: (l, 0, j)),
        ],
        out_specs=pl.BlockSpec((None, rows, tn), lambda l, j: (l, 0, j)),
        out_shape=jax.ShapeDtypeStruct((DEPTH, rows, ncols), F32),
        compiler_params=_params("parallel", "parallel"),
        name="modulation",
    )(cond, mod_w, mod_b.reshape(DEPTH, 1, ncols))


def _rms_modulate(x, norm_w, shift, scale):
    y = x * lax.rsqrt(jnp.mean(x * x, axis=-1, keepdims=True) + NORM_EPS)
    return (y * norm_w) * (1.0 + scale) + shift


def _inproj_kernel(x_ref, mod_ref, nw_ref, w_ref, o_ref, h_sc):
    @pl.when(pl.program_id(1) == 0)
    def _():
        h = _rms_modulate(x_ref[...], nw_ref[...], mod_ref[0:1, :], mod_ref[1:2, :])
        h_sc[...] = h.astype(BF16)

    o_ref[...] = _dot(h_sc[...], w_ref[...]).astype(o_ref.dtype)


def _inproj(x2d, mod, norm_w, w, out_dtype, rows_per_req, tm=512, tn=768):
    T = x2d.shape[0]
    ncols = w.shape[1]
    tn = min(tn, ncols)
    n_req = mod.shape[0]
    req = (lambda i: (i * tm) // rows_per_req) if n_req > 1 else (lambda i: 0)
    return pl.pallas_call(
        _inproj_kernel,
        grid=(T // tm, ncols // tn),
        in_specs=[
            pl.BlockSpec((tm, D_MODEL), lambda i, j: (i, 0)),
            pl.BlockSpec((None, 6, D_MODEL), lambda i, j: (req(i), 0, 0)),
            pl.BlockSpec((1, D_MODEL), lambda i, j: (0, 0)),
            pl.BlockSpec((D_MODEL, tn), lambda i, j: (0, j)),
        ],
        out_specs=pl.BlockSpec((tm, tn), lambda i, j: (i, j)),
        out_shape=jax.ShapeDtypeStruct((T, ncols), out_dtype),
        scratch_shapes=[pltpu.VMEM((tm, D_MODEL), BF16)],
        compiler_params=_params("parallel", "arbitrary"),
        name="inproj",
    )(x2d, mod, norm_w, w)


def _filter_kernel(z_ref, w1_ref, b1_ref, w2_ref, b2_ref, w3f_ref, b3f_ref, w3b_ref, b3b_ref,
                   decf_ref, decb_ref, fwd_ref, p_ref, *, L):
    hi = lax.Precision.HIGHEST
    h = jnp.sin(jnp.dot(z_ref[...], w1_ref[...], precision=hi, preferred_element_type=F32) + b1_ref[...])
    h = jnp.sin(jnp.dot(h, w2_ref[...], precision=hi, preferred_element_type=F32) + b2_ref[...])
    tc = w3f_ref.shape[1]
    pos = lax.broadcasted_iota(jnp.int32, (L, tc), 0)
    t = pos.astype(F32) / float(max(L - 1, 1))

    def taps(w3_ref, b3_ref, dec_ref):
        g = jnp.dot(h, w3_ref[...], precision=hi, preferred_element_type=F32) + b3_ref[...]
        return g * (jnp.exp(-t * jnp.abs(dec_ref[...])) + HYENA_WINDOW_SHIFT)

    hf = taps(w3f_ref, b3f_ref, decf_ref)
    hb = jnp.where(pos == 0, 0.0, taps(w3b_ref, b3b_ref, decb_ref))
    fwd = fwd_ref[...]
    tf = _dot(fwd, hf.astype(BF16))
    tb = _dot(fwd, hb.astype(BF16))
    k_re = tf[:L] + tb[:L]
    k_im = tb[L:] - tf[L:]
    k_ny = tf[L:] + tb[L:]
    first = pos == 0
    inv_n = 1.0 / (2 * L)
    p_ref[0] = jnp.where(first, k_re * inv_n, k_re * (2.0 * inv_n))
    p_ref[1] = jnp.where(first, 0.0, k_im * (2.0 * inv_n))
    p_ref[2] = jnp.where(first, k_ny * inv_n, k_re * (2.0 * inv_n))


def _hyena_spectra(L, w1, b1, w2, b2, w3, b3, decay, fwd_tab):
    pad_h = MLP_PAD - HYENA_FILTER_HIDDEN
    z = jnp.asarray(_pos_features(L))
    w1p = jnp.pad(w1, ((0, MLP_PAD - HYENA_POS_DIM), (0, pad_h)))
    b1p = jnp.pad(b1, (0, pad_h)).reshape(1, MLP_PAD)
    w2p = jnp.pad(w2, ((0, pad_h), (0, pad_h)))
    b2p = jnp.pad(b2, (0, pad_h)).reshape(1, MLP_PAD)
    w3p = jnp.pad(w3, ((0, pad_h), (0, 0)))
    ncol = w3.shape[1]
    b3r = b3.reshape(1, ncol)
    decr = decay.reshape(1, ncol)
    tc = 256
    nct = D_HYENA // tc
    per_dir = HYENA_ORDER * nct
    col_f = lambda o, c: (0, o * nct + c)
    col_b = lambda o, c: (0, per_dir + o * nct + c)
    return pl.pallas_call(
        functools.partial(_filter_kernel, L=L),
        grid=(HYENA_ORDER, nct),
        in_specs=[
            _const_spec((L, MLP_PAD)),
            _const_spec((MLP_PAD, MLP_PAD)), _const_spec((1, MLP_PAD)),
            _const_spec((MLP_PAD, MLP_PAD)), _const_spec((1, MLP_PAD)),
            pl.BlockSpec((MLP_PAD, tc), col_f), pl.BlockSpec((1, tc), col_f),
            pl.BlockSpec((MLP_PAD, tc), col_b), pl.BlockSpec((1, tc), col_b),
            pl.BlockSpec((1, tc), col_f), pl.BlockSpec((1, tc), col_b),
            _const_spec((2 * L, L)),
        ],
        out_specs=pl.BlockSpec((None, 3, L, tc), lambda o, c: (o, 0, 0, c)),
        out_shape=jax.ShapeDtypeStruct((HYENA_ORDER, 3, L, D_HYENA), F32),
        compiler_params=_params("parallel", "parallel"),
        name="hyena_spectra",
    )(z, w1p, b1p, w2p, b2p, w3p, b3r, w3p, b3r, decr, decr, fwd_tab)


def _conv3(x, w_ref, b_ref, first, last):
    L = x.shape[0]
    prev = jnp.where(first, 0.0, pltpu.roll(x, 1, 0))
    nxt = jnp.where(last, 0.0, pltpu.roll(x, L - 1, 0))
    return prev * w_ref[0:1, :] + x * w_ref[1:2, :] + nxt * w_ref[2:3, :] + b_ref[...]


def _seqmix_kernel(v_ref, x1_ref, x2_ref, u_ref, bg_ref, cg_ref,
                   wv_ref, wx1_ref, wx2_ref, bv_ref, bx1_ref, bx2_ref, scw_ref, scb_ref,
                   p_ref, hb_ref, fwd_ref, inv_ref, ya_ref, yc_ref, spec_sc, *, L):
    tc = v_ref.shape[1]
    pos = lax.broadcasted_iota(jnp.int32, (L, tc), 0)
    first = pos == 0
    last = pos == L - 1
    z = _conv3(v_ref[...].astype(F32), wv_ref, bv_ref, first, last)
    mults = (_conv3(x1_ref[...].astype(F32), wx1_ref, bx1_ref, first, last),
             _conv3(x2_ref[...].astype(F32), wx2_ref, bx2_ref, first, last))
    for o in range(HYENA_ORDER):
        ab = _dot(fwd_ref[...], z.astype(BF16))
        a = ab[:L]
        b = ab[L:]
        p2 = p_ref[o, 1]
        spec_sc[0:L, :] = (a * p_ref[o, 0] + b * p2).astype(BF16)
        spec_sc[L:2 * L, :] = (b * p_ref[o, 2] - a * p2).astype(BF16)
        y = _dot(inv_ref[...], spec_sc[...]) + hb_ref[o:o + 1, :] * z
        z = mults[o] * y
    ya_ref[...] = z.astype(ya_ref.dtype)
    gated = cg_ref[...].astype(F32) * u_ref[...].astype(F32)
    yc = bg_ref[...].astype(F32) * _conv3(gated, scw_ref, scb_ref, first, last)
    yc_ref[...] = yc.astype(yc_ref.dtype)


def _seqmix(main3d, hy_short_w, hy_short_b, sc_w, sc_b, spectra, hy_bias, fwd_tab, inv_tab):
    B, L, _ = main3d.shape
    tc = 256
    nct = D_HYENA // tc
    act = lambda off: pl.BlockSpec((None, L, tc), lambda c, b, off=off: (b, 0, off * nct + c))
    par3 = lambda off: pl.BlockSpec((3, tc), lambda c, b, off=off: (0, off * nct + c))
    par1 = lambda off: pl.BlockSpec((1, tc), lambda c, b, off=off: (0, off * nct + c))
    out = pl.BlockSpec((None, L, tc), lambda c, b: (b, 0, c))
    hsb = hy_short_b.reshape(1, HY_COLS)
    return pl.pallas_call(
        functools.partial(_seqmix_kernel, L=L),
        grid=(nct, B),
        in_specs=[
            act(0), act(1), act(2), act(3), act(4), act(5),
            par3(0), par3(1), par3(2), par1(0), par1(1), par1(2),
            par3(0), par1(0),
            pl.BlockSpec((HYENA_ORDER, 3, L, tc), lambda c, b: (0, 0, 0, c), pipeline_mode=pl.Buffered(1)),
            pl.BlockSpec((HYENA_ORDER, tc), lambda c, b: (0, c)),
            _const_spec((2 * L, L)), _const_spec((L, 2 * L)),
        ],
        out_specs=[out, out],
        out_shape=[jax.ShapeDtypeStruct((B, L, D_HYENA), BF16), jax.ShapeDtypeStruct((B, L, D_SCONV), BF16)],
        scratch_shapes=[pltpu.VMEM((2 * L, tc), BF16)],
        compiler_params=_params("arbitrary", "arbitrary"),
        name="seqmix",
    )(main3d, main3d, main3d, main3d, main3d, main3d,
      hy_short_w, hy_short_w, hy_short_w, hsb, hsb, hsb, sc_w, sc_b.reshape(1, D_SCONV),
      spectra, hy_bias, fwd_tab, inv_tab)


def _head_rms(x, ones_ref, w):
    sq = x * x
    hi = sq.astype(BF16)
    lo = (sq - hi.astype(F32)).astype(BF16)
    ones = ones_ref[...]
    width = x.shape[1]
    ss = _dot(hi, ones[:width, :width]) + _dot(lo, ones[:width, :width])
    return x * lax.rsqrt(ss * (1.0 / HEAD_DIM) + NORM_EPS) * w


def _rope(x, cos, sin):
    width = x.shape[1]
    lane = lax.broadcasted_iota(jnp.int32, x.shape, 1)
    half = HEAD_DIM // 4
    low = (lane % (2 * half)) < half
    rot = jnp.where(low, -pltpu.roll(x, width - half, 1), pltpu.roll(x, half, 1))
    return x * cos + rot * sin


def _attn_kernel(*refs, L, n_ctx, tq, rope):
    if rope:
        (q_ref, k_ref, v_ref, ck_ref, cv_ref, qw_ref, kw_ref, ones_ref, rep_ref,
         cosq_ref, sinq_ref, cosk_ref, sink_ref, yb_ref, k4_sc, v4_sc) = refs
    else:
        (q_ref, k_ref, v_ref, qw_ref, kw_ref, ones_ref, rep_ref,
         yb_ref, ko_ref, vo_ref, k4_sc, v4_sc) = refs

    @pl.when(pl.program_id(1) == 0)
    def _():
        kn = _head_rms(k_ref[...], ones_ref, kw_ref[...])
        v = v_ref[...]
        if rope:
            keys = _rope(kn, cosk_ref[...], sink_ref[...])
            keys = jnp.concatenate([ck_ref[...], keys], axis=0)
            vals = jnp.concatenate([cv_ref[...], v], axis=0)
        else:
            ko_ref[...] = kn
            vo_ref[...] = v
            keys, vals = kn, v
        keys = keys.astype(BF16)
        vals = vals.astype(BF16)
        for g in range(N_KV_HEADS):
            k4_sc[g] = _dot(keys, rep_ref[g]).astype(BF16)
            v4_sc[g] = _dot(vals, rep_ref[g]).astype(BF16)

    q = _head_rms(q_ref[...], ones_ref, qw_ref[...])
    if rope:
        q = _rope(q, cosq_ref[...], sinq_ref[...])
    q = q * (HEAD_DIM ** -0.5)
    gw = HEADS_PER_KV * HEAD_DIM
    lane = lax.broadcasted_iota(jnp.int32, (tq, gw), 1)
    for g in range(N_KV_HEADS):
        qg = q[:, g * gw:(g + 1) * gw]
        acc = jnp.zeros((tq, gw), F32)
        for h in range(HEADS_PER_KV):
            mine = (lane // HEAD_DIM) == h
            qm = jnp.where(mine, qg, 0.0).astype(BF16)
            s = lax.dot_general(qm, k4_sc[g], (((1,), (1,)), ((), ())), preferred_element_type=F32)
            p = jnp.exp(s - jnp.max(s, axis=-1, keepdims=True))
            denom = jnp.sum(p, axis=-1, keepdims=True)
            o4 = _dot(p.astype(BF16), v4_sc[g])
            acc = jnp.where(mine, o4 * (1.0 / denom), acc)
        yb_ref[:, g * gw:(g + 1) * gw] = acc.astype(yb_ref.dtype)


def _attention(qkv3d, q_norm_w, k_norm_w, ctx_k=None, ctx_v=None, layer=0):
    B, L, _ = qkv3d.shape
    rope = ctx_k is not None
    tq = 256
    n_ctx = ctx_k.shape[2] if rope else 0
    lk = L + n_ctx
    ones_np, rep_np = _head_tables()
    ones = jnp.asarray(ones_np).astype(BF16)
    rep = jnp.asarray(rep_np).astype(BF16)
    qw = jnp.tile(q_norm_w, N_HEADS).reshape(1, ATT_Q)
    kw = jnp.tile(k_norm_w, N_KV_HEADS).reshape(1, ATT_KV)
    kblk = ATT_Q // ATT_KV
    in_specs = [
        pl.BlockSpec((None, tq, ATT_Q), lambda b, i: (b, i, 0)),
        pl.BlockSpec((None, L, ATT_KV), lambda b, i: (b, 0, kblk)),
        pl.BlockSpec((None, L, ATT_KV), lambda b, i: (b, 0, kblk + 1)),
    ]
    args = [qkv3d, qkv3d, qkv3d]
    if rope:
        in_specs += [pl.BlockSpec((None, None, n_ctx, ATT_KV), lambda b, i: (b, layer, 0, 0))] * 2
        args += [ctx_k, ctx_v]
    in_specs += [_const_spec((1, ATT_Q)), _const_spec((1, ATT_KV)), _const_spec((ATT_Q, ATT_Q)),
                 _const_spec((N_KV_HEADS, ATT_KV, HEADS_PER_KV * HEAD_DIM))]
    args += [qw, kw, ones, rep]
    yb_shape = jax.ShapeDtypeStruct((B, L, ATT_Q), BF16)
    yb_spec = pl.BlockSpec((None, tq, ATT_Q), lambda b, i: (b, i, 0))
    if rope:
        cos_np, sin_np = _rope_tables(L)
        cos = jnp.asarray(cos_np)
        sin = jnp.asarray(sin_np)
        in_specs += [pl.BlockSpec((tq, ATT_Q), lambda b, i: (i, 0))] * 2
        in_specs += [_const_spec((L, ATT_KV))] * 2
        args += [cos, sin, cos[:, :ATT_KV], sin[:, :ATT_KV]]
        out_specs = yb_spec
        out_shape = yb_shape
    else:
        kv_spec = pl.BlockSpec((None, L, ATT_KV), lambda b, i: (b, 0, 0))
        kv_shape = jax.ShapeDtypeStruct((B, L, ATT_KV), F32)
        out_specs = [yb_spec, kv_spec, kv_spec]
        out_shape = [yb_shape, kv_shape, kv_shape]
    return pl.pallas_call(
        functools.partial(_attn_kernel, L=L, n_ctx=n_ctx, tq=tq, rope=rope),
        grid=(B, L // tq),
        in_specs=in_specs,
        out_specs=out_specs,
        out_shape=out_shape,
        scratch_shapes=[pltpu.VMEM((N_KV_HEADS, lk, HEADS_PER_KV * HEAD_DIM), BF16)] * 2,
        compiler_params=_params("parallel", "arbitrary"),
        name="attention",
    )(*args)


def _merge_kernel(ya_ref, yb_ref, yc_ref, ga_ref, gb_ref, gc_ref, x_ref, mod_ref, nw_ref,
                  wa_ref, wb_ref, wc_ref, wo_ref, xo_ref, h2_ref):
    merged = _sigmoid(ga_ref[...].astype(F32)) * _dot(ya_ref[...], wa_ref[...])
    merged += _sigmoid(gb_ref[...].astype(F32)) * _dot(yb_ref[...], wb_ref[...])
    merged += _sigmoid(gc_ref[...].astype(F32)) * _dot(yc_ref[...], wc_ref[...])
    x = x_ref[...] + mod_ref[2:3, :] * _dot(merged.astype(BF16), wo_ref[...])
    xo_ref[...] = x
    h2_ref[...] = _rms_modulate(x, nw_ref[...], mod_ref[3:4, :], mod_ref[4:5, :]).astype(BF16)


def _merge(ya, yb, yc, main2d, x2d, mod, norm_w, wa, wb, wc, wo, rows_per_req, tm=512):
    T = x2d.shape[0]
    n_req = mod.shape[0]
    req = (lambda i: (i * tm) // rows_per_req) if n_req > 1 else (lambda i: 0)
    gate0 = (HY_COLS + SC_COLS) // D_MODEL
    br = pl.BlockSpec((tm, D_HYENA), lambda i: (i, 0))
    gate = lambda k: pl.BlockSpec((tm, D_MODEL), lambda i, k=k: (i, gate0 + k))
    row = pl.BlockSpec((tm, D_MODEL), lambda i: (i, 0))
    return pl.pallas_call(
        _merge_kernel,
        grid=(T // tm,),
        in_specs=[
            br, br, br, gate(0), gate(1), gate(2), row,
            pl.BlockSpec((None, 6, D_MODEL), lambda i: (req(i), 0, 0)),
            _const_spec((1, D_MODEL)),
            _const_spec((D_HYENA, D_MODEL)), _const_spec((ATT_Q, D_MODEL)), _const_spec((D_SCONV, D_MODEL)),
            _const_spec((D_MODEL, D_MODEL)),
        ],
        out_specs=[row, row],
        out_shape=[jax.ShapeDtypeStruct((T, D_MODEL), F32), jax.ShapeDtypeStruct((T, D_MODEL), BF16)],
        compiler_params=_params("parallel"),
        name="merge",
    )(ya, yb, yc, main2d, main2d, main2d, x2d, mod, norm_w, wa, wb, wc, wo)


def _affinity_rank(r, n_tok):
    nblk = n_tok // LANES
    sub = lax.broadcasted_iota(jnp.int32, (LANES, LANES), 0)
    lane = lax.broadcasted_iota(jnp.int32, (LANES, LANES), 1)
    acc = [jnp.zeros((8, LANES), F32) for _ in range(nblk)]
    for mb in range(nblk):
        rm = r[:, mb * LANES:(mb + 1) * LANES]
        cm = jnp.broadcast_to(rm, (LANES, LANES)).T
        for nb in range(nblk):
            rn = r[:, nb * LANES:(nb + 1) * LANES]
            if mb < nb:
                ones = jnp.where(cm >= rn, 1.0, 0.0)
            elif mb > nb:
                ones = jnp.where(cm > rn, 1.0, 0.0)
            else:
                ones = jnp.where(sub < lane, jnp.where(cm >= rn, 1.0, 0.0), jnp.where(cm > rn, 1.0, 0.0))
            acc[nb] = acc[nb] + ones.reshape(LANES // 8, 8, LANES).sum(axis=0)
    return jnp.concatenate([a.sum(axis=0, keepdims=True) for a in acc], axis=1)


def _dispatch_kernel(h_ref, rw_ref, xs_ref, d_ref, g_ref, aff_sc, *, n_tok, cap):
    h = h_ref[...]
    logits = lax.dot_general(rw_ref[...], h, (((1,), (1,)), ((), ())), preferred_element_type=F32)
    ex = jnp.exp(logits - jnp.max(logits, axis=0, keepdims=True))
    aff_sc[...] = ex / jnp.sum(ex, axis=0, keepdims=True)
    row = lax.broadcasted_iota(jnp.int32, (cap, n_tok), 0).astype(F32)

    def one_expert(e, carry):
        r = aff_sc[pl.ds(e, 1), :]
        hit = row == _affinity_rank(r, n_tok)
        base = pl.multiple_of(e * cap, cap)
        d_ref[pl.ds(base, cap), :] = jnp.where(hit, 1.0, 0.0).astype(BF16)
        gate = jnp.sum(jnp.where(hit, r, 0.0), axis=1, keepdims=True)
        g_ref[pl.ds(base, cap), :] = jnp.broadcast_to(gate, (cap, LANES))
        return carry

    lax.fori_loop(0, N_EXPERTS, one_expert, 0)
    xs_ref[...] = _dot(d_ref[...], h).astype(BF16)


def _dispatch(h3d, router_wt):
    B, n_tok, _ = h3d.shape
    cap = CAPACITY_FACTOR * n_tok // N_EXPERTS
    rows = N_EXPERTS * cap
    blk = lambda w: pl.BlockSpec((None, rows, w), lambda b: (b, 0, 0))
    return pl.pallas_call(
        functools.partial(_dispatch_kernel, n_tok=n_tok, cap=cap),
        grid=(B,),
        in_specs=[
            pl.BlockSpec((None, n_tok, D_MODEL), lambda b: (b, 0, 0)),
            _const_spec((N_EXPERTS, D_MODEL)),
        ],
        out_specs=[blk(D_MODEL), blk(n_tok), blk(LANES)],
        out_shape=[jax.ShapeDtypeStruct((B, rows, D_MODEL), BF16),
                   jax.ShapeDtypeStruct((B, rows, n_tok), BF16),
                   jax.ShapeDtypeStruct((B, rows, LANES), F32)],
        scratch_shapes=[pltpu.VMEM((N_EXPERTS, n_tok), F32)],
        compiler_params=_params("parallel"),
        name="moe_dispatch",
    )(h3d, router_wt)


def _ffn_kernel(xp_ref, gp_ref, xs_ref, gs_ref, wg_ref, wu_ref, wd_ref, yp_ref, ys_ref):
    wg = wg_ref[...].astype(BF16)
    wu = wu_ref[...].astype(BF16)
    wd = wd_ref[...].astype(BF16)
    for x_ref, g_ref, y_ref in ((xp_ref, gp_ref, yp_ref), (xs_ref, gs_ref, ys_ref)):
        nb, cap, _ = x_ref.shape
        x = x_ref[...].reshape(nb * cap, D_MODEL)
        gate = g_ref[...].reshape(nb * cap, LANES)
        gate = jnp.concatenate([gate] * (EXPERT_FF // LANES), axis=1)
        act = _silu(_dot(x, wg)) * _dot(x, wu) * gate
        y = _dot(act.astype(BF16), wd)
        y_ref[...] = y.reshape(nb, cap, D_MODEL).astype(y_ref.dtype)


def _expert_ffn(xp, gp, xs, gs, w_gate, w_up, w_down):
    def act_spec(a):
        nb, _, cap, w = a.shape
        return pl.BlockSpec((nb, None, cap, w), lambda e: (0, e, 0, 0))

    return pl.pallas_call(
        _ffn_kernel,
        grid=(N_EXPERTS,),
        in_specs=[act_spec(xp), act_spec(gp), act_spec(xs), act_spec(gs),
                  pl.BlockSpec((None, D_MODEL, EXPERT_FF), lambda e: (e, 0, 0)),
                  pl.BlockSpec((None, D_MODEL, EXPERT_FF), lambda e: (e, 0, 0)),
                  pl.BlockSpec((None, EXPERT_FF, D_MODEL), lambda e: (e, 0, 0))],
        out_specs=[act_spec(xp), act_spec(xs)],
        out_shape=[jax.ShapeDtypeStruct(xp.shape, BF16), jax.ShapeDtypeStruct(xs.shape, BF16)],
        compiler_params=_params("parallel"),
        name="expert_ffn",
    )(xp, gp, xs, gs, w_gate, w_up, w_down)


def _combine_kernel(d_ref, y_ref, x_ref, mod_ref, o_ref):
    moe = lax.dot_general(d_ref[...], y_ref[...], (((0,), (0,)), ((), ())), preferred_element_type=F32)
    o_ref[...] = x_ref[...] + mod_ref[...] * moe


def _combine(dmat, y3d, x3d, mod_g2, tn=512):
    B, n_tok, _ = x3d.shape
    rows = dmat.shape[1]
    n_req = mod_g2.shape[0]
    req = (lambda b: b) if n_req > 1 else (lambda b: 0)
    return pl.pallas_call(
        _combine_kernel,
        grid=(B, D_MODEL // tn),
        in_specs=[
            pl.BlockSpec((None, rows, n_tok), lambda b, j: (b, 0, 0)),
            pl.BlockSpec((None, rows, tn), lambda b, j: (b, 0, j)),
            pl.BlockSpec((None, n_tok, tn), lambda b, j: (b, 0, j)),
            pl.BlockSpec((None, 1, tn), lambda b, j: (req(b), 0, j)),
        ],
        out_specs=pl.BlockSpec((None, n_tok, tn), lambda b, j: (b, 0, j)),
        out_shape=jax.ShapeDtypeStruct(x3d.shape, F32),
        compiler_params=_params("parallel", "arbitrary"),
        name="moe_combine",
    )(dmat, y3d, x3d, mod_g2)


def _token_mixers(x3d, mod, p, tabs, ctx=None, layer=0):
    B, L, _ = x3d.shape
    x2d = x3d.reshape(B * L, D_MODEL)
    main = _inproj(x2d, mod, p['norm1_w'], p['w_main'], BF16, L)
    qkv = _inproj(x2d, mod, p['norm1_w'], p['w_qkv'], F32, L)
    fwd_tab, inv_tab = tabs
    spectra = _hyena_spectra(L, p['hy_w1'], p['hy_b1'], p['hy_w2'], p['hy_b2'], p['hy_w3'], p['hy_b3'],
                             p['hy_decay'], fwd_tab)
    ya, yc = _seqmix(main.reshape(B, L, MAIN_COLS), p['hy_short_w'], p['hy_short_b'], p['sc_w'], p['sc_b'],
                     spectra, p['hy_bias'], fwd_tab, inv_tab)
    qkv3d = qkv.reshape(B, L, ATT_COLS)
    if ctx is None:
        yb, k, v = _attention(qkv3d, p['q_norm_w'], p['k_norm_w'])
    else:
        yb = _attention(qkv3d, p['q_norm_w'], p['k_norm_w'], ctx[0], ctx[1], layer)
        k = v = None
    x_mid, h2 = _merge(ya.reshape(B * L, D_HYENA), yb.reshape(B * L, ATT_Q), yc.reshape(B * L, D_SCONV),
                       main, x2d, mod, p['norm2_w'], p['w_br_a'], p['w_br_b'], p['w_br_c'], p['w_o'], L)
    return x_mid.reshape(B, L, D_MODEL), h2.reshape(B, L, D_MODEL), k, v


def _moe_split(a, n_exp):
    B, rows, w = a.shape
    return a.reshape(B, n_exp, rows // n_exp, w)


def kernel(x_prompt, x_sample, cache_k, cache_v, c, c_ctx, mod_w, mod_b, norm1_w, norm2_w, w_in, hy_short_w, hy_short_b, hy_w1, hy_b1, hy_w2, hy_b2, hy_w3, hy_b3, hy_decay, hy_bias, q_norm_w, k_norm_w, sc_w, sc_b, w_br_a, w_br_b, w_br_c, w_o, router_w, exp_w_gate, exp_w_up, exp_w_down):
    n_dec = x_sample.shape[0]
    n_ctx = cache_k.shape[2]
    lp = x_prompt.shape[1]
    ls = x_sample.shape[1]

    cond_rows = 16
    cond = jnp.concatenate([c_ctx[None, :], c, jnp.zeros((cond_rows - 1 - n_dec, D_MODEL), F32)], axis=0)
    mod = _modulation(cond, mod_w, mod_b).reshape(DEPTH, cond_rows, 6, D_MODEL)

    tabs_p = tuple(jnp.asarray(t).astype(BF16) for t in _dft_tables(lp))
    tabs_s = tuple(jnp.asarray(t).astype(BF16) for t in _dft_tables(ls))
    ctx_k = cache_k.reshape(n_dec, DEPTH, n_ctx, ATT_KV)
    ctx_v = cache_v.reshape(n_dec, DEPTH, n_ctx, ATT_KV)

    xp, xs = x_prompt, x_sample
    ks_new, vs_new = [], []
    for l in range(DEPTH):
        w_in_l = w_in[l]
        q0 = HY_COLS
        s0 = HY_COLS + ATT_COLS
        p = {
            'norm1_w': norm1_w[l].reshape(1, D_MODEL), 'norm2_w': norm2_w[l].reshape(1, D_MODEL),
            'w_main': jnp.concatenate([w_in_l[:, :q0], w_in_l[:, s0:]], axis=1).astype(BF16),
            'w_qkv': w_in_l[:, q0:s0].astype(BF16),
            'hy_short_w': hy_short_w[l], 'hy_short_b': hy_short_b[l],
            'hy_w1': hy_w1[l], 'hy_b1': hy_b1[l], 'hy_w2': hy_w2[l], 'hy_b2': hy_b2[l],
            'hy_w3': hy_w3[l], 'hy_b3': hy_b3[l], 'hy_decay': hy_decay[l], 'hy_bias': hy_bias[l],
            'q_norm_w': q_norm_w[l], 'k_norm_w': k_norm_w[l], 'sc_w': sc_w[l], 'sc_b': sc_b[l],
            'w_br_a': w_br_a[l].astype(BF16), 'w_br_b': w_br_b[l].astype(BF16),
            'w_br_c': w_br_c[l].astype(BF16), 'w_o': w_o[l].astype(BF16),
        }
        mod_p = mod[l, 0:1]
        mod_s = mod[l, 1:1 + n_dec]
        xp_mid, h2p, k_l, v_l = _token_mixers(xp, mod_p, p, tabs_p)
        xs_mid, h2s, _, _ = _token_mixers(xs, mod_s, p, tabs_s, (ctx_k, ctx_v), l)
        ks_new.append(k_l.reshape(k_l.shape[0], lp, N_KV_HEADS, HEAD_DIM))
        vs_new.append(v_l.reshape(v_l.shape[0], lp, N_KV_HEADS, HEAD_DIM))

        router_wt = router_w[l].T.astype(BF16)
        gp_x, gp_d, gp_g = _dispatch(h2p, router_wt)
        gs_x, gs_d, gs_g = _dispatch(h2s, router_wt)
        yp, ys = _expert_ffn(_moe_split(gp_x, N_EXPERTS), _moe_split(gp_g, N_EXPERTS),
                             _moe_split(gs_x, N_EXPERTS), _moe_split(gs_g, N_EXPERTS),
                             exp_w_gate[l], exp_w_up[l], exp_w_down[l])
        xp = _combine(gp_d, yp.reshape(gp_x.shape), xp_mid, mod_p[:, 5:6])
        xs = _combine(gs_d, ys.reshape(gs_x.shape), xs_mid, mod_s[:, 5:6])

    return (xp, xs, jnp.stack(ks_new, axis=1), jnp.stack(vs_new, axis=1))
```

```python
import functools
import math

import numpy as np
import jax
import jax.numpy as jnp
from jax import lax
from jax.experimental import pallas as pl
from jax.experimental.pallas import tpu as pltpu

D_MODEL = 1024
DEPTH = 2
GRID_W = 64
D_HYENA = 512
HYENA_ORDER = 2
HYENA_POS_BANDS = 16
HYENA_POS_DIM = 1 + 2 * HYENA_POS_BANDS
HYENA_FILTER_HIDDEN = 64
HYENA_WINDOW_SHIFT = 0.05
N_HEADS = 8
N_KV_HEADS = 2
HEAD_DIM = 64
HEADS_PER_KV = N_HEADS // N_KV_HEADS
ATT_Q = N_HEADS * HEAD_DIM
ATT_KV = N_KV_HEADS * HEAD_DIM
ROPE_THETA = 10000.0
D_SCONV = 512
N_EXPERTS = 16
EXPERT_FF = 512
CAPACITY_FACTOR = 2
NORM_EPS = 1e-6
HY_COLS = (HYENA_ORDER + 1) * D_HYENA
ATT_COLS = ATT_Q + 2 * ATT_KV
SC_COLS = 3 * D_SCONV
GATE_COLS = 3 * D_MODEL
MAIN_COLS = HY_COLS + SC_COLS + GATE_COLS
D_IN = MAIN_COLS + ATT_COLS

F32 = jnp.float32
BF16 = jnp.bfloat16

V7X_VMEM_BYTES = 64 * 1024 * 1024
VMEM_LIMIT = V7X_VMEM_BYTES - 8 * 1024 * 1024
LANES = 128
MLP_PAD = 128


def _params(*sem):
    return pltpu.CompilerParams(dimension_semantics=sem, vmem_limit_bytes=VMEM_LIMIT)


def _const_spec(shape):
    nd = len(shape)
    return pl.BlockSpec(shape, lambda *_: (0,) * nd, pipeline_mode=pl.Buffered(1))


def _layer_spec(shape, layer):
    nd = len(shape)
    return pl.BlockSpec((None,) + tuple(shape), lambda *_: (layer,) + (0,) * nd, pipeline_mode=pl.Buffered(1))


def _dot(a, b):
    return jnp.dot(a, b, preferred_element_type=F32)


def _silu(x):
    return x * (1.0 / (1.0 + jnp.exp(-x)))


def _sigmoid(x):
    return 1.0 / (1.0 + jnp.exp(-x))


@functools.lru_cache(maxsize=None)
def _dft_tables(L):
    n = 2 * L
    k = np.arange(L, dtype=np.int64)
    idx = (k[:, None] * k[None, :]) % n
    ang = idx.astype(np.float64) * (2.0 * np.pi / n)
    c = np.cos(ang)
    s = np.sin(ang)
    s[0, :] = 1.0 - 2.0 * (k % 2)
    fwd = np.concatenate([c, s], axis=0).astype(np.float32)
    inv = np.concatenate([c, s.T], axis=1).astype(np.float32)
    return fwd, inv


@functools.lru_cache(maxsize=None)
def _pos_features(L):
    n = np.arange(L, dtype=np.float64)
    t = n / max(L - 1, 1)
    bands = np.linspace(1e-4, HYENA_POS_BANDS - 1, HYENA_POS_BANDS)
    ang = 2.0 * math.pi * n[:, None] * bands[None, :] / L
    z = np.concatenate([t[:, None], np.cos(ang), np.sin(ang)], axis=-1)
    zp = np.zeros((L, MLP_PAD), np.float32)
    zp[:, :HYENA_POS_DIM] = z
    return zp


@functools.lru_cache(maxsize=None)
def _rope_tables(L):
    rows = L // GRID_W
    row = np.repeat(np.arange(rows, dtype=np.float64), GRID_W)
    col = np.tile(np.arange(GRID_W, dtype=np.float64), rows)
    axis_dim = HEAD_DIM // 2
    inv_freq = ROPE_THETA ** (-np.arange(0, axis_dim, 2, dtype=np.float64) / axis_dim)
    ar = row[:, None] * inv_freq[None, :]
    ac = col[:, None] * inv_freq[None, :]
    ang = np.concatenate([ar, ar, ac, ac], axis=-1)
    cos = np.cos(ang).astype(np.float32)
    sin = np.sin(ang).astype(np.float32)
    return np.tile(cos, (1, N_HEADS)), np.tile(sin, (1, N_HEADS))


@functools.lru_cache(maxsize=None)
def _head_tables():
    lane = np.arange(ATT_Q)
    block_ones = (lane[:, None] // HEAD_DIM == lane[None, :] // HEAD_DIM).astype(np.float32)
    src = np.arange(ATT_KV)
    dst = np.arange(HEADS_PER_KV * HEAD_DIM)
    rep = np.stack([(src[:, None] == g * HEAD_DIM + dst[None, :] % HEAD_DIM) for g in range(N_KV_HEADS)])
    return block_ones, rep.astype(np.float32)


@functools.lru_cache(maxsize=None)
def _prefix_table(n):
    i = np.arange(n)
    return (i[:, None] < i[None, :]).astype(np.float32)


def _mod_kernel(cond_ref, w_ref, b_ref, o_ref):
    a = _silu(cond_ref[...]).astype(BF16)
    o_ref[...] = _dot(a, w_ref[...].astype(BF16)) + b_ref[...]


def _modulation(cond, mod_w, mod_b):
    rows = cond.shape[0]
    tn = 1024
    ncols = mod_w.shape[-1]
    return pl.pallas_call(
        _mod_kernel,
        grid=(DEPTH, ncols // tn),
        in_specs=[
            pl.BlockSpec((rows, D_MODEL), lambda l, j: (0, 0)),
            pl.BlockSpec((None, D_MODEL, tn), lambda l, j: (l, 0, j)),
            pl.BlockSpec((None, 1, tn), lambda l, j: (l, 0, j)),
        ],
        out_specs=pl.BlockSpec((None, rows, tn), lambda l, j: (l, 0, j)),
        out_shape=jax.ShapeDtypeStruct((DEPTH, rows, ncols), F32),
        compiler_params=_params("parallel", "parallel"),
        name="modulation",
    )(cond, mod_w, mod_b.reshape(DEPTH, 1, ncols))


def _rms_modulate(x, norm_w, shift, scale):
    y = x * lax.rsqrt(jnp.mean(x * x, axis=-1, keepdims=True) + NORM_EPS)
    return (y * norm_w) * (1.0 + scale) + shift


def _inproj_kernel(x_ref, mod_ref, nw_ref, w_ref, main_ref, qkv_ref, h_sc):
    h = _rms_modulate(x_ref[...], nw_ref[...], mod_ref[0:1, :], mod_ref[1:2, :])
    h_sc[...] = h.astype(BF16)
    for lo in range(0, D_IN, ATT_COLS):
        y = _dot(h_sc[...], w_ref[:, lo:lo + ATT_COLS])
        if lo < HY_COLS:
            main_ref[:, lo:lo + ATT_COLS] = y.astype(BF16)
        elif lo == HY_COLS:
            qkv_ref[...] = y
        else:
            main_ref[:, lo - ATT_COLS:lo] = y.astype(BF16)


def _inproj(x2d, mod, norm_w, w_in_bf, layer, rows_per_req, tm=512):
    T = x2d.shape[0]
    n_req = mod.shape[0]
    req = (lambda i: (i * tm) // rows_per_req) if n_req > 1 else (lambda i: 0)
    return pl.pallas_call(
        _inproj_kernel,
        grid=(T // tm,),
        in_specs=[
            pl.BlockSpec((tm, D_MODEL), lambda i: (i, 0)),
            pl.BlockSpec((None, 6, D_MODEL), lambda i: (req(i), 0, 0)),
            _const_spec((1, D_MODEL)),
            _layer_spec((D_MODEL, D_IN), layer),
        ],
        out_specs=[pl.BlockSpec((tm, MAIN_COLS), lambda i: (i, 0)),
                   pl.BlockSpec((tm, ATT_COLS), lambda i: (i, 0))],
        out_shape=[jax.ShapeDtypeStruct((T, MAIN_COLS), BF16), jax.ShapeDtypeStruct((T, ATT_COLS), F32)],
        scratch_shapes=[pltpu.VMEM((tm, D_MODEL), BF16)],
        compiler_params=_params("parallel"),
        name="inproj",
    )(x2d, mod, norm_w, w_in_bf)


def _filter_kernel(z_ref, w1_ref, b1_ref, w2_ref, b2_ref, w3f_ref, b3f_ref, w3b_ref, b3b_ref,
                   decf_ref, decb_ref, fwd_ref, p_ref, h_sc, *, L):
    hi = lax.Precision.HIGHEST

    @pl.when((pl.program_id(0) == 0) & (pl.program_id(1) == 0))
    def _():
        h1 = jnp.sin(jnp.dot(z_ref[...], w1_ref[...], precision=hi, preferred_element_type=F32) + b1_ref[...])
        h_sc[...] = jnp.sin(jnp.dot(h1, w2_ref[...], precision=hi, preferred_element_type=F32) + b2_ref[...])

    h = h_sc[...]
    tc = w3f_ref.shape[1]
    pos = lax.broadcasted_iota(jnp.int32, (L, tc), 0)
    t = pos.astype(F32) / float(max(L - 1, 1))

    def taps(w3_ref, b3_ref, dec_ref):
        g = jnp.dot(h, w3_ref[...], precision=hi, preferred_element_type=F32) + b3_ref[...]
        return g * (jnp.exp(-t * jnp.abs(dec_ref[...])) + HYENA_WINDOW_SHIFT)

    hf = taps(w3f_ref, b3f_ref, decf_ref)
    hb = jnp.where(pos == 0, 0.0, taps(w3b_ref, b3b_ref, decb_ref))
    fwd = fwd_ref[...]
    tf = _dot(fwd, hf.astype(BF16))
    tb = _dot(fwd, hb.astype(BF16))
    k_re = tf[:L] + tb[:L]
    k_im = tb[L:] - tf[L:]
    k_ny = tf[L:] + tb[L:]
    first = pos == 0
    inv_n = 1.0 / (2 * L)
    p_ref[0] = jnp.where(first, k_re * inv_n, k_re * (2.0 * inv_n))
    p_ref[1] = jnp.where(first, 0.0, k_im * (2.0 * inv_n))
    p_ref[2] = jnp.where(first, k_ny * inv_n, k_re * (2.0 * inv_n))


def _hyena_spectra(L, w1, b1, w2, b2, w3, b3, decay, fwd_tab):
    pad_h = MLP_PAD - HYENA_FILTER_HIDDEN
    z = jnp.asarray(_pos_features(L))
    w1p = jnp.pad(w1, ((0, MLP_PAD - HYENA_POS_DIM), (0, pad_h)))
    b1p = jnp.pad(b1, (0, pad_h)).reshape(1, MLP_PAD)
    w2p = jnp.pad(w2, ((0, pad_h), (0, pad_h)))
    b2p = jnp.pad(b2, (0, pad_h)).reshape(1, MLP_PAD)
    w3p = jnp.pad(w3, ((0, pad_h), (0, 0)))
    ncol = w3.shape[1]
    b3r = b3.reshape(1, ncol)
    decr = decay.reshape(1, ncol)
    tc = 256
    nct = D_HYENA // tc
    per_dir = HYENA_ORDER * nct
    col_f = lambda o, c: (0, o * nct + c)
    col_b = lambda o, c: (0, per_dir + o * nct + c)
    return pl.pallas_call(
        functools.partial(_filter_kernel, L=L),
        grid=(HYENA_ORDER, nct),
        in_specs=[
            _const_spec((L, MLP_PAD)),
            _const_spec((MLP_PAD, MLP_PAD)), _const_spec((1, MLP_PAD)),
            _const_spec((MLP_PAD, MLP_PAD)), _const_spec((1, MLP_PAD)),
            pl.BlockSpec((MLP_PAD, tc), col_f), pl.BlockSpec((1, tc), col_f),
            pl.BlockSpec((MLP_PAD, tc), col_b), pl.BlockSpec((1, tc), col_b),
            pl.BlockSpec((1, tc), col_f), pl.BlockSpec((1, tc), col_b),
            _const_spec((2 * L, L)),
        ],
        out_specs=pl.BlockSpec((None, 3, L, tc), lambda o, c: (o, 0, 0, c)),
        out_shape=jax.ShapeDtypeStruct((HYENA_ORDER, 3, L, D_HYENA), F32),
        scratch_shapes=[pltpu.VMEM((L, MLP_PAD), F32)],
        compiler_params=_params("arbitrary", "arbitrary"),
        name="hyena_spectra",
    )(z, w1p, b1p, w2p, b2p, w3p, b3r, w3p, b3r, decr, decr, fwd_tab)


def _conv3(x, w_ref, b_ref, first, last):
    L = x.shape[0]
    prev = jnp.where(first, 0.0, pltpu.roll(x, 1, 0))
    nxt = jnp.where(last, 0.0, pltpu.roll(x, L - 1, 0))
    return prev * w_ref[0:1, :] + x * w_ref[1:2, :] + nxt * w_ref[2:3, :] + b_ref[...]


def _seqmix_kernel(v_ref, x1_ref, x2_ref, u_ref, bg_ref, cg_ref,
                   wv_ref, wx1_ref, wx2_ref, bv_ref, bx1_ref, bx2_ref, scw_ref, scb_ref,
                   p_ref, hb_ref, fwd_ref, inv_ref, ya_ref, yc_ref, spec_sc, *, L):
    n_req, _, tc = v_ref.shape
    pos = lax.broadcasted_iota(jnp.int32, (L, tc), 0)
    first = pos == 0
    last = pos == L - 1
    for r in range(n_req):
        z = _conv3(v_ref[r].astype(F32), wv_ref, bv_ref, first, last)
        mults = (_conv3(x1_ref[r].astype(F32), wx1_ref, bx1_ref, first, last),
                 _conv3(x2_ref[r].astype(F32), wx2_ref, bx2_ref, first, last))
        for o in range(HYENA_ORDER):
            ab = _dot(fwd_ref[...], z.astype(BF16))
            a = ab[:L]
            b = ab[L:]
            p2 = p_ref[o, 1]
            spec_sc[r, 0:L, :] = (a * p_ref[o, 0] + b * p2).astype(BF16)
            spec_sc[r, L:2 * L, :] = (b * p_ref[o, 2] - a * p2).astype(BF16)
            y = _dot(inv_ref[...], spec_sc[r]) + hb_ref[o:o + 1, :] * z
            z = mults[o] * y
        ya_ref[r] = z.astype(ya_ref.dtype)
        gated = cg_ref[r].astype(F32) * u_ref[r].astype(F32)
        yc = bg_ref[r].astype(F32) * _conv3(gated, scw_ref, scb_ref, first, last)
        yc_ref[r] = yc.astype(yc_ref.dtype)


def _seqmix(main3d, hy_short_w, hy_short_b, sc_w, sc_b, spectra, hy_bias, fwd_tab, inv_tab):
    B, L, _ = main3d.shape
    tc = 256
    nct = D_HYENA // tc
    nr = 2
    act = lambda off: pl.BlockSpec((nr, L, tc), lambda c, b, off=off: (b, 0, off * nct + c))
    par3 = lambda off: pl.BlockSpec((3, tc), lambda c, b, off=off: (0, off * nct + c))
    par1 = lambda off: pl.BlockSpec((1, tc), lambda c, b, off=off: (0, off * nct + c))
    out = pl.BlockSpec((nr, L, tc), lambda c, b: (b, 0, c))
    hsb = hy_short_b.reshape(1, HY_COLS)
    return pl.pallas_call(
        functools.partial(_seqmix_kernel, L=L),
        grid=(nct, B // nr),
        in_specs=[
            act(0), act(1), act(2), act(3), act(4), act(5),
            par3(0), par3(1), par3(2), par1(0), par1(1), par1(2),
            par3(0), par1(0),
            pl.BlockSpec((HYENA_ORDER, 3, L, tc), lambda c, b: (0, 0, 0, c), pipeline_mode=pl.Buffered(1)),
            pl.BlockSpec((HYENA_ORDER, tc), lambda c, b: (0, c)),
            _const_spec((2 * L, L)), _const_spec((L, 2 * L)),
        ],
        out_specs=[out, out],
        out_shape=[jax.ShapeDtypeStruct((B, L, D_HYENA), BF16), jax.ShapeDtypeStruct((B, L, D_SCONV), BF16)],
        scratch_shapes=[pltpu.VMEM((nr, 2 * L, tc), BF16)],
        compiler_params=_params("arbitrary", "arbitrary"),
        name="seqmix",
    )(main3d, main3d, main3d, main3d, main3d, main3d,
      hy_short_w, hy_short_w, hy_short_w, hsb, hsb, hsb, sc_w, sc_b.reshape(1, D_SCONV),
      spectra, hy_bias, fwd_tab, inv_tab)


def _head_rms(x, ones_ref, w):
    sq = x * x
    hi = sq.astype(BF16)
    lo = (sq - hi.astype(F32)).astype(BF16)
    ones = ones_ref[...]
    width = x.shape[1]
    ss = _dot(hi, ones[:width, :width]) + _dot(lo, ones[:width, :width])
    return x * lax.rsqrt(ss * (1.0 / HEAD_DIM) + NORM_EPS) * w


def _rope(x, cos, sin):
    width = x.shape[1]
    lane = lax.broadcasted_iota(jnp.int32, x.shape, 1)
    half = HEAD_DIM // 4
    low = (lane % (2 * half)) < half
    rot = jnp.where(low, -pltpu.roll(x, width - half, 1), pltpu.roll(x, half, 1))
    return x * cos + rot * sin


def _attn_kernel(*refs, L, n_ctx, tq, rope):
    if rope:
        (q_ref, k_ref, v_ref, ck_ref, cv_ref, qw_ref, kw_ref, ones_ref, rep_ref,
         cosq_ref, sinq_ref, cosk_ref, sink_ref, yb_ref, k4_sc, v4_sc) = refs
    else:
        (q_ref, k_ref, v_ref, qw_ref, kw_ref, ones_ref, rep_ref,
         yb_ref, ko_ref, vo_ref, k4_sc, v4_sc) = refs

    @pl.when(pl.program_id(1) == 0)
    def _():
        kn = _head_rms(k_ref[...], ones_ref, kw_ref[...])
        v = v_ref[...]
        if rope:
            keys = _rope(kn, cosk_ref[...], sink_ref[...])
            keys = jnp.concatenate([ck_ref[...], keys], axis=0)
            vals = jnp.concatenate([cv_ref[...], v], axis=0)
        else:
            ko_ref[...] = kn
            vo_ref[...] = v
            keys, vals = kn, v
        keys = keys.astype(BF16)
        vals = vals.astype(BF16)
        for g in range(N_KV_HEADS):
            k4_sc[g] = _dot(keys, rep_ref[g]).astype(BF16)
            v4_sc[g] = _dot(vals, rep_ref[g]).astype(BF16)

    q = _head_rms(q_ref[...], ones_ref, qw_ref[...])
    if rope:
        q = _rope(q, cosq_ref[...], sinq_ref[...])
    q = q * (HEAD_DIM ** -0.5)
    gw = HEADS_PER_KV * HEAD_DIM
    lane = lax.broadcasted_iota(jnp.int32, (tq, gw), 1)
    for g in range(N_KV_HEADS):
        qg = q[:, g * gw:(g + 1) * gw]
        acc = jnp.zeros((tq, gw), F32)
        for h in range(HEADS_PER_KV):
            mine = (lane // HEAD_DIM) == h
            qm = jnp.where(mine, qg, 0.0).astype(BF16)
            s = lax.dot_general(qm, k4_sc[g], (((1,), (1,)), ((), ())), preferred_element_type=F32)
            p = jnp.exp(s - jnp.max(s, axis=-1, keepdims=True))
            denom = jnp.sum(p, axis=-1, keepdims=True)
            o4 = _dot(p.astype(BF16), v4_sc[g])
            acc = jnp.where(mine, o4 * (1.0 / denom), acc)
        yb_ref[:, g * gw:(g + 1) * gw] = acc.astype(yb_ref.dtype)


def _attention(qkv3d, q_norm_w, k_norm_w, ctx_k=None, ctx_v=None, layer=0):
    B, L, _ = qkv3d.shape
    rope = ctx_k is not None
    tq = 256
    n_ctx = ctx_k.shape[2] if rope else 0
    lk = L + n_ctx
    ones_np, rep_np = _head_tables()
    ones = jnp.asarray(ones_np).astype(BF16)
    rep = jnp.asarray(rep_np).astype(BF16)
    qw = jnp.tile(q_norm_w, N_HEADS).reshape(1, ATT_Q)
    kw = jnp.tile(k_norm_w, N_KV_HEADS).reshape(1, ATT_KV)
    kblk = ATT_Q // ATT_KV
    in_specs = [
        pl.BlockSpec((None, tq, ATT_Q), lambda b, i: (b, i, 0)),
        pl.BlockSpec((None, L, ATT_KV), lambda b, i: (b, 0, kblk)),
        pl.BlockSpec((None, L, ATT_KV), lambda b, i: (b, 0, kblk + 1)),
    ]
    args = [qkv3d, qkv3d, qkv3d]
    if rope:
        in_specs += [pl.BlockSpec((None, None, n_ctx, ATT_KV), lambda b, i: (b, layer, 0, 0))] * 2
        args += [ctx_k, ctx_v]
    in_specs += [_const_spec((1, ATT_Q)), _const_spec((1, ATT_KV)), _const_spec((ATT_Q, ATT_Q)),
                 _const_spec((N_KV_HEADS, ATT_KV, HEADS_PER_KV * HEAD_DIM))]
    args += [qw, kw, ones, rep]
    yb_shape = jax.ShapeDtypeStruct((B, L, ATT_Q), BF16)
    yb_spec = pl.BlockSpec((None, tq, ATT_Q), lambda b, i: (b, i, 0))
    if rope:
        cos_np, sin_np = _rope_tables(L)
        cos = jnp.asarray(cos_np)
        sin = jnp.asarray(sin_np)
        in_specs += [pl.BlockSpec((tq, ATT_Q), lambda b, i: (i, 0))] * 2
        in_specs += [_const_spec((L, ATT_KV))] * 2
        args += [cos, sin, cos[:, :ATT_KV], sin[:, :ATT_KV]]
        out_specs = yb_spec
        out_shape = yb_shape
    else:
        kv_spec = pl.BlockSpec((None, L, ATT_KV), lambda b, i: (b, 0, 0))
        kv_shape = jax.ShapeDtypeStruct((B, L, ATT_KV), F32)
        out_specs = [yb_spec, kv_spec, kv_spec]
        out_shape = [yb_shape, kv_shape, kv_shape]
    return pl.pallas_call(
        functools.partial(_attn_kernel, L=L, n_ctx=n_ctx, tq=tq, rope=rope),
        grid=(B, L // tq),
        in_specs=in_specs,
        out_specs=out_specs,
        out_shape=out_shape,
        scratch_shapes=[pltpu.VMEM((N_KV_HEADS, lk, HEADS_PER_KV * HEAD_DIM), BF16)] * 2,
        compiler_params=_params("parallel", "arbitrary"),
        name="attention",
    )(*args)


def _merge_kernel(ya_ref, yb_ref, yc_ref, ga_ref, gb_ref, gc_ref, x_ref, mod_ref, nw_ref,
                  wa_ref, wb_ref, wc_ref, wo_ref, xo_ref, h2_ref):
    merged = _sigmoid(ga_ref[...].astype(F32)) * _dot(ya_ref[...], wa_ref[...])
    merged += _sigmoid(gb_ref[...].astype(F32)) * _dot(yb_ref[...], wb_ref[...])
    merged += _sigmoid(gc_ref[...].astype(F32)) * _dot(yc_ref[...], wc_ref[...])
    x = x_ref[...] + mod_ref[2:3, :] * _dot(merged.astype(BF16), wo_ref[...])
    xo_ref[...] = x
    h2_ref[...] = _rms_modulate(x, nw_ref[...], mod_ref[3:4, :], mod_ref[4:5, :]).astype(BF16)


def _merge(ya, yb, yc, main2d, x2d, mod, norm_w, wa, wb, wc, wo, layer, rows_per_req, tm=512):
    T = x2d.shape[0]
    n_req = mod.shape[0]
    req = (lambda i: (i * tm) // rows_per_req) if n_req > 1 else (lambda i: 0)
    gate0 = (HY_COLS + SC_COLS) // D_MODEL
    br = pl.BlockSpec((tm, D_HYENA), lambda i: (i, 0))
    gate = lambda k: pl.BlockSpec((tm, D_MODEL), lambda i, k=k: (i, gate0 + k))
    row = pl.BlockSpec((tm, D_MODEL), lambda i: (i, 0))
    return pl.pallas_call(
        _merge_kernel,
        grid=(T // tm,),
        in_specs=[
            br, br, br, gate(0), gate(1), gate(2), row,
            pl.BlockSpec((None, 6, D_MODEL), lambda i: (req(i), 0, 0)),
            _const_spec((1, D_MODEL)),
            _layer_spec((D_HYENA, D_MODEL), layer), _layer_spec((ATT_Q, D_MODEL), layer),
            _layer_spec((D_SCONV, D_MODEL), layer), _layer_spec((D_MODEL, D_MODEL), layer),
        ],
        out_specs=[row, row],
        out_shape=[jax.ShapeDtypeStruct((T, D_MODEL), F32), jax.ShapeDtypeStruct((T, D_MODEL), BF16)],
        compiler_params=_params("parallel"),
        name="merge",
    )(ya, yb, yc, main2d, main2d, main2d, x2d, mod, norm_w, wa, wb, wc, wo)


def _route_kernel(h_ref, rw_ref, tri_ref, aff_ref, slot_ref, *, cap):
    b = pl.program_id(0)
    logits = lax.dot_general(rw_ref[...], h_ref[...], (((1,), (1,)), ((), ())), preferred_element_type=F32)
    ex = jnp.exp(logits - jnp.max(logits, axis=0, keepdims=True))
    aff_ref[pl.ds(pl.multiple_of(b * N_EXPERTS, N_EXPERTS), N_EXPERTS), :] = ex / jnp.sum(ex, axis=0, keepdims=True)

    @pl.when(b == pl.num_programs(0) - 1)
    def _():
        aff = aff_ref[...]

        def count(mask):
            return jnp.sum(jnp.where(mask, 1.0, 0.0), axis=1, keepdims=True)

        kth = jnp.zeros((aff.shape[0], 1), jnp.int32)
        for bit in range(30, -1, -1):
            trial = kth | (1 << bit)
            enough = count(aff >= lax.bitcast_convert_type(trial, F32)) >= cap
            kth = jnp.where(enough, trial, kth)
        next_up = lax.bitcast_convert_type(kth + 1, F32)
        above = aff >= next_up
        tied = (aff >= lax.bitcast_convert_type(kth, F32)) & (aff < next_up)
        tri = tri_ref[...]
        tied_before = _dot(jnp.where(tied, 1.0, 0.0).astype(BF16), tri)
        chosen = above | (tied & (tied_before < (cap - count(above))))
        slot = _dot(jnp.where(chosen, 1.0, 0.0).astype(BF16), tri)
        slot_ref[...] = jnp.where(chosen, slot, -1.0)


def _route(h3d, router_wt):
    B, n_tok, _ = h3d.shape
    cap = CAPACITY_FACTOR * n_tok // N_EXPERTS
    tri = jnp.asarray(_prefix_table(n_tok)).astype(BF16)
    whole = pl.BlockSpec((B * N_EXPERTS, n_tok), lambda b: (0, 0))
    shape = jax.ShapeDtypeStruct((B * N_EXPERTS, n_tok), F32)
    return pl.pallas_call(
        functools.partial(_route_kernel, cap=cap),
        grid=(B,),
        in_specs=[
            pl.BlockSpec((None, n_tok, D_MODEL), lambda b: (b, 0, 0)),
            _const_spec((N_EXPERTS, D_MODEL)),
            _const_spec((n_tok, n_tok)),
        ],
        out_specs=[whole, whole],
        out_shape=[shape, shape],
        compiler_params=_params("arbitrary"),
        name="moe_route",
    )(h3d, router_wt, tri)


def _dispatch_kernel(h_ref, aff_ref, slot_ref, xs_ref, d_ref, g_ref, *, n_tok, cap):
    row = lax.broadcasted_iota(jnp.int32, (cap, n_tok), 0).astype(F32)

    def one_expert(e, carry):
        hit = row == slot_ref[pl.ds(e, 1), :]
        base = pl.multiple_of(e * cap, cap)
        d_ref[pl.ds(base, cap), :] = jnp.where(hit, 1.0, 0.0).astype(BF16)
        gate = jnp.sum(jnp.where(hit, aff_ref[pl.ds(e, 1), :], 0.0), axis=1, keepdims=True)
        g_ref[pl.ds(base, cap), :] = jnp.broadcast_to(gate, (cap, LANES))
        return carry

    lax.fori_loop(0, N_EXPERTS, one_expert, 0)
    xs_ref[...] = _dot(d_ref[...], h_ref[...]).astype(BF16)


def _dispatch(h3d, router_wt):
    B, n_tok, _ = h3d.shape
    cap = CAPACITY_FACTOR * n_tok // N_EXPERTS
    rows = N_EXPERTS * cap
    aff, slot = _route(h3d, router_wt)
    per_req = pl.BlockSpec((N_EXPERTS, n_tok), lambda b: (b, 0))
    blk = lambda w: pl.BlockSpec((None, rows, w), lambda b: (b, 0, 0))
    return pl.pallas_call(
        functools.partial(_dispatch_kernel, n_tok=n_tok, cap=cap),
        grid=(B,),
        in_specs=[pl.BlockSpec((None, n_tok, D_MODEL), lambda b: (b, 0, 0)), per_req, per_req],
        out_specs=[blk(D_MODEL), blk(n_tok), blk(LANES)],
        out_shape=[jax.ShapeDtypeStruct((B, rows, D_MODEL), BF16),
                   jax.ShapeDtypeStruct((B, rows, n_tok), BF16),
                   jax.ShapeDtypeStruct((B, rows, LANES), F32)],
        compiler_params=_params("parallel"),
        name="moe_dispatch",
    )(h3d, aff, slot)


def _ffn_kernel(xp_ref, gp_ref, xs_ref, gs_ref, wg_ref, wu_ref, wd_ref, yp_ref, ys_ref):
    wg = wg_ref[...].astype(BF16)
    wu = wu_ref[...].astype(BF16)
    wd = wd_ref[...].astype(BF16)
    for x_ref, g_ref, y_ref in ((xp_ref, gp_ref, yp_ref), (xs_ref, gs_ref, ys_ref)):
        nb, cap, _ = x_ref.shape
        x = x_ref[...].reshape(nb * cap, D_MODEL)
        gate = g_ref[...].reshape(nb * cap, LANES)
        gate = jnp.concatenate([gate] * (EXPERT_FF // LANES), axis=1)
        act = _silu(_dot(x, wg)) * _dot(x, wu) * gate
        y = _dot(act.astype(BF16), wd)
        y_ref[...] = y.reshape(nb, cap, D_MODEL).astype(y_ref.dtype)


def _expert_ffn(xp, gp, xs, gs, w_gate, w_up, w_down, layer):
    def act_spec(a):
        nb, _, cap, w = a.shape
        return pl.BlockSpec((nb, None, cap, w), lambda e: (0, e, 0, 0))

    def w_spec(a):
        return pl.BlockSpec((None, None) + a.shape[2:], lambda e: (layer, e, 0, 0))

    return pl.pallas_call(
        _ffn_kernel,
        grid=(N_EXPERTS,),
        in_specs=[act_spec(xp), act_spec(gp), act_spec(xs), act_spec(gs),
                  w_spec(w_gate), w_spec(w_up), w_spec(w_down)],
        out_specs=[act_spec(xp), act_spec(xs)],
        out_shape=[jax.ShapeDtypeStruct(xp.shape, BF16), jax.ShapeDtypeStruct(xs.shape, BF16)],
        compiler_params=_params("parallel"),
        name="expert_ffn",
    )(xp, gp, xs, gs, w_gate, w_up, w_down)


def _combine_kernel(d_ref, y_ref, x_ref, mod_ref, o_ref):
    moe = lax.dot_general(d_ref[...], y_ref[...], (((0,), (0,)), ((), ())), preferred_element_type=F32)
    o_ref[...] = x_ref[...] + mod_ref[...] * moe


def _combine(dmat, y3d, x3d, mod_g2, tn=512):
    B, n_tok, _ = x3d.shape
    rows = dmat.shape[1]
    n_req = mod_g2.shape[0]
    req = (lambda b: b) if n_req > 1 else (lambda b: 0)
    return pl.pallas_call(
        _combine_kernel,
        grid=(B, D_MODEL // tn),
        in_specs=[
            pl.BlockSpec((None, rows, n_tok), lambda b, j: (b, 0, 0)),
            pl.BlockSpec((None, rows, tn), lambda b, j: (b, 0, j)),
            pl.BlockSpec((None, n_tok, tn), lambda b, j: (b, 0, j)),
            pl.BlockSpec((None, 1, tn), lambda b, j: (req(b), 0, j)),
        ],
        out_specs=pl.BlockSpec((None, n_tok, tn), lambda b, j: (b, 0, j)),
        out_shape=jax.ShapeDtypeStruct(x3d.shape, F32),
        compiler_params=_params("parallel", "arbitrary"),
        name="moe_combine",
    )(dmat, y3d, x3d, mod_g2)


def _token_mixers(x3d, mod, p, tabs, layer, ctx=None):
    B, L, _ = x3d.shape
    x2d = x3d.reshape(B * L, D_MODEL)
    main, qkv = _inproj(x2d, mod, p['norm1_w'], p['w_in'], layer, L)
    fwd_tab, inv_tab = tabs
    spectra = _hyena_spectra(L, p['hy_w1'], p['hy_b1'], p['hy_w2'], p['hy_b2'], p['hy_w3'], p['hy_b3'],
                             p['hy_decay'], fwd_tab)
    ya, yc = _seqmix(main.reshape(B, L, MAIN_COLS), p['hy_short_w'], p['hy_short_b'], p['sc_w'], p['sc_b'],
                     spectra, p['hy_bias'], fwd_tab, inv_tab)
    qkv3d = qkv.reshape(B, L, ATT_COLS)
    if ctx is None:
        yb, k, v = _attention(qkv3d, p['q_norm_w'], p['k_norm_w'])
    else:
        yb = _attention(qkv3d, p['q_norm_w'], p['k_norm_w'], ctx[0], ctx[1], layer)
        k = v = None
    x_mid, h2 = _merge(ya.reshape(B * L, D_HYENA), yb.reshape(B * L, ATT_Q), yc.reshape(B * L, D_SCONV),
                       main, x2d, mod, p['norm2_w'], p['w_br_a'], p['w_br_b'], p['w_br_c'], p['w_o'], layer, L)
    return x_mid.reshape(B, L, D_MODEL), h2.reshape(B, L, D_MODEL), k, v


def _moe_split(a, n_exp):
    B, rows, w = a.shape
    return a.reshape(B, n_exp, rows // n_exp, w)


def kernel(x_prompt, x_sample, cache_k, cache_v, c, c_ctx, mod_w, mod_b, norm1_w, norm2_w, w_in, hy_short_w, hy_short_b, hy_w1, hy_b1, hy_w2, hy_b2, hy_w3, hy_b3, hy_decay, hy_bias, q_norm_w, k_norm_w, sc_w, sc_b, w_br_a, w_br_b, w_br_c, w_o, router_w, exp_w_gate, exp_w_up, exp_w_down):
    n_dec = x_sample.shape[0]
    n_ctx = cache_k.shape[2]
    lp = x_prompt.shape[1]
    ls = x_sample.shape[1]

    cond_rows = 16
    cond = jnp.concatenate([c_ctx[None, :], c, jnp.zeros((cond_rows - 1 - n_dec, D_MODEL), F32)], axis=0)
    mod = _modulation(cond, mod_w, mod_b).reshape(DEPTH, cond_rows, 6, D_MODEL)

    tabs_p = tuple(jnp.asarray(t).astype(BF16) for t in _dft_tables(lp))
    tabs_s = tuple(jnp.asarray(t).astype(BF16) for t in _dft_tables(ls))
    ctx_k = cache_k.reshape(n_dec, DEPTH, n_ctx, ATT_KV)
    ctx_v = cache_v.reshape(n_dec, DEPTH, n_ctx, ATT_KV)
    dense = {'w_in': w_in.astype(BF16), 'w_br_a': w_br_a.astype(BF16), 'w_br_b': w_br_b.astype(BF16),
             'w_br_c': w_br_c.astype(BF16), 'w_o': w_o.astype(BF16)}

    xp, xs = x_prompt, x_sample
    ks_new, vs_new = [], []
    for l in range(DEPTH):
        p = dict(dense)
        p.update({
            'norm1_w': norm1_w[l].reshape(1, D_MODEL), 'norm2_w': norm2_w[l].reshape(1, D_MODEL),
            'hy_short_w': hy_short_w[l], 'hy_short_b': hy_short_b[l],
            'hy_w1': hy_w1[l], 'hy_b1': hy_b1[l], 'hy_w2': hy_w2[l], 'hy_b2': hy_b2[l],
            'hy_w3': hy_w3[l], 'hy_b3': hy_b3[l], 'hy_decay': hy_decay[l], 'hy_bias': hy_bias[l],
            'q_norm_w': q_norm_w[l], 'k_norm_w': k_norm_w[l], 'sc_w': sc_w[l], 'sc_b': sc_b[l],
        })
        mod_p = mod[l, 0:1]
        mod_s = mod[l, 1:1 + n_dec]
        xp_mid, h2p, k_l, v_l = _token_mixers(xp, mod_p, p, tabs_p, l)
        xs_mid, h2s, _, _ = _token_mixers(xs, mod_s, p, tabs_s, l, (ctx_k, ctx_v))
        ks_new.append(k_l.reshape(k_l.shape[0], lp, N_KV_HEADS, HEAD_DIM))
        vs_new.append(v_l.reshape(v_l.shape[0], lp, N_KV_HEADS, HEAD_DIM))

        router_wt = router_w[l].T.astype(BF16)
        gp_x, gp_d, gp_g = _dispatch(h2p, router_wt)
        gs_x, gs_d, gs_g = _dispatch(h2s, router_wt)
        yp, ys = _expert_ffn(_moe_split(gp_x, N_EXPERTS), _moe_split(gp_g, N_EXPERTS),
                             _moe_split(gs_x, N_EXPERTS), _moe_split(gs_g, N_EXPERTS),
                             exp_w_gate, exp_w_up, exp_w_down, l)
        xp = _combine(gp_d, yp.reshape(gp_x.shape), xp_mid, mod_p[:, 5:6])
        xs = _combine(gs_d, ys.reshape(gs_x.shape), xs_mid, mod_s[:, 5:6])

    return (xp, xs, jnp.stack(ks_new, axis=1), jnp.stack(vs_new, axis=1))
```

```python
import functools
import math

import numpy as np
import jax
import jax.numpy as jnp
from jax import lax
from jax.experimental import pallas as pl
from jax.experimental.pallas import tpu as pltpu

D_MODEL = 1024
DEPTH = 2
GRID_W = 64
D_HYENA = 512
HYENA_ORDER = 2
HYENA_POS_BANDS = 16
HYENA_POS_DIM = 1 + 2 * HYENA_POS_BANDS
HYENA_FILTER_HIDDEN = 64
HYENA_WINDOW_SHIFT = 0.05
N_HEADS = 8
N_KV_HEADS = 2
HEAD_DIM = 64
HEADS_PER_KV = N_HEADS // N_KV_HEADS
ATT_Q = N_HEADS * HEAD_DIM
ATT_KV = N_KV_HEADS * HEAD_DIM
ROPE_THETA = 10000.0
D_SCONV = 512
N_EXPERTS = 16
EXPERT_FF = 512
CAPACITY_FACTOR = 2
NORM_EPS = 1e-6
HY_COLS = (HYENA_ORDER + 1) * D_HYENA
ATT_COLS = ATT_Q + 2 * ATT_KV
SC_COLS = 3 * D_SCONV
GATE_COLS = 3 * D_MODEL
MAIN_COLS = HY_COLS + SC_COLS + GATE_COLS
D_IN = MAIN_COLS + ATT_COLS

F32 = jnp.float32
BF16 = jnp.bfloat16

V7X_VMEM_BYTES = 64 * 1024 * 1024
VMEM_LIMIT = V7X_VMEM_BYTES - 8 * 1024 * 1024
LANES = 128
MLP_PAD = 128


def _params(*sem):
    return pltpu.CompilerParams(dimension_semantics=sem, vmem_limit_bytes=VMEM_LIMIT)


def _const_spec(shape):
    nd = len(shape)
    return pl.BlockSpec(shape, lambda *_: (0,) * nd, pipeline_mode=pl.Buffered(1))


def _layer_spec(shape, layer):
    nd = len(shape)
    return pl.BlockSpec((None,) + tuple(shape), lambda *_: (layer,) + (0,) * nd, pipeline_mode=pl.Buffered(1))


def _dot(a, b):
    return jnp.dot(a, b, preferred_element_type=F32)


def _silu(x):
    return x * (1.0 / (1.0 + jnp.exp(-x)))


def _sigmoid(x):
    return 1.0 / (1.0 + jnp.exp(-x))


@functools.lru_cache(maxsize=None)
def _dft_tables(L):
    n = 2 * L
    k = np.arange(L, dtype=np.int64)
    idx = (k[:, None] * k[None, :]) % n
    ang = idx.astype(np.float64) * (2.0 * np.pi / n)
    c = np.cos(ang)
    s = np.sin(ang)
    s[0, :] = 1.0 - 2.0 * (k % 2)
    fwd = np.concatenate([c, s], axis=0).astype(np.float32)
    inv = np.concatenate([c, s.T], axis=1).astype(np.float32)
    return fwd, inv


@functools.lru_cache(maxsize=None)
def _pos_features(L):
    n = np.arange(L, dtype=np.float64)
    t = n / max(L - 1, 1)
    bands = np.linspace(1e-4, HYENA_POS_BANDS - 1, HYENA_POS_BANDS)
    ang = 2.0 * math.pi * n[:, None] * bands[None, :] / L
    z = np.concatenate([t[:, None], np.cos(ang), np.sin(ang)], axis=-1)
    zp = np.zeros((L, MLP_PAD), np.float32)
    zp[:, :HYENA_POS_DIM] = z
    return zp


@functools.lru_cache(maxsize=None)
def _rope_tables(L):
    rows = L // GRID_W
    row = np.repeat(np.arange(rows, dtype=np.float64), GRID_W)
    col = np.tile(np.arange(GRID_W, dtype=np.float64), rows)
    axis_dim = HEAD_DIM // 2
    inv_freq = ROPE_THETA ** (-np.arange(0, axis_dim, 2, dtype=np.float64) / axis_dim)
    ar = row[:, None] * inv_freq[None, :]
    ac = col[:, None] * inv_freq[None, :]
    ang = np.concatenate([ar, ar, ac, ac], axis=-1)
    cos = np.cos(ang).astype(np.float32)
    sin = np.sin(ang).astype(np.float32)
    return np.tile(cos, (1, N_HEADS)), np.tile(sin, (1, N_HEADS))


@functools.lru_cache(maxsize=None)
def _head_tables():
    lane = np.arange(ATT_Q)
    block_ones = (lane[:, None] // HEAD_DIM == lane[None, :] // HEAD_DIM).astype(np.float32)
    src = np.arange(ATT_KV)
    dst = np.arange(HEADS_PER_KV * HEAD_DIM)
    rep = np.stack([(src[:, None] == g * HEAD_DIM + dst[None, :] % HEAD_DIM) for g in range(N_KV_HEADS)])
    return block_ones, rep.astype(np.float32)


@functools.lru_cache(maxsize=None)
def _prefix_table(n):
    i = np.arange(n)
    return (i[:, None] < i[None, :]).astype(np.float32)


def _mod_kernel(cond_ref, w_ref, b_ref, o_ref):
    a = _silu(cond_ref[...]).astype(BF16)
    o_ref[...] = _dot(a, w_ref[...].astype(BF16)) + b_ref[...]


def _modulation(cond, mod_w, mod_b):
    rows = cond.shape[0]
    tn = 1024
    ncols = mod_w.shape[-1]
    return pl.pallas_call(
        _mod_kernel,
        grid=(DEPTH, ncols // tn),
        in_specs=[
            pl.BlockSpec((rows, D_MODEL), lambda l, j: (0, 0)),
            pl.BlockSpec((None, D_MODEL, tn), lambda l, j: (l, 0, j)),
            pl.BlockSpec((None, 1, tn), lambda l, j: (l, 0, j)),
        ],
        out_specs=pl.BlockSpec((None, rows, tn), lambda l, j: (l, 0, j)),
        out_shape=jax.ShapeDtypeStruct((DEPTH, rows, ncols), F32),
        compiler_params=_params("parallel", "parallel"),
        name="modulation",
    )(cond, mod_w, mod_b.reshape(DEPTH, 1, ncols))


def _rms_modulate(x, norm_w, shift, scale):
    y = x * lax.rsqrt(jnp.mean(x * x, axis=-1, keepdims=True) + NORM_EPS)
    return (y * norm_w) * (1.0 + scale) + shift


def _inproj_kernel(x_ref, mod_ref, nw_ref, w_ref, main_ref, qkv_ref, h_sc):
    h = _rms_modulate(x_ref[...], nw_ref[...], mod_ref[0:1, :], mod_ref[1:2, :])
    h_sc[...] = h.astype(BF16)
    for lo in range(0, D_IN, ATT_COLS):
        y = _dot(h_sc[...], w_ref[:, lo:lo + ATT_COLS])
        if lo < HY_COLS:
            main_ref[:, lo:lo + ATT_COLS] = y.astype(BF16)
        elif lo == HY_COLS:
            qkv_ref[...] = y
        else:
            main_ref[:, lo - ATT_COLS:lo] = y.astype(BF16)


def _inproj(x2d, mod, norm_w, w_in_bf, layer, rows_per_req, tm=512):
    T = x2d.shape[0]
    n_req = mod.shape[0]
    req = (lambda i: (i * tm) // rows_per_req) if n_req > 1 else (lambda i: 0)
    return pl.pallas_call(
        _inproj_kernel,
        grid=(T // tm,),
        in_specs=[
            pl.BlockSpec((tm, D_MODEL), lambda i: (i, 0)),
            pl.BlockSpec((None, 6, D_MODEL), lambda i: (req(i), 0, 0)),
            _const_spec((1, D_MODEL)),
            _layer_spec((D_MODEL, D_IN), layer),
        ],
        out_specs=[pl.BlockSpec((tm, MAIN_COLS), lambda i: (i, 0)),
                   pl.BlockSpec((tm, ATT_COLS), lambda i: (i, 0))],
        out_shape=[jax.ShapeDtypeStruct((T, MAIN_COLS), BF16), jax.ShapeDtypeStruct((T, ATT_COLS), F32)],
        scratch_shapes=[pltpu.VMEM((tm, D_MODEL), BF16)],
        compiler_params=_params("parallel"),
        name="inproj",
    )(x2d, mod, norm_w, w_in_bf)


def _filter_kernel(z_ref, w1_ref, b1_ref, w2_ref, b2_ref, w3f_ref, b3f_ref, w3b_ref, b3b_ref,
                   decf_ref, decb_ref, fwd_ref, p_ref, h_sc, *, L):
    hi = lax.Precision.HIGHEST

    @pl.when((pl.program_id(0) == 0) & (pl.program_id(1) == 0))
    def _():
        h1 = jnp.sin(jnp.dot(z_ref[...], w1_ref[...], precision=hi, preferred_element_type=F32) + b1_ref[...])
        h_sc[...] = jnp.sin(jnp.dot(h1, w2_ref[...], precision=hi, preferred_element_type=F32) + b2_ref[...])

    h = h_sc[...]
    tc = w3f_ref.shape[1]
    pos = lax.broadcasted_iota(jnp.int32, (L, tc), 0)
    t = pos.astype(F32) / float(max(L - 1, 1))

    def taps(w3_ref, b3_ref, dec_ref):
        g = jnp.dot(h, w3_ref[...], precision=hi, preferred_element_type=F32) + b3_ref[...]
        return g * (jnp.exp(-t * jnp.abs(dec_ref[...])) + HYENA_WINDOW_SHIFT)

    hf = taps(w3f_ref, b3f_ref, decf_ref)
    hb = jnp.where(pos == 0, 0.0, taps(w3b_ref, b3b_ref, decb_ref))
    fwd = fwd_ref[...]
    tf = _dot(fwd, hf.astype(BF16))
    tb = _dot(fwd, hb.astype(BF16))
    k_re = tf[:L] + tb[:L]
    k_im = tb[L:] - tf[L:]
    k_ny = tf[L:] + tb[L:]
    first = pos == 0
    inv_n = 1.0 / (2 * L)
    p_ref[0] = jnp.where(first, k_re * inv_n, k_re * (2.0 * inv_n))
    p_ref[1] = jnp.where(first, 0.0, k_im * (2.0 * inv_n))
    p_ref[2] = jnp.where(first, k_ny * inv_n, k_re * (2.0 * inv_n))


def _hyena_spectra(L, w1, b1, w2, b2, w3, b3, decay, fwd_tab):
    pad_h = MLP_PAD - HYENA_FILTER_HIDDEN
    z = jnp.asarray(_pos_features(L))
    w1p = jnp.pad(w1, ((0, MLP_PAD - HYENA_POS_DIM), (0, pad_h)))
    b1p = jnp.pad(b1, (0, pad_h)).reshape(1, MLP_PAD)
    w2p = jnp.pad(w2, ((0, pad_h), (0, pad_h)))
    b2p = jnp.pad(b2, (0, pad_h)).reshape(1, MLP_PAD)
    w3p = jnp.pad(w3, ((0, pad_h), (0, 0)))
    ncol = w3.shape[1]
    b3r = b3.reshape(1, ncol)
    decr = decay.reshape(1, ncol)
    tc = 256
    nct = D_HYENA // tc
    per_dir = HYENA_ORDER * nct
    col_f = lambda o, c: (0, o * nct + c)
    col_b = lambda o, c: (0, per_dir + o * nct + c)
    return pl.pallas_call(
        functools.partial(_filter_kernel, L=L),
        grid=(HYENA_ORDER, nct),
        in_specs=[
            _const_spec((L, MLP_PAD)),
            _const_spec((MLP_PAD, MLP_PAD)), _const_spec((1, MLP_PAD)),
            _const_spec((MLP_PAD, MLP_PAD)), _const_spec((1, MLP_PAD)),
            pl.BlockSpec((MLP_PAD, tc), col_f), pl.BlockSpec((1, tc), col_f),
            pl.BlockSpec((MLP_PAD, tc), col_b), pl.BlockSpec((1, tc), col_b),
            pl.BlockSpec((1, tc), col_f), pl.BlockSpec((1, tc), col_b),
            _const_spec((2 * L, L)),
        ],
        out_specs=pl.BlockSpec((None, 3, L, tc), lambda o, c: (o, 0, 0, c)),
        out_shape=jax.ShapeDtypeStruct((HYENA_ORDER, 3, L, D_HYENA), F32),
        scratch_shapes=[pltpu.VMEM((L, MLP_PAD), F32)],
        compiler_params=_params("arbitrary", "arbitrary"),
        name="hyena_spectra",
    )(z, w1p, b1p, w2p, b2p, w3p, b3r, w3p, b3r, decr, decr, fwd_tab)


def _conv3(x, w_ref, b_ref, first, last):
    L = x.shape[0]
    prev = jnp.where(first, 0.0, pltpu.roll(x, 1, 0))
    nxt = jnp.where(last, 0.0, pltpu.roll(x, L - 1, 0))
    return prev * w_ref[0:1, :] + x * w_ref[1:2, :] + nxt * w_ref[2:3, :] + b_ref[...]


def _seqmix_kernel(v_ref, x1_ref, x2_ref, u_ref, bg_ref, cg_ref,
                   wv_ref, wx1_ref, wx2_ref, bv_ref, bx1_ref, bx2_ref, scw_ref, scb_ref,
                   p_ref, hb_ref, fwd_ref, inv_ref, ya_ref, yc_ref, spec_sc, *, L):
    n_req, _, tc = v_ref.shape
    pos = lax.broadcasted_iota(jnp.int32, (L, tc), 0)
    first = pos == 0
    last = pos == L - 1
    mult_refs = ((x1_ref, wx1_ref, bx1_ref), (x2_ref, wx2_ref, bx2_ref))

    def chain(r):
        z = _conv3(v_ref[r].astype(F32), wv_ref, bv_ref, first, last)
        yield
        for o in range(HYENA_ORDER):
            ab = _dot(fwd_ref[...], z.astype(BF16))
            yield
            a = ab[:L]
            b = ab[L:]
            p2 = p_ref[o, 1]
            spec_sc[r, 0:L, :] = (a * p_ref[o, 0] + b * p2).astype(BF16)
            spec_sc[r, L:2 * L, :] = (b * p_ref[o, 2] - a * p2).astype(BF16)
            x_ref, w_ref, b_ref = mult_refs[o]
            mult = _conv3(x_ref[r].astype(F32), w_ref, b_ref, first, last)
            yield
            y = _dot(inv_ref[...], spec_sc[r])
            yield
            z = mult * (y + hb_ref[o:o + 1, :] * z)
            if o == HYENA_ORDER - 1:
                ya_ref[r] = z.astype(ya_ref.dtype)
                gated = cg_ref[r].astype(F32) * u_ref[r].astype(F32)
                yc = bg_ref[r].astype(F32) * _conv3(gated, scw_ref, scb_ref, first, last)
                yc_ref[r] = yc.astype(yc_ref.dtype)
            yield

    chains = [chain(r) for r in range(n_req)]
    n_stages = 1 + 4 * HYENA_ORDER
    for tick in range(n_stages + n_req - 1):
        for r, c in enumerate(chains):
            if 0 <= tick - r < n_stages:
                next(c)


def _seqmix(main3d, hy_short_w, hy_short_b, sc_w, sc_b, spectra, hy_bias, fwd_tab, inv_tab):
    B, L, _ = main3d.shape
    tc = 256
    nct = D_HYENA // tc
    nr = 2
    act = lambda off: pl.BlockSpec((nr, L, tc), lambda c, b, off=off: (b, 0, off * nct + c))
    par3 = lambda off: pl.BlockSpec((3, tc), lambda c, b, off=off: (0, off * nct + c))
    par1 = lambda off: pl.BlockSpec((1, tc), lambda c, b, off=off: (0, off * nct + c))
    out = pl.BlockSpec((nr, L, tc), lambda c, b: (b, 0, c))
    hsb = hy_short_b.reshape(1, HY_COLS)
    return pl.pallas_call(
        functools.partial(_seqmix_kernel, L=L),
        grid=(nct, B // nr),
        in_specs=[
            act(0), act(1), act(2), act(3), act(4), act(5),
            par3(0), par3(1), par3(2), par1(0), par1(1), par1(2),
            par3(0), par1(0),
            pl.BlockSpec((HYENA_ORDER, 3, L, tc), lambda c, b: (0, 0, 0, c), pipeline_mode=pl.Buffered(1)),
            pl.BlockSpec((HYENA_ORDER, tc), lambda c, b: (0, c)),
            _const_spec((2 * L, L)), _const_spec((L, 2 * L)),
        ],
        out_specs=[out, out],
        out_shape=[jax.ShapeDtypeStruct((B, L, D_HYENA), BF16), jax.ShapeDtypeStruct((B, L, D_SCONV), BF16)],
        scratch_shapes=[pltpu.VMEM((nr, 2 * L, tc), BF16)],
        compiler_params=_params("arbitrary", "arbitrary"),
        name="seqmix",
    )(main3d, main3d, main3d, main3d, main3d, main3d,
      hy_short_w, hy_short_w, hy_short_w, hsb, hsb, hsb, sc_w, sc_b.reshape(1, D_SCONV),
      spectra, hy_bias, fwd_tab, inv_tab)


def _head_rms(x, ones_ref, w):
    sq = x * x
    hi = sq.astype(BF16)
    lo = (sq - hi.astype(F32)).astype(BF16)
    ones = ones_ref[...]
    width = x.shape[1]
    ss = _dot(hi, ones[:width, :width]) + _dot(lo, ones[:width, :width])
    return x * lax.rsqrt(ss * (1.0 / HEAD_DIM) + NORM_EPS) * w


def _rope(x, cos, sin):
    width = x.shape[1]
    lane = lax.broadcasted_iota(jnp.int32, x.shape, 1)
    half = HEAD_DIM // 4
    low = (lane % (2 * half)) < half
    rot = jnp.where(low, -pltpu.roll(x, width - half, 1), pltpu.roll(x, half, 1))
    return x * cos + rot * sin


def _attn_kernel(*refs, L, n_ctx, tq, rope):
    if rope:
        (q_ref, k_ref, v_ref, ck_ref, cv_ref, qw_ref, kw_ref, ones_ref, rep_ref, rept_ref,
         cosq_ref, sinq_ref, cosk_ref, sink_ref, yb_ref, k4_sc, v4_sc) = refs
    else:
        (q_ref, k_ref, v_ref, qw_ref, kw_ref, ones_ref, rep_ref, rept_ref,
         yb_ref, ko_ref, vo_ref, k4_sc, v4_sc) = refs

    @pl.when(pl.program_id(1) == 0)
    def _():
        kn = _head_rms(k_ref[...], ones_ref, kw_ref[...])
        v = v_ref[...]
        if rope:
            keys = _rope(kn, cosk_ref[...], sink_ref[...])
            keys = jnp.concatenate([ck_ref[...], keys], axis=0)
            vals = jnp.concatenate([cv_ref[...], v], axis=0)
        else:
            ko_ref[...] = kn
            vo_ref[...] = v
            keys, vals = kn, v
        keys = keys.astype(BF16)
        vals = vals.astype(BF16)
        for g in range(N_KV_HEADS):
            k4_sc[g] = lax.dot_general(rept_ref[g], keys, (((1,), (1,)), ((), ())),
                                       preferred_element_type=F32).astype(BF16)
            v4_sc[g] = _dot(vals, rep_ref[g]).astype(BF16)

    q = _head_rms(q_ref[...], ones_ref, qw_ref[...])
    if rope:
        q = _rope(q, cosq_ref[...], sinq_ref[...])
    q = q * (HEAD_DIM ** -0.5)
    gw = HEADS_PER_KV * HEAD_DIM
    lane = lax.broadcasted_iota(jnp.int32, (tq, gw), 1)
    for g in range(N_KV_HEADS):
        qg = q[:, g * gw:(g + 1) * gw]
        acc = jnp.zeros((tq, gw), F32)
        for h in range(HEADS_PER_KV):
            mine = (lane // HEAD_DIM) == h
            qm = jnp.where(mine, qg, 0.0).astype(BF16)
            s = _dot(qm, k4_sc[g])
            p = jnp.exp(s - jnp.max(s, axis=-1, keepdims=True))
            denom = jnp.sum(p, axis=-1, keepdims=True)
            o4 = _dot(p.astype(BF16), v4_sc[g])
            acc = jnp.where(mine, o4 * (1.0 / denom), acc)
        yb_ref[:, g * gw:(g + 1) * gw] = acc.astype(yb_ref.dtype)


def _attention(qkv3d, q_norm_w, k_norm_w, ctx_k=None, ctx_v=None, layer=0):
    B, L, _ = qkv3d.shape
    rope = ctx_k is not None
    tq = min(512, L)
    n_ctx = ctx_k.shape[2] if rope else 0
    lk = L + n_ctx
    ones_np, rep_np = _head_tables()
    ones = jnp.asarray(ones_np).astype(BF16)
    rep = jnp.asarray(rep_np).astype(BF16)
    rept = jnp.asarray(np.swapaxes(rep_np, 1, 2)).astype(BF16)
    qw = jnp.tile(q_norm_w, N_HEADS).reshape(1, ATT_Q)
    kw = jnp.tile(k_norm_w, N_KV_HEADS).reshape(1, ATT_KV)
    kblk = ATT_Q // ATT_KV
    in_specs = [
        pl.BlockSpec((None, tq, ATT_Q), lambda b, i: (b, i, 0)),
        pl.BlockSpec((None, L, ATT_KV), lambda b, i: (b, 0, kblk)),
        pl.BlockSpec((None, L, ATT_KV), lambda b, i: (b, 0, kblk + 1)),
    ]
    args = [qkv3d, qkv3d, qkv3d]
    if rope:
        in_specs += [pl.BlockSpec((None, None, n_ctx, ATT_KV), lambda b, i: (b, layer, 0, 0))] * 2
        args += [ctx_k, ctx_v]
    in_specs += [_const_spec((1, ATT_Q)), _const_spec((1, ATT_KV)), _const_spec((ATT_Q, ATT_Q)),
                 _const_spec((N_KV_HEADS, ATT_KV, HEADS_PER_KV * HEAD_DIM)),
                 _const_spec((N_KV_HEADS, HEADS_PER_KV * HEAD_DIM, ATT_KV))]
    args += [qw, kw, ones, rep, rept]
    yb_shape = jax.ShapeDtypeStruct((B, L, ATT_Q), BF16)
    yb_spec = pl.BlockSpec((None, tq, ATT_Q), lambda b, i: (b, i, 0))
    if rope:
        cos_np, sin_np = _rope_tables(L)
        cos = jnp.asarray(cos_np)
        sin = jnp.asarray(sin_np)
        in_specs += [pl.BlockSpec((tq, ATT_Q), lambda b, i: (i, 0))] * 2
        in_specs += [_const_spec((L, ATT_KV))] * 2
        args += [cos, sin, cos[:, :ATT_KV], sin[:, :ATT_KV]]
        out_specs = yb_spec
        out_shape = yb_shape
    else:
        kv_spec = pl.BlockSpec((None, L, ATT_KV), lambda b, i: (b, 0, 0))
        kv_shape = jax.ShapeDtypeStruct((B, L, ATT_KV), F32)
        out_specs = [yb_spec, kv_spec, kv_spec]
        out_shape = [yb_shape, kv_shape, kv_shape]
    return pl.pallas_call(
        functools.partial(_attn_kernel, L=L, n_ctx=n_ctx, tq=tq, rope=rope),
        grid=(B, L // tq),
        in_specs=in_specs,
        out_specs=out_specs,
        out_shape=out_shape,
        scratch_shapes=[pltpu.VMEM((N_KV_HEADS, HEADS_PER_KV * HEAD_DIM, lk), BF16),
                        pltpu.VMEM((N_KV_HEADS, lk, HEADS_PER_KV * HEAD_DIM), BF16)],
        compiler_params=_params("parallel", "arbitrary"),
        name="attention",
    )(*args)


def _merge_kernel(ya_ref, yb_ref, yc_ref, ga_ref, gb_ref, gc_ref, x_ref, mod_ref, nw_ref,
                  wa_ref, wb_ref, wc_ref, wo_ref, xo_ref, h2_ref):
    merged = _sigmoid(ga_ref[...].astype(F32)) * _dot(ya_ref[...], wa_ref[...])
    merged += _sigmoid(gb_ref[...].astype(F32)) * _dot(yb_ref[...], wb_ref[...])
    merged += _sigmoid(gc_ref[...].astype(F32)) * _dot(yc_ref[...], wc_ref[...])
    x = x_ref[...] + mod_ref[2:3, :] * _dot(merged.astype(BF16), wo_ref[...])
    xo_ref[...] = x
    h2_ref[...] = _rms_modulate(x, nw_ref[...], mod_ref[3:4, :], mod_ref[4:5, :]).astype(BF16)


def _merge(ya, yb, yc, main2d, x2d, mod, norm_w, wa, wb, wc, wo, layer, rows_per_req, tm=512):
    T = x2d.shape[0]
    n_req = mod.shape[0]
    req = (lambda i: (i * tm) // rows_per_req) if n_req > 1 else (lambda i: 0)
    gate0 = (HY_COLS + SC_COLS) // D_MODEL
    br = pl.BlockSpec((tm, D_HYENA), lambda i: (i, 0))
    gate = lambda k: pl.BlockSpec((tm, D_MODEL), lambda i, k=k: (i, gate0 + k))
    row = pl.BlockSpec((tm, D_MODEL), lambda i: (i, 0))
    return pl.pallas_call(
        _merge_kernel,
        grid=(T // tm,),
        in_specs=[
            br, br, br, gate(0), gate(1), gate(2), row,
            pl.BlockSpec((None, 6, D_MODEL), lambda i: (req(i), 0, 0)),
            _const_spec((1, D_MODEL)),
            _layer_spec((D_HYENA, D_MODEL), layer), _layer_spec((ATT_Q, D_MODEL), layer),
            _layer_spec((D_SCONV, D_MODEL), layer), _layer_spec((D_MODEL, D_MODEL), layer),
        ],
        out_specs=[row, row],
        out_shape=[jax.ShapeDtypeStruct((T, D_MODEL), F32), jax.ShapeDtypeStruct((T, D_MODEL), BF16)],
        compiler_params=_params("parallel"),
        name="merge",
    )(ya, yb, yc, main2d, main2d, main2d, x2d, mod, norm_w, wa, wb, wc, wo)


def _route_kernel(h_ref, rw_ref, tri_ref, aff_ref, slot_ref, *, cap):
    b = pl.program_id(0)
    logits = lax.dot_general(rw_ref[...], h_ref[...], (((1,), (1,)), ((), ())), preferred_element_type=F32)
    ex = jnp.exp(logits - jnp.max(logits, axis=0, keepdims=True))
    aff_ref[pl.ds(pl.multiple_of(b * N_EXPERTS, N_EXPERTS), N_EXPERTS), :] = ex / jnp.sum(ex, axis=0, keepdims=True)

    @pl.when(b == pl.num_programs(0) - 1)
    def _():
        aff = aff_ref[...]

        def count(mask):
            return jnp.sum(jnp.where(mask, 1.0, 0.0), axis=1, keepdims=True)

        kth = jnp.zeros((aff.shape[0], 1), jnp.int32)
        for bit in range(30, -1, -1):
            trial = kth | (1 << bit)
            enough = count(aff >= lax.bitcast_convert_type(trial, F32)) >= cap
            kth = jnp.where(enough, trial, kth)
        next_up = lax.bitcast_convert_type(kth + 1, F32)
        above = aff >= next_up
        tied = (aff >= lax.bitcast_convert_type(kth, F32)) & (aff < next_up)
        tri = tri_ref[...]
        tied_before = _dot(jnp.where(tied, 1.0, 0.0).astype(BF16), tri)
        chosen = above | (tied & (tied_before < (cap - count(above))))
        slot = _dot(jnp.where(chosen, 1.0, 0.0).astype(BF16), tri)
        slot_ref[...] = jnp.where(chosen, slot, -1.0)


def _route(h3d, router_wt):
    B, n_tok, _ = h3d.shape
    cap = CAPACITY_FACTOR * n_tok // N_EXPERTS
    tri = jnp.asarray(_prefix_table(n_tok)).astype(BF16)
    whole = pl.BlockSpec((B * N_EXPERTS, n_tok), lambda b: (0, 0))
    shape = jax.ShapeDtypeStruct((B * N_EXPERTS, n_tok), F32)
    return pl.pallas_call(
        functools.partial(_route_kernel, cap=cap),
        grid=(B,),
        in_specs=[
            pl.BlockSpec((None, n_tok, D_MODEL), lambda b: (b, 0, 0)),
            _const_spec((N_EXPERTS, D_MODEL)),
            _const_spec((n_tok, n_tok)),
        ],
        out_specs=[whole, whole],
        out_shape=[shape, shape],
        compiler_params=_params("arbitrary"),
        name="moe_route",
    )(h3d, router_wt, tri)


def _dispatch_kernel(h_ref, aff_ref, slot_ref, xs_ref, d_ref, g_ref, *, n_tok, cap):
    row = lax.broadcasted_iota(jnp.int32, (cap, n_tok), 0).astype(F32)

    def one_expert(e, carry):
        hit = row == slot_ref[pl.ds(e, 1), :]
        base = pl.multiple_of(e * cap, cap)
        d_ref[pl.ds(base, cap), :] = jnp.where(hit, 1.0, 0.0).astype(BF16)
        gate = jnp.sum(jnp.where(hit, aff_ref[pl.ds(e, 1), :], 0.0), axis=1, keepdims=True)
        g_ref[pl.ds(base, cap), :] = jnp.broadcast_to(gate, (cap, LANES))
        return carry

    lax.fori_loop(0, N_EXPERTS, one_expert, 0)
    xs_ref[...] = _dot(d_ref[...], h_ref[...]).astype(BF16)


def _dispatch(h3d, router_wt):
    B, n_tok, _ = h3d.shape
    cap = CAPACITY_FACTOR * n_tok // N_EXPERTS
    rows = N_EXPERTS * cap
    aff, slot = _route(h3d, router_wt)
    per_req = pl.BlockSpec((N_EXPERTS, n_tok), lambda b: (b, 0))
    blk = lambda w: pl.BlockSpec((None, rows, w), lambda b: (b, 0, 0))
    return pl.pallas_call(
        functools.partial(_dispatch_kernel, n_tok=n_tok, cap=cap),
        grid=(B,),
        in_specs=[pl.BlockSpec((None, n_tok, D_MODEL), lambda b: (b, 0, 0)), per_req, per_req],
        out_specs=[blk(D_MODEL), blk(n_tok), blk(LANES)],
        out_shape=[jax.ShapeDtypeStruct((B, rows, D_MODEL), BF16),
                   jax.ShapeDtypeStruct((B, rows, n_tok), BF16),
                   jax.ShapeDtypeStruct((B, rows, LANES), F32)],
        compiler_params=_params("parallel"),
        name="moe_dispatch",
    )(h3d, aff, slot)


def _ffn_kernel(xp_ref, gp_ref, xs_ref, gs_ref, wg_ref, wu_ref, wd_ref, yp_ref, ys_ref):
    wg = wg_ref[...].astype(BF16)
    wu = wu_ref[...].astype(BF16)
    wd = wd_ref[...].astype(BF16)
    for x_ref, g_ref, y_ref in ((xp_ref, gp_ref, yp_ref), (xs_ref, gs_ref, ys_ref)):
        nb, cap, _ = x_ref.shape
        x = x_ref[...].reshape(nb * cap, D_MODEL)
        gate = g_ref[...].reshape(nb * cap, LANES)
        gate = jnp.concatenate([gate] * (EXPERT_FF // LANES), axis=1)
        act = _silu(_dot(x, wg)) * _dot(x, wu) * gate
        y = _dot(act.astype(BF16), wd)
        y_ref[...] = y.reshape(nb, cap, D_MODEL).astype(y_ref.dtype)


def _expert_ffn(xp, gp, xs, gs, w_gate, w_up, w_down, layer):
    def act_spec(a):
        nb, _, cap, w = a.shape
        return pl.BlockSpec((nb, None, cap, w), lambda e: (0, e, 0, 0))

    def w_spec(a):
        return pl.BlockSpec((None, None) + a.shape[2:], lambda e: (layer, e, 0, 0))

    return pl.pallas_call(
        _ffn_kernel,
        grid=(N_EXPERTS,),
        in_specs=[act_spec(xp), act_spec(gp), act_spec(xs), act_spec(gs),
                  w_spec(w_gate), w_spec(w_up), w_spec(w_down)],
        out_specs=[act_spec(xp), act_spec(xs)],
        out_shape=[jax.ShapeDtypeStruct(xp.shape, BF16), jax.ShapeDtypeStruct(xs.shape, BF16)],
        compiler_params=_params("parallel"),
        name="expert_ffn",
    )(xp, gp, xs, gs, w_gate, w_up, w_down)


def _combine_kernel(d_ref, y_ref, x_ref, mod_ref, o_ref):
    moe = lax.dot_general(d_ref[...], y_ref[...], (((0,), (0,)), ((), ())), preferred_element_type=F32)
    o_ref[...] = x_ref[...] + mod_ref[...] * moe


def _combine(dmat, y3d, x3d, mod_g2, tn=D_MODEL):
    B, n_tok, _ = x3d.shape
    rows = dmat.shape[1]
    n_req = mod_g2.shape[0]
    req = (lambda b: b) if n_req > 1 else (lambda b: 0)
    return pl.pallas_call(
        _combine_kernel,
        grid=(B, D_MODEL // tn),
        in_specs=[
            pl.BlockSpec((None, rows, n_tok), lambda b, j: (b, 0, 0)),
            pl.BlockSpec((None, rows, tn), lambda b, j: (b, 0, j)),
            pl.BlockSpec((None, n_tok, tn), lambda b, j: (b, 0, j)),
            pl.BlockSpec((None, 1, tn), lambda b, j: (req(b), 0, j)),
        ],
        out_specs=pl.BlockSpec((None, n_tok, tn), lambda b, j: (b, 0, j)),
        out_shape=jax.ShapeDtypeStruct(x3d.shape, F32),
        compiler_params=_params("parallel", "arbitrary"),
        name="moe_combine",
    )(dmat, y3d, x3d, mod_g2)


def _token_mixers(x3d, mod, p, tabs, layer, ctx=None):
    B, L, _ = x3d.shape
    x2d = x3d.reshape(B * L, D_MODEL)
    main, qkv = _inproj(x2d, mod, p['norm1_w'], p['w_in'], layer, L)
    fwd_tab, inv_tab = tabs
    spectra = _hyena_spectra(L, p['hy_w1'], p['hy_b1'], p['hy_w2'], p['hy_b2'], p['hy_w3'], p['hy_b3'],
                             p['hy_decay'], fwd_tab)
    ya, yc = _seqmix(main.reshape(B, L, MAIN_COLS), p['hy_short_w'], p['hy_short_b'], p['sc_w'], p['sc_b'],
                     spectra, p['hy_bias'], fwd_tab, inv_tab)
    qkv3d = qkv.reshape(B, L, ATT_COLS)
    if ctx is None:
        yb, k, v = _attention(qkv3d, p['q_norm_w'], p['k_norm_w'])
    else:
        yb = _attention(qkv3d, p['q_norm_w'], p['k_norm_w'], ctx[0], ctx[1], layer)
        k = v = None
    x_mid, h2 = _merge(ya.reshape(B * L, D_HYENA), yb.reshape(B * L, ATT_Q), yc.reshape(B * L, D_SCONV),
                       main, x2d, mod, p['norm2_w'], p['w_br_a'], p['w_br_b'], p['w_br_c'], p['w_o'], layer, L)
    return x_mid.reshape(B, L, D_MODEL), h2.reshape(B, L, D_MODEL), k, v


def _moe_split(a, n_exp):
    B, rows, w = a.shape
    return a.reshape(B, n_exp, rows // n_exp, w)


def kernel(x_prompt, x_sample, cache_k, cache_v, c, c_ctx, mod_w, mod_b, norm1_w, norm2_w, w_in, hy_short_w, hy_short_b, hy_w1, hy_b1, hy_w2, hy_b2, hy_w3, hy_b3, hy_decay, hy_bias, q_norm_w, k_norm_w, sc_w, sc_b, w_br_a, w_br_b, w_br_c, w_o, router_w, exp_w_gate, exp_w_up, exp_w_down):
    n_dec = x_sample.shape[0]
    n_ctx = cache_k.shape[2]
    lp = x_prompt.shape[1]
    ls = x_sample.shape[1]

    cond_rows = 16
    cond = jnp.concatenate([c_ctx[None, :], c, jnp.zeros((cond_rows - 1 - n_dec, D_MODEL), F32)], axis=0)
    mod = _modulation(cond, mod_w, mod_b).reshape(DEPTH, cond_rows, 6, D_MODEL)

    tabs_p = tuple(jnp.asarray(t).astype(BF16) for t in _dft_tables(lp))
    tabs_s = tuple(jnp.asarray(t).astype(BF16) for t in _dft_tables(ls))
    ctx_k = cache_k.reshape(n_dec, DEPTH, n_ctx, ATT_KV)
    ctx_v = cache_v.reshape(n_dec, DEPTH, n_ctx, ATT_KV)
    dense = {'w_in': w_in.astype(BF16), 'w_br_a': w_br_a.astype(BF16), 'w_br_b': w_br_b.astype(BF16),
             'w_br_c': w_br_c.astype(BF16), 'w_o': w_o.astype(BF16)}

    xp, xs = x_prompt, x_sample
    ks_new, vs_new = [], []
    for l in range(DEPTH):
        p = dict(dense)
        p.update({
            'norm1_w': norm1_w[l].reshape(1, D_MODEL), 'norm2_w': norm2_w[l].reshape(1, D_MODEL),
            'hy_short_w': hy_short_w[l], 'hy_short_b': hy_short_b[l],
            'hy_w1': hy_w1[l], 'hy_b1': hy_b1[l], 'hy_w2': hy_w2[l], 'hy_b2': hy_b2[l],
            'hy_w3': hy_w3[l], 'hy_b3': hy_b3[l], 'hy_decay': hy_decay[l], 'hy_bias': hy_bias[l],
            'q_norm_w': q_norm_w[l], 'k_norm_w': k_norm_w[l], 'sc_w': sc_w[l], 'sc_b': sc_b[l],
        })
        mod_p = mod[l, 0:1]
        mod_s = mod[l, 1:1 + n_dec]
        xp_mid, h2p, k_l, v_l = _token_mixers(xp, mod_p, p, tabs_p, l)
        xs_mid, h2s, _, _ = _token_mixers(xs, mod_s, p, tabs_s, l, (ctx_k, ctx_v))
        ks_new.append(k_l.reshape(k_l.shape[0], lp, N_KV_HEADS, HEAD_DIM))
        vs_new.append(v_l.reshape(v_l.shape[0], lp, N_KV_HEADS, HEAD_DIM))

        router_wt = router_w[l].T.astype(BF16)
        gp_x, gp_d, gp_g = _dispatch(h2p, router_wt)
        gs_x, gs_d, gs_g = _dispatch(h2s, router_wt)
        yp, ys = _expert_ffn(_moe_split(gp_x, N_EXPERTS), _moe_split(gp_g, N_EXPERTS),
                             _moe_split(gs_x, N_EXPERTS), _moe_split(gs_g, N_EXPERTS),
                             exp_w_gate, exp_w_up, exp_w_down, l)
        xp = _combine(gp_d, yp.reshape(gp_x.shape), xp_mid, mod_p[:, 5:6])
        xs = _combine(gs_d, ys.reshape(gs_x.shape), xs_mid, mod_s[:, 5:6])

    return (xp, xs, jnp.stack(ks_new, axis=1), jnp.stack(vs_new, axis=1))
```

```python
import functools
import math

import numpy as np
import jax
import jax.numpy as jnp
from jax import lax
from jax.experimental import pallas as pl
from jax.experimental.pallas import tpu as pltpu

D_MODEL = 1024
DEPTH = 2
GRID_W = 64
D_HYENA = 512
HYENA_ORDER = 2
HYENA_POS_BANDS = 16
HYENA_POS_DIM = 1 + 2 * HYENA_POS_BANDS
HYENA_FILTER_HIDDEN = 64
HYENA_WINDOW_SHIFT = 0.05
N_HEADS = 8
N_KV_HEADS = 2
HEAD_DIM = 64
HEADS_PER_KV = N_HEADS // N_KV_HEADS
ATT_Q = N_HEADS * HEAD_DIM
ATT_KV = N_KV_HEADS * HEAD_DIM
ROPE_THETA = 10000.0
D_SCONV = 512
N_EXPERTS = 16
EXPERT_FF = 512
CAPACITY_FACTOR = 2
NORM_EPS = 1e-6
HY_COLS = (HYENA_ORDER + 1) * D_HYENA
ATT_COLS = ATT_Q + 2 * ATT_KV
SC_COLS = 3 * D_SCONV
GATE_COLS = 3 * D_MODEL
MAIN_COLS = HY_COLS + SC_COLS + GATE_COLS
D_IN = MAIN_COLS + ATT_COLS

F32 = jnp.float32
BF16 = jnp.bfloat16

V7X_VMEM_BYTES = 64 * 1024 * 1024
VMEM_LIMIT = V7X_VMEM_BYTES - 8 * 1024 * 1024
LANES = 128
MLP_PAD = 128


def _params(*sem):
    return pltpu.CompilerParams(dimension_semantics=sem, vmem_limit_bytes=VMEM_LIMIT)


def _const_spec(shape):
    nd = len(shape)
    return pl.BlockSpec(shape, lambda *_: (0,) * nd, pipeline_mode=pl.Buffered(1))


def _layer_spec(shape, layer):
    nd = len(shape)
    return pl.BlockSpec((None,) + tuple(shape), lambda *_: (layer,) + (0,) * nd, pipeline_mode=pl.Buffered(1))


def _dot(a, b):
    return jnp.dot(a, b, preferred_element_type=F32)


def _silu(x):
    return x * (1.0 / (1.0 + jnp.exp(-x)))


def _sigmoid(x):
    return 1.0 / (1.0 + jnp.exp(-x))


@functools.lru_cache(maxsize=None)
def _dft_tables(L):
    n = 2 * L
    k = np.arange(L, dtype=np.int64)
    idx = (k[:, None] * k[None, :]) % n
    ang = idx.astype(np.float64) * (2.0 * np.pi / n)
    c = np.cos(ang)
    s = np.sin(ang)
    s[0, :] = 1.0 - 2.0 * (k % 2)
    fwd = np.concatenate([c, s], axis=0).astype(np.float32)
    inv = np.concatenate([c, s.T], axis=1).astype(np.float32)
    return fwd, inv


@functools.lru_cache(maxsize=None)
def _pos_features(L):
    n = np.arange(L, dtype=np.float64)
    t = n / max(L - 1, 1)
    bands = np.linspace(1e-4, HYENA_POS_BANDS - 1, HYENA_POS_BANDS)
    ang = 2.0 * math.pi * n[:, None] * bands[None, :] / L
    z = np.concatenate([t[:, None], np.cos(ang), np.sin(ang)], axis=-1)
    zp = np.zeros((L, MLP_PAD), np.float32)
    zp[:, :HYENA_POS_DIM] = z
    return zp


@functools.lru_cache(maxsize=None)
def _rope_tables(L):
    rows = L // GRID_W
    row = np.repeat(np.arange(rows, dtype=np.float64), GRID_W)
    col = np.tile(np.arange(GRID_W, dtype=np.float64), rows)
    axis_dim = HEAD_DIM // 2
    inv_freq = ROPE_THETA ** (-np.arange(0, axis_dim, 2, dtype=np.float64) / axis_dim)
    ar = row[:, None] * inv_freq[None, :]
    ac = col[:, None] * inv_freq[None, :]
    ang = np.concatenate([ar, ar, ac, ac], axis=-1)
    cos = np.cos(ang).astype(np.float32)
    sin = np.sin(ang).astype(np.float32)
    return np.tile(cos, (1, N_HEADS)), np.tile(sin, (1, N_HEADS))


@functools.lru_cache(maxsize=None)
def _head_tables():
    lane = np.arange(ATT_Q)
    block_ones = (lane[:, None] // HEAD_DIM == lane[None, :] // HEAD_DIM).astype(np.float32)
    src = np.arange(ATT_KV)
    dst = np.arange(HEADS_PER_KV * HEAD_DIM)
    rep = np.stack([(src[:, None] == g * HEAD_DIM + dst[None, :] % HEAD_DIM) for g in range(N_KV_HEADS)])
    return block_ones, rep.astype(np.float32)


@functools.lru_cache(maxsize=None)
def _prefix_table(n):
    i = np.arange(n)
    return (i[:, None] < i[None, :]).astype(np.float32)


def _mod_kernel(cond_ref, w_ref, b_ref, o_ref):
    a = _silu(cond_ref[...]).astype(BF16)
    o_ref[...] = _dot(a, w_ref[...].astype(BF16)) + b_ref[...]


def _modulation(cond, mod_w, mod_b):
    rows = cond.shape[0]
    tn = 1024
    ncols = mod_w.shape[-1]
    return pl.pallas_call(
        _mod_kernel,
        grid=(DEPTH, ncols // tn),
        in_specs=[
            pl.BlockSpec((rows, D_MODEL), lambda l, j: (0, 0)),
            pl.BlockSpec((None, D_MODEL, tn), lambda l, j: (l, 0, j)),
            pl.BlockSpec((None, 1, tn), lambda l, j: (l, 0, j)),
        ],
        out_specs=pl.BlockSpec((None, rows, tn), lambda l, j: (l, 0, j)),
        out_shape=jax.ShapeDtypeStruct((DEPTH, rows, ncols), F32),
        compiler_params=_params("parallel", "parallel"),
        name="modulation",
    )(cond, mod_w, mod_b.reshape(DEPTH, 1, ncols))


def _rms_modulate(x, norm_w, shift, scale):
    y = x * lax.rsqrt(jnp.mean(x * x, axis=-1, keepdims=True) + NORM_EPS)
    return (y * norm_w) * (1.0 + scale) + shift


def _inproj_kernel(x_ref, mod_ref, nw_ref, w_ref, main_ref, qkv_ref, h_sc):
    h = _rms_modulate(x_ref[...], nw_ref[...], mod_ref[0:1, :], mod_ref[1:2, :])
    h_sc[...] = h.astype(BF16)
    for lo in range(0, D_IN, ATT_COLS):
        y = _dot(h_sc[...], w_ref[:, lo:lo + ATT_COLS])
        if lo < HY_COLS:
            main_ref[:, lo:lo + ATT_COLS] = y.astype(BF16)
        elif lo == HY_COLS:
            qkv_ref[...] = y
        else:
            main_ref[:, lo - ATT_COLS:lo] = y.astype(BF16)


def _inproj(x2d, mod, norm_w, w_in_bf, layer, rows_per_req, tm=512):
    T = x2d.shape[0]
    n_req = mod.shape[0]
    req = (lambda i: (i * tm) // rows_per_req) if n_req > 1 else (lambda i: 0)
    return pl.pallas_call(
        _inproj_kernel,
        grid=(T // tm,),
        in_specs=[
            pl.BlockSpec((tm, D_MODEL), lambda i: (i, 0)),
            pl.BlockSpec((None, 6, D_MODEL), lambda i: (req(i), 0, 0)),
            _const_spec((1, D_MODEL)),
            _layer_spec((D_MODEL, D_IN), layer),
        ],
        out_specs=[pl.BlockSpec((tm, MAIN_COLS), lambda i: (i, 0)),
                   pl.BlockSpec((tm, ATT_COLS), lambda i: (i, 0))],
        out_shape=[jax.ShapeDtypeStruct((T, MAIN_COLS), BF16), jax.ShapeDtypeStruct((T, ATT_COLS), F32)],
        scratch_shapes=[pltpu.VMEM((tm, D_MODEL), BF16)],
        compiler_params=_params("parallel"),
        name="inproj",
    )(x2d, mod, norm_w, w_in_bf)


def _filter_kernel(z_ref, w1_ref, b1_ref, w2_ref, b2_ref, w3f_ref, b3f_ref, w3b_ref, b3b_ref,
                   decf_ref, decb_ref, fwd_ref, p_ref, h_sc, *, L):
    hi = lax.Precision.HIGHEST

    @pl.when((pl.program_id(0) == 0) & (pl.program_id(1) == 0))
    def _():
        h1 = jnp.sin(jnp.dot(z_ref[...], w1_ref[...], precision=hi, preferred_element_type=F32) + b1_ref[...])
        h_sc[...] = jnp.sin(jnp.dot(h1, w2_ref[...], precision=hi, preferred_element_type=F32) + b2_ref[...])

    h = h_sc[...]
    tc = w3f_ref.shape[1]
    pos = lax.broadcasted_iota(jnp.int32, (L, tc), 0)
    t = pos.astype(F32) / float(max(L - 1, 1))

    def taps(w3_ref, b3_ref, dec_ref):
        g = jnp.dot(h, w3_ref[...], precision=hi, preferred_element_type=F32) + b3_ref[...]
        return g * (jnp.exp(-t * jnp.abs(dec_ref[...])) + HYENA_WINDOW_SHIFT)

    hf = taps(w3f_ref, b3f_ref, decf_ref)
    hb = jnp.where(pos == 0, 0.0, taps(w3b_ref, b3b_ref, decb_ref))
    fwd = fwd_ref[...]
    tf = _dot(fwd, hf.astype(BF16))
    tb = _dot(fwd, hb.astype(BF16))
    k_re = tf[:L] + tb[:L]
    k_im = tb[L:] - tf[L:]
    k_ny = tf[L:] + tb[L:]
    first = pos == 0
    inv_n = 1.0 / (2 * L)
    p_ref[0] = jnp.where(first, k_re * inv_n, k_re * (2.0 * inv_n))
    p_ref[1] = jnp.where(first, 0.0, k_im * (2.0 * inv_n))
    p_ref[2] = jnp.where(first, k_ny * inv_n, k_re * (2.0 * inv_n))


def _hyena_spectra(L, w1, b1, w2, b2, w3, b3, decay, fwd_tab):
    pad_h = MLP_PAD - HYENA_FILTER_HIDDEN
    z = jnp.asarray(_pos_features(L))
    w1p = jnp.pad(w1, ((0, MLP_PAD - HYENA_POS_DIM), (0, pad_h)))
    b1p = jnp.pad(b1, (0, pad_h)).reshape(1, MLP_PAD)
    w2p = jnp.pad(w2, ((0, pad_h), (0, pad_h)))
    b2p = jnp.pad(b2, (0, pad_h)).reshape(1, MLP_PAD)
    w3p = jnp.pad(w3, ((0, pad_h), (0, 0)))
    ncol = w3.shape[1]
    b3r = b3.reshape(1, ncol)
    decr = decay.reshape(1, ncol)
    tc = 256
    nct = D_HYENA // tc
    per_dir = HYENA_ORDER * nct
    col_f = lambda o, c: (0, o * nct + c)
    col_b = lambda o, c: (0, per_dir + o * nct + c)
    return pl.pallas_call(
        functools.partial(_filter_kernel, L=L),
        grid=(HYENA_ORDER, nct),
        in_specs=[
            _const_spec((L, MLP_PAD)),
            _const_spec((MLP_PAD, MLP_PAD)), _const_spec((1, MLP_PAD)),
            _const_spec((MLP_PAD, MLP_PAD)), _const_spec((1, MLP_PAD)),
            pl.BlockSpec((MLP_PAD, tc), col_f), pl.BlockSpec((1, tc), col_f),
            pl.BlockSpec((MLP_PAD, tc), col_b), pl.BlockSpec((1, tc), col_b),
            pl.BlockSpec((1, tc), col_f), pl.BlockSpec((1, tc), col_b),
            _const_spec((2 * L, L)),
        ],
        out_specs=pl.BlockSpec((None, 3, L, tc), lambda o, c: (o, 0, 0, c)),
        out_shape=jax.ShapeDtypeStruct((HYENA_ORDER, 3, L, D_HYENA), F32),
        scratch_shapes=[pltpu.VMEM((L, MLP_PAD), F32)],
        compiler_params=_params("arbitrary", "arbitrary"),
        name="hyena_spectra",
    )(z, w1p, b1p, w2p, b2p, w3p, b3r, w3p, b3r, decr, decr, fwd_tab)


def _conv3(x, w_ref, b_ref, first, last):
    L = x.shape[0]
    prev = jnp.where(first, 0.0, pltpu.roll(x, 1, 0))
    nxt = jnp.where(last, 0.0, pltpu.roll(x, L - 1, 0))
    return prev * w_ref[0:1, :] + x * w_ref[1:2, :] + nxt * w_ref[2:3, :] + b_ref[...]


def _seqmix_kernel(v_ref, x1_ref, x2_ref, u_ref, bg_ref, cg_ref,
                   wv_ref, wx1_ref, wx2_ref, bv_ref, bx1_ref, bx2_ref, scw_ref, scb_ref,
                   p_ref, hb_ref, fwd_ref, inv_ref, ya_ref, yc_ref, spec_sc, *, L):
    n_req, _, tc = v_ref.shape
    pos = lax.broadcasted_iota(jnp.int32, (L, tc), 0)
    first = pos == 0
    last = pos == L - 1
    mult_refs = ((x1_ref, wx1_ref, bx1_ref), (x2_ref, wx2_ref, bx2_ref))

    def chain(r):
        z = _conv3(v_ref[r].astype(F32), wv_ref, bv_ref, first, last)
        yield
        for o in range(HYENA_ORDER):
            ab = _dot(fwd_ref[...], z.astype(BF16))
            yield
            a = ab[:L]
            b = ab[L:]
            p2 = p_ref[o, 1]
            spec_sc[r, 0:L, :] = (a * p_ref[o, 0] + b * p2).astype(BF16)
            spec_sc[r, L:2 * L, :] = (b * p_ref[o, 2] - a * p2).astype(BF16)
            x_ref, w_ref, b_ref = mult_refs[o]
            mult = _conv3(x_ref[r].astype(F32), w_ref, b_ref, first, last)
            yield
            y = _dot(inv_ref[...], spec_sc[r])
            yield
            z = mult * (y + hb_ref[o:o + 1, :] * z)
            if o == 0:
                gated = cg_ref[r].astype(F32) * u_ref[r].astype(F32)
                yc = bg_ref[r].astype(F32) * _conv3(gated, scw_ref, scb_ref, first, last)
                yc_ref[r] = yc.astype(yc_ref.dtype)
            if o == HYENA_ORDER - 1:
                ya_ref[r] = z.astype(ya_ref.dtype)
            yield

    chains = [chain(r) for r in range(n_req)]
    n_stages = 1 + 4 * HYENA_ORDER
    for tick in range(n_stages + n_req - 1):
        for r, c in enumerate(chains):
            if 0 <= tick - r < n_stages:
                next(c)


def _seqmix(main3d, hy_short_w, hy_short_b, sc_w, sc_b, spectra, hy_bias, fwd_tab, inv_tab):
    B, L, _ = main3d.shape
    tc = 256
    nct = D_HYENA // tc
    nr = 2
    act = lambda off: pl.BlockSpec((nr, L, tc), lambda c, b, off=off: (b, 0, off * nct + c))
    par3 = lambda off: pl.BlockSpec((3, tc), lambda c, b, off=off: (0, off * nct + c))
    par1 = lambda off: pl.BlockSpec((1, tc), lambda c, b, off=off: (0, off * nct + c))
    out = pl.BlockSpec((nr, L, tc), lambda c, b: (b, 0, c))
    hsb = hy_short_b.reshape(1, HY_COLS)
    return pl.pallas_call(
        functools.partial(_seqmix_kernel, L=L),
        grid=(nct, B // nr),
        in_specs=[
            act(0), act(1), act(2), act(3), act(4), act(5),
            par3(0), par3(1), par3(2), par1(0), par1(1), par1(2),
            par3(0), par1(0),
            pl.BlockSpec((HYENA_ORDER, 3, L, tc), lambda c, b: (0, 0, 0, c), pipeline_mode=pl.Buffered(1)),
            pl.BlockSpec((HYENA_ORDER, tc), lambda c, b: (0, c)),
            _const_spec((2 * L, L)), _const_spec((L, 2 * L)),
        ],
        out_specs=[out, out],
        out_shape=[jax.ShapeDtypeStruct((B, L, D_HYENA), BF16), jax.ShapeDtypeStruct((B, L, D_SCONV), BF16)],
        scratch_shapes=[pltpu.VMEM((nr, 2 * L, tc), BF16)],
        compiler_params=_params("arbitrary", "arbitrary"),
        name="seqmix",
    )(main3d, main3d, main3d, main3d, main3d, main3d,
      hy_short_w, hy_short_w, hy_short_w, hsb, hsb, hsb, sc_w, sc_b.reshape(1, D_SCONV),
      spectra, hy_bias, fwd_tab, inv_tab)


def _head_rms(x, ones_ref, w):
    sq = x * x
    hi = sq.astype(BF16)
    lo = (sq - hi.astype(F32)).astype(BF16)
    ones = ones_ref[...]
    width = x.shape[1]
    ss = _dot(hi, ones[:width, :width]) + _dot(lo, ones[:width, :width])
    return x * lax.rsqrt(ss * (1.0 / HEAD_DIM) + NORM_EPS) * w


def _rope(x, cos, sin):
    width = x.shape[1]
    lane = lax.broadcasted_iota(jnp.int32, x.shape, 1)
    half = HEAD_DIM // 4
    low = (lane % (2 * half)) < half
    rot = jnp.where(low, -pltpu.roll(x, width - half, 1), pltpu.roll(x, half, 1))
    return x * cos + rot * sin


def _attn_kernel(*refs, L, n_ctx, tq, rope):
    if rope:
        (q_ref, k_ref, v_ref, ck_ref, cv_ref, qw_ref, kw_ref, ones_ref, rep_ref, rept_ref,
         cosq_ref, sinq_ref, cosk_ref, sink_ref, yb_ref, k4_sc, v4_sc) = refs
    else:
        (q_ref, k_ref, v_ref, qw_ref, kw_ref, ones_ref, rep_ref, rept_ref,
         yb_ref, ko_ref, vo_ref, k4_sc, v4_sc) = refs

    @pl.when(pl.program_id(1) == 0)
    def _():
        kn = _head_rms(k_ref[...], ones_ref, kw_ref[...])
        v = v_ref[...]
        if rope:
            keys = _rope(kn, cosk_ref[...], sink_ref[...])
            keys = jnp.concatenate([ck_ref[...], keys], axis=0)
            vals = jnp.concatenate([cv_ref[...], v], axis=0)
        else:
            ko_ref[...] = kn
            vo_ref[...] = v
            keys, vals = kn, v
        keys = keys.astype(BF16)
        vals = vals.astype(BF16)
        for g in range(N_KV_HEADS):
            k4_sc[g] = lax.dot_general(rept_ref[g], keys, (((1,), (1,)), ((), ())),
                                       preferred_element_type=F32).astype(BF16)
            v4_sc[g] = _dot(vals, rep_ref[g]).astype(BF16)

    q = _head_rms(q_ref[...], ones_ref, qw_ref[...])
    if rope:
        q = _rope(q, cosq_ref[...], sinq_ref[...])
    q = q * (HEAD_DIM ** -0.5)
    gw = HEADS_PER_KV * HEAD_DIM
    lane = lax.broadcasted_iota(jnp.int32, (tq, gw), 1)
    for g in range(N_KV_HEADS):
        qg = q[:, g * gw:(g + 1) * gw]
        acc = jnp.zeros((tq, gw), F32)
        for h in range(HEADS_PER_KV):
            mine = (lane // HEAD_DIM) == h
            qm = jnp.where(mine, qg, 0.0).astype(BF16)
            s = _dot(qm, k4_sc[g])
            p = jnp.exp(s - jnp.max(s, axis=-1, keepdims=True))
            denom = jnp.sum(p, axis=-1, keepdims=True)
            o4 = _dot(p.astype(BF16), v4_sc[g])
            acc = jnp.where(mine, o4 * (1.0 / denom), acc)
        yb_ref[:, g * gw:(g + 1) * gw] = acc.astype(yb_ref.dtype)


def _attention(qkv3d, q_norm_w, k_norm_w, ctx_k=None, ctx_v=None, layer=0):
    B, L, _ = qkv3d.shape
    rope = ctx_k is not None
    tq = min(1024, L)
    n_ctx = ctx_k.shape[2] if rope else 0
    lk = L + n_ctx
    ones_np, rep_np = _head_tables()
    ones = jnp.asarray(ones_np).astype(BF16)
    rep = jnp.asarray(rep_np).astype(BF16)
    rept = jnp.asarray(np.swapaxes(rep_np, 1, 2)).astype(BF16)
    qw = jnp.tile(q_norm_w, N_HEADS).reshape(1, ATT_Q)
    kw = jnp.tile(k_norm_w, N_KV_HEADS).reshape(1, ATT_KV)
    kblk = ATT_Q // ATT_KV
    in_specs = [
        pl.BlockSpec((None, tq, ATT_Q), lambda b, i: (b, i, 0)),
        pl.BlockSpec((None, L, ATT_KV), lambda b, i: (b, 0, kblk)),
        pl.BlockSpec((None, L, ATT_KV), lambda b, i: (b, 0, kblk + 1)),
    ]
    args = [qkv3d, qkv3d, qkv3d]
    if rope:
        in_specs += [pl.BlockSpec((None, None, n_ctx, ATT_KV), lambda b, i: (b, layer, 0, 0))] * 2
        args += [ctx_k, ctx_v]
    in_specs += [_const_spec((1, ATT_Q)), _const_spec((1, ATT_KV)), _const_spec((ATT_Q, ATT_Q)),
                 _const_spec((N_KV_HEADS, ATT_KV, HEADS_PER_KV * HEAD_DIM)),
                 _const_spec((N_KV_HEADS, HEADS_PER_KV * HEAD_DIM, ATT_KV))]
    args += [qw, kw, ones, rep, rept]
    yb_shape = jax.ShapeDtypeStruct((B, L, ATT_Q), BF16)
    yb_spec = pl.BlockSpec((None, tq, ATT_Q), lambda b, i: (b, i, 0))
    if rope:
        cos_np, sin_np = _rope_tables(L)
        cos = jnp.asarray(cos_np)
        sin = jnp.asarray(sin_np)
        in_specs += [pl.BlockSpec((tq, ATT_Q), lambda b, i: (i, 0))] * 2
        in_specs += [_const_spec((L, ATT_KV))] * 2
        args += [cos, sin, cos[:, :ATT_KV], sin[:, :ATT_KV]]
        out_specs = yb_spec
        out_shape = yb_shape
    else:
        kv_spec = pl.BlockSpec((None, L, ATT_KV), lambda b, i: (b, 0, 0))
        kv_shape = jax.ShapeDtypeStruct((B, L, ATT_KV), F32)
        out_specs = [yb_spec, kv_spec, kv_spec]
        out_shape = [yb_shape, kv_shape, kv_shape]
    return pl.pallas_call(
        functools.partial(_attn_kernel, L=L, n_ctx=n_ctx, tq=tq, rope=rope),
        grid=(B, L // tq),
        in_specs=in_specs,
        out_specs=out_specs,
        out_shape=out_shape,
        scratch_shapes=[pltpu.VMEM((N_KV_HEADS, HEADS_PER_KV * HEAD_DIM, lk), BF16),
                        pltpu.VMEM((N_KV_HEADS, lk, HEADS_PER_KV * HEAD_DIM), BF16)],
        compiler_params=_params("parallel", "arbitrary"),
        name="attention",
    )(*args)


def _merge_kernel(ya_ref, yb_ref, yc_ref, ga_ref, gb_ref, gc_ref, x_ref, mod_ref, nw_ref,
                  wa_ref, wb_ref, wc_ref, wo_ref, xo_ref, h2_ref):
    tm = x_ref.shape[0]
    n_chain = 2
    rows_per = tm // n_chain

    def chain(c):
        rows = pl.ds(c * rows_per, rows_per)
        da = _dot(ya_ref[rows, :], wa_ref[...])
        db = _dot(yb_ref[rows, :], wb_ref[...])
        dc = _dot(yc_ref[rows, :], wc_ref[...])
        yield
        merged = (_sigmoid(ga_ref[rows, :].astype(F32)) * da + _sigmoid(gb_ref[rows, :].astype(F32)) * db
                  + _sigmoid(gc_ref[rows, :].astype(F32)) * dc).astype(BF16)
        yield
        proj = _dot(merged, wo_ref[...])
        yield
        x = x_ref[rows, :] + mod_ref[2:3, :] * proj
        xo_ref[rows, :] = x
        h2_ref[rows, :] = _rms_modulate(x, nw_ref[...], mod_ref[3:4, :], mod_ref[4:5, :]).astype(BF16)
        yield

    chains = [chain(c) for c in range(n_chain)]
    n_stages = 4
    for tick in range(n_stages + n_chain - 1):
        for c, ch in enumerate(chains):
            if 0 <= tick - c < n_stages:
                next(ch)


def _merge(ya, yb, yc, main2d, x2d, mod, norm_w, wa, wb, wc, wo, layer, rows_per_req, tm=512):
    T = x2d.shape[0]
    n_req = mod.shape[0]
    req = (lambda i: (i * tm) // rows_per_req) if n_req > 1 else (lambda i: 0)
    gate0 = (HY_COLS + SC_COLS) // D_MODEL
    br = pl.BlockSpec((tm, D_HYENA), lambda i: (i, 0))
    gate = lambda k: pl.BlockSpec((tm, D_MODEL), lambda i, k=k: (i, gate0 + k))
    row = pl.BlockSpec((tm, D_MODEL), lambda i: (i, 0))
    return pl.pallas_call(
        _merge_kernel,
        grid=(T // tm,),
        in_specs=[
            br, br, br, gate(0), gate(1), gate(2), row,
            pl.BlockSpec((None, 6, D_MODEL), lambda i: (req(i), 0, 0)),
            _const_spec((1, D_MODEL)),
            _layer_spec((D_HYENA, D_MODEL), layer), _layer_spec((ATT_Q, D_MODEL), layer),
            _layer_spec((D_SCONV, D_MODEL), layer), _layer_spec((D_MODEL, D_MODEL), layer),
        ],
        out_specs=[row, row],
        out_shape=[jax.ShapeDtypeStruct((T, D_MODEL), F32), jax.ShapeDtypeStruct((T, D_MODEL), BF16)],
        compiler_params=_params("parallel"),
        name="merge",
    )(ya, yb, yc, main2d, main2d, main2d, x2d, mod, norm_w, wa, wb, wc, wo)


def _route_kernel(h_ref, rw_ref, tri_ref, aff_ref, slot_ref, *, cap):
    b = pl.program_id(0)
    logits = lax.dot_general(rw_ref[...], h_ref[...], (((1,), (1,)), ((), ())), preferred_element_type=F32)
    ex = jnp.exp(logits - jnp.max(logits, axis=0, keepdims=True))
    aff_ref[pl.ds(pl.multiple_of(b * N_EXPERTS, N_EXPERTS), N_EXPERTS), :] = ex / jnp.sum(ex, axis=0, keepdims=True)

    @pl.when(b == pl.num_programs(0) - 1)
    def _():
        aff = aff_ref[...]

        def count(mask):
            return jnp.sum(jnp.where(mask, 1.0, 0.0), axis=1, keepdims=True)

        kth = jnp.zeros((aff.shape[0], 1), jnp.int32)
        for bit in range(30, -1, -1):
            trial = kth | (1 << bit)
            enough = count(aff >= lax.bitcast_convert_type(trial, F32)) >= cap
            kth = jnp.where(enough, trial, kth)
        next_up = lax.bitcast_convert_type(kth + 1, F32)
        above = aff >= next_up
        tied = (aff >= lax.bitcast_convert_type(kth, F32)) & (aff < next_up)
        tri = tri_ref[...]
        tied_before = _dot(jnp.where(tied, 1.0, 0.0).astype(BF16), tri)
        chosen = above | (tied & (tied_before < (cap - count(above))))
        slot = _dot(jnp.where(chosen, 1.0, 0.0).astype(BF16), tri)
        slot_ref[...] = jnp.where(chosen, slot, -1.0)


def _route(h3d, router_wt):
    B, n_tok, _ = h3d.shape
    cap = CAPACITY_FACTOR * n_tok // N_EXPERTS
    tri = jnp.asarray(_prefix_table(n_tok)).astype(BF16)
    whole = pl.BlockSpec((B * N_EXPERTS, n_tok), lambda b: (0, 0))
    shape = jax.ShapeDtypeStruct((B * N_EXPERTS, n_tok), F32)
    return pl.pallas_call(
        functools.partial(_route_kernel, cap=cap),
        grid=(B,),
        in_specs=[
            pl.BlockSpec((None, n_tok, D_MODEL), lambda b: (b, 0, 0)),
            _const_spec((N_EXPERTS, D_MODEL)),
            _const_spec((n_tok, n_tok)),
        ],
        out_specs=[whole, whole],
        out_shape=[shape, shape],
        compiler_params=_params("arbitrary"),
        name="moe_route",
    )(h3d, router_wt, tri)


def _dispatch_kernel(h_ref, aff_ref, slot_ref, xs_ref, d_ref, g_ref, *, n_tok, cap):
    row = lax.broadcasted_iota(jnp.int32, (cap, n_tok), 0).astype(F32)

    def one_expert(e, carry):
        hit = row == slot_ref[pl.ds(e, 1), :]
        base = pl.multiple_of(e * cap, cap)
        d_ref[pl.ds(base, cap), :] = jnp.where(hit, 1.0, 0.0).astype(BF16)
        gate = jnp.sum(jnp.where(hit, aff_ref[pl.ds(e, 1), :], 0.0), axis=1, keepdims=True)
        g_ref[pl.ds(base, cap), :] = jnp.broadcast_to(gate, (cap, LANES))
        return carry

    lax.fori_loop(0, N_EXPERTS, one_expert, 0, unroll=4)
    xs_ref[...] = _dot(d_ref[...], h_ref[...]).astype(BF16)


def _dispatch(h3d, router_wt):
    B, n_tok, _ = h3d.shape
    cap = CAPACITY_FACTOR * n_tok // N_EXPERTS
    rows = N_EXPERTS * cap
    aff, slot = _route(h3d, router_wt)
    per_req = pl.BlockSpec((N_EXPERTS, n_tok), lambda b: (b, 0))
    blk = lambda w: pl.BlockSpec((None, rows, w), lambda b: (b, 0, 0))
    return pl.pallas_call(
        functools.partial(_dispatch_kernel, n_tok=n_tok, cap=cap),
        grid=(B,),
        in_specs=[pl.BlockSpec((None, n_tok, D_MODEL), lambda b: (b, 0, 0)), per_req, per_req],
        out_specs=[blk(D_MODEL), blk(n_tok), blk(LANES)],
        out_shape=[jax.ShapeDtypeStruct((B, rows, D_MODEL), BF16),
                   jax.ShapeDtypeStruct((B, rows, n_tok), BF16),
                   jax.ShapeDtypeStruct((B, rows, LANES), F32)],
        compiler_params=_params("parallel"),
        name="moe_dispatch",
    )(h3d, aff, slot)


def _ffn_kernel(xp_ref, gp_ref, xs_ref, gs_ref, wg_ref, wu_ref, wd_ref, yp_ref, ys_ref):
    wg = wg_ref[...].astype(BF16)
    wu = wu_ref[...].astype(BF16)
    wd = wd_ref[...].astype(BF16)
    for x_ref, g_ref, y_ref in ((xp_ref, gp_ref, yp_ref), (xs_ref, gs_ref, ys_ref)):
        nb, cap, _ = x_ref.shape
        x = x_ref[...].reshape(nb * cap, D_MODEL)
        gate = g_ref[...].reshape(nb * cap, LANES)
        gate = jnp.concatenate([gate] * (EXPERT_FF // LANES), axis=1)
        act = _silu(_dot(x, wg)) * _dot(x, wu) * gate
        y = _dot(act.astype(BF16), wd)
        y_ref[...] = y.reshape(nb, cap, D_MODEL).astype(y_ref.dtype)


def _expert_ffn(xp, gp, xs, gs, w_gate, w_up, w_down, layer):
    def act_spec(a):
        nb, _, cap, w = a.shape
        return pl.BlockSpec((nb, None, cap, w), lambda e: (0, e, 0, 0))

    def w_spec(a):
        return pl.BlockSpec((None, None) + a.shape[2:], lambda e: (layer, e, 0, 0))

    return pl.pallas_call(
        _ffn_kernel,
        grid=(N_EXPERTS,),
        in_specs=[act_spec(xp), act_spec(gp), act_spec(xs), act_spec(gs),
                  w_spec(w_gate), w_spec(w_up), w_spec(w_down)],
        out_specs=[act_spec(xp), act_spec(xs)],
        out_shape=[jax.ShapeDtypeStruct(xp.shape, BF16), jax.ShapeDtypeStruct(xs.shape, BF16)],
        compiler_params=_params("parallel"),
        name="expert_ffn",
    )(xp, gp, xs, gs, w_gate, w_up, w_down)


def _combine_kernel(d_ref, y_ref, x_ref, mod_ref, o_ref):
    moe = lax.dot_general(d_ref[...], y_ref[...], (((0,), (0,)), ((), ())), preferred_element_type=F32)
    o_ref[...] = x_ref[...] + mod_ref[...] * moe


def _combine(dmat, y3d, x3d, mod_g2, tn=D_MODEL):
    B, n_tok, _ = x3d.shape
    rows = dmat.shape[1]
    n_req = mod_g2.shape[0]
    req = (lambda b: b) if n_req > 1 else (lambda b: 0)
    return pl.pallas_call(
        _combine_kernel,
        grid=(B, D_MODEL // tn),
        in_specs=[
            pl.BlockSpec((None, rows, n_tok), lambda b, j: (b, 0, 0)),
            pl.BlockSpec((None, rows, tn), lambda b, j: (b, 0, j)),
            pl.BlockSpec((None, n_tok, tn), lambda b, j: (b, 0, j)),
            pl.BlockSpec((None, 1, tn), lambda b, j: (req(b), 0, j)),
        ],
        out_specs=pl.BlockSpec((None, n_tok, tn), lambda b, j: (b, 0, j)),
        out_shape=jax.ShapeDtypeStruct(x3d.shape, F32),
        compiler_params=_params("parallel", "arbitrary"),
        name="moe_combine",
    )(dmat, y3d, x3d, mod_g2)


def _token_mixers(x3d, mod, p, tabs, layer, ctx=None):
    B, L, _ = x3d.shape
    x2d = x3d.reshape(B * L, D_MODEL)
    main, qkv = _inproj(x2d, mod, p['norm1_w'], p['w_in'], layer, L)
    fwd_tab, inv_tab = tabs
    spectra = _hyena_spectra(L, p['hy_w1'], p['hy_b1'], p['hy_w2'], p['hy_b2'], p['hy_w3'], p['hy_b3'],
                             p['hy_decay'], fwd_tab)
    ya, yc = _seqmix(main.reshape(B, L, MAIN_COLS), p['hy_short_w'], p['hy_short_b'], p['sc_w'], p['sc_b'],
                     spectra, p['hy_bias'], fwd_tab, inv_tab)
    qkv3d = qkv.reshape(B, L, ATT_COLS)
    if ctx is None:
        yb, k, v = _attention(qkv3d, p['q_norm_w'], p['k_norm_w'])
    else:
        yb = _attention(qkv3d, p['q_norm_w'], p['k_norm_w'], ctx[0], ctx[1], layer)
        k = v = None
    x_mid, h2 = _merge(ya.reshape(B * L, D_HYENA), yb.reshape(B * L, ATT_Q), yc.reshape(B * L, D_SCONV),
                       main, x2d, mod, p['norm2_w'], p['w_br_a'], p['w_br_b'], p['w_br_c'], p['w_o'], layer, L)
    return x_mid.reshape(B, L, D_MODEL), h2.reshape(B, L, D_MODEL), k, v


def _moe_split(a, n_exp):
    B, rows, w = a.shape
    return a.reshape(B, n_exp, rows // n_exp, w)


def kernel(x_prompt, x_sample, cache_k, cache_v, c, c_ctx, mod_w, mod_b, norm1_w, norm2_w, w_in, hy_short_w, hy_short_b, hy_w1, hy_b1, hy_w2, hy_b2, hy_w3, hy_b3, hy_decay, hy_bias, q_norm_w, k_norm_w, sc_w, sc_b, w_br_a, w_br_b, w_br_c, w_o, router_w, exp_w_gate, exp_w_up, exp_w_down):
    n_dec = x_sample.shape[0]
    n_ctx = cache_k.shape[2]
    lp = x_prompt.shape[1]
    ls = x_sample.shape[1]

    cond_rows = 16
    cond = jnp.concatenate([c_ctx[None, :], c, jnp.zeros((cond_rows - 1 - n_dec, D_MODEL), F32)], axis=0)
    mod = _modulation(cond, mod_w, mod_b).reshape(DEPTH, cond_rows, 6, D_MODEL)

    tabs_p = tuple(jnp.asarray(t).astype(BF16) for t in _dft_tables(lp))
    tabs_s = tuple(jnp.asarray(t).astype(BF16) for t in _dft_tables(ls))
    ctx_k = cache_k.reshape(n_dec, DEPTH, n_ctx, ATT_KV)
    ctx_v = cache_v.reshape(n_dec, DEPTH, n_ctx, ATT_KV)
    dense = {'w_in': w_in.astype(BF16), 'w_br_a': w_br_a.astype(BF16), 'w_br_b': w_br_b.astype(BF16),
             'w_br_c': w_br_c.astype(BF16), 'w_o': w_o.astype(BF16)}

    xp, xs = x_prompt, x_sample
    ks_new, vs_new = [], []
    for l in range(DEPTH):
        p = dict(dense)
        p.update({
            'norm1_w': norm1_w[l].reshape(1, D_MODEL), 'norm2_w': norm2_w[l].reshape(1, D_MODEL),
            'hy_short_w': hy_short_w[l], 'hy_short_b': hy_short_b[l],
            'hy_w1': hy_w1[l], 'hy_b1': hy_b1[l], 'hy_w2': hy_w2[l], 'hy_b2': hy_b2[l],
            'hy_w3': hy_w3[l], 'hy_b3': hy_b3[l], 'hy_decay': hy_decay[l], 'hy_bias': hy_bias[l],
            'q_norm_w': q_norm_w[l], 'k_norm_w': k_norm_w[l], 'sc_w': sc_w[l], 'sc_b': sc_b[l],
        })
        mod_p = mod[l, 0:1]
        mod_s = mod[l, 1:1 + n_dec]
        xp_mid, h2p, k_l, v_l = _token_mixers(xp, mod_p, p, tabs_p, l)
        xs_mid, h2s, _, _ = _token_mixers(xs, mod_s, p, tabs_s, l, (ctx_k, ctx_v))
        ks_new.append(k_l.reshape(k_l.shape[0], lp, N_KV_HEADS, HEAD_DIM))
        vs_new.append(v_l.reshape(v_l.shape[0], lp, N_KV_HEADS, HEAD_DIM))

        router_wt = router_w[l].T.astype(BF16)
        gp_x, gp_d, gp_g = _dispatch(h2p, router_wt)
        gs_x, gs_d, gs_g = _dispatch(h2s, router_wt)
        yp, ys = _expert_ffn(_moe_split(gp_x, N_EXPERTS), _moe_split(gp_g, N_EXPERTS),
                             _moe_split(gs_x, N_EXPERTS), _moe_split(gs_g, N_EXPERTS),
                             exp_w_gate, exp_w_up, exp_w_down, l)
        xp = _combine(gp_d, yp.reshape(gp_x.shape), xp_mid, mod_p[:, 5:6])
        xs = _combine(gs_d, ys.reshape(gs_x.shape), xs_mid, mod_s[:, 5:6])

    return (xp, xs, jnp.stack(ks_new, axis=1), jnp.stack(vs_new, axis=1))
```

```python
import functools
import math

import numpy as np
import jax
import jax.numpy as jnp
from jax import lax
from jax.experimental import pallas as pl
from jax.experimental.pallas import tpu as pltpu

D_MODEL = 1024
DEPTH = 2
GRID_W = 64
D_HYENA = 512
HYENA_ORDER = 2
HYENA_POS_BANDS = 16
HYENA_POS_DIM = 1 + 2 * HYENA_POS_BANDS
HYENA_FILTER_HIDDEN = 64
HYENA_WINDOW_SHIFT = 0.05
N_HEADS = 8
N_KV_HEADS = 2
HEAD_DIM = 64
HEADS_PER_KV = N_HEADS // N_KV_HEADS
ATT_Q = N_HEADS * HEAD_DIM
ATT_KV = N_KV_HEADS * HEAD_DIM
ROPE_THETA = 10000.0
D_SCONV = 512
N_EXPERTS = 16
EXPERT_FF = 512
CAPACITY_FACTOR = 2
NORM_EPS = 1e-6
HY_COLS = (HYENA_ORDER + 1) * D_HYENA
ATT_COLS = ATT_Q + 2 * ATT_KV
SC_COLS = 3 * D_SCONV
GATE_COLS = 3 * D_MODEL
MAIN_COLS = HY_COLS + SC_COLS + GATE_COLS
D_IN = MAIN_COLS + ATT_COLS

F32 = jnp.float32
BF16 = jnp.bfloat16

V7X_VMEM_BYTES = 64 * 1024 * 1024
VMEM_LIMIT = V7X_VMEM_BYTES - 8 * 1024 * 1024
LANES = 128
MLP_PAD = 128
STEP_TOKENS = 1024


def _params(*sem):
    return pltpu.CompilerParams(dimension_semantics=sem, vmem_limit_bytes=VMEM_LIMIT)


def _const_spec(shape):
    nd = len(shape)
    return pl.BlockSpec(shape, lambda *_: (0,) * nd, pipeline_mode=pl.Buffered(1))


def _layer_spec(shape, layer):
    nd = len(shape)
    return pl.BlockSpec((None,) + tuple(shape), lambda *_: (layer,) + (0,) * nd, pipeline_mode=pl.Buffered(1))


def _dot(a, b):
    return jnp.dot(a, b, preferred_element_type=F32)


def _silu(x):
    return x * (1.0 / (1.0 + jnp.exp(-x)))


def _sigmoid(x):
    return 1.0 / (1.0 + jnp.exp(-x))


@functools.lru_cache(maxsize=None)
def _dft_tables(L):
    n = 2 * L
    k = np.arange(L, dtype=np.int64)
    idx = (k[:, None] * k[None, :]) % n
    ang = idx.astype(np.float64) * (2.0 * np.pi / n)
    c = np.cos(ang)
    s = np.sin(ang)
    s[0, :] = 1.0 - 2.0 * (k % 2)
    fwd = np.concatenate([c, s], axis=0).astype(np.float32)
    inv = np.concatenate([c, s.T], axis=1).astype(np.float32)
    return fwd, inv


@functools.lru_cache(maxsize=None)
def _pos_features(L):
    n = np.arange(L, dtype=np.float64)
    t = n / max(L - 1, 1)
    bands = np.linspace(1e-4, HYENA_POS_BANDS - 1, HYENA_POS_BANDS)
    ang = 2.0 * math.pi * n[:, None] * bands[None, :] / L
    z = np.concatenate([t[:, None], np.cos(ang), np.sin(ang)], axis=-1)
    zp = np.zeros((L, MLP_PAD), np.float32)
    zp[:, :HYENA_POS_DIM] = z
    return zp


@functools.lru_cache(maxsize=None)
def _rope_tables(L):
    rows = L // GRID_W
    row = np.repeat(np.arange(rows, dtype=np.float64), GRID_W)
    col = np.tile(np.arange(GRID_W, dtype=np.float64), rows)
    axis_dim = HEAD_DIM // 2
    inv_freq = ROPE_THETA ** (-np.arange(0, axis_dim, 2, dtype=np.float64) / axis_dim)
    ar = row[:, None] * inv_freq[None, :]
    ac = col[:, None] * inv_freq[None, :]
    ang = np.concatenate([ar, ar, ac, ac], axis=-1)
    cos = np.cos(ang).astype(np.float32)
    sin = np.sin(ang).astype(np.float32)
    return np.tile(cos, (1, N_HEADS)), np.tile(sin, (1, N_HEADS))


@functools.lru_cache(maxsize=None)
def _head_tables():
    lane = np.arange(ATT_Q)
    block_ones = (lane[:, None] // HEAD_DIM == lane[None, :] // HEAD_DIM).astype(np.float32)
    src = np.arange(ATT_KV)
    dst = np.arange(HEADS_PER_KV * HEAD_DIM)
    rep = np.stack([(src[:, None] == g * HEAD_DIM + dst[None, :] % HEAD_DIM) for g in range(N_KV_HEADS)])
    return block_ones, rep.astype(np.float32)


@functools.lru_cache(maxsize=None)
def _prefix_table(n):
    i = np.arange(n)
    return (i[:, None] < i[None, :]).astype(np.float32)


def _mod_kernel(cond_ref, w_ref, b_ref, o_ref):
    a = _silu(cond_ref[...]).astype(BF16)
    o_ref[...] = _dot(a, w_ref[...].astype(BF16)) + b_ref[...]


def _modulation(cond, mod_w, mod_b):
    rows = cond.shape[0]
    tn = 1024
    ncols = mod_w.shape[-1]
    return pl.pallas_call(
        _mod_kernel,
        grid=(DEPTH, ncols // tn),
        in_specs=[
            pl.BlockSpec((rows, D_MODEL), lambda l, j: (0, 0)),
            pl.BlockSpec((None, D_MODEL, tn), lambda l, j: (l, 0, j)),
            pl.BlockSpec((None, 1, tn), lambda l, j: (l, 0, j)),
        ],
        out_specs=pl.BlockSpec((None, rows, tn), lambda l, j: (l, 0, j)),
        out_shape=jax.ShapeDtypeStruct((DEPTH, rows, ncols), F32),
        compiler_params=_params("parallel", "parallel"),
        name="modulation",
    )(cond, mod_w, mod_b.reshape(DEPTH, 1, ncols))


def _rms_modulate(x, norm_w, shift, scale):
    y = x * lax.rsqrt(jnp.mean(x * x, axis=-1, keepdims=True) + NORM_EPS)
    return (y * norm_w) * (1.0 + scale) + shift


def _inproj_kernel(x_ref, mod_ref, nw_ref, w_ref, main_ref, qkv_ref, h_sc):
    h = _rms_modulate(x_ref[...], nw_ref[...], mod_ref[0:1, :], mod_ref[1:2, :])
    h_sc[...] = h.astype(BF16)
    for lo in range(0, D_IN, ATT_COLS):
        y = _dot(h_sc[...], w_ref[:, lo:lo + ATT_COLS])
        if lo < HY_COLS:
            main_ref[:, lo:lo + ATT_COLS] = y.astype(BF16)
        elif lo == HY_COLS:
            qkv_ref[...] = y
        else:
            main_ref[:, lo - ATT_COLS:lo] = y.astype(BF16)


def _inproj(x2d, mod, norm_w, w_in_bf, layer, rows_per_req, tm=512):
    T = x2d.shape[0]
    n_req = mod.shape[0]
    req = (lambda i: (i * tm) // rows_per_req) if n_req > 1 else (lambda i: 0)
    return pl.pallas_call(
        _inproj_kernel,
        grid=(T // tm,),
        in_specs=[
            pl.BlockSpec((tm, D_MODEL), lambda i: (i, 0)),
            pl.BlockSpec((None, 6, D_MODEL), lambda i: (req(i), 0, 0)),
            _const_spec((1, D_MODEL)),
            _layer_spec((D_MODEL, D_IN), layer),
        ],
        out_specs=[pl.BlockSpec((tm, MAIN_COLS), lambda i: (i, 0)),
                   pl.BlockSpec((tm, ATT_COLS), lambda i: (i, 0))],
        out_shape=[jax.ShapeDtypeStruct((T, MAIN_COLS), BF16), jax.ShapeDtypeStruct((T, ATT_COLS), F32)],
        scratch_shapes=[pltpu.VMEM((tm, D_MODEL), BF16)],
        compiler_params=_params("parallel"),
        name="inproj",
    )(x2d, mod, norm_w, w_in_bf)


def _filter_kernel(z_ref, w1_ref, b1_ref, w2_ref, b2_ref, w3f_ref, b3f_ref, w3b_ref, b3b_ref,
                   decf_ref, decb_ref, fwd_ref, p_ref, h_sc, *, L):
    hi = lax.Precision.HIGHEST

    @pl.when((pl.program_id(0) == 0) & (pl.program_id(1) == 0))
    def _():
        h1 = jnp.sin(jnp.dot(z_ref[...], w1_ref[...], precision=hi, preferred_element_type=F32) + b1_ref[...])
        h_sc[...] = jnp.sin(jnp.dot(h1, w2_ref[...], precision=hi, preferred_element_type=F32) + b2_ref[...])

    h = h_sc[...]
    tc = w3f_ref.shape[1]
    pos = lax.broadcasted_iota(jnp.int32, (L, tc), 0)
    t = pos.astype(F32) / float(max(L - 1, 1))

    def taps(w3_ref, b3_ref, dec_ref):
        g = jnp.dot(h, w3_ref[...], precision=hi, preferred_element_type=F32) + b3_ref[...]
        return g * (jnp.exp(-t * jnp.abs(dec_ref[...])) + HYENA_WINDOW_SHIFT)

    hf = taps(w3f_ref, b3f_ref, decf_ref)
    hb = jnp.where(pos == 0, 0.0, taps(w3b_ref, b3b_ref, decb_ref))
    fwd = fwd_ref[...]
    tf = _dot(fwd, hf.astype(BF16))
    tb = _dot(fwd, hb.astype(BF16))
    k_re = tf[:L] + tb[:L]
    k_im = tb[L:] - tf[L:]
    k_ny = tf[L:] + tb[L:]
    first = pos == 0
    inv_n = 1.0 / (2 * L)
    p_ref[0] = jnp.where(first, k_re * inv_n, k_re * (2.0 * inv_n))
    p_ref[1] = jnp.where(first, 0.0, k_im * (2.0 * inv_n))
    p_ref[2] = jnp.where(first, k_ny * inv_n, k_re * (2.0 * inv_n))


def _hyena_spectra(L, w1, b1, w2, b2, w3, b3, decay, fwd_tab):
    pad_h = MLP_PAD - HYENA_FILTER_HIDDEN
    z = jnp.asarray(_pos_features(L))
    w1p = jnp.pad(w1, ((0, MLP_PAD - HYENA_POS_DIM), (0, pad_h)))
    b1p = jnp.pad(b1, (0, pad_h)).reshape(1, MLP_PAD)
    w2p = jnp.pad(w2, ((0, pad_h), (0, pad_h)))
    b2p = jnp.pad(b2, (0, pad_h)).reshape(1, MLP_PAD)
    w3p = jnp.pad(w3, ((0, pad_h), (0, 0)))
    ncol = w3.shape[1]
    b3r = b3.reshape(1, ncol)
    decr = decay.reshape(1, ncol)
    tc = 256
    nct = D_HYENA // tc
    per_dir = HYENA_ORDER * nct
    col_f = lambda o, c: (0, o * nct + c)
    col_b = lambda o, c: (0, per_dir + o * nct + c)
    return pl.pallas_call(
        functools.partial(_filter_kernel, L=L),
        grid=(HYENA_ORDER, nct),
        in_specs=[
            _const_spec((L, MLP_PAD)),
            _const_spec((MLP_PAD, MLP_PAD)), _const_spec((1, MLP_PAD)),
            _const_spec((MLP_PAD, MLP_PAD)), _const_spec((1, MLP_PAD)),
            pl.BlockSpec((MLP_PAD, tc), col_f), pl.BlockSpec((1, tc), col_f),
            pl.BlockSpec((MLP_PAD, tc), col_b), pl.BlockSpec((1, tc), col_b),
            pl.BlockSpec((1, tc), col_f), pl.BlockSpec((1, tc), col_b),
            _const_spec((2 * L, L)),
        ],
        out_specs=pl.BlockSpec((None, 3, L, tc), lambda o, c: (o, 0, 0, c)),
        out_shape=jax.ShapeDtypeStruct((HYENA_ORDER, 3, L, D_HYENA), F32),
        scratch_shapes=[pltpu.VMEM((L, MLP_PAD), F32)],
        compiler_params=_params("arbitrary", "arbitrary"),
        name="hyena_spectra",
    )(z, w1p, b1p, w2p, b2p, w3p, b3r, w3p, b3r, decr, decr, fwd_tab)


def _conv3(x, w_ref, b_ref, first, last):
    L = x.shape[0]
    prev = jnp.where(first, 0.0, pltpu.roll(x, 1, 0))
    nxt = jnp.where(last, 0.0, pltpu.roll(x, L - 1, 0))
    return prev * w_ref[0:1, :] + x * w_ref[1:2, :] + nxt * w_ref[2:3, :] + b_ref[...]


def _seqmix_kernel(v_ref, x1_ref, x2_ref, u_ref, bg_ref, cg_ref,
                   wv_ref, wx1_ref, wx2_ref, bv_ref, bx1_ref, bx2_ref, scw_ref, scb_ref,
                   p_ref, hb_ref, fwd_ref, inv_ref, ya_ref, yc_ref, spec_sc, *, L):
    n_req, _, tc = v_ref.shape
    pos = lax.broadcasted_iota(jnp.int32, (L, tc), 0)
    first = pos == 0
    last = pos == L - 1
    mult_refs = ((x1_ref, wx1_ref, bx1_ref), (x2_ref, wx2_ref, bx2_ref))

    def chain(r):
        z = _conv3(v_ref[r].astype(F32), wv_ref, bv_ref, first, last)
        yield
        for o in range(HYENA_ORDER):
            ab = _dot(fwd_ref[...], z.astype(BF16))
            yield
            a = ab[:L]
            b = ab[L:]
            p2 = p_ref[o, 1]
            spec_sc[r, 0:L, :] = (a * p_ref[o, 0] + b * p2).astype(BF16)
            spec_sc[r, L:2 * L, :] = (b * p_ref[o, 2] - a * p2).astype(BF16)
            x_ref, w_ref, b_ref = mult_refs[o]
            mult = _conv3(x_ref[r].astype(F32), w_ref, b_ref, first, last)
            yield
            y = _dot(inv_ref[...], spec_sc[r])
            yield
            z = mult * (y + hb_ref[o:o + 1, :] * z)
            if o == 0:
                gated = cg_ref[r].astype(F32) * u_ref[r].astype(F32)
                yc = bg_ref[r].astype(F32) * _conv3(gated, scw_ref, scb_ref, first, last)
                yc_ref[r] = yc.astype(yc_ref.dtype)
            if o == HYENA_ORDER - 1:
                ya_ref[r] = z.astype(ya_ref.dtype)
            yield

    chains = [chain(r) for r in range(n_req)]
    n_stages = 1 + 4 * HYENA_ORDER
    for tick in range(n_stages + n_req - 1):
        for r, c in enumerate(chains):
            if 0 <= tick - r < n_stages:
                next(c)


def _seqmix(main3d, hy_short_w, hy_short_b, sc_w, sc_b, spectra, hy_bias, fwd_tab, inv_tab):
    B, L, _ = main3d.shape
    tc = 256
    nct = D_HYENA // tc
    nr = max(2, STEP_TOKENS // L)
    act = lambda off: pl.BlockSpec((nr, L, tc), lambda c, b, off=off: (b, 0, off * nct + c))
    par3 = lambda off: pl.BlockSpec((3, tc), lambda c, b, off=off: (0, off * nct + c))
    par1 = lambda off: pl.BlockSpec((1, tc), lambda c, b, off=off: (0, off * nct + c))
    out = pl.BlockSpec((nr, L, tc), lambda c, b: (b, 0, c))
    hsb = hy_short_b.reshape(1, HY_COLS)
    return pl.pallas_call(
        functools.partial(_seqmix_kernel, L=L),
        grid=(nct, B // nr),
        in_specs=[
            act(0), act(1), act(2), act(3), act(4), act(5),
            par3(0), par3(1), par3(2), par1(0), par1(1), par1(2),
            par3(0), par1(0),
            pl.BlockSpec((HYENA_ORDER, 3, L, tc), lambda c, b: (0, 0, 0, c), pipeline_mode=pl.Buffered(1)),
            pl.BlockSpec((HYENA_ORDER, tc), lambda c, b: (0, c)),
            _const_spec((2 * L, L)), _const_spec((L, 2 * L)),
        ],
        out_specs=[out, out],
        out_shape=[jax.ShapeDtypeStruct((B, L, D_HYENA), BF16), jax.ShapeDtypeStruct((B, L, D_SCONV), BF16)],
        scratch_shapes=[pltpu.VMEM((nr, 2 * L, tc), BF16)],
        compiler_params=_params("arbitrary", "arbitrary"),
        name="seqmix",
    )(main3d, main3d, main3d, main3d, main3d, main3d,
      hy_short_w, hy_short_w, hy_short_w, hsb, hsb, hsb, sc_w, sc_b.reshape(1, D_SCONV),
      spectra, hy_bias, fwd_tab, inv_tab)


def _head_rms(x, ones_ref, w):
    sq = x * x
    hi = sq.astype(BF16)
    lo = (sq - hi.astype(F32)).astype(BF16)
    ones = ones_ref[...]
    width = x.shape[1]
    ss = _dot(hi, ones[:width, :width]) + _dot(lo, ones[:width, :width])
    return x * lax.rsqrt(ss * (1.0 / HEAD_DIM) + NORM_EPS) * w


def _rope(x, cos, sin):
    width = x.shape[1]
    lane = lax.broadcasted_iota(jnp.int32, x.shape, 1)
    half = HEAD_DIM // 4
    low = (lane % (2 * half)) < half
    rot = jnp.where(low, -pltpu.roll(x, width - half, 1), pltpu.roll(x, half, 1))
    return x * cos + rot * sin


def _attn_kernel(*refs, L, n_ctx, tq, rope):
    if rope:
        (q_ref, k_ref, v_ref, ck_ref, cv_ref, qw_ref, kw_ref, ones_ref, rep_ref, rept_ref,
         cosq_ref, sinq_ref, cosk_ref, sink_ref, yb_ref, k4_sc, v4_sc) = refs
    else:
        (q_ref, k_ref, v_ref, qw_ref, kw_ref, ones_ref, rep_ref, rept_ref,
         yb_ref, ko_ref, vo_ref, k4_sc, v4_sc) = refs

    @pl.when(pl.program_id(1) == 0)
    def _():
        kn = _head_rms(k_ref[...], ones_ref, kw_ref[...])
        v = v_ref[...]
        if rope:
            keys = _rope(kn, cosk_ref[...], sink_ref[...])
            keys = jnp.concatenate([ck_ref[...], keys], axis=0)
            vals = jnp.concatenate([cv_ref[...], v], axis=0)
        else:
            ko_ref[...] = kn
            vo_ref[...] = v
            keys, vals = kn, v
        keys = keys.astype(BF16)
        vals = vals.astype(BF16)
        for g in range(N_KV_HEADS):
            k4_sc[g] = lax.dot_general(rept_ref[g], keys, (((1,), (1,)), ((), ())),
                                       preferred_element_type=F32).astype(BF16)
            v4_sc[g] = _dot(vals, rep_ref[g]).astype(BF16)

    q = _head_rms(q_ref[...], ones_ref, qw_ref[...])
    if rope:
        q = _rope(q, cosq_ref[...], sinq_ref[...])
    q = q * (HEAD_DIM ** -0.5)
    gw = HEADS_PER_KV * HEAD_DIM
    lane = lax.broadcasted_iota(jnp.int32, (tq, gw), 1)
    for g in range(N_KV_HEADS):
        qg = q[:, g * gw:(g + 1) * gw]
        acc = jnp.zeros((tq, gw), F32)
        for h in range(HEADS_PER_KV):
            mine = (lane // HEAD_DIM) == h
            qm = jnp.where(mine, qg, 0.0).astype(BF16)
            s = _dot(qm, k4_sc[g])
            p = jnp.exp(s - jnp.max(s, axis=-1, keepdims=True))
            denom = jnp.sum(p, axis=-1, keepdims=True)
            o4 = _dot(p.astype(BF16), v4_sc[g])
            acc = jnp.where(mine, o4 * (1.0 / denom), acc)
        yb_ref[:, g * gw:(g + 1) * gw] = acc.astype(yb_ref.dtype)


def _attention(qkv3d, q_norm_w, k_norm_w, ctx_k=None, ctx_v=None, layer=0):
    B, L, _ = qkv3d.shape
    rope = ctx_k is not None
    tq = min(1024, L)
    n_ctx = ctx_k.shape[2] if rope else 0
    lk = L + n_ctx
    ones_np, rep_np = _head_tables()
    ones = jnp.asarray(ones_np).astype(BF16)
    rep = jnp.asarray(rep_np).astype(BF16)
    rept = jnp.asarray(np.swapaxes(rep_np, 1, 2)).astype(BF16)
    qw = jnp.tile(q_norm_w, N_HEADS).reshape(1, ATT_Q)
    kw = jnp.tile(k_norm_w, N_KV_HEADS).reshape(1, ATT_KV)
    kblk = ATT_Q // ATT_KV
    in_specs = [
        pl.BlockSpec((None, tq, ATT_Q), lambda b, i: (b, i, 0)),
        pl.BlockSpec((None, L, ATT_KV), lambda b, i: (b, 0, kblk)),
        pl.BlockSpec((None, L, ATT_KV), lambda b, i: (b, 0, kblk + 1)),
    ]
    args = [qkv3d, qkv3d, qkv3d]
    if rope:
        in_specs += [pl.BlockSpec((None, None, n_ctx, ATT_KV), lambda b, i: (b, layer, 0, 0))] * 2
        args += [ctx_k, ctx_v]
    in_specs += [_const_spec((1, ATT_Q)), _const_spec((1, ATT_KV)), _const_spec((ATT_Q, ATT_Q)),
                 _const_spec((N_KV_HEADS, ATT_KV, HEADS_PER_KV * HEAD_DIM)),
                 _const_spec((N_KV_HEADS, HEADS_PER_KV * HEAD_DIM, ATT_KV))]
    args += [qw, kw, ones, rep, rept]
    yb_shape = jax.ShapeDtypeStruct((B, L, ATT_Q), BF16)
    yb_spec = pl.BlockSpec((None, tq, ATT_Q), lambda b, i: (b, i, 0))
    if rope:
        cos_np, sin_np = _rope_tables(L)
        cos = jnp.asarray(cos_np)
        sin = jnp.asarray(sin_np)
        in_specs += [pl.BlockSpec((tq, ATT_Q), lambda b, i: (i, 0))] * 2
        in_specs += [_const_spec((L, ATT_KV))] * 2
        args += [cos, sin, cos[:, :ATT_KV], sin[:, :ATT_KV]]
        out_specs = yb_spec
        out_shape = yb_shape
    else:
        kv_spec = pl.BlockSpec((None, L, ATT_KV), lambda b, i: (b, 0, 0))
        kv_shape = jax.ShapeDtypeStruct((B, L, ATT_KV), F32)
        out_specs = [yb_spec, kv_spec, kv_spec]
        out_shape = [yb_shape, kv_shape, kv_shape]
    return pl.pallas_call(
        functools.partial(_attn_kernel, L=L, n_ctx=n_ctx, tq=tq, rope=rope),
        grid=(B, L // tq),
        in_specs=in_specs,
        out_specs=out_specs,
        out_shape=out_shape,
        scratch_shapes=[pltpu.VMEM((N_KV_HEADS, HEADS_PER_KV * HEAD_DIM, lk), BF16),
                        pltpu.VMEM((N_KV_HEADS, lk, HEADS_PER_KV * HEAD_DIM), BF16)],
        compiler_params=_params("parallel", "arbitrary"),
        name="attention",
    )(*args)


def _merge_kernel(ya_ref, yb_ref, yc_ref, ga_ref, gb_ref, gc_ref, x_ref, mod_ref, nw_ref,
                  wa_ref, wb_ref, wc_ref, wo_ref, xo_ref, h2_ref):
    tm = x_ref.shape[0]
    n_chain = 2
    rows_per = tm // n_chain

    def chain(c):
        rows = pl.ds(c * rows_per, rows_per)
        da = _dot(ya_ref[rows, :], wa_ref[...])
        db = _dot(yb_ref[rows, :], wb_ref[...])
        dc = _dot(yc_ref[rows, :], wc_ref[...])
        yield
        merged = (_sigmoid(ga_ref[rows, :].astype(F32)) * da + _sigmoid(gb_ref[rows, :].astype(F32)) * db
                  + _sigmoid(gc_ref[rows, :].astype(F32)) * dc).astype(BF16)
        yield
        proj = _dot(merged, wo_ref[...])
        yield
        x = x_ref[rows, :] + mod_ref[2:3, :] * proj
        xo_ref[rows, :] = x
        h2_ref[rows, :] = _rms_modulate(x, nw_ref[...], mod_ref[3:4, :], mod_ref[4:5, :]).astype(BF16)
        yield

    chains = [chain(c) for c in range(n_chain)]
    n_stages = 4
    for tick in range(n_stages + n_chain - 1):
        for c, ch in enumerate(chains):
            if 0 <= tick - c < n_stages:
                next(ch)


def _merge(ya, yb, yc, main2d, x2d, mod, norm_w, wa, wb, wc, wo, layer, rows_per_req, tm=512):
    T = x2d.shape[0]
    n_req = mod.shape[0]
    req = (lambda i: (i * tm) // rows_per_req) if n_req > 1 else (lambda i: 0)
    gate0 = (HY_COLS + SC_COLS) // D_MODEL
    br = pl.BlockSpec((tm, D_HYENA), lambda i: (i, 0))
    gate = lambda k: pl.BlockSpec((tm, D_MODEL), lambda i, k=k: (i, gate0 + k))
    row = pl.BlockSpec((tm, D_MODEL), lambda i: (i, 0))
    return pl.pallas_call(
        _merge_kernel,
        grid=(T // tm,),
        in_specs=[
            br, br, br, gate(0), gate(1), gate(2), row,
            pl.BlockSpec((None, 6, D_MODEL), lambda i: (req(i), 0, 0)),
            _const_spec((1, D_MODEL)),
            _layer_spec((D_HYENA, D_MODEL), layer), _layer_spec((ATT_Q, D_MODEL), layer),
            _layer_spec((D_SCONV, D_MODEL), layer), _layer_spec((D_MODEL, D_MODEL), layer),
        ],
        out_specs=[row, row],
        out_shape=[jax.ShapeDtypeStruct((T, D_MODEL), F32), jax.ShapeDtypeStruct((T, D_MODEL), BF16)],
        compiler_params=_params("parallel"),
        name="merge",
    )(ya, yb, yc, main2d, main2d, main2d, x2d, mod, norm_w, wa, wb, wc, wo)


def _route_kernel(h_ref, rw_ref, tri_ref, aff_ref, slot_ref, *, cap):
    b = pl.program_id(0)
    n_req = h_ref.shape[0]
    for r in range(n_req):
        logits = lax.dot_general(rw_ref[...], h_ref[r], (((1,), (1,)), ((), ())), preferred_element_type=F32)
        ex = jnp.exp(logits - jnp.max(logits, axis=0, keepdims=True))
        first_row = pl.multiple_of((b * n_req + r) * N_EXPERTS, N_EXPERTS)
        aff_ref[pl.ds(first_row, N_EXPERTS), :] = ex / jnp.sum(ex, axis=0, keepdims=True)

    @pl.when(b == pl.num_programs(0) - 1)
    def _():
        aff = aff_ref[...]

        def count(mask):
            return jnp.sum(jnp.where(mask, 1.0, 0.0), axis=1, keepdims=True)

        kth = jnp.zeros((aff.shape[0], 1), jnp.int32)
        for bit in range(30, -1, -1):
            trial = kth | (1 << bit)
            enough = count(aff >= lax.bitcast_convert_type(trial, F32)) >= cap
            kth = jnp.where(enough, trial, kth)
        next_up = lax.bitcast_convert_type(kth + 1, F32)
        above = aff >= next_up
        tied = (aff >= lax.bitcast_convert_type(kth, F32)) & (aff < next_up)
        tri = tri_ref[...]
        tied_before = _dot(jnp.where(tied, 1.0, 0.0).astype(BF16), tri)
        chosen = above | (tied & (tied_before < (cap - count(above))))
        slot = _dot(jnp.where(chosen, 1.0, 0.0).astype(BF16), tri)
        slot_ref[...] = jnp.where(chosen, slot, -1.0)


def _route(h3d, router_wt):
    B, n_tok, _ = h3d.shape
    cap = CAPACITY_FACTOR * n_tok // N_EXPERTS
    tri = jnp.asarray(_prefix_table(n_tok)).astype(BF16)
    whole = pl.BlockSpec((B * N_EXPERTS, n_tok), lambda b: (0, 0))
    shape = jax.ShapeDtypeStruct((B * N_EXPERTS, n_tok), F32)
    nr = _requests_per_step(n_tok)
    return pl.pallas_call(
        functools.partial(_route_kernel, cap=cap),
        grid=(B // nr,),
        in_specs=[
            pl.BlockSpec((nr, n_tok, D_MODEL), lambda b: (b, 0, 0)),
            _const_spec((N_EXPERTS, D_MODEL)),
            _const_spec((n_tok, n_tok)),
        ],
        out_specs=[whole, whole],
        out_shape=[shape, shape],
        compiler_params=_params("arbitrary"),
        name="moe_route",
    )(h3d, router_wt, tri)


def _dispatch_kernel(h_ref, aff_ref, slot_ref, xs_ref, d_ref, g_ref, *, n_tok, cap):
    row = lax.broadcasted_iota(jnp.int32, (cap, n_tok), 0).astype(F32)
    for r in range(h_ref.shape[0]):
        def one_expert(e, carry, r=r):
            idx = r * N_EXPERTS + e
            hit = row == slot_ref[pl.ds(idx, 1), :]
            base = pl.multiple_of(e * cap, cap)
            d_ref[r, pl.ds(base, cap), :] = jnp.where(hit, 1.0, 0.0).astype(BF16)
            gate = jnp.sum(jnp.where(hit, aff_ref[pl.ds(idx, 1), :], 0.0), axis=1, keepdims=True)
            g_ref[r, pl.ds(base, cap), :] = jnp.broadcast_to(gate, (cap, LANES))
            return carry

        lax.fori_loop(0, N_EXPERTS, one_expert, 0, unroll=4)
        xs_ref[r] = _dot(d_ref[r], h_ref[r]).astype(BF16)


def _requests_per_step(n_tok):
    return max(1, STEP_TOKENS // n_tok)


def _dispatch(h3d, router_wt):
    B, n_tok, _ = h3d.shape
    cap = CAPACITY_FACTOR * n_tok // N_EXPERTS
    rows = N_EXPERTS * cap
    aff, slot = _route(h3d, router_wt)
    nr = _requests_per_step(n_tok)
    per_req = pl.BlockSpec((nr * N_EXPERTS, n_tok), lambda b: (b, 0))
    blk = lambda w: pl.BlockSpec((nr, rows, w), lambda b: (b, 0, 0))
    return pl.pallas_call(
        functools.partial(_dispatch_kernel, n_tok=n_tok, cap=cap),
        grid=(B // nr,),
        in_specs=[pl.BlockSpec((nr, n_tok, D_MODEL), lambda b: (b, 0, 0)), per_req, per_req],
        out_specs=[blk(D_MODEL), blk(n_tok), blk(LANES)],
        out_shape=[jax.ShapeDtypeStruct((B, rows, D_MODEL), BF16),
                   jax.ShapeDtypeStruct((B, rows, n_tok), BF16),
                   jax.ShapeDtypeStruct((B, rows, LANES), F32)],
        compiler_params=_params("parallel"),
        name="moe_dispatch",
    )(h3d, aff, slot)


def _ffn_kernel(xp_ref, gp_ref, xs_ref, gs_ref, wg_ref, wu_ref, wd_ref, yp_ref, ys_ref):
    wg = wg_ref[...].astype(BF16)
    wu = wu_ref[...].astype(BF16)
    wd = wd_ref[...].astype(BF16)
    for x_ref, g_ref, y_ref in ((xp_ref, gp_ref, yp_ref), (xs_ref, gs_ref, ys_ref)):
        nb, cap, _ = x_ref.shape
        x = x_ref[...].reshape(nb * cap, D_MODEL)
        gate = g_ref[...].reshape(nb * cap, LANES)
        gate = jnp.concatenate([gate] * (EXPERT_FF // LANES), axis=1)
        act = _silu(_dot(x, wg)) * _dot(x, wu) * gate
        y = _dot(act.astype(BF16), wd)
        y_ref[...] = y.reshape(nb, cap, D_MODEL).astype(y_ref.dtype)


def _expert_ffn(xp, gp, xs, gs, w_gate, w_up, w_down, layer):
    def act_spec(a):
        nb, _, cap, w = a.shape
        return pl.BlockSpec((nb, None, cap, w), lambda e: (0, e, 0, 0))

    def w_spec(a):
        return pl.BlockSpec((None, None) + a.shape[2:], lambda e: (layer, e, 0, 0))

    return pl.pallas_call(
        _ffn_kernel,
        grid=(N_EXPERTS,),
        in_specs=[act_spec(xp), act_spec(gp), act_spec(xs), act_spec(gs),
                  w_spec(w_gate), w_spec(w_up), w_spec(w_down)],
        out_specs=[act_spec(xp), act_spec(xs)],
        out_shape=[jax.ShapeDtypeStruct(xp.shape, BF16), jax.ShapeDtypeStruct(xs.shape, BF16)],
        compiler_params=_params("parallel"),
        name="expert_ffn",
    )(xp, gp, xs, gs, w_gate, w_up, w_down)


def _combine_kernel(d_ref, y_ref, x_ref, mod_ref, o_ref):
    for r in range(d_ref.shape[0]):
        moe = lax.dot_general(d_ref[r], y_ref[r], (((0,), (0,)), ((), ())), preferred_element_type=F32)
        o_ref[r] = x_ref[r] + mod_ref[min(r, mod_ref.shape[0] - 1)] * moe


def _combine(dmat, y3d, x3d, mod_g2):
    B, n_tok, _ = x3d.shape
    rows = dmat.shape[1]
    nr = _requests_per_step(n_tok)
    shared_mod = mod_g2.shape[0] == 1
    mod_spec = (pl.BlockSpec((1, 1, D_MODEL), lambda b: (0, 0, 0)) if shared_mod
                else pl.BlockSpec((nr, 1, D_MODEL), lambda b: (b, 0, 0)))
    return pl.pallas_call(
        _combine_kernel,
        grid=(B // nr,),
        in_specs=[
            pl.BlockSpec((nr, rows, n_tok), lambda b: (b, 0, 0)),
            pl.BlockSpec((nr, rows, D_MODEL), lambda b: (b, 0, 0)),
            pl.BlockSpec((nr, n_tok, D_MODEL), lambda b: (b, 0, 0)),
            mod_spec,
        ],
        out_specs=pl.BlockSpec((nr, n_tok, D_MODEL), lambda b: (b, 0, 0)),
        out_shape=jax.ShapeDtypeStruct(x3d.shape, F32),
        compiler_params=_params("parallel"),
        name="moe_combine",
    )(dmat, y3d, x3d, mod_g2)


def _token_mixers(x3d, mod, p, tabs, layer, ctx=None):
    B, L, _ = x3d.shape
    x2d = x3d.reshape(B * L, D_MODEL)
    main, qkv = _inproj(x2d, mod, p['norm1_w'], p['w_in'], layer, L)
    fwd_tab, inv_tab = tabs
    spectra = _hyena_spectra(L, p['hy_w1'], p['hy_b1'], p['hy_w2'], p['hy_b2'], p['hy_w3'], p['hy_b3'],
                             p['hy_decay'], fwd_tab)
    ya, yc = _seqmix(main.reshape(B, L, MAIN_COLS), p['hy_short_w'], p['hy_short_b'], p['sc_w'], p['sc_b'],
                     spectra, p['hy_bias'], fwd_tab, inv_tab)
    qkv3d = qkv.reshape(B, L, ATT_COLS)
    if ctx is None:
        yb, k, v = _attention(qkv3d, p['q_norm_w'], p['k_norm_w'])
    else:
        yb = _attention(qkv3d, p['q_norm_w'], p['k_norm_w'], ctx[0], ctx[1], layer)
        k = v = None
    x_mid, h2 = _merge(ya.reshape(B * L, D_HYENA), yb.reshape(B * L, ATT_Q), yc.reshape(B * L, D_SCONV),
                       main, x2d, mod, p['norm2_w'], p['w_br_a'], p['w_br_b'], p['w_br_c'], p['w_o'], layer, L)
    return x_mid.reshape(B, L, D_MODEL), h2.reshape(B, L, D_MODEL), k, v


def _moe_split(a, n_exp):
    B, rows, w = a.shape
    return a.reshape(B, n_exp, rows // n_exp, w)


def kernel(x_prompt, x_sample, cache_k, cache_v, c, c_ctx, mod_w, mod_b, norm1_w, norm2_w, w_in, hy_short_w, hy_short_b, hy_w1, hy_b1, hy_w2, hy_b2, hy_w3, hy_b3, hy_decay, hy_bias, q_norm_w, k_norm_w, sc_w, sc_b, w_br_a, w_br_b, w_br_c, w_o, router_w, exp_w_gate, exp_w_up, exp_w_down):
    n_dec = x_sample.shape[0]
    n_ctx = cache_k.shape[2]
    lp = x_prompt.shape[1]
    ls = x_sample.shape[1]

    cond_rows = 16
    cond = jnp.concatenate([c_ctx[None, :], c, jnp.zeros((cond_rows - 1 - n_dec, D_MODEL), F32)], axis=0)
    mod = _modulation(cond, mod_w, mod_b).reshape(DEPTH, cond_rows, 6, D_MODEL)

    tabs_p = tuple(jnp.asarray(t).astype(BF16) for t in _dft_tables(lp))
    tabs_s = tuple(jnp.asarray(t).astype(BF16) for t in _dft_tables(ls))
    ctx_k = cache_k.reshape(n_dec, DEPTH, n_ctx, ATT_KV)
    ctx_v = cache_v.reshape(n_dec, DEPTH, n_ctx, ATT_KV)
    dense = {'w_in': w_in.astype(BF16), 'w_br_a': w_br_a.astype(BF16), 'w_br_b': w_br_b.astype(BF16),
             'w_br_c': w_br_c.astype(BF16), 'w_o': w_o.astype(BF16)}

    xp, xs = x_prompt, x_sample
    ks_new, vs_new = [], []
    for l in range(DEPTH):
        p = dict(dense)
        p.update({
            'norm1_w': norm1_w[l].reshape(1, D_MODEL), 'norm2_w': norm2_w[l].reshape(1, D_MODEL),
            'hy_short_w': hy_short_w[l], 'hy_short_b': hy_short_b[l],
            'hy_w1': hy_w1[l], 'hy_b1': hy_b1[l], 'hy_w2': hy_w2[l], 'hy_b2': hy_b2[l],
            'hy_w3': hy_w3[l], 'hy_b3': hy_b3[l], 'hy_decay': hy_decay[l], 'hy_bias': hy_bias[l],
            'q_norm_w': q_norm_w[l], 'k_norm_w': k_norm_w[l], 'sc_w': sc_w[l], 'sc_b': sc_b[l],
        })
        mod_p = mod[l, 0:1]
        mod_s = mod[l, 1:1 + n_dec]
        xp_mid, h2p, k_l, v_l = _token_mixers(xp, mod_p, p, tabs_p, l)
        xs_mid, h2s, _, _ = _token_mixers(xs, mod_s, p, tabs_s, l, (ctx_k, ctx_v))
        ks_new.append(k_l.reshape(k_l.shape[0], lp, N_KV_HEADS, HEAD_DIM))
        vs_new.append(v_l.reshape(v_l.shape[0], lp, N_KV_HEADS, HEAD_DIM))

        router_wt = router_w[l].T.astype(BF16)
        gp_x, gp_d, gp_g = _dispatch(h2p, router_wt)
        gs_x, gs_d, gs_g = _dispatch(h2s, router_wt)
        yp, ys = _expert_ffn(_moe_split(gp_x, N_EXPERTS), _moe_split(gp_g, N_EXPERTS),
                             _moe_split(gs_x, N_EXPERTS), _moe_split(gs_g, N_EXPERTS),
                             exp_w_gate, exp_w_up, exp_w_down, l)
        xp = _combine(gp_d, yp.reshape(gp_x.shape), xp_mid, mod_p[:, 5:6])
        xs = _combine(gs_d, ys.reshape(gs_x.shape), xs_mid, mod_s[:, 5:6])

    return (xp, xs, jnp.stack(ks_new, axis=1), jnp.stack(vs_new, axis=1))
```

```python
import functools
import math

import numpy as np
import jax
import jax.numpy as jnp
from jax import lax
from jax.experimental import pallas as pl
from jax.experimental.pallas import tpu as pltpu

D_MODEL = 1024
DEPTH = 2
GRID_W = 64
D_HYENA = 512
HYENA_ORDER = 2
HYENA_POS_BANDS = 16
HYENA_POS_DIM = 1 + 2 * HYENA_POS_BANDS
HYENA_FILTER_HIDDEN = 64
HYENA_WINDOW_SHIFT = 0.05
N_HEADS = 8
N_KV_HEADS = 2
HEAD_DIM = 64
HEADS_PER_KV = N_HEADS // N_KV_HEADS
ATT_Q = N_HEADS * HEAD_DIM
ATT_KV = N_KV_HEADS * HEAD_DIM
ROPE_THETA = 10000.0
D_SCONV = 512
N_EXPERTS = 16
EXPERT_FF = 512
CAPACITY_FACTOR = 2
NORM_EPS = 1e-6
HY_COLS = (HYENA_ORDER + 1) * D_HYENA
ATT_COLS = ATT_Q + 2 * ATT_KV
SC_COLS = 3 * D_SCONV
GATE_COLS = 3 * D_MODEL
MAIN_COLS = HY_COLS + SC_COLS
MIX_COLS = MAIN_COLS + ATT_COLS
D_IN = MIX_COLS + GATE_COLS

F32 = jnp.float32
BF16 = jnp.bfloat16

V7X_VMEM_BYTES = 64 * 1024 * 1024
VMEM_LIMIT = V7X_VMEM_BYTES - 8 * 1024 * 1024
LANES = 128
MLP_PAD = 128
STEP_TOKENS = 1024


def _params(*sem):
    return pltpu.CompilerParams(dimension_semantics=sem, vmem_limit_bytes=VMEM_LIMIT)


def _const_spec(shape):
    nd = len(shape)
    return pl.BlockSpec(shape, lambda *_: (0,) * nd, pipeline_mode=pl.Buffered(1))


def _layer_spec(shape, layer):
    nd = len(shape)
    return pl.BlockSpec((None,) + tuple(shape), lambda *_: (layer,) + (0,) * nd, pipeline_mode=pl.Buffered(1))


def _dot(a, b):
    return jnp.dot(a, b, preferred_element_type=F32)


def _silu(x):
    return x * (1.0 / (1.0 + jnp.exp(-x)))


def _sigmoid(x):
    return 1.0 / (1.0 + jnp.exp(-x))


@functools.lru_cache(maxsize=None)
def _dft_tables(L):
    n = 2 * L
    k = np.arange(L, dtype=np.int64)
    idx = (k[:, None] * k[None, :]) % n
    ang = idx.astype(np.float64) * (2.0 * np.pi / n)
    c = np.cos(ang)
    s = np.sin(ang)
    s[0, :] = 1.0 - 2.0 * (k % 2)
    fwd = np.concatenate([c, s], axis=0).astype(np.float32)
    inv = np.concatenate([c, s.T], axis=1).astype(np.float32)
    return fwd, inv


@functools.lru_cache(maxsize=None)
def _pos_features(L):
    n = np.arange(L, dtype=np.float64)
    t = n / max(L - 1, 1)
    bands = np.linspace(1e-4, HYENA_POS_BANDS - 1, HYENA_POS_BANDS)
    ang = 2.0 * math.pi * n[:, None] * bands[None, :] / L
    z = np.concatenate([t[:, None], np.cos(ang), np.sin(ang)], axis=-1)
    zp = np.zeros((L, MLP_PAD), np.float32)
    zp[:, :HYENA_POS_DIM] = z
    return zp


@functools.lru_cache(maxsize=None)
def _rope_tables(L):
    rows = L // GRID_W
    row = np.repeat(np.arange(rows, dtype=np.float64), GRID_W)
    col = np.tile(np.arange(GRID_W, dtype=np.float64), rows)
    axis_dim = HEAD_DIM // 2
    inv_freq = ROPE_THETA ** (-np.arange(0, axis_dim, 2, dtype=np.float64) / axis_dim)
    ar = row[:, None] * inv_freq[None, :]
    ac = col[:, None] * inv_freq[None, :]
    ang = np.concatenate([ar, ar, ac, ac], axis=-1)
    cos = np.cos(ang).astype(np.float32)
    sin = np.sin(ang).astype(np.float32)
    return np.tile(cos, (1, N_HEADS)), np.tile(sin, (1, N_HEADS))


@functools.lru_cache(maxsize=None)
def _head_tables():
    lane = np.arange(ATT_Q)
    block_ones = (lane[:, None] // HEAD_DIM == lane[None, :] // HEAD_DIM).astype(np.float32)
    src = np.arange(ATT_KV)
    dst = np.arange(HEADS_PER_KV * HEAD_DIM)
    rep = np.stack([(src[:, None] == g * HEAD_DIM + dst[None, :] % HEAD_DIM) for g in range(N_KV_HEADS)])
    return block_ones, rep.astype(np.float32)


@functools.lru_cache(maxsize=None)
def _prefix_table(n):
    i = np.arange(n)
    return (i[:, None] < i[None, :]).astype(np.float32)


def _mod_kernel(cond_ref, w_ref, b_ref, o_ref):
    a = _silu(cond_ref[...]).astype(BF16)
    o_ref[...] = _dot(a, w_ref[...].astype(BF16)) + b_ref[...]


def _modulation(cond, mod_w, mod_b):
    rows = cond.shape[0]
    tn = 1024
    ncols = mod_w.shape[-1]
    return pl.pallas_call(
        _mod_kernel,
        grid=(DEPTH, ncols // tn),
        in_specs=[
            pl.BlockSpec((rows, D_MODEL), lambda l, j: (0, 0)),
            pl.BlockSpec((None, D_MODEL, tn), lambda l, j: (l, 0, j)),
            pl.BlockSpec((None, 1, tn), lambda l, j: (l, 0, j)),
        ],
        out_specs=pl.BlockSpec((None, rows, tn), lambda l, j: (l, 0, j)),
        out_shape=jax.ShapeDtypeStruct((DEPTH, rows, ncols), F32),
        compiler_params=_params("parallel", "parallel"),
        name="modulation",
    )(cond, mod_w, mod_b.reshape(DEPTH, 1, ncols))


def _rms_modulate(x, norm_w, shift, scale):
    y = x * lax.rsqrt(jnp.mean(x * x, axis=-1, keepdims=True) + NORM_EPS)
    return (y * norm_w) * (1.0 + scale) + shift


def _inproj_kernel(x_ref, mod_ref, nw_ref, w_ref, main_ref, qkv_ref, h_sc):
    h = _rms_modulate(x_ref[...], nw_ref[...], mod_ref[0:1, :], mod_ref[1:2, :])
    h_sc[...] = h.astype(BF16)
    for lo in range(0, MIX_COLS, ATT_COLS):
        y = _dot(h_sc[...], w_ref[:, lo:lo + ATT_COLS])
        if lo < HY_COLS:
            main_ref[:, lo:lo + ATT_COLS] = y.astype(BF16)
        elif lo == HY_COLS:
            qkv_ref[...] = y
        else:
            main_ref[:, lo - ATT_COLS:lo] = y.astype(BF16)


def _inproj(x2d, mod, norm_w, w_in_bf, layer, rows_per_req, tm=512):
    T = x2d.shape[0]
    n_req = mod.shape[0]
    req = (lambda i: (i * tm) // rows_per_req) if n_req > 1 else (lambda i: 0)
    return pl.pallas_call(
        _inproj_kernel,
        grid=(T // tm,),
        in_specs=[
            pl.BlockSpec((tm, D_MODEL), lambda i: (i, 0)),
            pl.BlockSpec((None, 6, D_MODEL), lambda i: (req(i), 0, 0)),
            _const_spec((1, D_MODEL)),
            _layer_spec((D_MODEL, MIX_COLS), layer),
        ],
        out_specs=[pl.BlockSpec((tm, MAIN_COLS), lambda i: (i, 0)),
                   pl.BlockSpec((tm, ATT_COLS), lambda i: (i, 0))],
        out_shape=[jax.ShapeDtypeStruct((T, MAIN_COLS), BF16), jax.ShapeDtypeStruct((T, ATT_COLS), F32)],
        scratch_shapes=[pltpu.VMEM((tm, D_MODEL), BF16)],
        compiler_params=_params("parallel"),
        name="inproj",
    )(x2d, mod, norm_w, w_in_bf)


def _filter_kernel(z_ref, w1_ref, b1_ref, w2_ref, b2_ref, w3f_ref, b3f_ref, w3b_ref, b3b_ref,
                   decf_ref, decb_ref, fwd_ref, p_ref, h_sc, *, L):
    hi = lax.Precision.HIGHEST

    @pl.when((pl.program_id(0) == 0) & (pl.program_id(1) == 0))
    def _():
        h1 = jnp.sin(jnp.dot(z_ref[...], w1_ref[...], precision=hi, preferred_element_type=F32) + b1_ref[...])
        h_sc[...] = jnp.sin(jnp.dot(h1, w2_ref[...], precision=hi, preferred_element_type=F32) + b2_ref[...])

    h = h_sc[...]
    tc = w3f_ref.shape[1]
    pos = lax.broadcasted_iota(jnp.int32, (L, tc), 0)
    t = pos.astype(F32) / float(max(L - 1, 1))

    def taps(w3_ref, b3_ref, dec_ref):
        g = jnp.dot(h, w3_ref[...], precision=hi, preferred_element_type=F32) + b3_ref[...]
        return g * (jnp.exp(-t * jnp.abs(dec_ref[...])) + HYENA_WINDOW_SHIFT)

    hf = taps(w3f_ref, b3f_ref, decf_ref)
    hb = jnp.where(pos == 0, 0.0, taps(w3b_ref, b3b_ref, decb_ref))
    fwd = fwd_ref[...]
    tf = _dot(fwd, hf.astype(BF16))
    tb = _dot(fwd, hb.astype(BF16))
    k_re = tf[:L] + tb[:L]
    k_im = tb[L:] - tf[L:]
    k_ny = tf[L:] + tb[L:]
    first = pos == 0
    inv_n = 1.0 / (2 * L)
    p_ref[0] = jnp.where(first, k_re * inv_n, k_re * (2.0 * inv_n))
    p_ref[1] = jnp.where(first, 0.0, k_im * (2.0 * inv_n))
    p_ref[2] = jnp.where(first, k_ny * inv_n, k_re * (2.0 * inv_n))


def _hyena_spectra(L, w1, b1, w2, b2, w3, b3, decay, fwd_tab):
    pad_h = MLP_PAD - HYENA_FILTER_HIDDEN
    z = jnp.asarray(_pos_features(L))
    w1p = jnp.pad(w1, ((0, MLP_PAD - HYENA_POS_DIM), (0, pad_h)))
    b1p = jnp.pad(b1, (0, pad_h)).reshape(1, MLP_PAD)
    w2p = jnp.pad(w2, ((0, pad_h), (0, pad_h)))
    b2p = jnp.pad(b2, (0, pad_h)).reshape(1, MLP_PAD)
    w3p = jnp.pad(w3, ((0, pad_h), (0, 0)))
    ncol = w3.shape[1]
    b3r = b3.reshape(1, ncol)
    decr = decay.reshape(1, ncol)
    tc = 256
    nct = D_HYENA // tc
    per_dir = HYENA_ORDER * nct
    col_f = lambda o, c: (0, o * nct + c)
    col_b = lambda o, c: (0, per_dir + o * nct + c)
    return pl.pallas_call(
        functools.partial(_filter_kernel, L=L),
        grid=(HYENA_ORDER, nct),
        in_specs=[
            _const_spec((L, MLP_PAD)),
            _const_spec((MLP_PAD, MLP_PAD)), _const_spec((1, MLP_PAD)),
            _const_spec((MLP_PAD, MLP_PAD)), _const_spec((1, MLP_PAD)),
            pl.BlockSpec((MLP_PAD, tc), col_f), pl.BlockSpec((1, tc), col_f),
            pl.BlockSpec((MLP_PAD, tc), col_b), pl.BlockSpec((1, tc), col_b),
            pl.BlockSpec((1, tc), col_f), pl.BlockSpec((1, tc), col_b),
            _const_spec((2 * L, L)),
        ],
        out_specs=pl.BlockSpec((None, 3, L, tc), lambda o, c: (o, 0, 0, c)),
        out_shape=jax.ShapeDtypeStruct((HYENA_ORDER, 3, L, D_HYENA), F32),
        scratch_shapes=[pltpu.VMEM((L, MLP_PAD), F32)],
        compiler_params=_params("arbitrary", "arbitrary"),
        name="hyena_spectra",
    )(z, w1p, b1p, w2p, b2p, w3p, b3r, w3p, b3r, decr, decr, fwd_tab)


def _conv3(x, w_ref, b_ref, first, last):
    L = x.shape[0]
    prev = jnp.where(first, 0.0, pltpu.roll(x, 1, 0))
    nxt = jnp.where(last, 0.0, pltpu.roll(x, L - 1, 0))
    return prev * w_ref[0:1, :] + x * w_ref[1:2, :] + nxt * w_ref[2:3, :] + b_ref[...]


def _seqmix_kernel(v_ref, x1_ref, x2_ref, u_ref, bg_ref, cg_ref,
                   wv_ref, wx1_ref, wx2_ref, bv_ref, bx1_ref, bx2_ref, scw_ref, scb_ref,
                   p_ref, hb_ref, fwd_ref, inv_ref, ya_ref, yc_ref, spec_sc, *, L):
    n_req, _, tc = v_ref.shape
    pos = lax.broadcasted_iota(jnp.int32, (L, tc), 0)
    first = pos == 0
    last = pos == L - 1
    mult_refs = ((x1_ref, wx1_ref, bx1_ref), (x2_ref, wx2_ref, bx2_ref))

    def chain(r):
        z = _conv3(v_ref[r].astype(F32), wv_ref, bv_ref, first, last)
        yield
        for o in range(HYENA_ORDER):
            ab = _dot(fwd_ref[...], z.astype(BF16))
            yield
            a = ab[:L]
            b = ab[L:]
            p2 = p_ref[o, 1]
            spec_sc[r, 0:L, :] = (a * p_ref[o, 0] + b * p2).astype(BF16)
            spec_sc[r, L:2 * L, :] = (b * p_ref[o, 2] - a * p2).astype(BF16)
            x_ref, w_ref, b_ref = mult_refs[o]
            mult = _conv3(x_ref[r].astype(F32), w_ref, b_ref, first, last)
            yield
            y = _dot(inv_ref[...], spec_sc[r])
            yield
            z = mult * (y + hb_ref[o:o + 1, :] * z)
            if o == 0:
                gated = cg_ref[r].astype(F32) * u_ref[r].astype(F32)
                yc = bg_ref[r].astype(F32) * _conv3(gated, scw_ref, scb_ref, first, last)
                yc_ref[r] = yc.astype(yc_ref.dtype)
            if o == HYENA_ORDER - 1:
                ya_ref[r] = z.astype(ya_ref.dtype)
            yield

    chains = [chain(r) for r in range(n_req)]
    n_stages = 1 + 4 * HYENA_ORDER
    for tick in range(n_stages + n_req - 1):
        for r, c in enumerate(chains):
            if 0 <= tick - r < n_stages:
                next(c)


def _seqmix(main3d, hy_short_w, hy_short_b, sc_w, sc_b, spectra, hy_bias, fwd_tab, inv_tab):
    B, L, _ = main3d.shape
    tc = 256
    nct = D_HYENA // tc
    nr = max(2, STEP_TOKENS // L)
    act = lambda off: pl.BlockSpec((nr, L, tc), lambda c, b, off=off: (b, 0, off * nct + c))
    par3 = lambda off: pl.BlockSpec((3, tc), lambda c, b, off=off: (0, off * nct + c))
    par1 = lambda off: pl.BlockSpec((1, tc), lambda c, b, off=off: (0, off * nct + c))
    out = pl.BlockSpec((nr, L, tc), lambda c, b: (b, 0, c))
    hsb = hy_short_b.reshape(1, HY_COLS)
    return pl.pallas_call(
        functools.partial(_seqmix_kernel, L=L),
        grid=(nct, B // nr),
        in_specs=[
            act(0), act(1), act(2), act(3), act(4), act(5),
            par3(0), par3(1), par3(2), par1(0), par1(1), par1(2),
            par3(0), par1(0),
            pl.BlockSpec((HYENA_ORDER, 3, L, tc), lambda c, b: (0, 0, 0, c), pipeline_mode=pl.Buffered(1)),
            pl.BlockSpec((HYENA_ORDER, tc), lambda c, b: (0, c)),
            _const_spec((2 * L, L)), _const_spec((L, 2 * L)),
        ],
        out_specs=[out, out],
        out_shape=[jax.ShapeDtypeStruct((B, L, D_HYENA), BF16), jax.ShapeDtypeStruct((B, L, D_SCONV), BF16)],
        scratch_shapes=[pltpu.VMEM((nr, 2 * L, tc), BF16)],
        compiler_params=_params("arbitrary", "arbitrary"),
        name="seqmix",
    )(main3d, main3d, main3d, main3d, main3d, main3d,
      hy_short_w, hy_short_w, hy_short_w, hsb, hsb, hsb, sc_w, sc_b.reshape(1, D_SCONV),
      spectra, hy_bias, fwd_tab, inv_tab)


def _head_rms(x, ones_ref, w):
    sq = x * x
    hi = sq.astype(BF16)
    lo = (sq - hi.astype(F32)).astype(BF16)
    ones = ones_ref[...]
    width = x.shape[1]
    ss = _dot(hi, ones[:width, :width]) + _dot(lo, ones[:width, :width])
    return x * lax.rsqrt(ss * (1.0 / HEAD_DIM) + NORM_EPS) * w


def _rope(x, cos, sin):
    width = x.shape[1]
    lane = lax.broadcasted_iota(jnp.int32, x.shape, 1)
    half = HEAD_DIM // 4
    low = (lane % (2 * half)) < half
    rot = jnp.where(low, -pltpu.roll(x, width - half, 1), pltpu.roll(x, half, 1))
    return x * cos + rot * sin


def _attn_kernel(*refs, L, n_ctx, tq, rope):
    if rope:
        (q_ref, k_ref, v_ref, ck_ref, cv_ref, qw_ref, kw_ref, ones_ref, rep_ref, rept_ref,
         cosq_ref, sinq_ref, cosk_ref, sink_ref, yb_ref, k4_sc, v4_sc) = refs
    else:
        (q_ref, k_ref, v_ref, qw_ref, kw_ref, ones_ref, rep_ref, rept_ref,
         yb_ref, ko_ref, vo_ref, k4_sc, v4_sc) = refs

    @pl.when(pl.program_id(1) == 0)
    def _():
        kn = _head_rms(k_ref[...], ones_ref, kw_ref[...])
        v = v_ref[...]
        if rope:
            keys = _rope(kn, cosk_ref[...], sink_ref[...])
            keys = jnp.concatenate([ck_ref[...], keys], axis=0)
            vals = jnp.concatenate([cv_ref[...], v], axis=0)
        else:
            ko_ref[...] = kn
            vo_ref[...] = v
            keys, vals = kn, v
        keys = keys.astype(BF16)
        vals = vals.astype(BF16)
        for g in range(N_KV_HEADS):
            k4_sc[g] = lax.dot_general(rept_ref[g], keys, (((1,), (1,)), ((), ())),
                                       preferred_element_type=F32).astype(BF16)
            v4_sc[g] = _dot(vals, rep_ref[g]).astype(BF16)

    q = _head_rms(q_ref[...], ones_ref, qw_ref[...])
    if rope:
        q = _rope(q, cosq_ref[...], sinq_ref[...])
    q = q * (HEAD_DIM ** -0.5)
    gw = HEADS_PER_KV * HEAD_DIM
    lane = lax.broadcasted_iota(jnp.int32, (tq, gw), 1)
    for g in range(N_KV_HEADS):
        qg = q[:, g * gw:(g + 1) * gw]
        acc = jnp.zeros((tq, gw), F32)
        for h in range(HEADS_PER_KV):
            mine = (lane // HEAD_DIM) == h
            qm = jnp.where(mine, qg, 0.0).astype(BF16)
            s = _dot(qm, k4_sc[g])
            p = jnp.exp(s - jnp.max(s, axis=-1, keepdims=True))
            denom = jnp.sum(p, axis=-1, keepdims=True)
            o4 = _dot(p.astype(BF16), v4_sc[g])
            acc = jnp.where(mine, o4 * (1.0 / denom), acc)
        yb_ref[:, g * gw:(g + 1) * gw] = acc.astype(yb_ref.dtype)


def _attention(qkv3d, q_norm_w, k_norm_w, ctx_k=None, ctx_v=None, layer=0):
    B, L, _ = qkv3d.shape
    rope = ctx_k is not None
    tq = min(1024, L)
    n_ctx = ctx_k.shape[2] if rope else 0
    lk = L + n_ctx
    ones_np, rep_np = _head_tables()
    ones = jnp.asarray(ones_np).astype(BF16)
    rep = jnp.asarray(rep_np).astype(BF16)
    rept = jnp.asarray(np.swapaxes(rep_np, 1, 2)).astype(BF16)
    qw = jnp.tile(q_norm_w, N_HEADS).reshape(1, ATT_Q)
    kw = jnp.tile(k_norm_w, N_KV_HEADS).reshape(1, ATT_KV)
    kblk = ATT_Q // ATT_KV
    in_specs = [
        pl.BlockSpec((None, tq, ATT_Q), lambda b, i: (b, i, 0)),
        pl.BlockSpec((None, L, ATT_KV), lambda b, i: (b, 0, kblk)),
        pl.BlockSpec((None, L, ATT_KV), lambda b, i: (b, 0, kblk + 1)),
    ]
    args = [qkv3d, qkv3d, qkv3d]
    if rope:
        in_specs += [pl.BlockSpec((None, None, n_ctx, ATT_KV), lambda b, i: (b, layer, 0, 0))] * 2
        args += [ctx_k, ctx_v]
    in_specs += [_const_spec((1, ATT_Q)), _const_spec((1, ATT_KV)), _const_spec((ATT_Q, ATT_Q)),
                 _const_spec((N_KV_HEADS, ATT_KV, HEADS_PER_KV * HEAD_DIM)),
                 _const_spec((N_KV_HEADS, HEADS_PER_KV * HEAD_DIM, ATT_KV))]
    args += [qw, kw, ones, rep, rept]
    yb_shape = jax.ShapeDtypeStruct((B, L, ATT_Q), BF16)
    yb_spec = pl.BlockSpec((None, tq, ATT_Q), lambda b, i: (b, i, 0))
    if rope:
        cos_np, sin_np = _rope_tables(L)
        cos = jnp.asarray(cos_np)
        sin = jnp.asarray(sin_np)
        in_specs += [pl.BlockSpec((tq, ATT_Q), lambda b, i: (i, 0))] * 2
        in_specs += [_const_spec((L, ATT_KV))] * 2
        args += [cos, sin, cos[:, :ATT_KV], sin[:, :ATT_KV]]
        out_specs = yb_spec
        out_shape = yb_shape
    else:
        kv_spec = pl.BlockSpec((None, L, ATT_KV), lambda b, i: (b, 0, 0))
        kv_shape = jax.ShapeDtypeStruct((B, L, ATT_KV), F32)
        out_specs = [yb_spec, kv_spec, kv_spec]
        out_shape = [yb_shape, kv_shape, kv_shape]
    return pl.pallas_call(
        functools.partial(_attn_kernel, L=L, n_ctx=n_ctx, tq=tq, rope=rope),
        grid=(B, L // tq),
        in_specs=in_specs,
        out_specs=out_specs,
        out_shape=out_shape,
        scratch_shapes=[pltpu.VMEM((N_KV_HEADS, HEADS_PER_KV * HEAD_DIM, lk), BF16),
                        pltpu.VMEM((N_KV_HEADS, lk, HEADS_PER_KV * HEAD_DIM), BF16)],
        compiler_params=_params("parallel", "arbitrary"),
        name="attention",
    )(*args)


def _merge_kernel(ya_ref, yb_ref, yc_ref, x_ref, mod_ref, nw1_ref, nw2_ref, *rest):
    gate_w_refs = rest[:GATE_COLS // ATT_COLS]
    wa_ref, wb_ref, wc_ref, wo_ref, xo_ref, h2_ref = rest[GATE_COLS // ATT_COLS:]
    tm = x_ref.shape[0]
    n_chain = 2
    rows_per = tm // n_chain

    def chain(c):
        rows = pl.ds(c * rows_per, rows_per)
        x = x_ref[rows, :]
        h = _rms_modulate(x, nw1_ref[...], mod_ref[0:1, :], mod_ref[1:2, :]).astype(BF16)
        yield
        gates = jnp.concatenate([_dot(h, w_ref[...]) for w_ref in gate_w_refs], axis=1)
        da = _dot(ya_ref[rows, :], wa_ref[...])
        db = _dot(yb_ref[rows, :], wb_ref[...])
        dc = _dot(yc_ref[rows, :], wc_ref[...])
        yield
        merged = (_sigmoid(gates[:, 0:D_MODEL]) * da + _sigmoid(gates[:, D_MODEL:2 * D_MODEL]) * db
                  + _sigmoid(gates[:, 2 * D_MODEL:]) * dc).astype(BF16)
        yield
        proj = _dot(merged, wo_ref[...])
        yield
        x = x + mod_ref[2:3, :] * proj
        xo_ref[rows, :] = x
        h2_ref[rows, :] = _rms_modulate(x, nw2_ref[...], mod_ref[3:4, :], mod_ref[4:5, :]).astype(BF16)
        yield

    chains = [chain(c) for c in range(n_chain)]
    n_stages = 5
    for tick in range(n_stages + n_chain - 1):
        for c, ch in enumerate(chains):
            if 0 <= tick - c < n_stages:
                next(ch)


def _merge(ya, yb, yc, x2d, mod, norm1_w, norm2_w, w_in_bf, wa, wb, wc, wo, layer, rows_per_req, tm=512):
    T = x2d.shape[0]
    n_req = mod.shape[0]
    req = (lambda i: (i * tm) // rows_per_req) if n_req > 1 else (lambda i: 0)
    br = pl.BlockSpec((tm, D_HYENA), lambda i: (i, 0))
    row = pl.BlockSpec((tm, D_MODEL), lambda i: (i, 0))
    gate_blk0 = MIX_COLS // ATT_COLS
    gate_w = [pl.BlockSpec((None, D_MODEL, ATT_COLS), lambda i, k=k: (layer, 0, gate_blk0 + k),
                           pipeline_mode=pl.Buffered(1)) for k in range(GATE_COLS // ATT_COLS)]
    return pl.pallas_call(
        _merge_kernel,
        grid=(T // tm,),
        in_specs=[
            br, br, br, row,
            pl.BlockSpec((None, 6, D_MODEL), lambda i: (req(i), 0, 0)),
            _const_spec((1, D_MODEL)), _const_spec((1, D_MODEL)),
            *gate_w,
            _layer_spec((D_HYENA, D_MODEL), layer), _layer_spec((ATT_Q, D_MODEL), layer),
            _layer_spec((D_SCONV, D_MODEL), layer), _layer_spec((D_MODEL, D_MODEL), layer),
        ],
        out_specs=[row, row],
        out_shape=[jax.ShapeDtypeStruct((T, D_MODEL), F32), jax.ShapeDtypeStruct((T, D_MODEL), BF16)],
        compiler_params=_params("parallel"),
        name="merge",
    )(ya, yb, yc, x2d, mod, norm1_w, norm2_w, *([w_in_bf] * len(gate_w)), wa, wb, wc, wo)


def _route_kernel(h_ref, rw_ref, tri_ref, aff_ref, slot_ref, *, cap):
    b = pl.program_id(0)
    n_req = h_ref.shape[0]
    for r in range(n_req):
        logits = lax.dot_general(rw_ref[...], h_ref[r], (((1,), (1,)), ((), ())), preferred_element_type=F32)
        ex = jnp.exp(logits - jnp.max(logits, axis=0, keepdims=True))
        first_row = pl.multiple_of((b * n_req + r) * N_EXPERTS, N_EXPERTS)
        aff_ref[pl.ds(first_row, N_EXPERTS), :] = ex / jnp.sum(ex, axis=0, keepdims=True)

    @pl.when(b == pl.num_programs(0) - 1)
    def _():
        aff = aff_ref[...]

        def count(mask):
            return jnp.sum(jnp.where(mask, 1.0, 0.0), axis=1, keepdims=True)

        kth = jnp.zeros((aff.shape[0], 1), jnp.int32)
        for bit in range(30, -1, -1):
            trial = kth | (1 << bit)
            enough = count(aff >= lax.bitcast_convert_type(trial, F32)) >= cap
            kth = jnp.where(enough, trial, kth)
        next_up = lax.bitcast_convert_type(kth + 1, F32)
        above = aff >= next_up
        tied = (aff >= lax.bitcast_convert_type(kth, F32)) & (aff < next_up)
        tri = tri_ref[...]
        tied_before = _dot(jnp.where(tied, 1.0, 0.0).astype(BF16), tri)
        chosen = above | (tied & (tied_before < (cap - count(above))))
        slot = _dot(jnp.where(chosen, 1.0, 0.0).astype(BF16), tri)
        slot_ref[...] = jnp.where(chosen, slot, -1.0)


def _route(h3d, router_wt):
    B, n_tok, _ = h3d.shape
    cap = CAPACITY_FACTOR * n_tok // N_EXPERTS
    tri = jnp.asarray(_prefix_table(n_tok)).astype(BF16)
    whole = pl.BlockSpec((B * N_EXPERTS, n_tok), lambda b: (0, 0))
    shape = jax.ShapeDtypeStruct((B * N_EXPERTS, n_tok), F32)
    nr = _requests_per_step(n_tok)
    return pl.pallas_call(
        functools.partial(_route_kernel, cap=cap),
        grid=(B // nr,),
        in_specs=[
            pl.BlockSpec((nr, n_tok, D_MODEL), lambda b: (b, 0, 0)),
            _const_spec((N_EXPERTS, D_MODEL)),
            _const_spec((n_tok, n_tok)),
        ],
        out_specs=[whole, whole],
        out_shape=[shape, shape],
        compiler_params=_params("arbitrary"),
        name="moe_route",
    )(h3d, router_wt, tri)


def _dispatch_kernel(h_ref, aff_ref, slot_ref, xs_ref, d_ref, g_ref, *, n_tok, cap):
    row = lax.broadcasted_iota(jnp.int32, (cap, n_tok), 0).astype(F32)
    for r in range(h_ref.shape[0]):
        def one_expert(e, carry, r=r):
            idx = r * N_EXPERTS + e
            hit = row == slot_ref[pl.ds(idx, 1), :]
            base = pl.multiple_of(e * cap, cap)
            d_ref[r, pl.ds(base, cap), :] = jnp.where(hit, 1.0, 0.0).astype(BF16)
            gate = jnp.sum(jnp.where(hit, aff_ref[pl.ds(idx, 1), :], 0.0), axis=1, keepdims=True)
            g_ref[r, pl.ds(base, cap), :] = jnp.broadcast_to(gate, (cap, LANES))
            return carry

        lax.fori_loop(0, N_EXPERTS, one_expert, 0, unroll=4)
        xs_ref[r] = _dot(d_ref[r], h_ref[r]).astype(BF16)


def _requests_per_step(n_tok):
    return max(1, STEP_TOKENS // n_tok)


def _dispatch(h3d, router_wt):
    B, n_tok, _ = h3d.shape
    cap = CAPACITY_FACTOR * n_tok // N_EXPERTS
    rows = N_EXPERTS * cap
    aff, slot = _route(h3d, router_wt)
    nr = _requests_per_step(n_tok)
    per_req = pl.BlockSpec((nr * N_EXPERTS, n_tok), lambda b: (b, 0))
    blk = lambda w: pl.BlockSpec((nr, rows, w), lambda b: (b, 0, 0))
    return pl.pallas_call(
        functools.partial(_dispatch_kernel, n_tok=n_tok, cap=cap),
        grid=(B // nr,),
        in_specs=[pl.BlockSpec((nr, n_tok, D_MODEL), lambda b: (b, 0, 0)), per_req, per_req],
        out_specs=[blk(D_MODEL), blk(n_tok), blk(LANES)],
        out_shape=[jax.ShapeDtypeStruct((B, rows, D_MODEL), BF16),
                   jax.ShapeDtypeStruct((B, rows, n_tok), BF16),
                   jax.ShapeDtypeStruct((B, rows, LANES), F32)],
        compiler_params=_params("parallel"),
        name="moe_dispatch",
    )(h3d, aff, slot)


def _ffn_kernel(xp_ref, gp_ref, xs_ref, gs_ref, wg_ref, wu_ref, wd_ref, yp_ref, ys_ref):
    wg = wg_ref[...].astype(BF16)
    wu = wu_ref[...].astype(BF16)
    wd = wd_ref[...].astype(BF16)
    for x_ref, g_ref, y_ref in ((xp_ref, gp_ref, yp_ref), (xs_ref, gs_ref, ys_ref)):
        nb, cap, _ = x_ref.shape
        x = x_ref[...].reshape(nb * cap, D_MODEL)
        gate = g_ref[...].reshape(nb * cap, LANES)
        gate = jnp.concatenate([gate] * (EXPERT_FF // LANES), axis=1)
        act = _silu(_dot(x, wg)) * _dot(x, wu) * gate
        y = _dot(act.astype(BF16), wd)
        y_ref[...] = y.reshape(nb, cap, D_MODEL).astype(y_ref.dtype)


def _expert_ffn(xp, gp, xs, gs, w_gate, w_up, w_down, layer):
    def act_spec(a):
        nb, _, cap, w = a.shape
        return pl.BlockSpec((nb, None, cap, w), lambda e: (0, e, 0, 0))

    def w_spec(a):
        return pl.BlockSpec((None, None) + a.shape[2:], lambda e: (layer, e, 0, 0))

    return pl.pallas_call(
        _ffn_kernel,
        grid=(N_EXPERTS,),
        in_specs=[act_spec(xp), act_spec(gp), act_spec(xs), act_spec(gs),
                  w_spec(w_gate), w_spec(w_up), w_spec(w_down)],
        out_specs=[act_spec(xp), act_spec(xs)],
        out_shape=[jax.ShapeDtypeStruct(xp.shape, BF16), jax.ShapeDtypeStruct(xs.shape, BF16)],
        compiler_params=_params("parallel"),
        name="expert_ffn",
    )(xp, gp, xs, gs, w_gate, w_up, w_down)


def _combine_kernel(d_ref, y_ref, x_ref, mod_ref, o_ref):
    for r in range(d_ref.shape[0]):
        moe = lax.dot_general(d_ref[r], y_ref[r], (((0,), (0,)), ((), ())), preferred_element_type=F32)
        o_ref[r] = x_ref[r] + mod_ref[min(r, mod_ref.shape[0] - 1)] * moe


def _combine(dmat, y3d, x3d, mod_g2):
    B, n_tok, _ = x3d.shape
    rows = dmat.shape[1]
    nr = _requests_per_step(n_tok)
    shared_mod = mod_g2.shape[0] == 1
    mod_spec = (pl.BlockSpec((1, 1, D_MODEL), lambda b: (0, 0, 0)) if shared_mod
                else pl.BlockSpec((nr, 1, D_MODEL), lambda b: (b, 0, 0)))
    return pl.pallas_call(
        _combine_kernel,
        grid=(B // nr,),
        in_specs=[
            pl.BlockSpec((nr, rows, n_tok), lambda b: (b, 0, 0)),
            pl.BlockSpec((nr, rows, D_MODEL), lambda b: (b, 0, 0)),
            pl.BlockSpec((nr, n_tok, D_MODEL), lambda b: (b, 0, 0)),
            mod_spec,
        ],
        out_specs=pl.BlockSpec((nr, n_tok, D_MODEL), lambda b: (b, 0, 0)),
        out_shape=jax.ShapeDtypeStruct(x3d.shape, F32),
        compiler_params=_params("parallel"),
        name="moe_combine",
    )(dmat, y3d, x3d, mod_g2)


def _token_mixers(x3d, mod, p, tabs, layer, ctx=None):
    B, L, _ = x3d.shape
    x2d = x3d.reshape(B * L, D_MODEL)
    main, qkv = _inproj(x2d, mod, p['norm1_w'], p['w_in'], layer, L)
    fwd_tab, inv_tab = tabs
    spectra = _hyena_spectra(L, p['hy_w1'], p['hy_b1'], p['hy_w2'], p['hy_b2'], p['hy_w3'], p['hy_b3'],
                             p['hy_decay'], fwd_tab)
    ya, yc = _seqmix(main.reshape(B, L, MAIN_COLS), p['hy_short_w'], p['hy_short_b'], p['sc_w'], p['sc_b'],
                     spectra, p['hy_bias'], fwd_tab, inv_tab)
    qkv3d = qkv.reshape(B, L, ATT_COLS)
    if ctx is None:
        yb, k, v = _attention(qkv3d, p['q_norm_w'], p['k_norm_w'])
    else:
        yb = _attention(qkv3d, p['q_norm_w'], p['k_norm_w'], ctx[0], ctx[1], layer)
        k = v = None
    x_mid, h2 = _merge(ya.reshape(B * L, D_HYENA), yb.reshape(B * L, ATT_Q), yc.reshape(B * L, D_SCONV),
                       x2d, mod, p['norm1_w'], p['norm2_w'], p['w_in'], p['w_br_a'], p['w_br_b'], p['w_br_c'],
                       p['w_o'], layer, L)
    return x_mid.reshape(B, L, D_MODEL), h2.reshape(B, L, D_MODEL), k, v


def _moe_split(a, n_exp):
    B, rows, w = a.shape
    return a.reshape(B, n_exp, rows // n_exp, w)


def kernel(x_prompt, x_sample, cache_k, cache_v, c, c_ctx, mod_w, mod_b, norm1_w, norm2_w, w_in, hy_short_w, hy_short_b, hy_w1, hy_b1, hy_w2, hy_b2, hy_w3, hy_b3, hy_decay, hy_bias, q_norm_w, k_norm_w, sc_w, sc_b, w_br_a, w_br_b, w_br_c, w_o, router_w, exp_w_gate, exp_w_up, exp_w_down):
    n_dec = x_sample.shape[0]
    n_ctx = cache_k.shape[2]
    lp = x_prompt.shape[1]
    ls = x_sample.shape[1]

    cond_rows = 16
    cond = jnp.concatenate([c_ctx[None, :], c, jnp.zeros((cond_rows - 1 - n_dec, D_MODEL), F32)], axis=0)
    mod = _modulation(cond, mod_w, mod_b).reshape(DEPTH, cond_rows, 6, D_MODEL)

    tabs_p = tuple(jnp.asarray(t).astype(BF16) for t in _dft_tables(lp))
    tabs_s = tuple(jnp.asarray(t).astype(BF16) for t in _dft_tables(ls))
    ctx_k = cache_k.reshape(n_dec, DEPTH, n_ctx, ATT_KV)
    ctx_v = cache_v.reshape(n_dec, DEPTH, n_ctx, ATT_KV)
    dense = {'w_in': w_in.astype(BF16), 'w_br_a': w_br_a.astype(BF16), 'w_br_b': w_br_b.astype(BF16),
             'w_br_c': w_br_c.astype(BF16), 'w_o': w_o.astype(BF16)}

    xp, xs = x_prompt, x_sample
    ks_new, vs_new = [], []
    for l in range(DEPTH):
        p = dict(dense)
        p.update({
            'norm1_w': norm1_w[l].reshape(1, D_MODEL), 'norm2_w': norm2_w[l].reshape(1, D_MODEL),
            'hy_short_w': hy_short_w[l], 'hy_short_b': hy_short_b[l],
            'hy_w1': hy_w1[l], 'hy_b1': hy_b1[l], 'hy_w2': hy_w2[l], 'hy_b2': hy_b2[l],
            'hy_w3': hy_w3[l], 'hy_b3': hy_b3[l], 'hy_decay': hy_decay[l], 'hy_bias': hy_bias[l],
            'q_norm_w': q_norm_w[l], 'k_norm_w': k_norm_w[l], 'sc_w': sc_w[l], 'sc_b': sc_b[l],
        })
        mod_p = mod[l, 0:1]
        mod_s = mod[l, 1:1 + n_dec]
        xp_mid, h2p, k_l, v_l = _token_mixers(xp, mod_p, p, tabs_p, l)
        xs_mid, h2s, _, _ = _token_mixers(xs, mod_s, p, tabs_s, l, (ctx_k, ctx_v))
        ks_new.append(k_l.reshape(k_l.shape[0], lp, N_KV_HEADS, HEAD_DIM))
        vs_new.append(v_l.reshape(v_l.shape[0], lp, N_KV_HEADS, HEAD_DIM))

        router_wt = router_w[l].T.astype(BF16)
        gp_x, gp_d, gp_g = _dispatch(h2p, router_wt)
        gs_x, gs_d, gs_g = _dispatch(h2s, router_wt)
        yp, ys = _expert_ffn(_moe_split(gp_x, N_EXPERTS), _moe_split(gp_g, N_EXPERTS),
                             _moe_split(gs_x, N_EXPERTS), _moe_split(gs_g, N_EXPERTS),
                             exp_w_gate, exp_w_up, exp_w_down, l)
        xp = _combine(gp_d, yp.reshape(gp_x.shape), xp_mid, mod_p[:, 5:6])
        xs = _combine(gs_d, ys.reshape(gs_x.shape), xs_mid, mod_s[:, 5:6])

    return (xp, xs, jnp.stack(ks_new, axis=1), jnp.stack(vs_new, axis=1))
```

```python
import functools
import math

import numpy as np
import jax
import jax.numpy as jnp
from jax import lax
from jax.experimental import pallas as pl
from jax.experimental.pallas import tpu as pltpu

D_MODEL = 1024
DEPTH = 2
GRID_W = 64
D_HYENA = 512
HYENA_ORDER = 2
HYENA_POS_BANDS = 16
HYENA_POS_DIM = 1 + 2 * HYENA_POS_BANDS
HYENA_FILTER_HIDDEN = 64
HYENA_WINDOW_SHIFT = 0.05
N_HEADS = 8
N_KV_HEADS = 2
HEAD_DIM = 64
HEADS_PER_KV = N_HEADS // N_KV_HEADS
ATT_Q = N_HEADS * HEAD_DIM
ATT_KV = N_KV_HEADS * HEAD_DIM
ROPE_THETA = 10000.0
D_SCONV = 512
N_EXPERTS = 16
EXPERT_FF = 512
CAPACITY_FACTOR = 2
NORM_EPS = 1e-6
HY_COLS = (HYENA_ORDER + 1) * D_HYENA
ATT_COLS = ATT_Q + 2 * ATT_KV
SC_COLS = 3 * D_SCONV
GATE_COLS = 3 * D_MODEL
MAIN_COLS = HY_COLS + SC_COLS
MIX_COLS = MAIN_COLS + ATT_COLS
D_IN = MIX_COLS + GATE_COLS

F32 = jnp.float32
BF16 = jnp.bfloat16

V7X_VMEM_BYTES = 64 * 1024 * 1024
VMEM_LIMIT = V7X_VMEM_BYTES - 8 * 1024 * 1024
LANES = 128
MLP_PAD = 128
STEP_TOKENS = 1024


def _params(*sem):
    return pltpu.CompilerParams(dimension_semantics=sem, vmem_limit_bytes=VMEM_LIMIT)


def _const_spec(shape):
    nd = len(shape)
    return pl.BlockSpec(shape, lambda *_: (0,) * nd, pipeline_mode=pl.Buffered(1))


def _layer_spec(shape, layer):
    nd = len(shape)
    return pl.BlockSpec((None,) + tuple(shape), lambda *_: (layer,) + (0,) * nd, pipeline_mode=pl.Buffered(1))


def _dot(a, b):
    return jnp.dot(a, b, preferred_element_type=F32)


def _silu(x):
    return x * (1.0 / (1.0 + jnp.exp(-x)))


def _sigmoid(x):
    return 1.0 / (1.0 + jnp.exp(-x))


@functools.lru_cache(maxsize=None)
def _dft_tables(L):
    n = 2 * L
    k = np.arange(L, dtype=np.int64)
    idx = (k[:, None] * k[None, :]) % n
    ang = idx.astype(np.float64) * (2.0 * np.pi / n)
    c = np.cos(ang)
    s = np.sin(ang)
    s[0, :] = 1.0 - 2.0 * (k % 2)
    fwd = np.concatenate([c, s], axis=0).astype(np.float32)
    inv = np.concatenate([c, s.T], axis=1).astype(np.float32)
    return fwd, inv


def _folded_order(L):
    h = L // 2
    return np.concatenate([np.arange(h), [h], np.arange(L - 1, h, -1)])


@functools.lru_cache(maxsize=None)
def _split_dft_tables(L):
    h = L // 2
    k = np.arange(h, dtype=np.int64)
    alt = 1.0 - 2.0 * (k % 2)
    tabs = []
    for odd in (0, 1):
        idx = (k[:, None] * (2 * k[None, :] + odd)) % (2 * L)
        ang = idx.astype(np.float64) * (np.pi / L)
        c = np.cos(ang)
        s = np.sin(ang)
        s[0, :] = alt
        tabs.append((np.concatenate([c, s], axis=0).astype(np.float32),
                     np.concatenate([c.T, s.T], axis=1).astype(np.float32)))
    full_fwd, _ = _dft_tables(L)
    order = _folded_order(L)
    folded = np.concatenate([full_fwd[:L][order], full_fwd[L:][order]], axis=0)
    return tabs[0][0], tabs[1][0], tabs[0][1], tabs[1][1], folded


@functools.lru_cache(maxsize=None)
def _pos_features(L):
    n = np.arange(L, dtype=np.float64)
    t = n / max(L - 1, 1)
    bands = np.linspace(1e-4, HYENA_POS_BANDS - 1, HYENA_POS_BANDS)
    ang = 2.0 * math.pi * n[:, None] * bands[None, :] / L
    z = np.concatenate([t[:, None], np.cos(ang), np.sin(ang)], axis=-1)
    zp = np.zeros((L, MLP_PAD), np.float32)
    zp[:, :HYENA_POS_DIM] = z
    return zp


@functools.lru_cache(maxsize=None)
def _rope_tables(L):
    rows = L // GRID_W
    row = np.repeat(np.arange(rows, dtype=np.float64), GRID_W)
    col = np.tile(np.arange(GRID_W, dtype=np.float64), rows)
    axis_dim = HEAD_DIM // 2
    inv_freq = ROPE_THETA ** (-np.arange(0, axis_dim, 2, dtype=np.float64) / axis_dim)
    ar = row[:, None] * inv_freq[None, :]
    ac = col[:, None] * inv_freq[None, :]
    ang = np.concatenate([ar, ar, ac, ac], axis=-1)
    cos = np.cos(ang).astype(np.float32)
    sin = np.sin(ang).astype(np.float32)
    return np.tile(cos, (1, N_HEADS)), np.tile(sin, (1, N_HEADS))


@functools.lru_cache(maxsize=None)
def _head_tables():
    lane = np.arange(ATT_Q)
    block_ones = (lane[:, None] // HEAD_DIM == lane[None, :] // HEAD_DIM).astype(np.float32)
    src = np.arange(ATT_KV)
    dst = np.arange(HEADS_PER_KV * HEAD_DIM)
    rep = np.stack([(src[:, None] == g * HEAD_DIM + dst[None, :] % HEAD_DIM) for g in range(N_KV_HEADS)])
    return block_ones, rep.astype(np.float32)


@functools.lru_cache(maxsize=None)
def _prefix_table(n):
    i = np.arange(n)
    return (i[:, None] < i[None, :]).astype(np.float32)


def _mod_kernel(cond_ref, w_ref, b_ref, o_ref):
    a = _silu(cond_ref[...]).astype(BF16)
    o_ref[...] = _dot(a, w_ref[...].astype(BF16)) + b_ref[...]


def _modulation(cond, mod_w, mod_b):
    rows = cond.shape[0]
    tn = 1024
    ncols = mod_w.shape[-1]
    return pl.pallas_call(
        _mod_kernel,
        grid=(DEPTH, ncols // tn),
        in_specs=[
            pl.BlockSpec((rows, D_MODEL), lambda l, j: (0, 0)),
            pl.BlockSpec((None, D_MODEL, tn), lambda l, j: (l, 0, j)),
            pl.BlockSpec((None, 1, tn), lambda l, j: (l, 0, j)),
        ],
        out_specs=pl.BlockSpec((None, rows, tn), lambda l, j: (l, 0, j)),
        out_shape=jax.ShapeDtypeStruct((DEPTH, rows, ncols), F32),
        compiler_params=_params("parallel", "parallel"),
        name="modulation",
    )(cond, mod_w, mod_b.reshape(DEPTH, 1, ncols))


def _rms_modulate(x, norm_w, shift, scale):
    y = x * lax.rsqrt(jnp.mean(x * x, axis=-1, keepdims=True) + NORM_EPS)
    return (y * norm_w) * (1.0 + scale) + shift


def _inproj_kernel(x_ref, mod_ref, nw_ref, w_ref, main_ref, qkv_ref, h_sc):
    h = _rms_modulate(x_ref[...], nw_ref[...], mod_ref[0:1, :], mod_ref[1:2, :])
    h_sc[...] = h.astype(BF16)
    for lo in range(0, MIX_COLS, ATT_COLS):
        y = _dot(h_sc[...], w_ref[:, lo:lo + ATT_COLS])
        if lo < HY_COLS:
            main_ref[:, lo:lo + ATT_COLS] = y.astype(BF16)
        elif lo == HY_COLS:
            qkv_ref[...] = y
        else:
            main_ref[:, lo - ATT_COLS:lo] = y.astype(BF16)


def _inproj(x2d, mod, norm_w, w_in_bf, layer, rows_per_req, tm=512):
    T = x2d.shape[0]
    n_req = mod.shape[0]
    req = (lambda i: (i * tm) // rows_per_req) if n_req > 1 else (lambda i: 0)
    return pl.pallas_call(
        _inproj_kernel,
        grid=(T // tm,),
        in_specs=[
            pl.BlockSpec((tm, D_MODEL), lambda i: (i, 0)),
            pl.BlockSpec((None, 6, D_MODEL), lambda i: (req(i), 0, 0)),
            _const_spec((1, D_MODEL)),
            _layer_spec((D_MODEL, MIX_COLS), layer),
        ],
        out_specs=[pl.BlockSpec((tm, MAIN_COLS), lambda i: (i, 0)),
                   pl.BlockSpec((tm, ATT_COLS), lambda i: (i, 0))],
        out_shape=[jax.ShapeDtypeStruct((T, MAIN_COLS), BF16), jax.ShapeDtypeStruct((T, ATT_COLS), F32)],
        scratch_shapes=[pltpu.VMEM((tm, D_MODEL), BF16)],
        compiler_params=_params("parallel"),
        name="inproj",
    )(x2d, mod, norm_w, w_in_bf)


def _filter_kernel(z_ref, w1_ref, b1_ref, w2_ref, b2_ref, w3f_ref, b3f_ref, w3b_ref, b3b_ref,
                   decf_ref, decb_ref, fwd_ref, p_ref, h_sc, *, L):
    hi = lax.Precision.HIGHEST

    @pl.when((pl.program_id(0) == 0) & (pl.program_id(1) == 0))
    def _():
        h1 = jnp.sin(jnp.dot(z_ref[...], w1_ref[...], precision=hi, preferred_element_type=F32) + b1_ref[...])
        h_sc[...] = jnp.sin(jnp.dot(h1, w2_ref[...], precision=hi, preferred_element_type=F32) + b2_ref[...])

    h = h_sc[...]
    tc = w3f_ref.shape[1]
    pos = lax.broadcasted_iota(jnp.int32, (L, tc), 0)
    t = pos.astype(F32) / float(max(L - 1, 1))

    def taps(w3_ref, b3_ref, dec_ref):
        g = jnp.dot(h, w3_ref[...], precision=hi, preferred_element_type=F32) + b3_ref[...]
        return g * (jnp.exp(-t * jnp.abs(dec_ref[...])) + HYENA_WINDOW_SHIFT)

    hf = taps(w3f_ref, b3f_ref, decf_ref)
    hb = jnp.where(pos == 0, 0.0, taps(w3b_ref, b3b_ref, decb_ref))
    fwd = fwd_ref[...]
    tf = _dot(fwd, hf.astype(BF16))
    tb = _dot(fwd, hb.astype(BF16))
    k_re = tf[:L] + tb[:L]
    k_im = tb[L:] - tf[L:]
    k_ny = tf[L:] + tb[L:]
    first = pos == 0
    inv_n = 1.0 / (2 * L)
    p_ref[0] = jnp.where(first, k_re * inv_n, k_re * (2.0 * inv_n))
    p_ref[1] = jnp.where(first, 0.0, k_im * (2.0 * inv_n))
    p_ref[2] = jnp.where(first, k_ny * inv_n, k_re * (2.0 * inv_n))


def _hyena_spectra(L, w1, b1, w2, b2, w3, b3, decay, fwd_tab):
    pad_h = MLP_PAD - HYENA_FILTER_HIDDEN
    z = jnp.asarray(_pos_features(L))
    w1p = jnp.pad(w1, ((0, MLP_PAD - HYENA_POS_DIM), (0, pad_h)))
    b1p = jnp.pad(b1, (0, pad_h)).reshape(1, MLP_PAD)
    w2p = jnp.pad(w2, ((0, pad_h), (0, pad_h)))
    b2p = jnp.pad(b2, (0, pad_h)).reshape(1, MLP_PAD)
    w3p = jnp.pad(w3, ((0, pad_h), (0, 0)))
    ncol = w3.shape[1]
    b3r = b3.reshape(1, ncol)
    decr = decay.reshape(1, ncol)
    tc = 256
    nct = D_HYENA // tc
    per_dir = HYENA_ORDER * nct
    col_f = lambda o, c: (0, o * nct + c)
    col_b = lambda o, c: (0, per_dir + o * nct + c)
    return pl.pallas_call(
        functools.partial(_filter_kernel, L=L),
        grid=(HYENA_ORDER, nct),
        in_specs=[
            _const_spec((L, MLP_PAD)),
            _const_spec((MLP_PAD, MLP_PAD)), _const_spec((1, MLP_PAD)),
            _const_spec((MLP_PAD, MLP_PAD)), _const_spec((1, MLP_PAD)),
            pl.BlockSpec((MLP_PAD, tc), col_f), pl.BlockSpec((1, tc), col_f),
            pl.BlockSpec((MLP_PAD, tc), col_b), pl.BlockSpec((1, tc), col_b),
            pl.BlockSpec((1, tc), col_f), pl.BlockSpec((1, tc), col_b),
            _const_spec((2 * L, L)),
        ],
        out_specs=pl.BlockSpec((None, 3, L, tc), lambda o, c: (o, 0, 0, c)),
        out_shape=jax.ShapeDtypeStruct((HYENA_ORDER, 3, L, D_HYENA), F32),
        scratch_shapes=[pltpu.VMEM((L, MLP_PAD), F32)],
        compiler_params=_params("arbitrary", "arbitrary"),
        name="hyena_spectra",
    )(z, w1p, b1p, w2p, b2p, w3p, b3r, w3p, b3r, decr, decr, fwd_tab)


def _conv3(x, w_ref, b_ref, first, last):
    L = x.shape[0]
    prev = jnp.where(first, 0.0, pltpu.roll(x, 1, 0))
    nxt = jnp.where(last, 0.0, pltpu.roll(x, L - 1, 0))
    return prev * w_ref[0:1, :] + x * w_ref[1:2, :] + nxt * w_ref[2:3, :] + b_ref[...]


def _seqmix_kernel(v_ref, x1_ref, x2_ref, u_ref, bg_ref, cg_ref,
                   wv_ref, wx1_ref, wx2_ref, bv_ref, bx1_ref, bx2_ref, scw_ref, scb_ref,
                   p_ref, hb_ref, fwde_ref, fwdo_ref, inve_ref, invo_ref, ya_ref, yc_ref,
                   spec_sc, par_sc, *, L):
    n_req, _, tc = v_ref.shape
    half = L // 2
    n_lt = tc // LANES
    pos = lax.broadcasted_iota(jnp.int32, (L, tc), 0)
    first = pos == 0
    last = pos == L - 1
    row0 = lax.broadcasted_iota(jnp.int32, (half, tc), 0) == 0
    mult_refs = ((x1_ref, wx1_ref, bx1_ref), (x2_ref, wx2_ref, bx2_ref))

    def split_parity(r, z):
        for c in range(n_lt):
            par_sc[r, c] = z[:, c * LANES:(c + 1) * LANES]
        return tuple(jnp.concatenate([par_sc[r, c, pl.ds(par, half, stride=2), :] for c in range(n_lt)], axis=1)
                     for par in (0, 1))

    def join_parity(r, even, odd):
        for c in range(n_lt):
            par_sc[r, c, pl.ds(0, half, stride=2), :] = even[:, c * LANES:(c + 1) * LANES]
            par_sc[r, c, pl.ds(1, half, stride=2), :] = odd[:, c * LANES:(c + 1) * LANES]
        return jnp.concatenate([par_sc[r, c] for c in range(n_lt)], axis=1)

    def chain(r):
        z = _conv3(v_ref[r].astype(F32), wv_ref, bv_ref, first, last)
        z_even, z_odd = split_parity(r, z)
        yield
        for o in range(HYENA_ORDER):
            te = _dot(fwde_ref[...], z_even.astype(BF16))
            to = _dot(fwdo_ref[...], z_odd.astype(BF16))
            yield
            ce, se, co, so = te[:half], te[half:], to[:half], to[half:]
            a_lo = ce + co
            b_lo = jnp.where(row0, ce - co, se + so)
            a_up = jnp.where(row0, se, ce - co)
            b_up = jnp.where(row0, so, so - se)
            p1, p2, p4 = p_ref[o, 0], p_ref[o, 1], p_ref[o, 2]
            yr_lo = a_lo * p1[:half] + b_lo * p2[:half]
            yi_lo = b_lo * p4[:half] - a_lo * p2[:half]
            yr_up = a_up * p1[half:] + b_up * p2[half:]
            yi_up = b_up * p4[half:] - a_up * p2[half:]
            spec_sc[r, 0, 0:half, :] = jnp.where(row0, yr_lo + yi_lo, yr_lo + yr_up).astype(BF16)
            spec_sc[r, 0, half:L, :] = jnp.where(row0, yr_up, yi_lo - yi_up).astype(BF16)
            spec_sc[r, 1, 0:half, :] = jnp.where(row0, yr_lo - yi_lo, yr_lo - yr_up).astype(BF16)
            spec_sc[r, 1, half:L, :] = jnp.where(row0, yi_up, yi_lo + yi_up).astype(BF16)
            x_ref, w_ref, b_ref = mult_refs[o]
            mult = _conv3(x_ref[r].astype(F32), w_ref, b_ref, first, last)
            yield
            y_even = _dot(inve_ref[...], spec_sc[r, 0])
            y_odd = _dot(invo_ref[...], spec_sc[r, 1])
            yield
            z = mult * (join_parity(r, y_even, y_odd) + hb_ref[o:o + 1, :] * z)
            if o < HYENA_ORDER - 1:
                z_even, z_odd = split_parity(r, z)
            if o == 0:
                gated = cg_ref[r].astype(F32) * u_ref[r].astype(F32)
                yc = bg_ref[r].astype(F32) * _conv3(gated, scw_ref, scb_ref, first, last)
                yc_ref[r] = yc.astype(yc_ref.dtype)
            if o == HYENA_ORDER - 1:
                ya_ref[r] = z.astype(ya_ref.dtype)
            yield

    chains = [chain(r) for r in range(n_req)]
    n_stages = 1 + 4 * HYENA_ORDER
    for tick in range(n_stages + n_req - 1):
        for r, c in enumerate(chains):
            if 0 <= tick - r < n_stages:
                next(c)


def _seqmix(main3d, hy_short_w, hy_short_b, sc_w, sc_b, spectra, hy_bias, tabs):
    B, L, _ = main3d.shape
    tc = 256
    nct = D_HYENA // tc
    nr = max(2, STEP_TOKENS // L)
    act = lambda off: pl.BlockSpec((nr, L, tc), lambda c, b, off=off: (b, 0, off * nct + c))
    par3 = lambda off: pl.BlockSpec((3, tc), lambda c, b, off=off: (0, off * nct + c))
    par1 = lambda off: pl.BlockSpec((1, tc), lambda c, b, off=off: (0, off * nct + c))
    out = pl.BlockSpec((nr, L, tc), lambda c, b: (b, 0, c))
    hsb = hy_short_b.reshape(1, HY_COLS)
    return pl.pallas_call(
        functools.partial(_seqmix_kernel, L=L),
        grid=(nct, B // nr),
        in_specs=[
            act(0), act(1), act(2), act(3), act(4), act(5),
            par3(0), par3(1), par3(2), par1(0), par1(1), par1(2),
            par3(0), par1(0),
            pl.BlockSpec((HYENA_ORDER, 3, L, tc), lambda c, b: (0, 0, 0, c), pipeline_mode=pl.Buffered(1)),
            pl.BlockSpec((HYENA_ORDER, tc), lambda c, b: (0, c)),
            _const_spec((L, L // 2)), _const_spec((L, L // 2)), _const_spec((L // 2, L)), _const_spec((L // 2, L)),
        ],
        out_specs=[out, out],
        out_shape=[jax.ShapeDtypeStruct((B, L, D_HYENA), BF16), jax.ShapeDtypeStruct((B, L, D_SCONV), BF16)],
        scratch_shapes=[pltpu.VMEM((nr, 2, L, tc), BF16), pltpu.VMEM((nr, tc // LANES, L, LANES), F32)],
        compiler_params=_params("arbitrary", "arbitrary"),
        name="seqmix",
    )(main3d, main3d, main3d, main3d, main3d, main3d,
      hy_short_w, hy_short_w, hy_short_w, hsb, hsb, hsb, sc_w, sc_b.reshape(1, D_SCONV),
      spectra, hy_bias, *tabs[:4])


def _head_rms(x, ones_ref, w):
    sq = x * x
    hi = sq.astype(BF16)
    lo = (sq - hi.astype(F32)).astype(BF16)
    ones = ones_ref[...]
    width = x.shape[1]
    ss = _dot(hi, ones[:width, :width]) + _dot(lo, ones[:width, :width])
    return x * lax.rsqrt(ss * (1.0 / HEAD_DIM) + NORM_EPS) * w


def _rope(x, cos, sin):
    width = x.shape[1]
    lane = lax.broadcasted_iota(jnp.int32, x.shape, 1)
    half = HEAD_DIM // 4
    low = (lane % (2 * half)) < half
    rot = jnp.where(low, -pltpu.roll(x, width - half, 1), pltpu.roll(x, half, 1))
    return x * cos + rot * sin


def _attn_kernel(*refs, L, n_ctx, tq, rope):
    if rope:
        (q_ref, k_ref, v_ref, ck_ref, cv_ref, qw_ref, kw_ref, ones_ref, rep_ref, rept_ref,
         cosq_ref, sinq_ref, cosk_ref, sink_ref, yb_ref, k4_sc, v4_sc) = refs
    else:
        (q_ref, k_ref, v_ref, qw_ref, kw_ref, ones_ref, rep_ref, rept_ref,
         yb_ref, ko_ref, vo_ref, k4_sc, v4_sc) = refs

    @pl.when(pl.program_id(1) == 0)
    def _():
        kn = _head_rms(k_ref[...], ones_ref, kw_ref[...])
        v = v_ref[...]
        if rope:
            keys = _rope(kn, cosk_ref[...], sink_ref[...])
            keys = jnp.concatenate([ck_ref[...], keys], axis=0)
            vals = jnp.concatenate([cv_ref[...], v], axis=0)
        else:
            ko_ref[...] = kn
            vo_ref[...] = v
            keys, vals = kn, v
        keys = keys.astype(BF16)
        vals = vals.astype(BF16)
        for g in range(N_KV_HEADS):
            k4_sc[g] = lax.dot_general(rept_ref[g], keys, (((1,), (1,)), ((), ())),
                                       preferred_element_type=F32).astype(BF16)
            v4_sc[g] = _dot(vals, rep_ref[g]).astype(BF16)

    q = _head_rms(q_ref[...], ones_ref, qw_ref[...])
    if rope:
        q = _rope(q, cosq_ref[...], sinq_ref[...])
    q = q * (HEAD_DIM ** -0.5)
    gw = HEADS_PER_KV * HEAD_DIM
    lane = lax.broadcasted_iota(jnp.int32, (tq, gw), 1)
    for g in range(N_KV_HEADS):
        qg = q[:, g * gw:(g + 1) * gw]
        acc = jnp.zeros((tq, gw), F32)
        for h in range(HEADS_PER_KV):
            mine = (lane // HEAD_DIM) == h
            qm = jnp.where(mine, qg, 0.0).astype(BF16)
            s = _dot(qm, k4_sc[g])
            p = jnp.exp(s - jnp.max(s, axis=-1, keepdims=True))
            denom = jnp.sum(p, axis=-1, keepdims=True)
            o4 = _dot(p.astype(BF16), v4_sc[g])
            acc = jnp.where(mine, o4 * (1.0 / denom), acc)
        yb_ref[:, g * gw:(g + 1) * gw] = acc.astype(yb_ref.dtype)


def _attention(qkv3d, q_norm_w, k_norm_w, ctx_k=None, ctx_v=None, layer=0):
    B, L, _ = qkv3d.shape
    rope = ctx_k is not None
    tq = min(1024, L)
    n_ctx = ctx_k.shape[2] if rope else 0
    lk = L + n_ctx
    ones_np, rep_np = _head_tables()
    ones = jnp.asarray(ones_np).astype(BF16)
    rep = jnp.asarray(rep_np).astype(BF16)
    rept = jnp.asarray(np.swapaxes(rep_np, 1, 2)).astype(BF16)
    qw = jnp.tile(q_norm_w, N_HEADS).reshape(1, ATT_Q)
    kw = jnp.tile(k_norm_w, N_KV_HEADS).reshape(1, ATT_KV)
    kblk = ATT_Q // ATT_KV
    in_specs = [
        pl.BlockSpec((None, tq, ATT_Q), lambda b, i: (b, i, 0)),
        pl.BlockSpec((None, L, ATT_KV), lambda b, i: (b, 0, kblk)),
        pl.BlockSpec((None, L, ATT_KV), lambda b, i: (b, 0, kblk + 1)),
    ]
    args = [qkv3d, qkv3d, qkv3d]
    if rope:
        in_specs += [pl.BlockSpec((None, None, n_ctx, ATT_KV), lambda b, i: (b, layer, 0, 0))] * 2
        args += [ctx_k, ctx_v]
    in_specs += [_const_spec((1, ATT_Q)), _const_spec((1, ATT_KV)), _const_spec((ATT_Q, ATT_Q)),
                 _const_spec((N_KV_HEADS, ATT_KV, HEADS_PER_KV * HEAD_DIM)),
                 _const_spec((N_KV_HEADS, HEADS_PER_KV * HEAD_DIM, ATT_KV))]
    args += [qw, kw, ones, rep, rept]
    yb_shape = jax.ShapeDtypeStruct((B, L, ATT_Q), BF16)
    yb_spec = pl.BlockSpec((None, tq, ATT_Q), lambda b, i: (b, i, 0))
    if rope:
        cos_np, sin_np = _rope_tables(L)
        cos = jnp.asarray(cos_np)
        sin = jnp.asarray(sin_np)
        in_specs += [pl.BlockSpec((tq, ATT_Q), lambda b, i: (i, 0))] * 2
        in_specs += [_const_spec((L, ATT_KV))] * 2
        args += [cos, sin, cos[:, :ATT_KV], sin[:, :ATT_KV]]
        out_specs = yb_spec
        out_shape = yb_shape
    else:
        kv_spec = pl.BlockSpec((None, L, ATT_KV), lambda b, i: (b, 0, 0))
        kv_shape = jax.ShapeDtypeStruct((B, L, ATT_KV), F32)
        out_specs = [yb_spec, kv_spec, kv_spec]
        out_shape = [yb_shape, kv_shape, kv_shape]
    return pl.pallas_call(
        functools.partial(_attn_kernel, L=L, n_ctx=n_ctx, tq=tq, rope=rope),
        grid=(B, L // tq),
        in_specs=in_specs,
        out_specs=out_specs,
        out_shape=out_shape,
        scratch_shapes=[pltpu.VMEM((N_KV_HEADS, HEADS_PER_KV * HEAD_DIM, lk), BF16),
                        pltpu.VMEM((N_KV_HEADS, lk, HEADS_PER_KV * HEAD_DIM), BF16)],
        compiler_params=_params("parallel", "arbitrary"),
        name="attention",
    )(*args)


def _merge_kernel(ya_ref, yb_ref, yc_ref, x_ref, mod_ref, nw1_ref, nw2_ref, *rest):
    gate_w_refs = rest[:GATE_COLS // ATT_COLS]
    wa_ref, wb_ref, wc_ref, wo_ref, xo_ref, h2_ref = rest[GATE_COLS // ATT_COLS:]
    tm = x_ref.shape[0]
    n_chain = 2
    rows_per = tm // n_chain

    def chain(c):
        rows = pl.ds(c * rows_per, rows_per)
        x = x_ref[rows, :]
        h = _rms_modulate(x, nw1_ref[...], mod_ref[0:1, :], mod_ref[1:2, :]).astype(BF16)
        yield
        gates = jnp.concatenate([_dot(h, w_ref[...]) for w_ref in gate_w_refs], axis=1)
        da = _dot(ya_ref[rows, :], wa_ref[...])
        db = _dot(yb_ref[rows, :], wb_ref[...])
        dc = _dot(yc_ref[rows, :], wc_ref[...])
        yield
        merged = (_sigmoid(gates[:, 0:D_MODEL]) * da + _sigmoid(gates[:, D_MODEL:2 * D_MODEL]) * db
                  + _sigmoid(gates[:, 2 * D_MODEL:]) * dc).astype(BF16)
        yield
        proj = _dot(merged, wo_ref[...])
        yield
        x = x + mod_ref[2:3, :] * proj
        xo_ref[rows, :] = x
        h2_ref[rows, :] = _rms_modulate(x, nw2_ref[...], mod_ref[3:4, :], mod_ref[4:5, :]).astype(BF16)
        yield

    chains = [chain(c) for c in range(n_chain)]
    n_stages = 5
    for tick in range(n_stages + n_chain - 1):
        for c, ch in enumerate(chains):
            if 0 <= tick - c < n_stages:
                next(ch)


def _merge(ya, yb, yc, x2d, mod, norm1_w, norm2_w, w_in_bf, wa, wb, wc, wo, layer, rows_per_req, tm=512):
    T = x2d.shape[0]
    n_req = mod.shape[0]
    req = (lambda i: (i * tm) // rows_per_req) if n_req > 1 else (lambda i: 0)
    br = pl.BlockSpec((tm, D_HYENA), lambda i: (i, 0))
    row = pl.BlockSpec((tm, D_MODEL), lambda i: (i, 0))
    gate_blk0 = MIX_COLS // ATT_COLS
    gate_w = [pl.BlockSpec((None, D_MODEL, ATT_COLS), lambda i, k=k: (layer, 0, gate_blk0 + k),
                           pipeline_mode=pl.Buffered(1)) for k in range(GATE_COLS // ATT_COLS)]
    return pl.pallas_call(
        _merge_kernel,
        grid=(T // tm,),
        in_specs=[
            br, br, br, row,
            pl.BlockSpec((None, 6, D_MODEL), lambda i: (req(i), 0, 0)),
            _const_spec((1, D_MODEL)), _const_spec((1, D_MODEL)),
            *gate_w,
            _layer_spec((D_HYENA, D_MODEL), layer), _layer_spec((ATT_Q, D_MODEL), layer),
            _layer_spec((D_SCONV, D_MODEL), layer), _layer_spec((D_MODEL, D_MODEL), layer),
        ],
        out_specs=[row, row],
        out_shape=[jax.ShapeDtypeStruct((T, D_MODEL), F32), jax.ShapeDtypeStruct((T, D_MODEL), BF16)],
        compiler_params=_params("parallel"),
        name="merge",
    )(ya, yb, yc, x2d, mod, norm1_w, norm2_w, *([w_in_bf] * len(gate_w)), wa, wb, wc, wo)


def _route_kernel(h_ref, rw_ref, tri_ref, aff_ref, slot_ref, *, cap):
    b = pl.program_id(0)
    n_req = h_ref.shape[0]
    for r in range(n_req):
        logits = lax.dot_general(rw_ref[...], h_ref[r], (((1,), (1,)), ((), ())), preferred_element_type=F32)
        ex = jnp.exp(logits - jnp.max(logits, axis=0, keepdims=True))
        first_row = pl.multiple_of((b * n_req + r) * N_EXPERTS, N_EXPERTS)
        aff_ref[pl.ds(first_row, N_EXPERTS), :] = ex / jnp.sum(ex, axis=0, keepdims=True)

    @pl.when(b == pl.num_programs(0) - 1)
    def _():
        aff = aff_ref[...]

        def count(mask):
            return jnp.sum(jnp.where(mask, 1.0, 0.0), axis=1, keepdims=True)

        kth = jnp.zeros((aff.shape[0], 1), jnp.int32)
        for bit in range(30, -1, -1):
            trial = kth | (1 << bit)
            enough = count(aff >= lax.bitcast_convert_type(trial, F32)) >= cap
            kth = jnp.where(enough, trial, kth)
        next_up = lax.bitcast_convert_type(kth + 1, F32)
        above = aff >= next_up
        tied = (aff >= lax.bitcast_convert_type(kth, F32)) & (aff < next_up)
        tri = tri_ref[...]
        tied_before = _dot(jnp.where(tied, 1.0, 0.0).astype(BF16), tri)
        chosen = above | (tied & (tied_before < (cap - count(above))))
        slot = _dot(jnp.where(chosen, 1.0, 0.0).astype(BF16), tri)
        slot_ref[...] = jnp.where(chosen, slot, -1.0)


def _route(h3d, router_wt):
    B, n_tok, _ = h3d.shape
    cap = CAPACITY_FACTOR * n_tok // N_EXPERTS
    tri = jnp.asarray(_prefix_table(n_tok)).astype(BF16)
    whole = pl.BlockSpec((B * N_EXPERTS, n_tok), lambda b: (0, 0))
    shape = jax.ShapeDtypeStruct((B * N_EXPERTS, n_tok), F32)
    nr = _requests_per_step(n_tok)
    return pl.pallas_call(
        functools.partial(_route_kernel, cap=cap),
        grid=(B // nr,),
        in_specs=[
            pl.BlockSpec((nr, n_tok, D_MODEL), lambda b: (b, 0, 0)),
            _const_spec((N_EXPERTS, D_MODEL)),
            _const_spec((n_tok, n_tok)),
        ],
        out_specs=[whole, whole],
        out_shape=[shape, shape],
        compiler_params=_params("arbitrary"),
        name="moe_route",
    )(h3d, router_wt, tri)


def _dispatch_kernel(h_ref, aff_ref, slot_ref, xs_ref, d_ref, g_ref, *, n_tok, cap):
    row = lax.broadcasted_iota(jnp.int32, (cap, n_tok), 0).astype(F32)
    for r in range(h_ref.shape[0]):
        def one_expert(e, carry, r=r):
            idx = r * N_EXPERTS + e
            hit = row == slot_ref[pl.ds(idx, 1), :]
            base = pl.multiple_of(e * cap, cap)
            d_ref[r, pl.ds(base, cap), :] = jnp.where(hit, 1.0, 0.0).astype(BF16)
            gate = jnp.sum(jnp.where(hit, aff_ref[pl.ds(idx, 1), :], 0.0), axis=1, keepdims=True)
            g_ref[r, pl.ds(base, cap), :] = jnp.broadcast_to(gate, (cap, LANES))
            return carry

        lax.fori_loop(0, N_EXPERTS, one_expert, 0, unroll=4)
        xs_ref[r] = _dot(d_ref[r], h_ref[r]).astype(BF16)


def _requests_per_step(n_tok):
    return max(1, STEP_TOKENS // n_tok)


def _dispatch(h3d, router_wt):
    B, n_tok, _ = h3d.shape
    cap = CAPACITY_FACTOR * n_tok // N_EXPERTS
    rows = N_EXPERTS * cap
    aff, slot = _route(h3d, router_wt)
    nr = _requests_per_step(n_tok)
    per_req = pl.BlockSpec((nr * N_EXPERTS, n_tok), lambda b: (b, 0))
    blk = lambda w: pl.BlockSpec((nr, rows, w), lambda b: (b, 0, 0))
    return pl.pallas_call(
        functools.partial(_dispatch_kernel, n_tok=n_tok, cap=cap),
        grid=(B // nr,),
        in_specs=[pl.BlockSpec((nr, n_tok, D_MODEL), lambda b: (b, 0, 0)), per_req, per_req],
        out_specs=[blk(D_MODEL), blk(n_tok), blk(LANES)],
        out_shape=[jax.ShapeDtypeStruct((B, rows, D_MODEL), BF16),
                   jax.ShapeDtypeStruct((B, rows, n_tok), BF16),
                   jax.ShapeDtypeStruct((B, rows, LANES), F32)],
        compiler_params=_params("parallel"),
        name="moe_dispatch",
    )(h3d, aff, slot)


def _ffn_kernel(xp_ref, gp_ref, xs_ref, gs_ref, wg_ref, wu_ref, wd_ref, yp_ref, ys_ref):
    wg = wg_ref[...].astype(BF16)
    wu = wu_ref[...].astype(BF16)
    wd = wd_ref[...].astype(BF16)
    for x_ref, g_ref, y_ref in ((xp_ref, gp_ref, yp_ref), (xs_ref, gs_ref, ys_ref)):
        nb, cap, _ = x_ref.shape
        x = x_ref[...].reshape(nb * cap, D_MODEL)
        gate = g_ref[...].reshape(nb * cap, LANES)
        gate = jnp.concatenate([gate] * (EXPERT_FF // LANES), axis=1)
        act = _silu(_dot(x, wg)) * _dot(x, wu) * gate
        y = _dot(act.astype(BF16), wd)
        y_ref[...] = y.reshape(nb, cap, D_MODEL).astype(y_ref.dtype)


def _expert_ffn(xp, gp, xs, gs, w_gate, w_up, w_down, layer):
    def act_spec(a):
        nb, _, cap, w = a.shape
        return pl.BlockSpec((nb, None, cap, w), lambda e: (0, e, 0, 0))

    def w_spec(a):
        return pl.BlockSpec((None, None) + a.shape[2:], lambda e: (layer, e, 0, 0))

    return pl.pallas_call(
        _ffn_kernel,
        grid=(N_EXPERTS,),
        in_specs=[act_spec(xp), act_spec(gp), act_spec(xs), act_spec(gs),
                  w_spec(w_gate), w_spec(w_up), w_spec(w_down)],
        out_specs=[act_spec(xp), act_spec(xs)],
        out_shape=[jax.ShapeDtypeStruct(xp.shape, BF16), jax.ShapeDtypeStruct(xs.shape, BF16)],
        compiler_params=_params("parallel"),
        name="expert_ffn",
    )(xp, gp, xs, gs, w_gate, w_up, w_down)


def _combine_kernel(d_ref, y_ref, x_ref, mod_ref, o_ref):
    for r in range(d_ref.shape[0]):
        moe = lax.dot_general(d_ref[r], y_ref[r], (((0,), (0,)), ((), ())), preferred_element_type=F32)
        o_ref[r] = x_ref[r] + mod_ref[min(r, mod_ref.shape[0] - 1)] * moe


def _combine(dmat, y3d, x3d, mod_g2):
    B, n_tok, _ = x3d.shape
    rows = dmat.shape[1]
    nr = _requests_per_step(n_tok)
    shared_mod = mod_g2.shape[0] == 1
    mod_spec = (pl.BlockSpec((1, 1, D_MODEL), lambda b: (0, 0, 0)) if shared_mod
                else pl.BlockSpec((nr, 1, D_MODEL), lambda b: (b, 0, 0)))
    return pl.pallas_call(
        _combine_kernel,
        grid=(B // nr,),
        in_specs=[
            pl.BlockSpec((nr, rows, n_tok), lambda b: (b, 0, 0)),
            pl.BlockSpec((nr, rows, D_MODEL), lambda b: (b, 0, 0)),
            pl.BlockSpec((nr, n_tok, D_MODEL), lambda b: (b, 0, 0)),
            mod_spec,
        ],
        out_specs=pl.BlockSpec((nr, n_tok, D_MODEL), lambda b: (b, 0, 0)),
        out_shape=jax.ShapeDtypeStruct(x3d.shape, F32),
        compiler_params=_params("parallel"),
        name="moe_combine",
    )(dmat, y3d, x3d, mod_g2)


def _token_mixers(x3d, mod, p, tabs, layer, ctx=None):
    B, L, _ = x3d.shape
    x2d = x3d.reshape(B * L, D_MODEL)
    main, qkv = _inproj(x2d, mod, p['norm1_w'], p['w_in'], layer, L)
    spectra = _hyena_spectra(L, p['hy_w1'], p['hy_b1'], p['hy_w2'], p['hy_b2'], p['hy_w3'], p['hy_b3'],
                             p['hy_decay'], tabs[4])
    ya, yc = _seqmix(main.reshape(B, L, MAIN_COLS), p['hy_short_w'], p['hy_short_b'], p['sc_w'], p['sc_b'],
                     spectra, p['hy_bias'], tabs)
    qkv3d = qkv.reshape(B, L, ATT_COLS)
    if ctx is None:
        yb, k, v = _attention(qkv3d, p['q_norm_w'], p['k_norm_w'])
    else:
        yb = _attention(qkv3d, p['q_norm_w'], p['k_norm_w'], ctx[0], ctx[1], layer)
        k = v = None
    x_mid, h2 = _merge(ya.reshape(B * L, D_HYENA), yb.reshape(B * L, ATT_Q), yc.reshape(B * L, D_SCONV),
                       x2d, mod, p['norm1_w'], p['norm2_w'], p['w_in'], p['w_br_a'], p['w_br_b'], p['w_br_c'],
                       p['w_o'], layer, L)
    return x_mid.reshape(B, L, D_MODEL), h2.reshape(B, L, D_MODEL), k, v


def _moe_split(a, n_exp):
    B, rows, w = a.shape
    return a.reshape(B, n_exp, rows // n_exp, w)


def kernel(x_prompt, x_sample, cache_k, cache_v, c, c_ctx, mod_w, mod_b, norm1_w, norm2_w, w_in, hy_short_w, hy_short_b, hy_w1, hy_b1, hy_w2, hy_b2, hy_w3, hy_b3, hy_decay, hy_bias, q_norm_w, k_norm_w, sc_w, sc_b, w_br_a, w_br_b, w_br_c, w_o, router_w, exp_w_gate, exp_w_up, exp_w_down):
    n_dec = x_sample.shape[0]
    n_ctx = cache_k.shape[2]
    lp = x_prompt.shape[1]
    ls = x_sample.shape[1]

    cond_rows = 16
    cond = jnp.concatenate([c_ctx[None, :], c, jnp.zeros((cond_rows - 1 - n_dec, D_MODEL), F32)], axis=0)
    mod = _modulation(cond, mod_w, mod_b).reshape(DEPTH, cond_rows, 6, D_MODEL)

    tabs_p = tuple(jnp.asarray(t).astype(BF16) for t in _split_dft_tables(lp))
    tabs_s = tuple(jnp.asarray(t).astype(BF16) for t in _split_dft_tables(ls))
    ctx_k = cache_k.reshape(n_dec, DEPTH, n_ctx, ATT_KV)
    ctx_v = cache_v.reshape(n_dec, DEPTH, n_ctx, ATT_KV)
    dense = {'w_in': w_in.astype(BF16), 'w_br_a': w_br_a.astype(BF16), 'w_br_b': w_br_b.astype(BF16),
             'w_br_c': w_br_c.astype(BF16), 'w_o': w_o.astype(BF16)}

    xp, xs = x_prompt, x_sample
    ks_new, vs_new = [], []
    for l in range(DEPTH):
        p = dict(dense)
        p.update({
            'norm1_w': norm1_w[l].reshape(1, D_MODEL), 'norm2_w': norm2_w[l].reshape(1, D_MODEL),
            'hy_short_w': hy_short_w[l], 'hy_short_b': hy_short_b[l],
            'hy_w1': hy_w1[l], 'hy_b1': hy_b1[l], 'hy_w2': hy_w2[l], 'hy_b2': hy_b2[l],
            'hy_w3': hy_w3[l], 'hy_b3': hy_b3[l], 'hy_decay': hy_decay[l], 'hy_bias': hy_bias[l],
            'q_norm_w': q_norm_w[l], 'k_norm_w': k_norm_w[l], 'sc_w': sc_w[l], 'sc_b': sc_b[l],
        })
        mod_p = mod[l, 0:1]
        mod_s = mod[l, 1:1 + n_dec]
        xp_mid, h2p, k_l, v_l = _token_mixers(xp, mod_p, p, tabs_p, l)
        xs_mid, h2s, _, _ = _token_mixers(xs, mod_s, p, tabs_s, l, (ctx_k, ctx_v))
        ks_new.append(k_l.reshape(k_l.shape[0], lp, N_KV_HEADS, HEAD_DIM))
        vs_new.append(v_l.reshape(v_l.shape[0], lp, N_KV_HEADS, HEAD_DIM))

        router_wt = router_w[l].T.astype(BF16)
        gp_x, gp_d, gp_g = _dispatch(h2p, router_wt)
        gs_x, gs_d, gs_g = _dispatch(h2s, router_wt)
        yp, ys = _expert_ffn(_moe_split(gp_x, N_EXPERTS), _moe_split(gp_g, N_EXPERTS),
                             _moe_split(gs_x, N_EXPERTS), _moe_split(gs_g, N_EXPERTS),
                             exp_w_gate, exp_w_up, exp_w_down, l)
        xp = _combine(gp_d, yp.reshape(gp_x.shape), xp_mid, mod_p[:, 5:6])
        xs = _combine(gs_d, ys.reshape(gs_x.shape), xs_mid, mod_s[:, 5:6])

    return (xp, xs, jnp.stack(ks_new, axis=1), jnp.stack(vs_new, axis=1))
```

```python
import functools
import math

import numpy as np
import jax
import jax.numpy as jnp
from jax import lax
from jax.experimental import pallas as pl
from jax.experimental.pallas import tpu as pltpu

D_MODEL = 1024
DEPTH = 2
GRID_W = 64
D_HYENA = 512
HYENA_ORDER = 2
HYENA_POS_BANDS = 16
HYENA_POS_DIM = 1 + 2 * HYENA_POS_BANDS
HYENA_FILTER_HIDDEN = 64
HYENA_WINDOW_SHIFT = 0.05
N_HEADS = 8
N_KV_HEADS = 2
HEAD_DIM = 64
HEADS_PER_KV = N_HEADS // N_KV_HEADS
ATT_Q = N_HEADS * HEAD_DIM
ATT_KV = N_KV_HEADS * HEAD_DIM
ROPE_THETA = 10000.0
D_SCONV = 512
N_EXPERTS = 16
EXPERT_FF = 512
CAPACITY_FACTOR = 2
NORM_EPS = 1e-6
HY_COLS = (HYENA_ORDER + 1) * D_HYENA
ATT_COLS = ATT_Q + 2 * ATT_KV
SC_COLS = 3 * D_SCONV
GATE_COLS = 3 * D_MODEL
MAIN_COLS = HY_COLS + SC_COLS
MIX_COLS = MAIN_COLS + ATT_COLS
D_IN = MIX_COLS + GATE_COLS

F32 = jnp.float32
BF16 = jnp.bfloat16

V7X_VMEM_BYTES = 64 * 1024 * 1024
VMEM_LIMIT = V7X_VMEM_BYTES - 8 * 1024 * 1024
LANES = 128
MLP_PAD = 128
STEP_TOKENS = 1024


def _params(*sem):
    return pltpu.CompilerParams(dimension_semantics=sem, vmem_limit_bytes=VMEM_LIMIT)


def _const_spec(shape):
    nd = len(shape)
    return pl.BlockSpec(shape, lambda *_: (0,) * nd, pipeline_mode=pl.Buffered(1))


def _layer_spec(shape, layer):
    nd = len(shape)
    return pl.BlockSpec((None,) + tuple(shape), lambda *_: (layer,) + (0,) * nd, pipeline_mode=pl.Buffered(1))


def _dot(a, b):
    return jnp.dot(a, b, preferred_element_type=F32)


def _silu(x):
    return x * (1.0 / (1.0 + jnp.exp(-x)))


def _sigmoid(x):
    return 1.0 / (1.0 + jnp.exp(-x))


@functools.lru_cache(maxsize=None)
def _dft_tables(L):
    n = 2 * L
    k = np.arange(L, dtype=np.int64)
    idx = (k[:, None] * k[None, :]) % n
    ang = idx.astype(np.float64) * (2.0 * np.pi / n)
    c = np.cos(ang)
    s = np.sin(ang)
    s[0, :] = 1.0 - 2.0 * (k % 2)
    fwd = np.concatenate([c, s], axis=0).astype(np.float32)
    inv = np.concatenate([c, s.T], axis=1).astype(np.float32)
    return fwd, inv


def _folded_order(L):
    h = L // 2
    return np.concatenate([np.arange(h), [h], np.arange(L - 1, h, -1)])


@functools.lru_cache(maxsize=None)
def _split_dft_tables(L):
    h = L // 2
    k = np.arange(h, dtype=np.int64)
    alt = 1.0 - 2.0 * (k % 2)
    tabs = []
    for odd in (0, 1):
        idx = (k[:, None] * (2 * k[None, :] + odd)) % (2 * L)
        ang = idx.astype(np.float64) * (np.pi / L)
        c = np.cos(ang)
        s = np.sin(ang)
        s[0, :] = alt
        tabs.append((np.concatenate([c, s], axis=0).astype(np.float32),
                     np.concatenate([c.T, s.T], axis=1).astype(np.float32)))
    full_fwd, _ = _dft_tables(L)
    order = _folded_order(L)
    folded = np.concatenate([full_fwd[:L][order], full_fwd[L:][order]], axis=0)
    return tabs[0][0], tabs[1][0], tabs[0][1], tabs[1][1], folded


@functools.lru_cache(maxsize=None)
def _pos_features(L):
    n = np.arange(L, dtype=np.float64)
    t = n / max(L - 1, 1)
    bands = np.linspace(1e-4, HYENA_POS_BANDS - 1, HYENA_POS_BANDS)
    ang = 2.0 * math.pi * n[:, None] * bands[None, :] / L
    z = np.concatenate([t[:, None], np.cos(ang), np.sin(ang)], axis=-1)
    zp = np.zeros((L, MLP_PAD), np.float32)
    zp[:, :HYENA_POS_DIM] = z
    return zp


@functools.lru_cache(maxsize=None)
def _rope_tables(L):
    rows = L // GRID_W
    row = np.repeat(np.arange(rows, dtype=np.float64), GRID_W)
    col = np.tile(np.arange(GRID_W, dtype=np.float64), rows)
    axis_dim = HEAD_DIM // 2
    inv_freq = ROPE_THETA ** (-np.arange(0, axis_dim, 2, dtype=np.float64) / axis_dim)
    ar = row[:, None] * inv_freq[None, :]
    ac = col[:, None] * inv_freq[None, :]
    ang = np.concatenate([ar, ar, ac, ac], axis=-1)
    cos = np.cos(ang).astype(np.float32)
    sin = np.sin(ang).astype(np.float32)
    return np.tile(cos, (1, N_HEADS)), np.tile(sin, (1, N_HEADS))


@functools.lru_cache(maxsize=None)
def _head_tables():
    lane = np.arange(ATT_Q)
    block_ones = (lane[:, None] // HEAD_DIM == lane[None, :] // HEAD_DIM).astype(np.float32)
    src = np.arange(ATT_KV)
    dst = np.arange(HEADS_PER_KV * HEAD_DIM)
    rep = np.stack([(src[:, None] == g * HEAD_DIM + dst[None, :] % HEAD_DIM) for g in range(N_KV_HEADS)])
    return block_ones, rep.astype(np.float32)


@functools.lru_cache(maxsize=None)
def _prefix_table(n):
    i = np.arange(n)
    return (i[:, None] < i[None, :]).astype(np.float32)


def _mod_kernel(cond_ref, w_ref, b_ref, o_ref):
    a = _silu(cond_ref[...]).astype(BF16)
    o_ref[...] = _dot(a, w_ref[...].astype(BF16)) + b_ref[...]


def _modulation(cond, mod_w, mod_b):
    rows = cond.shape[0]
    tn = 1024
    ncols = mod_w.shape[-1]
    return pl.pallas_call(
        _mod_kernel,
        grid=(DEPTH, ncols // tn),
        in_specs=[
            pl.BlockSpec((rows, D_MODEL), lambda l, j: (0, 0)),
            pl.BlockSpec((None, D_MODEL, tn), lambda l, j: (l, 0, j)),
            pl.BlockSpec((None, 1, tn), lambda l, j: (l, 0, j)),
        ],
        out_specs=pl.BlockSpec((None, rows, tn), lambda l, j: (l, 0, j)),
        out_shape=jax.ShapeDtypeStruct((DEPTH, rows, ncols), F32),
        compiler_params=_params("parallel", "parallel"),
        name="modulation",
    )(cond, mod_w, mod_b.reshape(DEPTH, 1, ncols))


def _rms_modulate(x, norm_w, shift, scale):
    y = x * lax.rsqrt(jnp.mean(x * x, axis=-1, keepdims=True) + NORM_EPS)
    return (y * norm_w) * (1.0 + scale) + shift


def _inproj_kernel(x_ref, mod_ref, nw_ref, w_ref, main_ref, qkv_ref, h_sc):
    h = _rms_modulate(x_ref[...], nw_ref[...], mod_ref[0:1, :], mod_ref[1:2, :])
    h_sc[...] = h.astype(BF16)
    for lo in range(0, MIX_COLS, ATT_COLS):
        y = _dot(h_sc[...], w_ref[:, lo:lo + ATT_COLS])
        if lo < HY_COLS:
            main_ref[:, lo:lo + ATT_COLS] = y.astype(BF16)
        elif lo == HY_COLS:
            qkv_ref[...] = y
        else:
            main_ref[:, lo - ATT_COLS:lo] = y.astype(BF16)


def _inproj(x2d, mod, norm_w, w_in_bf, layer, rows_per_req, tm=1024):
    T = x2d.shape[0]
    n_req = mod.shape[0]
    req = (lambda i: (i * tm) // rows_per_req) if n_req > 1 else (lambda i: 0)
    return pl.pallas_call(
        _inproj_kernel,
        grid=(T // tm,),
        in_specs=[
            pl.BlockSpec((tm, D_MODEL), lambda i: (i, 0)),
            pl.BlockSpec((None, 6, D_MODEL), lambda i: (req(i), 0, 0)),
            _const_spec((1, D_MODEL)),
            _layer_spec((D_MODEL, MIX_COLS), layer),
        ],
        out_specs=[pl.BlockSpec((tm, MAIN_COLS), lambda i: (i, 0)),
                   pl.BlockSpec((tm, ATT_COLS), lambda i: (i, 0))],
        out_shape=[jax.ShapeDtypeStruct((T, MAIN_COLS), BF16), jax.ShapeDtypeStruct((T, ATT_COLS), F32)],
        scratch_shapes=[pltpu.VMEM((tm, D_MODEL), BF16)],
        compiler_params=_params("parallel"),
        name="inproj",
    )(x2d, mod, norm_w, w_in_bf)


def _filter_kernel(z_ref, w1_ref, b1_ref, w2_ref, b2_ref, w3f_ref, b3f_ref, w3b_ref, b3b_ref,
                   decf_ref, decb_ref, fwd_ref, p_ref, h_sc, *, L):
    hi = lax.Precision.HIGHEST

    @pl.when((pl.program_id(0) == 0) & (pl.program_id(1) == 0))
    def _():
        h1 = jnp.sin(jnp.dot(z_ref[...], w1_ref[...], precision=hi, preferred_element_type=F32) + b1_ref[...])
        h_sc[...] = jnp.sin(jnp.dot(h1, w2_ref[...], precision=hi, preferred_element_type=F32) + b2_ref[...])

    h = h_sc[...]
    tc = w3f_ref.shape[1]
    pos = lax.broadcasted_iota(jnp.int32, (L, tc), 0)
    t = pos.astype(F32) / float(max(L - 1, 1))

    def taps(w3_ref, b3_ref, dec_ref):
        g = jnp.dot(h, w3_ref[...], precision=hi, preferred_element_type=F32) + b3_ref[...]
        return g * (jnp.exp(-t * jnp.abs(dec_ref[...])) + HYENA_WINDOW_SHIFT)

    hf = taps(w3f_ref, b3f_ref, decf_ref)
    hb = jnp.where(pos == 0, 0.0, taps(w3b_ref, b3b_ref, decb_ref))
    fwd = fwd_ref[...]
    tf = _dot(fwd, hf.astype(BF16))
    tb = _dot(fwd, hb.astype(BF16))
    k_re = tf[:L] + tb[:L]
    k_im = tb[L:] - tf[L:]
    k_ny = tf[L:] + tb[L:]
    first = pos == 0
    inv_n = 1.0 / (2 * L)
    p_ref[0] = jnp.where(first, k_re * inv_n, k_re * (2.0 * inv_n))
    p_ref[1] = jnp.where(first, 0.0, k_im * (2.0 * inv_n))
    p_ref[2] = jnp.where(first, k_ny * inv_n, k_re * (2.0 * inv_n))


def _hyena_spectra(L, w1, b1, w2, b2, w3, b3, decay, fwd_tab):
    pad_h = MLP_PAD - HYENA_FILTER_HIDDEN
    z = jnp.asarray(_pos_features(L))
    w1p = jnp.pad(w1, ((0, MLP_PAD - HYENA_POS_DIM), (0, pad_h)))
    b1p = jnp.pad(b1, (0, pad_h)).reshape(1, MLP_PAD)
    w2p = jnp.pad(w2, ((0, pad_h), (0, pad_h)))
    b2p = jnp.pad(b2, (0, pad_h)).reshape(1, MLP_PAD)
    w3p = jnp.pad(w3, ((0, pad_h), (0, 0)))
    ncol = w3.shape[1]
    b3r = b3.reshape(1, ncol)
    decr = decay.reshape(1, ncol)
    tc = 256
    nct = D_HYENA // tc
    per_dir = HYENA_ORDER * nct
    col_f = lambda o, c: (0, o * nct + c)
    col_b = lambda o, c: (0, per_dir + o * nct + c)
    return pl.pallas_call(
        functools.partial(_filter_kernel, L=L),
        grid=(HYENA_ORDER, nct),
        in_specs=[
            _const_spec((L, MLP_PAD)),
            _const_spec((MLP_PAD, MLP_PAD)), _const_spec((1, MLP_PAD)),
            _const_spec((MLP_PAD, MLP_PAD)), _const_spec((1, MLP_PAD)),
            pl.BlockSpec((MLP_PAD, tc), col_f), pl.BlockSpec((1, tc), col_f),
            pl.BlockSpec((MLP_PAD, tc), col_b), pl.BlockSpec((1, tc), col_b),
            pl.BlockSpec((1, tc), col_f), pl.BlockSpec((1, tc), col_b),
            _const_spec((2 * L, L)),
        ],
        out_specs=pl.BlockSpec((None, 3, L, tc), lambda o, c: (o, 0, 0, c)),
        out_shape=jax.ShapeDtypeStruct((HYENA_ORDER, 3, L, D_HYENA), F32),
        scratch_shapes=[pltpu.VMEM((L, MLP_PAD), F32)],
        compiler_params=_params("arbitrary", "arbitrary"),
        name="hyena_spectra",
    )(z, w1p, b1p, w2p, b2p, w3p, b3r, w3p, b3r, decr, decr, fwd_tab)


def _conv3(x, w_ref, b_ref, first, last):
    L = x.shape[0]
    prev = jnp.where(first, 0.0, pltpu.roll(x, 1, 0))
    nxt = jnp.where(last, 0.0, pltpu.roll(x, L - 1, 0))
    return prev * w_ref[0:1, :] + x * w_ref[1:2, :] + nxt * w_ref[2:3, :] + b_ref[...]


def _seqmix_kernel(v_ref, x1_ref, x2_ref, u_ref, bg_ref, cg_ref,
                   wv_ref, wx1_ref, wx2_ref, bv_ref, bx1_ref, bx2_ref, scw_ref, scb_ref,
                   p_ref, hb_ref, fwde_ref, fwdo_ref, inve_ref, invo_ref, ya_ref, yc_ref,
                   spec_sc, par_sc, *, L):
    n_req, _, tc = v_ref.shape
    half = L // 2
    n_lt = tc // LANES
    pos = lax.broadcasted_iota(jnp.int32, (L, tc), 0)
    first = pos == 0
    last = pos == L - 1
    row0 = lax.broadcasted_iota(jnp.int32, (half, tc), 0) == 0
    mult_refs = ((x1_ref, wx1_ref, bx1_ref), (x2_ref, wx2_ref, bx2_ref))

    def split_parity(r, z):
        for c in range(n_lt):
            par_sc[r, c] = z[:, c * LANES:(c + 1) * LANES]
        return tuple(jnp.concatenate([par_sc[r, c, pl.ds(par, half, stride=2), :] for c in range(n_lt)], axis=1)
                     for par in (0, 1))

    def join_parity(r, even, odd):
        for c in range(n_lt):
            par_sc[r, c, pl.ds(0, half, stride=2), :] = even[:, c * LANES:(c + 1) * LANES]
            par_sc[r, c, pl.ds(1, half, stride=2), :] = odd[:, c * LANES:(c + 1) * LANES]
        return jnp.concatenate([par_sc[r, c] for c in range(n_lt)], axis=1)

    def chain(r):
        z = _conv3(v_ref[r].astype(F32), wv_ref, bv_ref, first, last)
        z_even, z_odd = split_parity(r, z)
        yield
        for o in range(HYENA_ORDER):
            te = _dot(fwde_ref[...], z_even.astype(BF16))
            to = _dot(fwdo_ref[...], z_odd.astype(BF16))
            yield
            ce, se, co, so = te[:half], te[half:], to[:half], to[half:]
            a_lo = ce + co
            b_lo = jnp.where(row0, ce - co, se + so)
            a_up = jnp.where(row0, se, ce - co)
            b_up = jnp.where(row0, so, so - se)
            p1, p2, p4 = p_ref[o, 0], p_ref[o, 1], p_ref[o, 2]
            yr_lo = a_lo * p1[:half] + b_lo * p2[:half]
            yi_lo = b_lo * p4[:half] - a_lo * p2[:half]
            yr_up = a_up * p1[half:] + b_up * p2[half:]
            yi_up = b_up * p4[half:] - a_up * p2[half:]
            spec_sc[r, 0, 0:half, :] = jnp.where(row0, yr_lo + yi_lo, yr_lo + yr_up).astype(BF16)
            spec_sc[r, 0, half:L, :] = jnp.where(row0, yr_up, yi_lo - yi_up).astype(BF16)
            spec_sc[r, 1, 0:half, :] = jnp.where(row0, yr_lo - yi_lo, yr_lo - yr_up).astype(BF16)
            spec_sc[r, 1, half:L, :] = jnp.where(row0, yi_up, yi_lo + yi_up).astype(BF16)
            x_ref, w_ref, b_ref = mult_refs[o]
            mult = _conv3(x_ref[r].astype(F32), w_ref, b_ref, first, last)
            yield
            y_even = _dot(inve_ref[...], spec_sc[r, 0])
            y_odd = _dot(invo_ref[...], spec_sc[r, 1])
            yield
            z = mult * (join_parity(r, y_even, y_odd) + hb_ref[o:o + 1, :] * z)
            if o < HYENA_ORDER - 1:
                z_even, z_odd = split_parity(r, z)
            if o == 0:
                gated = cg_ref[r].astype(F32) * u_ref[r].astype(F32)
                yc = bg_ref[r].astype(F32) * _conv3(gated, scw_ref, scb_ref, first, last)
                yc_ref[r] = yc.astype(yc_ref.dtype)
            if o == HYENA_ORDER - 1:
                ya_ref[r] = z.astype(ya_ref.dtype)
            yield

    chains = [chain(r) for r in range(n_req)]
    n_stages = 1 + 4 * HYENA_ORDER
    for tick in range(n_stages + n_req - 1):
        for r, c in enumerate(chains):
            if 0 <= tick - r < n_stages:
                next(c)


def _seqmix(main3d, hy_short_w, hy_short_b, sc_w, sc_b, spectra, hy_bias, tabs):
    B, L, _ = main3d.shape
    tc = 256
    nct = D_HYENA // tc
    nr = max(2, STEP_TOKENS // L)
    act = lambda off: pl.BlockSpec((nr, L, tc), lambda c, b, off=off: (b, 0, off * nct + c))
    par3 = lambda off: pl.BlockSpec((3, tc), lambda c, b, off=off: (0, off * nct + c))
    par1 = lambda off: pl.BlockSpec((1, tc), lambda c, b, off=off: (0, off * nct + c))
    out = pl.BlockSpec((nr, L, tc), lambda c, b: (b, 0, c))
    hsb = hy_short_b.reshape(1, HY_COLS)
    return pl.pallas_call(
        functools.partial(_seqmix_kernel, L=L),
        grid=(nct, B // nr),
        in_specs=[
            act(0), act(1), act(2), act(3), act(4), act(5),
            par3(0), par3(1), par3(2), par1(0), par1(1), par1(2),
            par3(0), par1(0),
            pl.BlockSpec((HYENA_ORDER, 3, L, tc), lambda c, b: (0, 0, 0, c), pipeline_mode=pl.Buffered(1)),
            pl.BlockSpec((HYENA_ORDER, tc), lambda c, b: (0, c)),
            _const_spec((L, L // 2)), _const_spec((L, L // 2)), _const_spec((L // 2, L)), _const_spec((L // 2, L)),
        ],
        out_specs=[out, out],
        out_shape=[jax.ShapeDtypeStruct((B, L, D_HYENA), BF16), jax.ShapeDtypeStruct((B, L, D_SCONV), BF16)],
        scratch_shapes=[pltpu.VMEM((nr, 2, L, tc), BF16), pltpu.VMEM((nr, tc // LANES, L, LANES), F32)],
        compiler_params=_params("arbitrary", "arbitrary"),
        name="seqmix",
    )(main3d, main3d, main3d, main3d, main3d, main3d,
      hy_short_w, hy_short_w, hy_short_w, hsb, hsb, hsb, sc_w, sc_b.reshape(1, D_SCONV),
      spectra, hy_bias, *tabs[:4])


def _head_rms(x, ones_ref, w):
    sq = x * x
    hi = sq.astype(BF16)
    lo = (sq - hi.astype(F32)).astype(BF16)
    ones = ones_ref[...]
    width = x.shape[1]
    ss = _dot(hi, ones[:width, :width]) + _dot(lo, ones[:width, :width])
    return x * lax.rsqrt(ss * (1.0 / HEAD_DIM) + NORM_EPS) * w


def _rope(x, cos, sin):
    width = x.shape[1]
    lane = lax.broadcasted_iota(jnp.int32, x.shape, 1)
    half = HEAD_DIM // 4
    low = (lane % (2 * half)) < half
    rot = jnp.where(low, -pltpu.roll(x, width - half, 1), pltpu.roll(x, half, 1))
    return x * cos + rot * sin


def _attn_kernel(*refs, rope):
    if rope:
        (q_ref, k_ref, v_ref, ck_ref, cv_ref, qw_ref, kw_ref, ones_ref, rep_ref, rept_ref,
         cos_ref, sin_ref, yb_ref) = refs
    else:
        q_ref, k_ref, v_ref, qw_ref, kw_ref, ones_ref, rep_ref, rept_ref, yb_ref, ko_ref, vo_ref = refs
    n_req, L, _ = q_ref.shape
    gw = HEADS_PER_KV * HEAD_DIM
    lane = lax.broadcasted_iota(jnp.int32, (L, gw), 1)
    for r in range(n_req):
        kn = _head_rms(k_ref[r], ones_ref, kw_ref[...])
        v = v_ref[r]
        q = _head_rms(q_ref[r], ones_ref, qw_ref[...])
        if rope:
            keys = _rope(kn, cos_ref[:, :ATT_KV], sin_ref[:, :ATT_KV])
            keys = jnp.concatenate([ck_ref[r], keys], axis=0)
            vals = jnp.concatenate([cv_ref[r], v], axis=0)
            q = _rope(q, cos_ref[...], sin_ref[...])
        else:
            ko_ref[r] = kn
            vo_ref[r] = v
            keys, vals = kn, v
        keys = keys.astype(BF16)
        vals = vals.astype(BF16)
        q = q * (HEAD_DIM ** -0.5)
        for g in range(N_KV_HEADS):
            k4 = lax.dot_general(rept_ref[g], keys, (((1,), (1,)), ((), ())), preferred_element_type=F32).astype(BF16)
            v4 = _dot(vals, rep_ref[g]).astype(BF16)
            qg = q[:, g * gw:(g + 1) * gw]
            acc = jnp.zeros((L, gw), F32)
            for h in range(HEADS_PER_KV):
                mine = (lane // HEAD_DIM) == h
                qm = jnp.where(mine, qg, 0.0).astype(BF16)
                s = _dot(qm, k4)
                p = jnp.exp(s - jnp.max(s, axis=-1, keepdims=True))
                denom = jnp.sum(p, axis=-1, keepdims=True)
                o4 = _dot(p.astype(BF16), v4)
                acc = jnp.where(mine, o4 * (1.0 / denom), acc)
            yb_ref[r, :, g * gw:(g + 1) * gw] = acc.astype(yb_ref.dtype)


def _attention(qkv3d, q_norm_w, k_norm_w, ctx_k=None, ctx_v=None, layer=0):
    B, L, _ = qkv3d.shape
    rope = ctx_k is not None
    nr = _requests_per_step(L)
    ones_np, rep_np = _head_tables()
    ones = jnp.asarray(ones_np).astype(BF16)
    rep = jnp.asarray(rep_np).astype(BF16)
    rept = jnp.asarray(np.swapaxes(rep_np, 1, 2)).astype(BF16)
    qw = jnp.tile(q_norm_w, N_HEADS).reshape(1, ATT_Q)
    kw = jnp.tile(k_norm_w, N_KV_HEADS).reshape(1, ATT_KV)
    kblk = ATT_Q // ATT_KV
    in_specs = [
        pl.BlockSpec((nr, L, ATT_Q), lambda b: (b, 0, 0)),
        pl.BlockSpec((nr, L, ATT_KV), lambda b: (b, 0, kblk)),
        pl.BlockSpec((nr, L, ATT_KV), lambda b: (b, 0, kblk + 1)),
    ]
    args = [qkv3d, qkv3d, qkv3d]
    if rope:
        n_ctx = ctx_k.shape[2]
        in_specs += [pl.BlockSpec((nr, None, n_ctx, ATT_KV), lambda b: (b, layer, 0, 0))] * 2
        args += [ctx_k, ctx_v]
    in_specs += [_const_spec((1, ATT_Q)), _const_spec((1, ATT_KV)), _const_spec((ATT_Q, ATT_Q)),
                 _const_spec((N_KV_HEADS, ATT_KV, HEADS_PER_KV * HEAD_DIM)),
                 _const_spec((N_KV_HEADS, HEADS_PER_KV * HEAD_DIM, ATT_KV))]
    args += [qw, kw, ones, rep, rept]
    yb_shape = jax.ShapeDtypeStruct((B, L, ATT_Q), BF16)
    yb_spec = pl.BlockSpec((nr, L, ATT_Q), lambda b: (b, 0, 0))
    if rope:
        cos_np, sin_np = _rope_tables(L)
        in_specs += [_const_spec((L, ATT_Q))] * 2
        args += [jnp.asarray(cos_np), jnp.asarray(sin_np)]
        out_specs = yb_spec
        out_shape = yb_shape
    else:
        kv_spec = pl.BlockSpec((nr, L, ATT_KV), lambda b: (b, 0, 0))
        kv_shape = jax.ShapeDtypeStruct((B, L, ATT_KV), F32)
        out_specs = [yb_spec, kv_spec, kv_spec]
        out_shape = [yb_shape, kv_shape, kv_shape]
    return pl.pallas_call(
        functools.partial(_attn_kernel, rope=rope),
        grid=(B // nr,),
        in_specs=in_specs,
        out_specs=out_specs,
        out_shape=out_shape,
        compiler_params=_params("parallel"),
        name="attention",
    )(*args)


def _merge_kernel(ya_ref, yb_ref, yc_ref, x_ref, mod_ref, nw1_ref, nw2_ref, *rest):
    gate_w_refs = rest[:GATE_COLS // ATT_COLS]
    wa_ref, wb_ref, wc_ref, wo_ref, xo_ref, h2_ref = rest[GATE_COLS // ATT_COLS:]
    tm = x_ref.shape[0]
    n_chain = 2
    rows_per = tm // n_chain

    def chain(c):
        rows = pl.ds(c * rows_per, rows_per)
        x = x_ref[rows, :]
        h = _rms_modulate(x, nw1_ref[...], mod_ref[0:1, :], mod_ref[1:2, :]).astype(BF16)
        yield
        gates = jnp.concatenate([_dot(h, w_ref[...]) for w_ref in gate_w_refs], axis=1)
        da = _dot(ya_ref[rows, :], wa_ref[...])
        db = _dot(yb_ref[rows, :], wb_ref[...])
        dc = _dot(yc_ref[rows, :], wc_ref[...])
        yield
        merged = (_sigmoid(gates[:, 0:D_MODEL]) * da + _sigmoid(gates[:, D_MODEL:2 * D_MODEL]) * db
                  + _sigmoid(gates[:, 2 * D_MODEL:]) * dc).astype(BF16)
        yield
        proj = _dot(merged, wo_ref[...])
        yield
        x = x + mod_ref[2:3, :] * proj
        xo_ref[rows, :] = x
        h2_ref[rows, :] = _rms_modulate(x, nw2_ref[...], mod_ref[3:4, :], mod_ref[4:5, :]).astype(BF16)
        yield

    chains = [chain(c) for c in range(n_chain)]
    n_stages = 5
    for tick in range(n_stages + n_chain - 1):
        for c, ch in enumerate(chains):
            if 0 <= tick - c < n_stages:
                next(ch)


def _merge(ya, yb, yc, x2d, mod, norm1_w, norm2_w, w_in_bf, wa, wb, wc, wo, layer, rows_per_req, tm=512):
    T = x2d.shape[0]
    n_req = mod.shape[0]
    req = (lambda i: (i * tm) // rows_per_req) if n_req > 1 else (lambda i: 0)
    br = pl.BlockSpec((tm, D_HYENA), lambda i: (i, 0))
    row = pl.BlockSpec((tm, D_MODEL), lambda i: (i, 0))
    gate_blk0 = MIX_COLS // ATT_COLS
    gate_w = [pl.BlockSpec((None, D_MODEL, ATT_COLS), lambda i, k=k: (layer, 0, gate_blk0 + k),
                           pipeline_mode=pl.Buffered(1)) for k in range(GATE_COLS // ATT_COLS)]
    return pl.pallas_call(
        _merge_kernel,
        grid=(T // tm,),
        in_specs=[
            br, br, br, row,
            pl.BlockSpec((None, 6, D_MODEL), lambda i: (req(i), 0, 0)),
            _const_spec((1, D_MODEL)), _const_spec((1, D_MODEL)),
            *gate_w,
            _layer_spec((D_HYENA, D_MODEL), layer), _layer_spec((ATT_Q, D_MODEL), layer),
            _layer_spec((D_SCONV, D_MODEL), layer), _layer_spec((D_MODEL, D_MODEL), layer),
        ],
        out_specs=[row, row],
        out_shape=[jax.ShapeDtypeStruct((T, D_MODEL), F32), jax.ShapeDtypeStruct((T, D_MODEL), BF16)],
        compiler_params=_params("parallel"),
        name="merge",
    )(ya, yb, yc, x2d, mod, norm1_w, norm2_w, *([w_in_bf] * len(gate_w)), wa, wb, wc, wo)


def _route_kernel(h_ref, rw_ref, tri_ref, aff_ref, slot_ref, *, cap):
    b = pl.program_id(0)
    n_req = h_ref.shape[0]
    for r in range(n_req):
        logits = lax.dot_general(rw_ref[...], h_ref[r], (((1,), (1,)), ((), ())), preferred_element_type=F32)
        ex = jnp.exp(logits - jnp.max(logits, axis=0, keepdims=True))
        first_row = pl.multiple_of((b * n_req + r) * N_EXPERTS, N_EXPERTS)
        aff_ref[pl.ds(first_row, N_EXPERTS), :] = ex / jnp.sum(ex, axis=0, keepdims=True)

    @pl.when(b == pl.num_programs(0) - 1)
    def _():
        aff = aff_ref[...]

        def count(mask):
            return jnp.sum(jnp.where(mask, 1.0, 0.0), axis=1, keepdims=True)

        kth = jnp.zeros((aff.shape[0], 1), jnp.int32)
        for bit in range(30, -1, -1):
            trial = kth | (1 << bit)
            enough = count(aff >= lax.bitcast_convert_type(trial, F32)) >= cap
            kth = jnp.where(enough, trial, kth)
        next_up = lax.bitcast_convert_type(kth + 1, F32)
        above = aff >= next_up
        tied = (aff >= lax.bitcast_convert_type(kth, F32)) & (aff < next_up)
        tri = tri_ref[...]
        tied_before = _dot(jnp.where(tied, 1.0, 0.0).astype(BF16), tri)
        chosen = above | (tied & (tied_before < (cap - count(above))))
        slot = _dot(jnp.where(chosen, 1.0, 0.0).astype(BF16), tri)
        slot_ref[...] = jnp.where(chosen, slot, -1.0)


def _route(h3d, router_wt):
    B, n_tok, _ = h3d.shape
    cap = CAPACITY_FACTOR * n_tok // N_EXPERTS
    tri = jnp.asarray(_prefix_table(n_tok)).astype(BF16)
    whole = pl.BlockSpec((B * N_EXPERTS, n_tok), lambda b: (0, 0))
    shape = jax.ShapeDtypeStruct((B * N_EXPERTS, n_tok), F32)
    nr = _requests_per_step(n_tok)
    return pl.pallas_call(
        functools.partial(_route_kernel, cap=cap),
        grid=(B // nr,),
        in_specs=[
            pl.BlockSpec((nr, n_tok, D_MODEL), lambda b: (b, 0, 0)),
            _const_spec((N_EXPERTS, D_MODEL)),
            _const_spec((n_tok, n_tok)),
        ],
        out_specs=[whole, whole],
        out_shape=[shape, shape],
        compiler_params=_params("arbitrary"),
        name="moe_route",
    )(h3d, router_wt, tri)


def _dispatch_kernel(h_ref, aff_ref, slot_ref, xs_ref, d_ref, g_ref, *, n_tok, cap):
    row = lax.broadcasted_iota(jnp.int32, (cap, n_tok), 0).astype(F32)
    for r in range(h_ref.shape[0]):
        def one_expert(e, carry, r=r):
            idx = r * N_EXPERTS + e
            hit = row == slot_ref[pl.ds(idx, 1), :]
            base = pl.multiple_of(e * cap, cap)
            d_ref[r, pl.ds(base, cap), :] = jnp.where(hit, 1.0, 0.0).astype(BF16)
            gate = jnp.sum(jnp.where(hit, aff_ref[pl.ds(idx, 1), :], 0.0), axis=1, keepdims=True)
            g_ref[r, pl.ds(base, cap), :] = jnp.broadcast_to(gate, (cap, LANES))
            return carry

        lax.fori_loop(0, N_EXPERTS, one_expert, 0, unroll=4)
        xs_ref[r] = _dot(d_ref[r], h_ref[r]).astype(BF16)


def _requests_per_step(n_tok):
    return max(1, STEP_TOKENS // n_tok)


def _dispatch(h3d, router_wt):
    B, n_tok, _ = h3d.shape
    cap = CAPACITY_FACTOR * n_tok // N_EXPERTS
    rows = N_EXPERTS * cap
    aff, slot = _route(h3d, router_wt)
    nr = _requests_per_step(n_tok)
    per_req = pl.BlockSpec((nr * N_EXPERTS, n_tok), lambda b: (b, 0))
    blk = lambda w: pl.BlockSpec((nr, rows, w), lambda b: (b, 0, 0))
    return pl.pallas_call(
        functools.partial(_dispatch_kernel, n_tok=n_tok, cap=cap),
        grid=(B // nr,),
        in_specs=[pl.BlockSpec((nr, n_tok, D_MODEL), lambda b: (b, 0, 0)), per_req, per_req],
        out_specs=[blk(D_MODEL), blk(n_tok), blk(LANES)],
        out_shape=[jax.ShapeDtypeStruct((B, rows, D_MODEL), BF16),
                   jax.ShapeDtypeStruct((B, rows, n_tok), BF16),
                   jax.ShapeDtypeStruct((B, rows, LANES), F32)],
        compiler_params=_params("parallel"),
        name="moe_dispatch",
    )(h3d, aff, slot)


def _ffn_kernel(xp_ref, gp_ref, xs_ref, gs_ref, wg_ref, wu_ref, wd_ref, yp_ref, ys_ref):
    wg = wg_ref[...].astype(BF16)
    wu = wu_ref[...].astype(BF16)
    wd = wd_ref[...].astype(BF16)
    for x_ref, g_ref, y_ref in ((xp_ref, gp_ref, yp_ref), (xs_ref, gs_ref, ys_ref)):
        nb, cap, _ = x_ref.shape
        x = x_ref[...].reshape(nb * cap, D_MODEL)
        gate = g_ref[...].reshape(nb * cap, LANES)
        gate = jnp.concatenate([gate] * (EXPERT_FF // LANES), axis=1)
        act = _silu(_dot(x, wg)) * _dot(x, wu) * gate
        y = _dot(act.astype(BF16), wd)
        y_ref[...] = y.reshape(nb, cap, D_MODEL).astype(y_ref.dtype)


def _expert_ffn(xp, gp, xs, gs, w_gate, w_up, w_down, layer):
    def act_spec(a):
        nb, _, cap, w = a.shape
        return pl.BlockSpec((nb, None, cap, w), lambda e: (0, e, 0, 0))

    def w_spec(a):
        return pl.BlockSpec((None, None) + a.shape[2:], lambda e: (layer, e, 0, 0))

    return pl.pallas_call(
        _ffn_kernel,
        grid=(N_EXPERTS,),
        in_specs=[act_spec(xp), act_spec(gp), act_spec(xs), act_spec(gs),
                  w_spec(w_gate), w_spec(w_up), w_spec(w_down)],
        out_specs=[act_spec(xp), act_spec(xs)],
        out_shape=[jax.ShapeDtypeStruct(xp.shape, BF16), jax.ShapeDtypeStruct(xs.shape, BF16)],
        compiler_params=_params("parallel"),
        name="expert_ffn",
    )(xp, gp, xs, gs, w_gate, w_up, w_down)


def _combine_kernel(d_ref, y_ref, x_ref, mod_ref, o_ref):
    for r in range(d_ref.shape[0]):
        moe = lax.dot_general(d_ref[r], y_ref[r], (((0,), (0,)), ((), ())), preferred_element_type=F32)
        o_ref[r] = x_ref[r] + mod_ref[min(r, mod_ref.shape[0] - 1)] * moe


def _combine(dmat, y3d, x3d, mod_g2):
    B, n_tok, _ = x3d.shape
    rows = dmat.shape[1]
    nr = _requests_per_step(n_tok)
    shared_mod = mod_g2.shape[0] == 1
    mod_spec = (pl.BlockSpec((1, 1, D_MODEL), lambda b: (0, 0, 0)) if shared_mod
                else pl.BlockSpec((nr, 1, D_MODEL), lambda b: (b, 0, 0)))
    return pl.pallas_call(
        _combine_kernel,
        grid=(B // nr,),
        in_specs=[
            pl.BlockSpec((nr, rows, n_tok), lambda b: (b, 0, 0)),
            pl.BlockSpec((nr, rows, D_MODEL), lambda b: (b, 0, 0)),
            pl.BlockSpec((nr, n_tok, D_MODEL), lambda b: (b, 0, 0)),
            mod_spec,
        ],
        out_specs=pl.BlockSpec((nr, n_tok, D_MODEL), lambda b: (b, 0, 0)),
        out_shape=jax.ShapeDtypeStruct(x3d.shape, F32),
        compiler_params=_params("parallel"),
        name="moe_combine",
    )(dmat, y3d, x3d, mod_g2)


def _token_mixers(x3d, mod, p, tabs, layer, ctx=None):
    B, L, _ = x3d.shape
    x2d = x3d.reshape(B * L, D_MODEL)
    main, qkv = _inproj(x2d, mod, p['norm1_w'], p['w_in'], layer, L)
    spectra = _hyena_spectra(L, p['hy_w1'], p['hy_b1'], p['hy_w2'], p['hy_b2'], p['hy_w3'], p['hy_b3'],
                             p['hy_decay'], tabs[4])
    ya, yc = _seqmix(main.reshape(B, L, MAIN_COLS), p['hy_short_w'], p['hy_short_b'], p['sc_w'], p['sc_b'],
                     spectra, p['hy_bias'], tabs)
    qkv3d = qkv.reshape(B, L, ATT_COLS)
    if ctx is None:
        yb, k, v = _attention(qkv3d, p['q_norm_w'], p['k_norm_w'])
    else:
        yb = _attention(qkv3d, p['q_norm_w'], p['k_norm_w'], ctx[0], ctx[1], layer)
        k = v = None
    x_mid, h2 = _merge(ya.reshape(B * L, D_HYENA), yb.reshape(B * L, ATT_Q), yc.reshape(B * L, D_SCONV),
                       x2d, mod, p['norm1_w'], p['norm2_w'], p['w_in'], p['w_br_a'], p['w_br_b'], p['w_br_c'],
                       p['w_o'], layer, L)
    return x_mid.reshape(B, L, D_MODEL), h2.reshape(B, L, D_MODEL), k, v


def _moe_split(a, n_exp):
    B, rows, w = a.shape
    return a.reshape(B, n_exp, rows // n_exp, w)


def kernel(x_prompt, x_sample, cache_k, cache_v, c, c_ctx, mod_w, mod_b, norm1_w, norm2_w, w_in, hy_short_w, hy_short_b, hy_w1, hy_b1, hy_w2, hy_b2, hy_w3, hy_b3, hy_decay, hy_bias, q_norm_w, k_norm_w, sc_w, sc_b, w_br_a, w_br_b, w_br_c, w_o, router_w, exp_w_gate, exp_w_up, exp_w_down):
    n_dec = x_sample.shape[0]
    n_ctx = cache_k.shape[2]
    lp = x_prompt.shape[1]
    ls = x_sample.shape[1]

    cond_rows = 16
    cond = jnp.concatenate([c_ctx[None, :], c, jnp.zeros((cond_rows - 1 - n_dec, D_MODEL), F32)], axis=0)
    mod = _modulation(cond, mod_w, mod_b).reshape(DEPTH, cond_rows, 6, D_MODEL)

    tabs_p = tuple(jnp.asarray(t).astype(BF16) for t in _split_dft_tables(lp))
    tabs_s = tuple(jnp.asarray(t).astype(BF16) for t in _split_dft_tables(ls))
    ctx_k = cache_k.reshape(n_dec, DEPTH, n_ctx, ATT_KV)
    ctx_v = cache_v.reshape(n_dec, DEPTH, n_ctx, ATT_KV)
    dense = {'w_in': w_in.astype(BF16), 'w_br_a': w_br_a.astype(BF16), 'w_br_b': w_br_b.astype(BF16),
             'w_br_c': w_br_c.astype(BF16), 'w_o': w_o.astype(BF16)}

    xp, xs = x_prompt, x_sample
    ks_new, vs_new = [], []
    for l in range(DEPTH):
        p = dict(dense)
        p.update({
            'norm1_w': norm1_w[l].reshape(1, D_MODEL), 'norm2_w': norm2_w[l].reshape(1, D_MODEL),
            'hy_short_w': hy_short_w[l], 'hy_short_b': hy_short_b[l],
            'hy_w1': hy_w1[l], 'hy_b1': hy_b1[l], 'hy_w2': hy_w2[l], 'hy_b2': hy_b2[l],
            'hy_w3': hy_w3[l], 'hy_b3': hy_b3[l], 'hy_decay': hy_decay[l], 'hy_bias': hy_bias[l],
            'q_norm_w': q_norm_w[l], 'k_norm_w': k_norm_w[l], 'sc_w': sc_w[l], 'sc_b': sc_b[l],
        })
        mod_p = mod[l, 0:1]
        mod_s = mod[l, 1:1 + n_dec]
        xp_mid, h2p, k_l, v_l = _token_mixers(xp, mod_p, p, tabs_p, l)
        xs_mid, h2s, _, _ = _token_mixers(xs, mod_s, p, tabs_s, l, (ctx_k, ctx_v))
        ks_new.append(k_l.reshape(k_l.shape[0], lp, N_KV_HEADS, HEAD_DIM))
        vs_new.append(v_l.reshape(v_l.shape[0], lp, N_KV_HEADS, HEAD_DIM))

        router_wt = router_w[l].T.astype(BF16)
        gp_x, gp_d, gp_g = _dispatch(h2p, router_wt)
        gs_x, gs_d, gs_g = _dispatch(h2s, router_wt)
        yp, ys = _expert_ffn(_moe_split(gp_x, N_EXPERTS), _moe_split(gp_g, N_EXPERTS),
                             _moe_split(gs_x, N_EXPERTS), _moe_split(gs_g, N_EXPERTS),
                             exp_w_gate, exp_w_up, exp_w_down, l)
        xp = _combine(gp_d, yp.reshape(gp_x.shape), xp_mid, mod_p[:, 5:6])
        xs = _combine(gs_d, ys.reshape(gs_x.shape), xs_mid, mod_s[:, 5:6])

    return (xp, xs, jnp.stack(ks_new, axis=1), jnp.stack(vs_new, axis=1))
```

```python
import functools
import math

import numpy as np
import jax
import jax.numpy as jnp
from jax import lax
from jax.experimental import pallas as pl
from jax.experimental.pallas import tpu as pltpu

D_MODEL = 1024
DEPTH = 2
GRID_W = 64
D_HYENA = 512
HYENA_ORDER = 2
HYENA_POS_BANDS = 16
HYENA_POS_DIM = 1 + 2 * HYENA_POS_BANDS
HYENA_FILTER_HIDDEN = 64
HYENA_WINDOW_SHIFT = 0.05
N_HEADS = 8
N_KV_HEADS = 2
HEAD_DIM = 64
HEADS_PER_KV = N_HEADS // N_KV_HEADS
ATT_Q = N_HEADS * HEAD_DIM
ATT_KV = N_KV_HEADS * HEAD_DIM
ROPE_THETA = 10000.0
D_SCONV = 512
N_EXPERTS = 16
EXPERT_FF = 512
CAPACITY_FACTOR = 2
NORM_EPS = 1e-6
HY_COLS = (HYENA_ORDER + 1) * D_HYENA
ATT_COLS = ATT_Q + 2 * ATT_KV
SC_COLS = 3 * D_SCONV
GATE_COLS = 3 * D_MODEL
MAIN_COLS = HY_COLS + SC_COLS
MIX_COLS = MAIN_COLS + ATT_COLS
D_IN = MIX_COLS + GATE_COLS

F32 = jnp.float32
BF16 = jnp.bfloat16

V7X_VMEM_BYTES = 64 * 1024 * 1024
VMEM_LIMIT = V7X_VMEM_BYTES - 8 * 1024 * 1024
LANES = 128
MLP_PAD = 128
STEP_TOKENS = 1024
INPROJ_ROWS = 1024
MERGE_ROWS = 512
HYENA_CHANNEL_TILE = 256
MOD_COL_TILE = 1024
COND_ROWS = 16


def _params(*sem):
    return pltpu.CompilerParams(dimension_semantics=sem, vmem_limit_bytes=VMEM_LIMIT)


def _const_spec(shape):
    nd = len(shape)
    return pl.BlockSpec(shape, lambda *_: (0,) * nd, pipeline_mode=pl.Buffered(1))


def _layer_spec(shape, layer):
    nd = len(shape)
    return pl.BlockSpec((None,) + tuple(shape), lambda *_: (layer,) + (0,) * nd, pipeline_mode=pl.Buffered(1))


def _dot(a, b):
    return jnp.dot(a, b, preferred_element_type=F32)


def _silu(x):
    return x * (1.0 / (1.0 + jnp.exp(-x)))


def _sigmoid(x):
    return 1.0 / (1.0 + jnp.exp(-x))


@functools.lru_cache(maxsize=None)
def _split_dft_tables(L):
    h = L // 2
    k = np.arange(h, dtype=np.int64)
    alt = 1.0 - 2.0 * (k % 2)
    fwd, inv = [], []
    for odd in (0, 1):
        idx = (k[:, None] * (2 * k[None, :] + odd)) % (2 * L)
        ang = idx.astype(np.float64) * (np.pi / L)
        c = np.cos(ang)
        s = np.sin(ang)
        s[0, :] = alt
        fwd.append(np.concatenate([c, s], axis=0).astype(np.float32))
        inv.append(np.concatenate([c.T, s.T], axis=1).astype(np.float32))
    return fwd[0], fwd[1], inv[0], inv[1]


@functools.lru_cache(maxsize=None)
def _pos_features(L):
    n = np.arange(L, dtype=np.float64)
    t = n / max(L - 1, 1)
    bands = np.linspace(1e-4, HYENA_POS_BANDS - 1, HYENA_POS_BANDS)
    ang = 2.0 * math.pi * n[:, None] * bands[None, :] / L
    z = np.concatenate([t[:, None], np.cos(ang), np.sin(ang)], axis=-1)
    zp = np.zeros((L, MLP_PAD), np.float32)
    zp[:, :HYENA_POS_DIM] = z
    return zp


@functools.lru_cache(maxsize=None)
def _rope_tables(L):
    rows = L // GRID_W
    row = np.repeat(np.arange(rows, dtype=np.float64), GRID_W)
    col = np.tile(np.arange(GRID_W, dtype=np.float64), rows)
    axis_dim = HEAD_DIM // 2
    inv_freq = ROPE_THETA ** (-np.arange(0, axis_dim, 2, dtype=np.float64) / axis_dim)
    ar = row[:, None] * inv_freq[None, :]
    ac = col[:, None] * inv_freq[None, :]
    ang = np.concatenate([ar, ar, ac, ac], axis=-1)
    cos = np.cos(ang).astype(np.float32)
    sin = np.sin(ang).astype(np.float32)
    return np.tile(cos, (1, N_HEADS)), np.tile(sin, (1, N_HEADS))


@functools.lru_cache(maxsize=None)
def _head_tables():
    lane = np.arange(ATT_Q)
    block_ones = (lane[:, None] // HEAD_DIM == lane[None, :] // HEAD_DIM).astype(np.float32)
    src = np.arange(ATT_KV)
    dst = np.arange(HEADS_PER_KV * HEAD_DIM)
    rep = np.stack([(src[:, None] == g * HEAD_DIM + dst[None, :] % HEAD_DIM) for g in range(N_KV_HEADS)])
    return block_ones, rep.astype(np.float32)


@functools.lru_cache(maxsize=None)
def _prefix_table(n):
    i = np.arange(n)
    return (i[:, None] < i[None, :]).astype(np.float32)


def _mod_kernel(cond_ref, w_ref, b_ref, o_ref):
    a = _silu(cond_ref[...]).astype(BF16)
    o_ref[...] = _dot(a, w_ref[...].astype(BF16)) + b_ref[...]


def _modulation(cond, mod_w, mod_b):
    rows = cond.shape[0]
    tn = MOD_COL_TILE
    ncols = mod_w.shape[-1]
    return pl.pallas_call(
        _mod_kernel,
        grid=(DEPTH, ncols // tn),
        in_specs=[
            pl.BlockSpec((rows, D_MODEL), lambda l, j: (0, 0)),
            pl.BlockSpec((None, D_MODEL, tn), lambda l, j: (l, 0, j)),
            pl.BlockSpec((None, 1, tn), lambda l, j: (l, 0, j)),
        ],
        out_specs=pl.BlockSpec((None, rows, tn), lambda l, j: (l, 0, j)),
        out_shape=jax.ShapeDtypeStruct((DEPTH, rows, ncols), F32),
        compiler_params=_params("parallel", "parallel"),
        name="modulation",
    )(cond, mod_w, mod_b.reshape(DEPTH, 1, ncols))


def _rms_modulate(x, norm_w, shift, scale):
    y = x * lax.rsqrt(jnp.mean(x * x, axis=-1, keepdims=True) + NORM_EPS)
    return (y * norm_w) * (1.0 + scale) + shift


def _inproj_kernel(x_ref, mod_ref, nw_ref, w_ref, main_ref, qkv_ref, h_sc):
    h = _rms_modulate(x_ref[...], nw_ref[...], mod_ref[0:1, :], mod_ref[1:2, :])
    h_sc[...] = h.astype(BF16)
    for lo in range(0, MIX_COLS, ATT_COLS):
        y = _dot(h_sc[...], w_ref[:, lo:lo + ATT_COLS])
        if lo < HY_COLS:
            main_ref[:, lo:lo + ATT_COLS] = y.astype(BF16)
        elif lo == HY_COLS:
            qkv_ref[...] = y
        else:
            main_ref[:, lo - ATT_COLS:lo] = y.astype(BF16)


def _inproj(x2d, mod, norm_w, w_in_bf, layer, rows_per_req, tm=INPROJ_ROWS):
    T = x2d.shape[0]
    n_req = mod.shape[0]
    req = (lambda i: (i * tm) // rows_per_req) if n_req > 1 else (lambda i: 0)
    return pl.pallas_call(
        _inproj_kernel,
        grid=(T // tm,),
        in_specs=[
            pl.BlockSpec((tm, D_MODEL), lambda i: (i, 0)),
            pl.BlockSpec((None, 6, D_MODEL), lambda i: (req(i), 0, 0)),
            _const_spec((1, D_MODEL)),
            _layer_spec((D_MODEL, MIX_COLS), layer),
        ],
        out_specs=[pl.BlockSpec((tm, MAIN_COLS), lambda i: (i, 0)),
                   pl.BlockSpec((tm, ATT_COLS), lambda i: (i, 0))],
        out_shape=[jax.ShapeDtypeStruct((T, MAIN_COLS), BF16), jax.ShapeDtypeStruct((T, ATT_COLS), F32)],
        scratch_shapes=[pltpu.VMEM((tm, D_MODEL), BF16)],
        compiler_params=_params("parallel"),
        name="inproj",
    )(x2d, mod, norm_w, w_in_bf)


def _split_parity(par_sc, z):
    half = z.shape[0] // 2
    n_lt = z.shape[1] // LANES
    for c in range(n_lt):
        par_sc[c] = z[:, c * LANES:(c + 1) * LANES]
    return tuple(jnp.concatenate([par_sc[c, pl.ds(par, half, stride=2), :] for c in range(n_lt)], axis=1)
                 for par in (0, 1))


def _join_parity(par_sc, even, odd):
    half = even.shape[0]
    n_lt = even.shape[1] // LANES
    for c in range(n_lt):
        par_sc[c, pl.ds(0, half, stride=2), :] = even[:, c * LANES:(c + 1) * LANES]
        par_sc[c, pl.ds(1, half, stride=2), :] = odd[:, c * LANES:(c + 1) * LANES]
    return jnp.concatenate([par_sc[c] for c in range(n_lt)], axis=1)


def _fold_spectrum(te, to):
    half = te.shape[0] // 2
    row0 = lax.broadcasted_iota(jnp.int32, (half, te.shape[1]), 0) == 0
    ce, se, co, so = te[:half], te[half:], to[:half], to[half:]
    a = jnp.concatenate([ce + co, jnp.where(row0, se, ce - co)], axis=0)
    b = jnp.concatenate([jnp.where(row0, ce - co, se + so), jnp.where(row0, so, so - se)], axis=0)
    return a, b


def _unfold_spectrum(yr, yi):
    half = yr.shape[0] // 2
    row0 = lax.broadcasted_iota(jnp.int32, (half, yr.shape[1]), 0) == 0
    yr_lo, yr_up, yi_lo, yi_up = yr[:half], yr[half:], yi[:half], yi[half:]
    even = jnp.concatenate([jnp.where(row0, yr_lo + yi_lo, yr_lo + yr_up),
                            jnp.where(row0, yr_up, yi_lo - yi_up)], axis=0)
    odd = jnp.concatenate([jnp.where(row0, yr_lo - yi_lo, yr_lo - yr_up),
                           jnp.where(row0, yi_up, yi_lo + yi_up)], axis=0)
    return even, odd


def _filter_kernel(z_ref, w1_ref, b1_ref, w2_ref, b2_ref, w3f_ref, b3f_ref, w3b_ref, b3b_ref,
                   decf_ref, decb_ref, fwde_ref, fwdo_ref, p_ref, h_sc, par_sc, *, L):
    hi = lax.Precision.HIGHEST

    @pl.when((pl.program_id(0) == 0) & (pl.program_id(1) == 0))
    def _():
        h1 = jnp.sin(jnp.dot(z_ref[...], w1_ref[...], precision=hi, preferred_element_type=F32) + b1_ref[...])
        h_sc[...] = jnp.sin(jnp.dot(h1, w2_ref[...], precision=hi, preferred_element_type=F32) + b2_ref[...])

    h = h_sc[...]
    tc = w3f_ref.shape[1]
    pos = lax.broadcasted_iota(jnp.int32, (L, tc), 0)
    t = pos.astype(F32) / float(max(L - 1, 1))

    def taps(w3_ref, b3_ref, dec_ref):
        g = jnp.dot(h, w3_ref[...], precision=hi, preferred_element_type=F32) + b3_ref[...]
        return g * (jnp.exp(-t * jnp.abs(dec_ref[...])) + HYENA_WINDOW_SHIFT)

    hf = taps(w3f_ref, b3f_ref, decf_ref)
    hb = jnp.where(pos == 0, 0.0, taps(w3b_ref, b3b_ref, decb_ref))

    def transform(taps_lc):
        even, odd = _split_parity(par_sc, taps_lc)
        return _fold_spectrum(_dot(fwde_ref[...], even.astype(BF16)), _dot(fwdo_ref[...], odd.astype(BF16)))

    fa, fb = transform(hf)
    ba, bb = transform(hb)
    k_re = fa + ba
    k_im = bb - fb
    k_ny = fb + bb
    first = pos == 0
    inv_n = 1.0 / (2 * L)
    p_ref[0] = jnp.where(first, k_re * inv_n, k_re * (2.0 * inv_n))
    p_ref[1] = jnp.where(first, 0.0, k_im * (2.0 * inv_n))
    p_ref[2] = jnp.where(first, k_ny * inv_n, k_re * (2.0 * inv_n))


def _hyena_spectra(L, w1, b1, w2, b2, w3, b3, decay, fwd_even, fwd_odd):
    pad_h = MLP_PAD - HYENA_FILTER_HIDDEN
    z = jnp.asarray(_pos_features(L))
    w1p = jnp.pad(w1, ((0, MLP_PAD - HYENA_POS_DIM), (0, pad_h)))
    b1p = jnp.pad(b1, (0, pad_h)).reshape(1, MLP_PAD)
    w2p = jnp.pad(w2, ((0, pad_h), (0, pad_h)))
    b2p = jnp.pad(b2, (0, pad_h)).reshape(1, MLP_PAD)
    w3p = jnp.pad(w3, ((0, pad_h), (0, 0)))
    ncol = w3.shape[1]
    b3r = b3.reshape(1, ncol)
    decr = decay.reshape(1, ncol)
    tc = HYENA_CHANNEL_TILE
    nct = D_HYENA // tc
    per_dir = HYENA_ORDER * nct
    col_f = lambda o, c: (0, o * nct + c)
    col_b = lambda o, c: (0, per_dir + o * nct + c)
    return pl.pallas_call(
        functools.partial(_filter_kernel, L=L),
        grid=(HYENA_ORDER, nct),
        in_specs=[
            _const_spec((L, MLP_PAD)),
            _const_spec((MLP_PAD, MLP_PAD)), _const_spec((1, MLP_PAD)),
            _const_spec((MLP_PAD, MLP_PAD)), _const_spec((1, MLP_PAD)),
            pl.BlockSpec((MLP_PAD, tc), col_f), pl.BlockSpec((1, tc), col_f),
            pl.BlockSpec((MLP_PAD, tc), col_b), pl.BlockSpec((1, tc), col_b),
            pl.BlockSpec((1, tc), col_f), pl.BlockSpec((1, tc), col_b),
            _const_spec((L, L // 2)), _const_spec((L, L // 2)),
        ],
        out_specs=pl.BlockSpec((None, 3, L, tc), lambda o, c: (o, 0, 0, c)),
        out_shape=jax.ShapeDtypeStruct((HYENA_ORDER, 3, L, D_HYENA), F32),
        scratch_shapes=[pltpu.VMEM((L, MLP_PAD), F32), pltpu.VMEM((tc // LANES, L, LANES), F32)],
        compiler_params=_params("arbitrary", "arbitrary"),
        name="hyena_spectra",
    )(z, w1p, b1p, w2p, b2p, w3p, b3r, w3p, b3r, decr, decr, fwd_even, fwd_odd)


def _conv3(x, w_ref, b_ref, first, last):
    L = x.shape[0]
    prev = jnp.where(first, 0.0, pltpu.roll(x, 1, 0))
    nxt = jnp.where(last, 0.0, pltpu.roll(x, L - 1, 0))
    return prev * w_ref[0:1, :] + x * w_ref[1:2, :] + nxt * w_ref[2:3, :] + b_ref[...]


def _seqmix_kernel(v_ref, x1_ref, x2_ref, u_ref, bg_ref, cg_ref,
                   wv_ref, wx1_ref, wx2_ref, bv_ref, bx1_ref, bx2_ref, scw_ref, scb_ref,
                   p_ref, hb_ref, fwde_ref, fwdo_ref, inve_ref, invo_ref, ya_ref, yc_ref,
                   spec_sc, par_sc, *, L):
    n_req, _, tc = v_ref.shape
    pos = lax.broadcasted_iota(jnp.int32, (L, tc), 0)
    first = pos == 0
    last = pos == L - 1
    mult_refs = ((x1_ref, wx1_ref, bx1_ref), (x2_ref, wx2_ref, bx2_ref))

    def chain(r):
        z = _conv3(v_ref[r].astype(F32), wv_ref, bv_ref, first, last)
        z_even, z_odd = _split_parity(par_sc.at[r], z)
        yield
        for o in range(HYENA_ORDER):
            te = _dot(fwde_ref[...], z_even.astype(BF16))
            to = _dot(fwdo_ref[...], z_odd.astype(BF16))
            yield
            a, b = _fold_spectrum(te, to)
            p2 = p_ref[o, 1]
            even, odd = _unfold_spectrum(a * p_ref[o, 0] + b * p2, b * p_ref[o, 2] - a * p2)
            spec_sc[r, 0] = even.astype(BF16)
            spec_sc[r, 1] = odd.astype(BF16)
            x_ref, w_ref, b_ref = mult_refs[o]
            mult = _conv3(x_ref[r].astype(F32), w_ref, b_ref, first, last)
            yield
            y_even = _dot(inve_ref[...], spec_sc[r, 0])
            y_odd = _dot(invo_ref[...], spec_sc[r, 1])
            yield
            z = mult * (_join_parity(par_sc.at[r], y_even, y_odd) + hb_ref[o:o + 1, :] * z)
            if o < HYENA_ORDER - 1:
                z_even, z_odd = _split_parity(par_sc.at[r], z)
            if o == 0:
                gated = cg_ref[r].astype(F32) * u_ref[r].astype(F32)
                yc = bg_ref[r].astype(F32) * _conv3(gated, scw_ref, scb_ref, first, last)
                yc_ref[r] = yc.astype(yc_ref.dtype)
            if o == HYENA_ORDER - 1:
                ya_ref[r] = z.astype(ya_ref.dtype)
            yield

    chains = [chain(r) for r in range(n_req)]
    n_stages = 1 + 4 * HYENA_ORDER
    for tick in range(n_stages + n_req - 1):
        for r, c in enumerate(chains):
            if 0 <= tick - r < n_stages:
                next(c)


def _seqmix(main3d, hy_short_w, hy_short_b, sc_w, sc_b, spectra, hy_bias, tabs):
    B, L, _ = main3d.shape
    tc = HYENA_CHANNEL_TILE
    nct = D_HYENA // tc
    nr = max(2, STEP_TOKENS // L)
    act = lambda off: pl.BlockSpec((nr, L, tc), lambda c, b, off=off: (b, 0, off * nct + c))
    par3 = lambda off: pl.BlockSpec((3, tc), lambda c, b, off=off: (0, off * nct + c))
    par1 = lambda off: pl.BlockSpec((1, tc), lambda c, b, off=off: (0, off * nct + c))
    out = pl.BlockSpec((nr, L, tc), lambda c, b: (b, 0, c))
    hsb = hy_short_b.reshape(1, HY_COLS)
    return pl.pallas_call(
        functools.partial(_seqmix_kernel, L=L),
        grid=(nct, B // nr),
        in_specs=[
            act(0), act(1), act(2), act(3), act(4), act(5),
            par3(0), par3(1), par3(2), par1(0), par1(1), par1(2),
            par3(0), par1(0),
            pl.BlockSpec((HYENA_ORDER, 3, L, tc), lambda c, b: (0, 0, 0, c), pipeline_mode=pl.Buffered(1)),
            pl.BlockSpec((HYENA_ORDER, tc), lambda c, b: (0, c)),
            _const_spec((L, L // 2)), _const_spec((L, L // 2)), _const_spec((L // 2, L)), _const_spec((L // 2, L)),
        ],
        out_specs=[out, out],
        out_shape=[jax.ShapeDtypeStruct((B, L, D_HYENA), BF16), jax.ShapeDtypeStruct((B, L, D_SCONV), BF16)],
        scratch_shapes=[pltpu.VMEM((nr, 2, L, tc), BF16), pltpu.VMEM((nr, tc // LANES, L, LANES), F32)],
        compiler_params=_params("arbitrary", "arbitrary"),
        name="seqmix",
    )(main3d, main3d, main3d, main3d, main3d, main3d,
      hy_short_w, hy_short_w, hy_short_w, hsb, hsb, hsb, sc_w, sc_b.reshape(1, D_SCONV),
      spectra, hy_bias, *tabs)


def _head_rms(x, ones_ref, w):
    sq = x * x
    hi = sq.astype(BF16)
    lo = (sq - hi.astype(F32)).astype(BF16)
    ones = ones_ref[...]
    width = x.shape[1]
    ss = _dot(hi, ones[:width, :width]) + _dot(lo, ones[:width, :width])
    return x * lax.rsqrt(ss * (1.0 / HEAD_DIM) + NORM_EPS) * w


def _rope(x, cos, sin):
    width = x.shape[1]
    lane = lax.broadcasted_iota(jnp.int32, x.shape, 1)
    half = HEAD_DIM // 4
    low = (lane % (2 * half)) < half
    rot = jnp.where(low, -pltpu.roll(x, width - half, 1), pltpu.roll(x, half, 1))
    return x * cos + rot * sin


def _attn_kernel(*refs, rope):
    if rope:
        (q_ref, k_ref, v_ref, ck_ref, cv_ref, qw_ref, kw_ref, ones_ref, rep_ref, rept_ref,
         cos_ref, sin_ref, yb_ref) = refs
    else:
        q_ref, k_ref, v_ref, qw_ref, kw_ref, ones_ref, rep_ref, rept_ref, yb_ref, ko_ref, vo_ref = refs
    n_req, L, _ = q_ref.shape
    gw = HEADS_PER_KV * HEAD_DIM
    lane = lax.broadcasted_iota(jnp.int32, (L, gw), 1)
    for r in range(n_req):
        kn = _head_rms(k_ref[r], ones_ref, kw_ref[...])
        v = v_ref[r]
        q = _head_rms(q_ref[r], ones_ref, qw_ref[...])
        if rope:
            keys = _rope(kn, cos_ref[:, :ATT_KV], sin_ref[:, :ATT_KV])
            keys = jnp.concatenate([ck_ref[r], keys], axis=0)
            vals = jnp.concatenate([cv_ref[r], v], axis=0)
            q = _rope(q, cos_ref[...], sin_ref[...])
        else:
            ko_ref[r] = kn
            vo_ref[r] = v
            keys, vals = kn, v
        keys = keys.astype(BF16)
        vals = vals.astype(BF16)
        q = q * (HEAD_DIM ** -0.5)
        for g in range(N_KV_HEADS):
            k4 = lax.dot_general(rept_ref[g], keys, (((1,), (1,)), ((), ())), preferred_element_type=F32).astype(BF16)
            v4 = _dot(vals, rep_ref[g]).astype(BF16)
            qg = q[:, g * gw:(g + 1) * gw]
            acc = jnp.zeros((L, gw), F32)
            for h in range(HEADS_PER_KV):
                mine = (lane // HEAD_DIM) == h
                qm = jnp.where(mine, qg, 0.0).astype(BF16)
                s = _dot(qm, k4)
                p = jnp.exp(s - jnp.max(s, axis=-1, keepdims=True))
                denom = jnp.sum(p, axis=-1, keepdims=True)
                o4 = _dot(p.astype(BF16), v4)
                acc = jnp.where(mine, o4 * (1.0 / denom), acc)
            yb_ref[r, :, g * gw:(g + 1) * gw] = acc.astype(yb_ref.dtype)


def _attention(qkv3d, q_norm_w, k_norm_w, ctx_k=None, ctx_v=None, layer=0):
    B, L, _ = qkv3d.shape
    rope = ctx_k is not None
    nr = _requests_per_step(L)
    ones_np, rep_np = _head_tables()
    ones = jnp.asarray(ones_np).astype(BF16)
    rep = jnp.asarray(rep_np).astype(BF16)
    rept = jnp.asarray(np.swapaxes(rep_np, 1, 2)).astype(BF16)
    qw = jnp.tile(q_norm_w, N_HEADS).reshape(1, ATT_Q)
    kw = jnp.tile(k_norm_w, N_KV_HEADS).reshape(1, ATT_KV)
    kblk = ATT_Q // ATT_KV
    in_specs = [
        pl.BlockSpec((nr, L, ATT_Q), lambda b: (b, 0, 0)),
        pl.BlockSpec((nr, L, ATT_KV), lambda b: (b, 0, kblk)),
        pl.BlockSpec((nr, L, ATT_KV), lambda b: (b, 0, kblk + 1)),
    ]
    args = [qkv3d, qkv3d, qkv3d]
    if rope:
        n_ctx = ctx_k.shape[2]
        in_specs += [pl.BlockSpec((nr, None, n_ctx, ATT_KV), lambda b: (b, layer, 0, 0))] * 2
        args += [ctx_k, ctx_v]
    in_specs += [_const_spec((1, ATT_Q)), _const_spec((1, ATT_KV)), _const_spec((ATT_Q, ATT_Q)),
                 _const_spec((N_KV_HEADS, ATT_KV, HEADS_PER_KV * HEAD_DIM)),
                 _const_spec((N_KV_HEADS, HEADS_PER_KV * HEAD_DIM, ATT_KV))]
    args += [qw, kw, ones, rep, rept]
    yb_shape = jax.ShapeDtypeStruct((B, L, ATT_Q), BF16)
    yb_spec = pl.BlockSpec((nr, L, ATT_Q), lambda b: (b, 0, 0))
    if rope:
        cos_np, sin_np = _rope_tables(L)
        in_specs += [_const_spec((L, ATT_Q))] * 2
        args += [jnp.asarray(cos_np), jnp.asarray(sin_np)]
        out_specs = yb_spec
        out_shape = yb_shape
    else:
        kv_spec = pl.BlockSpec((nr, L, ATT_KV), lambda b: (b, 0, 0))
        kv_shape = jax.ShapeDtypeStruct((B, L, ATT_KV), F32)
        out_specs = [yb_spec, kv_spec, kv_spec]
        out_shape = [yb_shape, kv_shape, kv_shape]
    return pl.pallas_call(
        functools.partial(_attn_kernel, rope=rope),
        grid=(B // nr,),
        in_specs=in_specs,
        out_specs=out_specs,
        out_shape=out_shape,
        compiler_params=_params("parallel"),
        name="attention",
    )(*args)


def _merge_kernel(ya_ref, yb_ref, yc_ref, x_ref, mod_ref, nw1_ref, nw2_ref, *rest):
    gate_w_refs = rest[:GATE_COLS // ATT_COLS]
    wa_ref, wb_ref, wc_ref, wo_ref, xo_ref, h2_ref = rest[GATE_COLS // ATT_COLS:]
    tm = x_ref.shape[0]
    n_chain = 2
    rows_per = tm // n_chain

    def chain(c):
        rows = pl.ds(c * rows_per, rows_per)
        x = x_ref[rows, :]
        h = _rms_modulate(x, nw1_ref[...], mod_ref[0:1, :], mod_ref[1:2, :]).astype(BF16)
        yield
        gates = jnp.concatenate([_dot(h, w_ref[...]) for w_ref in gate_w_refs], axis=1)
        da = _dot(ya_ref[rows, :], wa_ref[...])
        db = _dot(yb_ref[rows, :], wb_ref[...])
        dc = _dot(yc_ref[rows, :], wc_ref[...])
        yield
        merged = (_sigmoid(gates[:, 0:D_MODEL]) * da + _sigmoid(gates[:, D_MODEL:2 * D_MODEL]) * db
                  + _sigmoid(gates[:, 2 * D_MODEL:]) * dc).astype(BF16)
        yield
        proj = _dot(merged, wo_ref[...])
        yield
        x = x + mod_ref[2:3, :] * proj
        xo_ref[rows, :] = x
        h2_ref[rows, :] = _rms_modulate(x, nw2_ref[...], mod_ref[3:4, :], mod_ref[4:5, :]).astype(BF16)
        yield

    chains = [chain(c) for c in range(n_chain)]
    n_stages = 5
    for tick in range(n_stages + n_chain - 1):
        for c, ch in enumerate(chains):
            if 0 <= tick - c < n_stages:
                next(ch)


def _merge(ya, yb, yc, x2d, mod, norm1_w, norm2_w, w_in_bf, wa, wb, wc, wo, layer, rows_per_req, tm=MERGE_ROWS):
    T = x2d.shape[0]
    n_req = mod.shape[0]
    req = (lambda i: (i * tm) // rows_per_req) if n_req > 1 else (lambda i: 0)
    br = pl.BlockSpec((tm, D_HYENA), lambda i: (i, 0))
    row = pl.BlockSpec((tm, D_MODEL), lambda i: (i, 0))
    gate_blk0 = MIX_COLS // ATT_COLS
    gate_w = [pl.BlockSpec((None, D_MODEL, ATT_COLS), lambda i, k=k: (layer, 0, gate_blk0 + k),
                           pipeline_mode=pl.Buffered(1)) for k in range(GATE_COLS // ATT_COLS)]
    return pl.pallas_call(
        _merge_kernel,
        grid=(T // tm,),
        in_specs=[
            br, br, br, row,
            pl.BlockSpec((None, 6, D_MODEL), lambda i: (req(i), 0, 0)),
            _const_spec((1, D_MODEL)), _const_spec((1, D_MODEL)),
            *gate_w,
            _layer_spec((D_HYENA, D_MODEL), layer), _layer_spec((ATT_Q, D_MODEL), layer),
            _layer_spec((D_SCONV, D_MODEL), layer), _layer_spec((D_MODEL, D_MODEL), layer),
        ],
        out_specs=[row, row],
        out_shape=[jax.ShapeDtypeStruct((T, D_MODEL), F32), jax.ShapeDtypeStruct((T, D_MODEL), BF16)],
        compiler_params=_params("parallel"),
        name="merge",
    )(ya, yb, yc, x2d, mod, norm1_w, norm2_w, *([w_in_bf] * len(gate_w)), wa, wb, wc, wo)


def _route_kernel(h_ref, rw_ref, tri_ref, aff_ref, slot_ref, *, cap):
    b = pl.program_id(0)
    n_req = h_ref.shape[0]
    for r in range(n_req):
        logits = lax.dot_general(rw_ref[...], h_ref[r], (((1,), (1,)), ((), ())), preferred_element_type=F32)
        ex = jnp.exp(logits - jnp.max(logits, axis=0, keepdims=True))
        first_row = pl.multiple_of((b * n_req + r) * N_EXPERTS, N_EXPERTS)
        aff_ref[pl.ds(first_row, N_EXPERTS), :] = ex / jnp.sum(ex, axis=0, keepdims=True)

    @pl.when(b == pl.num_programs(0) - 1)
    def _():
        aff = aff_ref[...]

        def count(mask):
            return jnp.sum(jnp.where(mask, 1.0, 0.0), axis=1, keepdims=True)

        kth = jnp.zeros((aff.shape[0], 1), jnp.int32)
        for bit in range(30, -1, -1):
            trial = kth | (1 << bit)
            enough = count(aff >= lax.bitcast_convert_type(trial, F32)) >= cap
            kth = jnp.where(enough, trial, kth)
        next_up = lax.bitcast_convert_type(kth + 1, F32)
        above = aff >= next_up
        tied = (aff >= lax.bitcast_convert_type(kth, F32)) & (aff < next_up)
        tri = tri_ref[...]
        tied_before = _dot(jnp.where(tied, 1.0, 0.0).astype(BF16), tri)
        chosen = above | (tied & (tied_before < (cap - count(above))))
        slot = _dot(jnp.where(chosen, 1.0, 0.0).astype(BF16), tri)
        slot_ref[...] = jnp.where(chosen, slot, -1.0)


def _route(h3d, router_wt):
    B, n_tok, _ = h3d.shape
    cap = CAPACITY_FACTOR * n_tok // N_EXPERTS
    tri = jnp.asarray(_prefix_table(n_tok)).astype(BF16)
    whole = pl.BlockSpec((B * N_EXPERTS, n_tok), lambda b: (0, 0))
    shape = jax.ShapeDtypeStruct((B * N_EXPERTS, n_tok), F32)
    nr = _requests_per_step(n_tok)
    return pl.pallas_call(
        functools.partial(_route_kernel, cap=cap),
        grid=(B // nr,),
        in_specs=[
            pl.BlockSpec((nr, n_tok, D_MODEL), lambda b: (b, 0, 0)),
            _const_spec((N_EXPERTS, D_MODEL)),
            _const_spec((n_tok, n_tok)),
        ],
        out_specs=[whole, whole],
        out_shape=[shape, shape],
        compiler_params=_params("arbitrary"),
        name="moe_route",
    )(h3d, router_wt, tri)


def _dispatch_kernel(h_ref, aff_ref, slot_ref, xs_ref, d_ref, g_ref, *, n_tok, cap):
    row = lax.broadcasted_iota(jnp.int32, (cap, n_tok), 0).astype(F32)
    n_part = 2
    per_part = N_EXPERTS // n_part
    for r in range(h_ref.shape[0]):
        for part in range(n_part):
            for e in range(part * per_part, (part + 1) * per_part):
                idx = r * N_EXPERTS + e
                hit = row == slot_ref[idx:idx + 1, :]
                d_ref[r, e * cap:(e + 1) * cap, :] = jnp.where(hit, 1.0, 0.0).astype(BF16)
                gate = jnp.sum(jnp.where(hit, aff_ref[idx:idx + 1, :], 0.0), axis=1, keepdims=True)
                g_ref[r, e * cap:(e + 1) * cap, :] = jnp.broadcast_to(gate, (cap, LANES))
            rows = slice(part * per_part * cap, (part + 1) * per_part * cap)
            xs_ref[r, rows, :] = _dot(d_ref[r, rows, :], h_ref[r]).astype(BF16)


def _requests_per_step(n_tok):
    return max(1, STEP_TOKENS // n_tok)


def _dispatch(h3d, router_wt):
    B, n_tok, _ = h3d.shape
    cap = CAPACITY_FACTOR * n_tok // N_EXPERTS
    rows = N_EXPERTS * cap
    aff, slot = _route(h3d, router_wt)
    nr = _requests_per_step(n_tok)
    per_req = pl.BlockSpec((nr * N_EXPERTS, n_tok), lambda b: (b, 0))
    blk = lambda w: pl.BlockSpec((nr, rows, w), lambda b: (b, 0, 0))
    return pl.pallas_call(
        functools.partial(_dispatch_kernel, n_tok=n_tok, cap=cap),
        grid=(B // nr,),
        in_specs=[pl.BlockSpec((nr, n_tok, D_MODEL), lambda b: (b, 0, 0)), per_req, per_req],
        out_specs=[blk(D_MODEL), blk(n_tok), blk(LANES)],
        out_shape=[jax.ShapeDtypeStruct((B, rows, D_MODEL), BF16),
                   jax.ShapeDtypeStruct((B, rows, n_tok), BF16),
                   jax.ShapeDtypeStruct((B, rows, LANES), F32)],
        compiler_params=_params("parallel"),
        name="moe_dispatch",
    )(h3d, aff, slot)


def _ffn_kernel(xp_ref, gp_ref, xs_ref, gs_ref, wg_ref, wu_ref, wd_ref, yp_ref, ys_ref):
    wg = wg_ref[...].astype(BF16)
    wu = wu_ref[...].astype(BF16)
    wd = wd_ref[...].astype(BF16)
    for x_ref, g_ref, y_ref in ((xp_ref, gp_ref, yp_ref), (xs_ref, gs_ref, ys_ref)):
        nb, cap, _ = x_ref.shape
        x = x_ref[...].reshape(nb * cap, D_MODEL)
        gate = g_ref[...].reshape(nb * cap, LANES)
        gate = jnp.concatenate([gate] * (EXPERT_FF // LANES), axis=1)
        act = _silu(_dot(x, wg)) * _dot(x, wu) * gate
        y = _dot(act.astype(BF16), wd)
        y_ref[...] = y.reshape(nb, cap, D_MODEL).astype(y_ref.dtype)


def _expert_ffn(xp, gp, xs, gs, w_gate, w_up, w_down, layer):
    def act_spec(a):
        nb, _, cap, w = a.shape
        return pl.BlockSpec((nb, None, cap, w), lambda e: (0, e, 0, 0))

    def w_spec(a):
        return pl.BlockSpec((None, None) + a.shape[2:], lambda e: (layer, e, 0, 0))

    return pl.pallas_call(
        _ffn_kernel,
        grid=(N_EXPERTS,),
        in_specs=[act_spec(xp), act_spec(gp), act_spec(xs), act_spec(gs),
                  w_spec(w_gate), w_spec(w_up), w_spec(w_down)],
        out_specs=[act_spec(xp), act_spec(xs)],
        out_shape=[jax.ShapeDtypeStruct(xp.shape, BF16), jax.ShapeDtypeStruct(xs.shape, BF16)],
        compiler_params=_params("parallel"),
        name="expert_ffn",
    )(xp, gp, xs, gs, w_gate, w_up, w_down)


def _combine_kernel(d_ref, y_ref, x_ref, mod_ref, o_ref):
    for r in range(d_ref.shape[0]):
        moe = lax.dot_general(d_ref[r], y_ref[r], (((0,), (0,)), ((), ())), preferred_element_type=F32)
        o_ref[r] = x_ref[r] + mod_ref[min(r, mod_ref.shape[0] - 1)] * moe


def _combine(dmat, y3d, x3d, mod_g2):
    B, n_tok, _ = x3d.shape
    rows = dmat.shape[1]
    nr = _requests_per_step(n_tok)
    shared_mod = mod_g2.shape[0] == 1
    mod_spec = (pl.BlockSpec((1, 1, D_MODEL), lambda b: (0, 0, 0)) if shared_mod
                else pl.BlockSpec((nr, 1, D_MODEL), lambda b: (b, 0, 0)))
    return pl.pallas_call(
        _combine_kernel,
        grid=(B // nr,),
        in_specs=[
            pl.BlockSpec((nr, rows, n_tok), lambda b: (b, 0, 0)),
            pl.BlockSpec((nr, rows, D_MODEL), lambda b: (b, 0, 0)),
            pl.BlockSpec((nr, n_tok, D_MODEL), lambda b: (b, 0, 0)),
            mod_spec,
        ],
        out_specs=pl.BlockSpec((nr, n_tok, D_MODEL), lambda b: (b, 0, 0)),
        out_shape=jax.ShapeDtypeStruct(x3d.shape, F32),
        compiler_params=_params("parallel"),
        name="moe_combine",
    )(dmat, y3d, x3d, mod_g2)


def _token_mixers(x3d, mod, p, tabs, layer, ctx=None):
    B, L, _ = x3d.shape
    x2d = x3d.reshape(B * L, D_MODEL)
    main, qkv = _inproj(x2d, mod, p['norm1_w'], p['w_in'], layer, L)
    spectra = _hyena_spectra(L, p['hy_w1'], p['hy_b1'], p['hy_w2'], p['hy_b2'], p['hy_w3'], p['hy_b3'],
                             p['hy_decay'], tabs[0], tabs[1])
    ya, yc = _seqmix(main.reshape(B, L, MAIN_COLS), p['hy_short_w'], p['hy_short_b'], p['sc_w'], p['sc_b'],
                     spectra, p['hy_bias'], tabs)
    qkv3d = qkv.reshape(B, L, ATT_COLS)
    if ctx is None:
        yb, k, v = _attention(qkv3d, p['q_norm_w'], p['k_norm_w'])
    else:
        yb = _attention(qkv3d, p['q_norm_w'], p['k_norm_w'], ctx[0], ctx[1], layer)
        k = v = None
    x_mid, h2 = _merge(ya.reshape(B * L, D_HYENA), yb.reshape(B * L, ATT_Q), yc.reshape(B * L, D_SCONV),
                       x2d, mod, p['norm1_w'], p['norm2_w'], p['w_in'], p['w_br_a'], p['w_br_b'], p['w_br_c'],
                       p['w_o'], layer, L)
    return x_mid.reshape(B, L, D_MODEL), h2.reshape(B, L, D_MODEL), k, v


def _moe_split(a, n_exp):
    B, rows, w = a.shape
    return a.reshape(B, n_exp, rows // n_exp, w)


def kernel(x_prompt, x_sample, cache_k, cache_v, c, c_ctx, mod_w, mod_b, norm1_w, norm2_w, w_in, hy_short_w, hy_short_b, hy_w1, hy_b1, hy_w2, hy_b2, hy_w3, hy_b3, hy_decay, hy_bias, q_norm_w, k_norm_w, sc_w, sc_b, w_br_a, w_br_b, w_br_c, w_o, router_w, exp_w_gate, exp_w_up, exp_w_down):
    n_dec = x_sample.shape[0]
    n_ctx = cache_k.shape[2]
    lp = x_prompt.shape[1]
    ls = x_sample.shape[1]

    cond_rows = COND_ROWS
    cond = jnp.concatenate([c_ctx[None, :], c, jnp.zeros((cond_rows - 1 - n_dec, D_MODEL), F32)], axis=0)
    mod = _modulation(cond, mod_w, mod_b).reshape(DEPTH, cond_rows, 6, D_MODEL)

    tabs_p = tuple(jnp.asarray(t).astype(BF16) for t in _split_dft_tables(lp))
    tabs_s = tuple(jnp.asarray(t).astype(BF16) for t in _split_dft_tables(ls))
    ctx_k = cache_k.reshape(n_dec, DEPTH, n_ctx, ATT_KV)
    ctx_v = cache_v.reshape(n_dec, DEPTH, n_ctx, ATT_KV)
    dense = {'w_in': w_in.astype(BF16), 'w_br_a': w_br_a.astype(BF16), 'w_br_b': w_br_b.astype(BF16),
             'w_br_c': w_br_c.astype(BF16), 'w_o': w_o.astype(BF16)}

    xp, xs = x_prompt, x_sample
    ks_new, vs_new = [], []
    for l in range(DEPTH):
        p = dict(dense)
        p.update({
            'norm1_w': norm1_w[l].reshape(1, D_MODEL), 'norm2_w': norm2_w[l].reshape(1, D_MODEL),
            'hy_short_w': hy_short_w[l], 'hy_short_b': hy_short_b[l],
            'hy_w1': hy_w1[l], 'hy_b1': hy_b1[l], 'hy_w2': hy_w2[l], 'hy_b2': hy_b2[l],
            'hy_w3': hy_w3[l], 'hy_b3': hy_b3[l], 'hy_decay': hy_decay[l], 'hy_bias': hy_bias[l],
            'q_norm_w': q_norm_w[l], 'k_norm_w': k_norm_w[l], 'sc_w': sc_w[l], 'sc_b': sc_b[l],
        })
        mod_p = mod[l, 0:1]
        mod_s = mod[l, 1:1 + n_dec]
        xp_mid, h2p, k_l, v_l = _token_mixers(xp, mod_p, p, tabs_p, l)
        xs_mid, h2s, _, _ = _token_mixers(xs, mod_s, p, tabs_s, l, (ctx_k, ctx_v))
        ks_new.append(k_l.reshape(k_l.shape[0], lp, N_KV_HEADS, HEAD_DIM))
        vs_new.append(v_l.reshape(v_l.shape[0], lp, N_KV_HEADS, HEAD_DIM))

        router_wt = router_w[l].T.astype(BF16)
        gp_x, gp_d, gp_g = _dispatch(h2p, router_wt)
        gs_x, gs_d, gs_g = _dispatch(h2s, router_wt)
        yp, ys = _expert_ffn(_moe_split(gp_x, N_EXPERTS), _moe_split(gp_g, N_EXPERTS),
                             _moe_split(gs_x, N_EXPERTS), _moe_split(gs_g, N_EXPERTS),
                             exp_w_gate, exp_w_up, exp_w_down, l)
        xp = _combine(gp_d, yp.reshape(gp_x.shape), xp_mid, mod_p[:, 5:6])
        xs = _combine(gs_d, ys.reshape(gs_x.shape), xs_mid, mod_s[:, 5:6])

    return (xp, xs, jnp.stack(ks_new, axis=1), jnp.stack(vs_new, axis=1))
```

```python
import functools
import math

import numpy as np
import jax
import jax.numpy as jnp
from jax import lax
from jax.experimental import pallas as pl
from jax.experimental.pallas import tpu as pltpu

D_MODEL = 1024
DEPTH = 2
GRID_W = 64
D_HYENA = 512
HYENA_ORDER = 2
HYENA_POS_BANDS = 16
HYENA_POS_DIM = 1 + 2 * HYENA_POS_BANDS
HYENA_FILTER_HIDDEN = 64
HYENA_WINDOW_SHIFT = 0.05
N_HEADS = 8
N_KV_HEADS = 2
HEAD_DIM = 64
HEADS_PER_KV = N_HEADS // N_KV_HEADS
ATT_Q = N_HEADS * HEAD_DIM
ATT_KV = N_KV_HEADS * HEAD_DIM
ROPE_THETA = 10000.0
D_SCONV = 512
N_EXPERTS = 16
EXPERT_FF = 512
CAPACITY_FACTOR = 2
NORM_EPS = 1e-6
HY_COLS = (HYENA_ORDER + 1) * D_HYENA
ATT_COLS = ATT_Q + 2 * ATT_KV
SC_COLS = 3 * D_SCONV
GATE_COLS = 3 * D_MODEL
MIX_COLS = HY_COLS + ATT_COLS + SC_COLS
D_IN = MIX_COLS + GATE_COLS

F32 = jnp.float32
BF16 = jnp.bfloat16

V7X_VMEM_BYTES = 64 * 1024 * 1024
VMEM_LIMIT = V7X_VMEM_BYTES - 8 * 1024 * 1024
LANES = 128
MLP_PAD = 128
STEP_TOKENS = 1024
INPROJ_ROWS = 1024
MERGE_ROWS = 512
HYENA_CHANNEL_TILE = 256
MOD_COL_TILE = 1024
COND_ROWS = 16


def _params(*sem):
    return pltpu.CompilerParams(dimension_semantics=sem, vmem_limit_bytes=VMEM_LIMIT)


def _const_spec(shape):
    nd = len(shape)
    return pl.BlockSpec(shape, lambda *_: (0,) * nd, pipeline_mode=pl.Buffered(1))


def _layer_spec(shape, layer):
    nd = len(shape)
    return pl.BlockSpec((None,) + tuple(shape), lambda *_: (layer,) + (0,) * nd, pipeline_mode=pl.Buffered(1))


def _dot(a, b):
    return jnp.dot(a, b, preferred_element_type=F32)


def _silu(x):
    return x * (1.0 / (1.0 + jnp.exp(-x)))


def _sigmoid(x):
    return 1.0 / (1.0 + jnp.exp(-x))


@functools.lru_cache(maxsize=None)
def _split_dft_tables(L):
    h = L // 2
    k = np.arange(h, dtype=np.int64)
    alt = 1.0 - 2.0 * (k % 2)
    fwd, inv = [], []
    for odd in (0, 1):
        idx = (k[:, None] * (2 * k[None, :] + odd)) % (2 * L)
        ang = idx.astype(np.float64) * (np.pi / L)
        c = np.cos(ang)
        s = np.sin(ang)
        s[0, :] = alt
        fwd.append(np.concatenate([c, s], axis=0).astype(np.float32))
        inv.append(np.concatenate([c.T, s.T], axis=1).astype(np.float32))
    return fwd[0], fwd[1], inv[0], inv[1]


@functools.lru_cache(maxsize=None)
def _pos_features(L):
    n = np.arange(L, dtype=np.float64)
    t = n / max(L - 1, 1)
    bands = np.linspace(1e-4, HYENA_POS_BANDS - 1, HYENA_POS_BANDS)
    ang = 2.0 * math.pi * n[:, None] * bands[None, :] / L
    z = np.concatenate([t[:, None], np.cos(ang), np.sin(ang)], axis=-1)
    zp = np.zeros((L, MLP_PAD), np.float32)
    zp[:, :HYENA_POS_DIM] = z
    return zp


@functools.lru_cache(maxsize=None)
def _rope_tables(L):
    rows = L // GRID_W
    row = np.repeat(np.arange(rows, dtype=np.float64), GRID_W)
    col = np.tile(np.arange(GRID_W, dtype=np.float64), rows)
    axis_dim = HEAD_DIM // 2
    inv_freq = ROPE_THETA ** (-np.arange(0, axis_dim, 2, dtype=np.float64) / axis_dim)
    ar = row[:, None] * inv_freq[None, :]
    ac = col[:, None] * inv_freq[None, :]
    ang = np.concatenate([ar, ar, ac, ac], axis=-1)
    cos = np.cos(ang).astype(np.float32)
    sin = np.sin(ang).astype(np.float32)
    return np.tile(cos, (1, N_HEADS)), np.tile(sin, (1, N_HEADS))


@functools.lru_cache(maxsize=None)
def _head_tables():
    lane = np.arange(ATT_Q)
    block_ones = (lane[:, None] // HEAD_DIM == lane[None, :] // HEAD_DIM).astype(np.float32)
    src = np.arange(ATT_KV)
    dst = np.arange(HEADS_PER_KV * HEAD_DIM)
    rep = np.stack([(src[:, None] == g * HEAD_DIM + dst[None, :] % HEAD_DIM) for g in range(N_KV_HEADS)])
    return block_ones, rep.astype(np.float32)


@functools.lru_cache(maxsize=None)
def _prefix_table(n):
    i = np.arange(n)
    return (i[:, None] < i[None, :]).astype(np.float32)


def _mod_kernel(cond_ref, w_ref, b_ref, o_ref):
    a = _silu(cond_ref[...]).astype(BF16)
    o_ref[...] = _dot(a, w_ref[...].astype(BF16)) + b_ref[...]


def _modulation(cond, mod_w, mod_b):
    rows = cond.shape[0]
    tn = MOD_COL_TILE
    ncols = mod_w.shape[-1]
    return pl.pallas_call(
        _mod_kernel,
        grid=(DEPTH, ncols // tn),
        in_specs=[
            pl.BlockSpec((rows, D_MODEL), lambda l, j: (0, 0)),
            pl.BlockSpec((None, D_MODEL, tn), lambda l, j: (l, 0, j)),
            pl.BlockSpec((None, 1, tn), lambda l, j: (l, 0, j)),
        ],
        out_specs=pl.BlockSpec((None, rows, tn), lambda l, j: (l, 0, j)),
        out_shape=jax.ShapeDtypeStruct((DEPTH, rows, ncols), F32),
        compiler_params=_params("parallel", "parallel"),
        name="modulation",
    )(cond, mod_w, mod_b.reshape(DEPTH, 1, ncols))


def _rms_modulate(x, norm_w, shift, scale):
    y = x * lax.rsqrt(jnp.mean(x * x, axis=-1, keepdims=True) + NORM_EPS)
    return (y * norm_w) * (1.0 + scale) + shift


def _conv3(x, w, b, first, last):
    n = x.shape[0]
    prev = jnp.where(first, 0.0, pltpu.roll(x, 1, 0))
    nxt = jnp.where(last, 0.0, pltpu.roll(x, n - 1, 0))
    return prev * w[0:1, :] + x * w[1:2, :] + nxt * w[2:3, :] + b


def _inproj_kernel(x_ref, mod_ref, nw_ref, w_ref, hyw_ref, hyb_ref, scw_ref, scb_ref,
                   hyc_ref, yc_ref, qkv_ref, h_sc, *, L):
    h = _rms_modulate(x_ref[...], nw_ref[...], mod_ref[0:1, :], mod_ref[1:2, :])
    h_sc[...] = h.astype(BF16)
    tm = x_ref.shape[0]

    def edges(cols):
        pos = lax.broadcasted_iota(jnp.int32, (tm, cols), 0) % L
        return pos == 0, pos == L - 1

    first, last = edges(ATT_COLS)
    for lo in range(0, HY_COLS, ATT_COLS):
        cols = slice(lo, lo + ATT_COLS)
        y = _dot(h_sc[...], w_ref[:, cols])
        hyc_ref[:, cols] = _conv3(y, hyw_ref[:, cols], hyb_ref[:, cols], first, last).astype(BF16)
    qkv_ref[...] = _dot(h_sc[...], w_ref[:, HY_COLS:HY_COLS + ATT_COLS])
    sc0 = HY_COLS + ATT_COLS
    u, bgate, cgate = (_dot(h_sc[...], w_ref[:, sc0 + k * D_SCONV:sc0 + (k + 1) * D_SCONV]) for k in range(3))
    first, last = edges(D_SCONV)
    yc_ref[...] = (bgate * _conv3(cgate * u, scw_ref[...], scb_ref[...], first, last)).astype(BF16)


def _inproj(x2d, mod, norm_w, w_in_bf, hy_short_w, hy_short_b, sc_w, sc_b, layer, rows_per_req, tm=INPROJ_ROWS):
    T = x2d.shape[0]
    n_req = mod.shape[0]
    req = (lambda i: (i * tm) // rows_per_req) if n_req > 1 else (lambda i: 0)
    out = lambda w: pl.BlockSpec((tm, w), lambda i: (i, 0))
    return pl.pallas_call(
        functools.partial(_inproj_kernel, L=rows_per_req),
        grid=(T // tm,),
        in_specs=[
            pl.BlockSpec((tm, D_MODEL), lambda i: (i, 0)),
            pl.BlockSpec((None, 6, D_MODEL), lambda i: (req(i), 0, 0)),
            _const_spec((1, D_MODEL)),
            _layer_spec((D_MODEL, MIX_COLS), layer),
            _const_spec((3, HY_COLS)), _const_spec((1, HY_COLS)),
            _const_spec((3, D_SCONV)), _const_spec((1, D_SCONV)),
        ],
        out_specs=[out(HY_COLS), out(D_SCONV), out(ATT_COLS)],
        out_shape=[jax.ShapeDtypeStruct((T, HY_COLS), BF16), jax.ShapeDtypeStruct((T, D_SCONV), BF16),
                   jax.ShapeDtypeStruct((T, ATT_COLS), F32)],
        scratch_shapes=[pltpu.VMEM((tm, D_MODEL), BF16)],
        compiler_params=_params("parallel"),
        name="inproj",
    )(x2d, mod, norm_w, w_in_bf, hy_short_w, hy_short_b.reshape(1, HY_COLS), sc_w, sc_b.reshape(1, D_SCONV))


def _split_parity(par_sc, z):
    half = z.shape[0] // 2
    n_lt = z.shape[1] // LANES
    for c in range(n_lt):
        par_sc[c] = z[:, c * LANES:(c + 1) * LANES]
    return tuple(jnp.concatenate([par_sc[c, pl.ds(par, half, stride=2), :] for c in range(n_lt)], axis=1)
                 for par in (0, 1))


def _join_parity(par_sc, even, odd):
    half = even.shape[0]
    n_lt = even.shape[1] // LANES
    for c in range(n_lt):
        par_sc[c, pl.ds(0, half, stride=2), :] = even[:, c * LANES:(c + 1) * LANES]
        par_sc[c, pl.ds(1, half, stride=2), :] = odd[:, c * LANES:(c + 1) * LANES]
    return jnp.concatenate([par_sc[c] for c in range(n_lt)], axis=1)


def _fold_spectrum(te, to):
    half = te.shape[0] // 2
    row0 = lax.broadcasted_iota(jnp.int32, (half, te.shape[1]), 0) == 0
    ce, se, co, so = te[:half], te[half:], to[:half], to[half:]
    a = jnp.concatenate([ce + co, jnp.where(row0, se, ce - co)], axis=0)
    b = jnp.concatenate([jnp.where(row0, ce - co, se + so), jnp.where(row0, so, so - se)], axis=0)
    return a, b


def _unfold_spectrum(yr, yi):
    half = yr.shape[0] // 2
    row0 = lax.broadcasted_iota(jnp.int32, (half, yr.shape[1]), 0) == 0
    yr_lo, yr_up, yi_lo, yi_up = yr[:half], yr[half:], yi[:half], yi[half:]
    even = jnp.concatenate([jnp.where(row0, yr_lo + yi_lo, yr_lo + yr_up),
                            jnp.where(row0, yr_up, yi_lo - yi_up)], axis=0)
    odd = jnp.concatenate([jnp.where(row0, yr_lo - yi_lo, yr_lo - yr_up),
                           jnp.where(row0, yi_up, yi_lo + yi_up)], axis=0)
    return even, odd


def _filter_kernel(z_ref, w1_ref, b1_ref, w2_ref, b2_ref, w3f_ref, b3f_ref, w3b_ref, b3b_ref,
                   decf_ref, decb_ref, fwde_ref, fwdo_ref, p_ref, h_sc, par_sc, *, L):
    hi = lax.Precision.HIGHEST

    @pl.when((pl.program_id(0) == 0) & (pl.program_id(1) == 0))
    def _():
        h1 = jnp.sin(jnp.dot(z_ref[...], w1_ref[...], precision=hi, preferred_element_type=F32) + b1_ref[...])
        h_sc[...] = jnp.sin(jnp.dot(h1, w2_ref[...], precision=hi, preferred_element_type=F32) + b2_ref[...])

    h = h_sc[...]
    tc = w3f_ref.shape[1]
    pos = lax.broadcasted_iota(jnp.int32, (L, tc), 0)
    t = pos.astype(F32) / float(max(L - 1, 1))

    def taps(w3_ref, b3_ref, dec_ref):
        g = jnp.dot(h, w3_ref[...], precision=hi, preferred_element_type=F32) + b3_ref[...]
        return g * (jnp.exp(-t * jnp.abs(dec_ref[...])) + HYENA_WINDOW_SHIFT)

    hf = taps(w3f_ref, b3f_ref, decf_ref)
    hb = jnp.where(pos == 0, 0.0, taps(w3b_ref, b3b_ref, decb_ref))

    def transform(taps_lc):
        even, odd = _split_parity(par_sc, taps_lc)
        return _fold_spectrum(_dot(fwde_ref[...], even.astype(BF16)), _dot(fwdo_ref[...], odd.astype(BF16)))

    fa, fb = transform(hf)
    ba, bb = transform(hb)
    k_re = fa + ba
    k_im = bb - fb
    k_ny = fb + bb
    first = pos == 0
    inv_n = 1.0 / (2 * L)
    p_ref[0] = jnp.where(first, k_re * inv_n, k_re * (2.0 * inv_n))
    p_ref[1] = jnp.where(first, 0.0, k_im * (2.0 * inv_n))
    p_ref[2] = jnp.where(first, k_ny * inv_n, k_re * (2.0 * inv_n))


def _hyena_spectra(L, w1, b1, w2, b2, w3, b3, decay, fwd_even, fwd_odd):
    pad_h = MLP_PAD - HYENA_FILTER_HIDDEN
    z = jnp.asarray(_pos_features(L))
    w1p = jnp.pad(w1, ((0, MLP_PAD - HYENA_POS_DIM), (0, pad_h)))
    b1p = jnp.pad(b1, (0, pad_h)).reshape(1, MLP_PAD)
    w2p = jnp.pad(w2, ((0, pad_h), (0, pad_h)))
    b2p = jnp.pad(b2, (0, pad_h)).reshape(1, MLP_PAD)
    w3p = jnp.pad(w3, ((0, pad_h), (0, 0)))
    ncol = w3.shape[1]
    b3r = b3.reshape(1, ncol)
    decr = decay.reshape(1, ncol)
    tc = HYENA_CHANNEL_TILE
    nct = D_HYENA // tc
    per_dir = HYENA_ORDER * nct
    col_f = lambda o, c: (0, o * nct + c)
    col_b = lambda o, c: (0, per_dir + o * nct + c)
    return pl.pallas_call(
        functools.partial(_filter_kernel, L=L),
        grid=(HYENA_ORDER, nct),
        in_specs=[
            _const_spec((L, MLP_PAD)),
            _const_spec((MLP_PAD, MLP_PAD)), _const_spec((1, MLP_PAD)),
            _const_spec((MLP_PAD, MLP_PAD)), _const_spec((1, MLP_PAD)),
            pl.BlockSpec((MLP_PAD, tc), col_f), pl.BlockSpec((1, tc), col_f),
            pl.BlockSpec((MLP_PAD, tc), col_b), pl.BlockSpec((1, tc), col_b),
            pl.BlockSpec((1, tc), col_f), pl.BlockSpec((1, tc), col_b),
            _const_spec((L, L // 2)), _const_spec((L, L // 2)),
        ],
        out_specs=pl.BlockSpec((None, 3, L, tc), lambda o, c: (o, 0, 0, c)),
        out_shape=jax.ShapeDtypeStruct((HYENA_ORDER, 3, L, D_HYENA), F32),
        scratch_shapes=[pltpu.VMEM((L, MLP_PAD), F32), pltpu.VMEM((tc // LANES, L, LANES), F32)],
        compiler_params=_params("arbitrary", "arbitrary"),
        name="hyena_spectra",
    )(z, w1p, b1p, w2p, b2p, w3p, b3r, w3p, b3r, decr, decr, fwd_even, fwd_odd)


def _seqmix_kernel(v_ref, x1_ref, x2_ref, p_ref, hb_ref, fwde_ref, fwdo_ref, inve_ref, invo_ref, ya_ref,
                   spec_sc, par_sc):
    n_req = v_ref.shape[0]
    mult_refs = (x1_ref, x2_ref)

    def chain(r):
        z = v_ref[r].astype(F32)
        z_even, z_odd = _split_parity(par_sc.at[r], z)
        yield
        for o in range(HYENA_ORDER):
            te = _dot(fwde_ref[...], z_even.astype(BF16))
            to = _dot(fwdo_ref[...], z_odd.astype(BF16))
            yield
            a, b = _fold_spectrum(te, to)
            p2 = p_ref[o, 1]
            even, odd = _unfold_spectrum(a * p_ref[o, 0] + b * p2, b * p_ref[o, 2] - a * p2)
            spec_sc[r, 0] = even.astype(BF16)
            spec_sc[r, 1] = odd.astype(BF16)
            yield
            y_even = _dot(inve_ref[...], spec_sc[r, 0])
            y_odd = _dot(invo_ref[...], spec_sc[r, 1])
            yield
            z = mult_refs[o][r].astype(F32) * (_join_parity(par_sc.at[r], y_even, y_odd) + hb_ref[o:o + 1, :] * z)
            if o < HYENA_ORDER - 1:
                z_even, z_odd = _split_parity(par_sc.at[r], z)
            else:
                ya_ref[r] = z.astype(ya_ref.dtype)
            yield

    chains = [chain(r) for r in range(n_req)]
    n_stages = 1 + 4 * HYENA_ORDER
    for tick in range(n_stages + n_req - 1):
        for r, c in enumerate(chains):
            if 0 <= tick - r < n_stages:
                next(c)


def _seqmix(hyc3d, spectra, hy_bias, tabs):
    B, L, _ = hyc3d.shape
    tc = HYENA_CHANNEL_TILE
    nct = D_HYENA // tc
    nr = max(2, STEP_TOKENS // L)
    act = lambda off: pl.BlockSpec((nr, L, tc), lambda c, b, off=off: (b, 0, off * nct + c))
    return pl.pallas_call(
        _seqmix_kernel,
        grid=(nct, B // nr),
        in_specs=[
            act(0), act(1), act(2),
            pl.BlockSpec((HYENA_ORDER, 3, L, tc), lambda c, b: (0, 0, 0, c), pipeline_mode=pl.Buffered(1)),
            pl.BlockSpec((HYENA_ORDER, tc), lambda c, b: (0, c)),
            _const_spec((L, L // 2)), _const_spec((L, L // 2)), _const_spec((L // 2, L)), _const_spec((L // 2, L)),
        ],
        out_specs=pl.BlockSpec((nr, L, tc), lambda c, b: (b, 0, c)),
        out_shape=jax.ShapeDtypeStruct((B, L, D_HYENA), BF16),
        scratch_shapes=[pltpu.VMEM((nr, 2, L, tc), BF16), pltpu.VMEM((nr, tc // LANES, L, LANES), F32)],
        compiler_params=_params("arbitrary", "arbitrary"),
        name="seqmix",
    )(hyc3d, hyc3d, hyc3d, spectra, hy_bias, *tabs)


def _head_rms(x, ones_ref, w):
    sq = x * x
    hi = sq.astype(BF16)
    lo = (sq - hi.astype(F32)).astype(BF16)
    ones = ones_ref[...]
    width = x.shape[1]
    ss = _dot(hi, ones[:width, :width]) + _dot(lo, ones[:width, :width])
    return x * lax.rsqrt(ss * (1.0 / HEAD_DIM) + NORM_EPS) * w


def _rope(x, cos, sin):
    width = x.shape[1]
    lane = lax.broadcasted_iota(jnp.int32, x.shape, 1)
    half = HEAD_DIM // 4
    low = (lane % (2 * half)) < half
    rot = jnp.where(low, -pltpu.roll(x, width - half, 1), pltpu.roll(x, half, 1))
    return x * cos + rot * sin


def _attn_kernel(*refs, rope):
    if rope:
        (q_ref, k_ref, v_ref, ck_ref, cv_ref, qw_ref, kw_ref, ones_ref, rep_ref, rept_ref,
         cos_ref, sin_ref, yb_ref) = refs
    else:
        q_ref, k_ref, v_ref, qw_ref, kw_ref, ones_ref, rep_ref, rept_ref, yb_ref, ko_ref, vo_ref = refs
    n_req, L, _ = q_ref.shape
    gw = HEADS_PER_KV * HEAD_DIM
    lane = lax.broadcasted_iota(jnp.int32, (L, gw), 1)
    for r in range(n_req):
        kn = _head_rms(k_ref[r], ones_ref, kw_ref[...])
        v = v_ref[r]
        q = _head_rms(q_ref[r], ones_ref, qw_ref[...])
        if rope:
            keys = _rope(kn, cos_ref[:, :ATT_KV], sin_ref[:, :ATT_KV])
            keys = jnp.concatenate([ck_ref[r], keys], axis=0)
            vals = jnp.concatenate([cv_ref[r], v], axis=0)
            q = _rope(q, cos_ref[...], sin_ref[...])
        else:
            ko_ref[r] = kn
            vo_ref[r] = v
            keys, vals = kn, v
        keys = keys.astype(BF16)
        vals = vals.astype(BF16)
        q = q * (HEAD_DIM ** -0.5)
        for g in range(N_KV_HEADS):
            k4 = lax.dot_general(rept_ref[g], keys, (((1,), (1,)), ((), ())), preferred_element_type=F32).astype(BF16)
            v4 = _dot(vals, rep_ref[g]).astype(BF16)
            qg = q[:, g * gw:(g + 1) * gw]
            acc = jnp.zeros((L, gw), F32)
            for h in range(HEADS_PER_KV):
                mine = (lane // HEAD_DIM) == h
                qm = jnp.where(mine, qg, 0.0).astype(BF16)
                s = _dot(qm, k4)
                p = jnp.exp(s - jnp.max(s, axis=-1, keepdims=True))
                denom = jnp.sum(p, axis=-1, keepdims=True)
                o4 = _dot(p.astype(BF16), v4)
                acc = jnp.where(mine, o4 * (1.0 / denom), acc)
            yb_ref[r, :, g * gw:(g + 1) * gw] = acc.astype(yb_ref.dtype)


def _attention(qkv3d, q_norm_w, k_norm_w, ctx_k=None, ctx_v=None, layer=0):
    B, L, _ = qkv3d.shape
    rope = ctx_k is not None
    nr = _requests_per_step(L)
    ones_np, rep_np = _head_tables()
    ones = jnp.asarray(ones_np).astype(BF16)
    rep = jnp.asarray(rep_np).astype(BF16)
    rept = jnp.asarray(np.swapaxes(rep_np, 1, 2)).astype(BF16)
    qw = jnp.tile(q_norm_w, N_HEADS).reshape(1, ATT_Q)
    kw = jnp.tile(k_norm_w, N_KV_HEADS).reshape(1, ATT_KV)
    kblk = ATT_Q // ATT_KV
    in_specs = [
        pl.BlockSpec((nr, L, ATT_Q), lambda b: (b, 0, 0)),
        pl.BlockSpec((nr, L, ATT_KV), lambda b: (b, 0, kblk)),
        pl.BlockSpec((nr, L, ATT_KV), lambda b: (b, 0, kblk + 1)),
    ]
    args = [qkv3d, qkv3d, qkv3d]
    if rope:
        n_ctx = ctx_k.shape[2]
        in_specs += [pl.BlockSpec((nr, None, n_ctx, ATT_KV), lambda b: (b, layer, 0, 0))] * 2
        args += [ctx_k, ctx_v]
    in_specs += [_const_spec((1, ATT_Q)), _const_spec((1, ATT_KV)), _const_spec((ATT_Q, ATT_Q)),
                 _const_spec((N_KV_HEADS, ATT_KV, HEADS_PER_KV * HEAD_DIM)),
                 _const_spec((N_KV_HEADS, HEADS_PER_KV * HEAD_DIM, ATT_KV))]
    args += [qw, kw, ones, rep, rept]
    yb_shape = jax.ShapeDtypeStruct((B, L, ATT_Q), BF16)
    yb_spec = pl.BlockSpec((nr, L, ATT_Q), lambda b: (b, 0, 0))
    if rope:
        cos_np, sin_np = _rope_tables(L)
        in_specs += [_const_spec((L, ATT_Q))] * 2
        args += [jnp.asarray(cos_np), jnp.asarray(sin_np)]
        out_specs = yb_spec
        out_shape = yb_shape
    else:
        kv_spec = pl.BlockSpec((nr, L, ATT_KV), lambda b: (b, 0, 0))
        kv_shape = jax.ShapeDtypeStruct((B, L, ATT_KV), F32)
        out_specs = [yb_spec, kv_spec, kv_spec]
        out_shape = [yb_shape, kv_shape, kv_shape]
    return pl.pallas_call(
        functools.partial(_attn_kernel, rope=rope),
        grid=(B // nr,),
        in_specs=in_specs,
        out_specs=out_specs,
        out_shape=out_shape,
        compiler_params=_params("parallel"),
        name="attention",
    )(*args)


def _merge_kernel(ya_ref, yb_ref, yc_ref, x_ref, mod_ref, nw1_ref, nw2_ref, *rest):
    gate_w_refs = rest[:GATE_COLS // ATT_COLS]
    wa_ref, wb_ref, wc_ref, wo_ref, xo_ref, h2_ref = rest[GATE_COLS // ATT_COLS:]
    tm = x_ref.shape[0]
    n_chain = 2
    rows_per = tm // n_chain

    def chain(c):
        rows = pl.ds(c * rows_per, rows_per)
        x = x_ref[rows, :]
        h = _rms_modulate(x, nw1_ref[...], mod_ref[0:1, :], mod_ref[1:2, :]).astype(BF16)
        yield
        gates = jnp.concatenate([_dot(h, w_ref[...]) for w_ref in gate_w_refs], axis=1)
        da = _dot(ya_ref[rows, :], wa_ref[...])
        db = _dot(yb_ref[rows, :], wb_ref[...])
        dc = _dot(yc_ref[rows, :], wc_ref[...])
        yield
        merged = (_sigmoid(gates[:, 0:D_MODEL]) * da + _sigmoid(gates[:, D_MODEL:2 * D_MODEL]) * db
                  + _sigmoid(gates[:, 2 * D_MODEL:]) * dc).astype(BF16)
        yield
        proj = _dot(merged, wo_ref[...])
        yield
        x = x + mod_ref[2:3, :] * proj
        xo_ref[rows, :] = x
        h2_ref[rows, :] = _rms_modulate(x, nw2_ref[...], mod_ref[3:4, :], mod_ref[4:5, :]).astype(BF16)
        yield

    chains = [chain(c) for c in range(n_chain)]
    n_stages = 5
    for tick in range(n_stages + n_chain - 1):
        for c, ch in enumerate(chains):
            if 0 <= tick - c < n_stages:
                next(ch)


def _merge(ya, yb, yc, x2d, mod, norm1_w, norm2_w, w_in_bf, wa, wb, wc, wo, layer, rows_per_req, tm=MERGE_ROWS):
    T = x2d.shape[0]
    n_req = mod.shape[0]
    req = (lambda i: (i * tm) // rows_per_req) if n_req > 1 else (lambda i: 0)
    br = pl.BlockSpec((tm, D_HYENA), lambda i: (i, 0))
    row = pl.BlockSpec((tm, D_MODEL), lambda i: (i, 0))
    gate_blk0 = MIX_COLS // ATT_COLS
    gate_w = [pl.BlockSpec((None, D_MODEL, ATT_COLS), lambda i, k=k: (layer, 0, gate_blk0 + k),
                           pipeline_mode=pl.Buffered(1)) for k in range(GATE_COLS // ATT_COLS)]
    return pl.pallas_call(
        _merge_kernel,
        grid=(T // tm,),
        in_specs=[
            br, br, br, row,
            pl.BlockSpec((None, 6, D_MODEL), lambda i: (req(i), 0, 0)),
            _const_spec((1, D_MODEL)), _const_spec((1, D_MODEL)),
            *gate_w,
            _layer_spec((D_HYENA, D_MODEL), layer), _layer_spec((ATT_Q, D_MODEL), layer),
            _layer_spec((D_SCONV, D_MODEL), layer), _layer_spec((D_MODEL, D_MODEL), layer),
        ],
        out_specs=[row, row],
        out_shape=[jax.ShapeDtypeStruct((T, D_MODEL), F32), jax.ShapeDtypeStruct((T, D_MODEL), BF16)],
        compiler_params=_params("parallel"),
        name="merge",
    )(ya, yb, yc, x2d, mod, norm1_w, norm2_w, *([w_in_bf] * len(gate_w)), wa, wb, wc, wo)


def _route_kernel(h_ref, rw_ref, tri_ref, aff_ref, slot_ref, *, cap):
    b = pl.program_id(0)
    n_req = h_ref.shape[0]
    for r in range(n_req):
        logits = lax.dot_general(rw_ref[...], h_ref[r], (((1,), (1,)), ((), ())), preferred_element_type=F32)
        ex = jnp.exp(logits - jnp.max(logits, axis=0, keepdims=True))
        first_row = pl.multiple_of((b * n_req + r) * N_EXPERTS, N_EXPERTS)
        aff_ref[pl.ds(first_row, N_EXPERTS), :] = ex / jnp.sum(ex, axis=0, keepdims=True)

    @pl.when(b == pl.num_programs(0) - 1)
    def _():
        aff = aff_ref[...]

        def count(mask):
            return jnp.sum(jnp.where(mask, 1.0, 0.0), axis=1, keepdims=True)

        kth = jnp.zeros((aff.shape[0], 1), jnp.int32)
        for bit in range(30, -1, -1):
            trial = kth | (1 << bit)
            enough = count(aff >= lax.bitcast_convert_type(trial, F32)) >= cap
            kth = jnp.where(enough, trial, kth)
        next_up = lax.bitcast_convert_type(kth + 1, F32)
        above = aff >= next_up
        tied = (aff >= lax.bitcast_convert_type(kth, F32)) & (aff < next_up)
        tri = tri_ref[...]
        tied_before = _dot(jnp.where(tied, 1.0, 0.0).astype(BF16), tri)
        chosen = above | (tied & (tied_before < (cap - count(above))))
        slot = _dot(jnp.where(chosen, 1.0, 0.0).astype(BF16), tri)
        slot_ref[...] = jnp.where(chosen, slot, -1.0)


def _route(h3d, router_wt):
    B, n_tok, _ = h3d.shape
    cap = CAPACITY_FACTOR * n_tok // N_EXPERTS
    tri = jnp.asarray(_prefix_table(n_tok)).astype(BF16)
    whole = pl.BlockSpec((B * N_EXPERTS, n_tok), lambda b: (0, 0))
    shape = jax.ShapeDtypeStruct((B * N_EXPERTS, n_tok), F32)
    nr = _requests_per_step(n_tok)
    return pl.pallas_call(
        functools.partial(_route_kernel, cap=cap),
        grid=(B // nr,),
        in_specs=[
            pl.BlockSpec((nr, n_tok, D_MODEL), lambda b: (b, 0, 0)),
            _const_spec((N_EXPERTS, D_MODEL)),
            _const_spec((n_tok, n_tok)),
        ],
        out_specs=[whole, whole],
        out_shape=[shape, shape],
        compiler_params=_params("arbitrary"),
        name="moe_route",
    )(h3d, router_wt, tri)


def _dispatch_kernel(h_ref, aff_ref, slot_ref, xs_ref, d_ref, g_ref, *, n_tok, cap):
    row = lax.broadcasted_iota(jnp.int32, (cap, n_tok), 0).astype(F32)
    n_part = 2
    per_part = N_EXPERTS // n_part
    for r in range(h_ref.shape[0]):
        for part in range(n_part):
            for e in range(part * per_part, (part + 1) * per_part):
                idx = r * N_EXPERTS + e
                hit = row == slot_ref[idx:idx + 1, :]
                d_ref[r, e * cap:(e + 1) * cap, :] = jnp.where(hit, 1.0, 0.0).astype(BF16)
                gate = jnp.sum(jnp.where(hit, aff_ref[idx:idx + 1, :], 0.0), axis=1, keepdims=True)
                g_ref[r, e * cap:(e + 1) * cap, :] = jnp.broadcast_to(gate, (cap, LANES))
            rows = slice(part * per_part * cap, (part + 1) * per_part * cap)
            xs_ref[r, rows, :] = _dot(d_ref[r, rows, :], h_ref[r]).astype(BF16)


def _requests_per_step(n_tok):
    return max(1, STEP_TOKENS // n_tok)


def _dispatch(h3d, router_wt):
    B, n_tok, _ = h3d.shape
    cap = CAPACITY_FACTOR * n_tok // N_EXPERTS
    rows = N_EXPERTS * cap
    aff, slot = _route(h3d, router_wt)
    nr = _requests_per_step(n_tok)
    per_req = pl.BlockSpec((nr * N_EXPERTS, n_tok), lambda b: (b, 0))
    blk = lambda w: pl.BlockSpec((nr, rows, w), lambda b: (b, 0, 0))
    return pl.pallas_call(
        functools.partial(_dispatch_kernel, n_tok=n_tok, cap=cap),
        grid=(B // nr,),
        in_specs=[pl.BlockSpec((nr, n_tok, D_MODEL), lambda b: (b, 0, 0)), per_req, per_req],
        out_specs=[blk(D_MODEL), blk(n_tok), blk(LANES)],
        out_shape=[jax.ShapeDtypeStruct((B, rows, D_MODEL), BF16),
                   jax.ShapeDtypeStruct((B, rows, n_tok), BF16),
                   jax.ShapeDtypeStruct((B, rows, LANES), F32)],
        compiler_params=_params("parallel"),
        name="moe_dispatch",
    )(h3d, aff, slot)


def _ffn_kernel(xp_ref, gp_ref, xs_ref, gs_ref, wg_ref, wu_ref, wd_ref, yp_ref, ys_ref):
    wg = wg_ref[...].astype(BF16)
    wu = wu_ref[...].astype(BF16)
    wd = wd_ref[...].astype(BF16)
    for x_ref, g_ref, y_ref in ((xp_ref, gp_ref, yp_ref), (xs_ref, gs_ref, ys_ref)):
        nb, cap, _ = x_ref.shape
        x = x_ref[...].reshape(nb * cap, D_MODEL)
        gate = g_ref[...].reshape(nb * cap, LANES)
        gate = jnp.concatenate([gate] * (EXPERT_FF // LANES), axis=1)
        act = _silu(_dot(x, wg)) * _dot(x, wu) * gate
        y = _dot(act.astype(BF16), wd)
        y_ref[...] = y.reshape(nb, cap, D_MODEL).astype(y_ref.dtype)


def _expert_ffn(xp, gp, xs, gs, w_gate, w_up, w_down, layer):
    def act_spec(a):
        nb, _, cap, w = a.shape
        return pl.BlockSpec((nb, None, cap, w), lambda e: (0, e, 0, 0))

    def w_spec(a):
        return pl.BlockSpec((None, None) + a.shape[2:], lambda e: (layer, e, 0, 0))

    return pl.pallas_call(
        _ffn_kernel,
        grid=(N_EXPERTS,),
        in_specs=[act_spec(xp), act_spec(gp), act_spec(xs), act_spec(gs),
                  w_spec(w_gate), w_spec(w_up), w_spec(w_down)],
        out_specs=[act_spec(xp), act_spec(xs)],
        out_shape=[jax.ShapeDtypeStruct(xp.shape, BF16), jax.ShapeDtypeStruct(xs.shape, BF16)],
        compiler_params=_params("parallel"),
        name="expert_ffn",
    )(xp, gp, xs, gs, w_gate, w_up, w_down)


def _combine_kernel(d_ref, y_ref, x_ref, mod_ref, o_ref):
    for r in range(d_ref.shape[0]):
        moe = lax.dot_general(d_ref[r], y_ref[r], (((0,), (0,)), ((), ())), preferred_element_type=F32)
        o_ref[r] = x_ref[r] + mod_ref[min(r, mod_ref.shape[0] - 1)] * moe


def _combine(dmat, y3d, x3d, mod_g2):
    B, n_tok, _ = x3d.shape
    rows = dmat.shape[1]
    nr = _requests_per_step(n_tok)
    shared_mod = mod_g2.shape[0] == 1
    mod_spec = (pl.BlockSpec((1, 1, D_MODEL), lambda b: (0, 0, 0)) if shared_mod
                else pl.BlockSpec((nr, 1, D_MODEL), lambda b: (b, 0, 0)))
    return pl.pallas_call(
        _combine_kernel,
        grid=(B // nr,),
        in_specs=[
            pl.BlockSpec((nr, rows, n_tok), lambda b: (b, 0, 0)),
            pl.BlockSpec((nr, rows, D_MODEL), lambda b: (b, 0, 0)),
            pl.BlockSpec((nr, n_tok, D_MODEL), lambda b: (b, 0, 0)),
            mod_spec,
        ],
        out_specs=pl.BlockSpec((nr, n_tok, D_MODEL), lambda b: (b, 0, 0)),
        out_shape=jax.ShapeDtypeStruct(x3d.shape, F32),
        compiler_params=_params("parallel"),
        name="moe_combine",
    )(dmat, y3d, x3d, mod_g2)


def _token_mixers(x3d, mod, p, tabs, layer, ctx=None):
    B, L, _ = x3d.shape
    x2d = x3d.reshape(B * L, D_MODEL)
    hyc, yc, qkv = _inproj(x2d, mod, p['norm1_w'], p['w_in'], p['hy_short_w'], p['hy_short_b'], p['sc_w'],
                           p['sc_b'], layer, L)
    spectra = _hyena_spectra(L, p['hy_w1'], p['hy_b1'], p['hy_w2'], p['hy_b2'], p['hy_w3'], p['hy_b3'],
                             p['hy_decay'], tabs[0], tabs[1])
    ya = _seqmix(hyc.reshape(B, L, HY_COLS), spectra, p['hy_bias'], tabs)
    qkv3d = qkv.reshape(B, L, ATT_COLS)
    if ctx is None:
        yb, k, v = _attention(qkv3d, p['q_norm_w'], p['k_norm_w'])
    else:
        yb = _attention(qkv3d, p['q_norm_w'], p['k_norm_w'], ctx[0], ctx[1], layer)
        k = v = None
    x_mid, h2 = _merge(ya.reshape(B * L, D_HYENA), yb.reshape(B * L, ATT_Q), yc,
                       x2d, mod, p['norm1_w'], p['norm2_w'], p['w_in'], p['w_br_a'], p['w_br_b'], p['w_br_c'],
                       p['w_o'], layer, L)
    return x_mid.reshape(B, L, D_MODEL), h2.reshape(B, L, D_MODEL), k, v


def _moe_split(a, n_exp):
    B, rows, w = a.shape
    return a.reshape(B, n_exp, rows // n_exp, w)


def kernel(x_prompt, x_sample, cache_k, cache_v, c, c_ctx, mod_w, mod_b, norm1_w, norm2_w, w_in, hy_short_w, hy_short_b, hy_w1, hy_b1, hy_w2, hy_b2, hy_w3, hy_b3, hy_decay, hy_bias, q_norm_w, k_norm_w, sc_w, sc_b, w_br_a, w_br_b, w_br_c, w_o, router_w, exp_w_gate, exp_w_up, exp_w_down):
    n_dec = x_sample.shape[0]
    n_ctx = cache_k.shape[2]
    lp = x_prompt.shape[1]
    ls = x_sample.shape[1]

    cond_rows = COND_ROWS
    cond = jnp.concatenate([c_ctx[None, :], c, jnp.zeros((cond_rows - 1 - n_dec, D_MODEL), F32)], axis=0)
    mod = _modulation(cond, mod_w, mod_b).reshape(DEPTH, cond_rows, 6, D_MODEL)

    tabs_p = tuple(jnp.asarray(t).astype(BF16) for t in _split_dft_tables(lp))
    tabs_s = tuple(jnp.asarray(t).astype(BF16) for t in _split_dft_tables(ls))
    ctx_k = cache_k.reshape(n_dec, DEPTH, n_ctx, ATT_KV)
    ctx_v = cache_v.reshape(n_dec, DEPTH, n_ctx, ATT_KV)
    dense = {'w_in': w_in.astype(BF16), 'w_br_a': w_br_a.astype(BF16), 'w_br_b': w_br_b.astype(BF16),
             'w_br_c': w_br_c.astype(BF16), 'w_o': w_o.astype(BF16)}

    xp, xs = x_prompt, x_sample
    ks_new, vs_new = [], []
    for l in range(DEPTH):
        p = dict(dense)
        p.update({
            'norm1_w': norm1_w[l].reshape(1, D_MODEL), 'norm2_w': norm2_w[l].reshape(1, D_MODEL),
            'hy_short_w': hy_short_w[l], 'hy_short_b': hy_short_b[l],
            'hy_w1': hy_w1[l], 'hy_b1': hy_b1[l], 'hy_w2': hy_w2[l], 'hy_b2': hy_b2[l],
            'hy_w3': hy_w3[l], 'hy_b3': hy_b3[l], 'hy_decay': hy_decay[l], 'hy_bias': hy_bias[l],
            'q_norm_w': q_norm_w[l], 'k_norm_w': k_norm_w[l], 'sc_w': sc_w[l], 'sc_b': sc_b[l],
        })
        mod_p = mod[l, 0:1]
        mod_s = mod[l, 1:1 + n_dec]
        xp_mid, h2p, k_l, v_l = _token_mixers(xp, mod_p, p, tabs_p, l)
        xs_mid, h2s, _, _ = _token_mixers(xs, mod_s, p, tabs_s, l, (ctx_k, ctx_v))
        ks_new.append(k_l.reshape(k_l.shape[0], lp, N_KV_HEADS, HEAD_DIM))
        vs_new.append(v_l.reshape(v_l.shape[0], lp, N_KV_HEADS, HEAD_DIM))

        router_wt = router_w[l].T.astype(BF16)
        gp_x, gp_d, gp_g = _dispatch(h2p, router_wt)
        gs_x, gs_d, gs_g = _dispatch(h2s, router_wt)
        yp, ys = _expert_ffn(_moe_split(gp_x, N_EXPERTS), _moe_split(gp_g, N_EXPERTS),
                             _moe_split(gs_x, N_EXPERTS), _moe_split(gs_g, N_EXPERTS),
                             exp_w_gate, exp_w_up, exp_w_down, l)
        xp = _combine(gp_d, yp.reshape(gp_x.shape), xp_mid, mod_p[:, 5:6])
        xs = _combine(gs_d, ys.reshape(gs_x.shape), xs_mid, mod_s[:, 5:6])

    return (xp, xs, jnp.stack(ks_new, axis=1), jnp.stack(vs_new, axis=1))
```

```python
import functools
import math

import numpy as np
import jax
import jax.numpy as jnp
from jax import lax
from jax.experimental import pallas as pl
from jax.experimental.pallas import tpu as pltpu

D_MODEL = 1024
DEPTH = 2
GRID_W = 64
D_HYENA = 512
HYENA_ORDER = 2
HYENA_POS_BANDS = 16
HYENA_POS_DIM = 1 + 2 * HYENA_POS_BANDS
HYENA_FILTER_HIDDEN = 64
HYENA_WINDOW_SHIFT = 0.05
N_HEADS = 8
N_KV_HEADS = 2
HEAD_DIM = 64
HEADS_PER_KV = N_HEADS // N_KV_HEADS
ATT_Q = N_HEADS * HEAD_DIM
ATT_KV = N_KV_HEADS * HEAD_DIM
ROPE_THETA = 10000.0
D_SCONV = 512
N_EXPERTS = 16
EXPERT_FF = 512
CAPACITY_FACTOR = 2
NORM_EPS = 1e-6
HY_COLS = (HYENA_ORDER + 1) * D_HYENA
ATT_COLS = ATT_Q + 2 * ATT_KV
SC_COLS = 3 * D_SCONV
GATE_COLS = 3 * D_MODEL
MIX_COLS = HY_COLS + ATT_COLS + SC_COLS
D_IN = MIX_COLS + GATE_COLS

F32 = jnp.float32
BF16 = jnp.bfloat16

V7X_VMEM_BYTES = 64 * 1024 * 1024
VMEM_LIMIT = V7X_VMEM_BYTES - 8 * 1024 * 1024
LANES = 128
MLP_PAD = 128
STEP_TOKENS = 1024
INPROJ_ROWS = 1024
MERGE_ROWS = 1024
HYENA_CHANNEL_TILE = 256
MOD_COL_TILE = 2048
COND_ROWS = 16


def _params(*sem):
    return pltpu.CompilerParams(dimension_semantics=sem, vmem_limit_bytes=VMEM_LIMIT)


def _const_spec(shape):
    nd = len(shape)
    return pl.BlockSpec(shape, lambda *_: (0,) * nd, pipeline_mode=pl.Buffered(1))


def _layer_spec(shape, layer):
    nd = len(shape)
    return pl.BlockSpec((None,) + tuple(shape), lambda *_: (layer,) + (0,) * nd, pipeline_mode=pl.Buffered(1))


def _dot(a, b):
    return jnp.dot(a, b, preferred_element_type=F32)


def _silu(x):
    return x * (1.0 / (1.0 + jnp.exp(-x)))


def _sigmoid(x):
    return 1.0 / (1.0 + jnp.exp(-x))


@functools.lru_cache(maxsize=None)
def _split_dft_tables(L):
    h = L // 2
    k = np.arange(h, dtype=np.int64)
    alt = 1.0 - 2.0 * (k % 2)
    fwd, inv = [], []
    for odd in (0, 1):
        idx = (k[:, None] * (2 * k[None, :] + odd)) % (2 * L)
        ang = idx.astype(np.float64) * (np.pi / L)
        c = np.cos(ang)
        s = np.sin(ang)
        s[0, :] = alt
        fwd.append(np.concatenate([c, s], axis=0).astype(np.float32))
        inv.append(np.concatenate([c.T, s.T], axis=1).astype(np.float32))
    return fwd[0], fwd[1], inv[0], inv[1]


@functools.lru_cache(maxsize=None)
def _pos_features(L):
    n = np.arange(L, dtype=np.float64)
    t = n / max(L - 1, 1)
    bands = np.linspace(1e-4, HYENA_POS_BANDS - 1, HYENA_POS_BANDS)
    ang = 2.0 * math.pi * n[:, None] * bands[None, :] / L
    z = np.concatenate([t[:, None], np.cos(ang), np.sin(ang)], axis=-1)
    zp = np.zeros((L, MLP_PAD), np.float32)
    zp[:, :HYENA_POS_DIM] = z
    return zp


@functools.lru_cache(maxsize=None)
def _rope_tables(L):
    rows = L // GRID_W
    row = np.repeat(np.arange(rows, dtype=np.float64), GRID_W)
    col = np.tile(np.arange(GRID_W, dtype=np.float64), rows)
    axis_dim = HEAD_DIM // 2
    inv_freq = ROPE_THETA ** (-np.arange(0, axis_dim, 2, dtype=np.float64) / axis_dim)
    ar = row[:, None] * inv_freq[None, :]
    ac = col[:, None] * inv_freq[None, :]
    ang = np.concatenate([ar, ar, ac, ac], axis=-1)
    cos = np.cos(ang).astype(np.float32)
    sin = np.sin(ang).astype(np.float32)
    return np.tile(cos, (1, N_HEADS)), np.tile(sin, (1, N_HEADS))


@functools.lru_cache(maxsize=None)
def _head_tables():
    lane = np.arange(ATT_Q)
    block_ones = (lane[:, None] // HEAD_DIM == lane[None, :] // HEAD_DIM).astype(np.float32)
    src = np.arange(ATT_KV)
    dst = np.arange(HEADS_PER_KV * HEAD_DIM)
    rep = np.stack([(src[:, None] == g * HEAD_DIM + dst[None, :] % HEAD_DIM) for g in range(N_KV_HEADS)])
    return block_ones, rep.astype(np.float32)


@functools.lru_cache(maxsize=None)
def _prefix_table(n):
    i = np.arange(n)
    return (i[:, None] < i[None, :]).astype(np.float32)


def _mod_kernel(cond_ref, w_ref, b_ref, o_ref):
    a = _silu(cond_ref[...]).astype(BF16)
    o_ref[...] = _dot(a, w_ref[...].astype(BF16)) + b_ref[...]


def _modulation(cond, mod_w, mod_b):
    rows = cond.shape[0]
    tn = MOD_COL_TILE
    ncols = mod_w.shape[-1]
    return pl.pallas_call(
        _mod_kernel,
        grid=(DEPTH, ncols // tn),
        in_specs=[
            pl.BlockSpec((rows, D_MODEL), lambda l, j: (0, 0)),
            pl.BlockSpec((None, D_MODEL, tn), lambda l, j: (l, 0, j)),
            pl.BlockSpec((None, 1, tn), lambda l, j: (l, 0, j)),
        ],
        out_specs=pl.BlockSpec((None, rows, tn), lambda l, j: (l, 0, j)),
        out_shape=jax.ShapeDtypeStruct((DEPTH, rows, ncols), F32),
        compiler_params=_params("parallel", "parallel"),
        name="modulation",
    )(cond, mod_w, mod_b.reshape(DEPTH, 1, ncols))


def _rms_modulate(x, norm_w, shift, scale):
    y = x * lax.rsqrt(jnp.mean(x * x, axis=-1, keepdims=True) + NORM_EPS)
    return (y * norm_w) * (1.0 + scale) + shift


def _conv3(x, w, b, first, last):
    n = x.shape[0]
    prev = jnp.where(first, 0.0, pltpu.roll(x, 1, 0))
    nxt = jnp.where(last, 0.0, pltpu.roll(x, n - 1, 0))
    return prev * w[0:1, :] + x * w[1:2, :] + nxt * w[2:3, :] + b


def _inproj_kernel(x_ref, mod_ref, nw_ref, w_ref, hyw_ref, hyb_ref, scw_ref, scb_ref,
                   hyc_ref, yc_ref, qkv_ref, h_sc, *, L):
    h = _rms_modulate(x_ref[...], nw_ref[...], mod_ref[0:1, :], mod_ref[1:2, :])
    h_sc[...] = h.astype(BF16)
    tm = x_ref.shape[0]

    def edges(cols):
        pos = lax.broadcasted_iota(jnp.int32, (tm, cols), 0) % L
        return pos == 0, pos == L - 1

    first, last = edges(ATT_COLS)
    for lo in range(0, HY_COLS, ATT_COLS):
        cols = slice(lo, lo + ATT_COLS)
        y = _dot(h_sc[...], w_ref[:, cols])
        hyc_ref[:, cols] = _conv3(y, hyw_ref[:, cols], hyb_ref[:, cols], first, last).astype(BF16)
    qkv_ref[...] = _dot(h_sc[...], w_ref[:, HY_COLS:HY_COLS + ATT_COLS])
    sc0 = HY_COLS + ATT_COLS
    u, bgate, cgate = (_dot(h_sc[...], w_ref[:, sc0 + k * D_SCONV:sc0 + (k + 1) * D_SCONV]) for k in range(3))
    first, last = edges(D_SCONV)
    yc_ref[...] = (bgate * _conv3(cgate * u, scw_ref[...], scb_ref[...], first, last)).astype(BF16)


def _inproj(x2d, mod, norm_w, w_in_bf, hy_short_w, hy_short_b, sc_w, sc_b, layer, rows_per_req, tm=INPROJ_ROWS):
    T = x2d.shape[0]
    n_req = mod.shape[0]
    req = (lambda i: (i * tm) // rows_per_req) if n_req > 1 else (lambda i: 0)
    out = lambda w: pl.BlockSpec((tm, w), lambda i: (i, 0))
    return pl.pallas_call(
        functools.partial(_inproj_kernel, L=rows_per_req),
        grid=(T // tm,),
        in_specs=[
            pl.BlockSpec((tm, D_MODEL), lambda i: (i, 0)),
            pl.BlockSpec((None, 6, D_MODEL), lambda i: (req(i), 0, 0)),
            _const_spec((1, D_MODEL)),
            _layer_spec((D_MODEL, MIX_COLS), layer),
            _const_spec((3, HY_COLS)), _const_spec((1, HY_COLS)),
            _const_spec((3, D_SCONV)), _const_spec((1, D_SCONV)),
        ],
        out_specs=[out(HY_COLS), out(D_SCONV), out(ATT_COLS)],
        out_shape=[jax.ShapeDtypeStruct((T, HY_COLS), BF16), jax.ShapeDtypeStruct((T, D_SCONV), BF16),
                   jax.ShapeDtypeStruct((T, ATT_COLS), F32)],
        scratch_shapes=[pltpu.VMEM((tm, D_MODEL), BF16)],
        compiler_params=_params("parallel"),
        name="inproj",
    )(x2d, mod, norm_w, w_in_bf, hy_short_w, hy_short_b.reshape(1, HY_COLS), sc_w, sc_b.reshape(1, D_SCONV))


def _split_parity(par_sc, z):
    half = z.shape[0] // 2
    n_lt = z.shape[1] // LANES
    for c in range(n_lt):
        par_sc[c] = z[:, c * LANES:(c + 1) * LANES]
    return tuple(jnp.concatenate([par_sc[c, pl.ds(par, half, stride=2), :] for c in range(n_lt)], axis=1)
                 for par in (0, 1))


def _join_parity(par_sc, even, odd):
    half = even.shape[0]
    n_lt = even.shape[1] // LANES
    for c in range(n_lt):
        par_sc[c, pl.ds(0, half, stride=2), :] = even[:, c * LANES:(c + 1) * LANES]
        par_sc[c, pl.ds(1, half, stride=2), :] = odd[:, c * LANES:(c + 1) * LANES]
    return jnp.concatenate([par_sc[c] for c in range(n_lt)], axis=1)


def _fold_spectrum(te, to):
    half = te.shape[0] // 2
    row0 = lax.broadcasted_iota(jnp.int32, (half, te.shape[1]), 0) == 0
    ce, se, co, so = te[:half], te[half:], to[:half], to[half:]
    a = jnp.concatenate([ce + co, jnp.where(row0, se, ce - co)], axis=0)
    b = jnp.concatenate([jnp.where(row0, ce - co, se + so), jnp.where(row0, so, so - se)], axis=0)
    return a, b


def _unfold_spectrum(yr, yi):
    half = yr.shape[0] // 2
    row0 = lax.broadcasted_iota(jnp.int32, (half, yr.shape[1]), 0) == 0
    yr_lo, yr_up, yi_lo, yi_up = yr[:half], yr[half:], yi[:half], yi[half:]
    even = jnp.concatenate([jnp.where(row0, yr_lo + yi_lo, yr_lo + yr_up),
                            jnp.where(row0, yr_up, yi_lo - yi_up)], axis=0)
    odd = jnp.concatenate([jnp.where(row0, yr_lo - yi_lo, yr_lo - yr_up),
                           jnp.where(row0, yi_up, yi_lo + yi_up)], axis=0)
    return even, odd


def _filter_kernel(z_ref, w1_ref, b1_ref, w2_ref, b2_ref, w3f_ref, b3f_ref, w3b_ref, b3b_ref,
                   decf_ref, decb_ref, fwde_ref, fwdo_ref, p_ref, h_sc, par_sc, *, L):
    hi = lax.Precision.HIGHEST

    @pl.when((pl.program_id(0) == 0) & (pl.program_id(1) == 0))
    def _():
        h1 = jnp.sin(jnp.dot(z_ref[...], w1_ref[...], precision=hi, preferred_element_type=F32) + b1_ref[...])
        h_sc[...] = jnp.sin(jnp.dot(h1, w2_ref[...], precision=hi, preferred_element_type=F32) + b2_ref[...])

    h = h_sc[...]
    tc = w3f_ref.shape[1]
    pos = lax.broadcasted_iota(jnp.int32, (L, tc), 0)
    t = pos.astype(F32) / float(max(L - 1, 1))

    def taps(w3_ref, b3_ref, dec_ref):
        g = jnp.dot(h, w3_ref[...], precision=hi, preferred_element_type=F32) + b3_ref[...]
        return g * (jnp.exp(-t * jnp.abs(dec_ref[...])) + HYENA_WINDOW_SHIFT)

    hf = taps(w3f_ref, b3f_ref, decf_ref)
    hb = jnp.where(pos == 0, 0.0, taps(w3b_ref, b3b_ref, decb_ref))

    def transform(taps_lc):
        even, odd = _split_parity(par_sc, taps_lc)
        return _fold_spectrum(_dot(fwde_ref[...], even.astype(BF16)), _dot(fwdo_ref[...], odd.astype(BF16)))

    fa, fb = transform(hf)
    ba, bb = transform(hb)
    k_re = fa + ba
    k_im = bb - fb
    k_ny = fb + bb
    first = pos == 0
    inv_n = 1.0 / (2 * L)
    p_ref[0] = jnp.where(first, k_re * inv_n, k_re * (2.0 * inv_n))
    p_ref[1] = jnp.where(first, 0.0, k_im * (2.0 * inv_n))
    p_ref[2] = jnp.where(first, k_ny * inv_n, k_re * (2.0 * inv_n))


def _hyena_spectra(L, w1, b1, w2, b2, w3, b3, decay, fwd_even, fwd_odd):
    pad_h = MLP_PAD - HYENA_FILTER_HIDDEN
    z = jnp.asarray(_pos_features(L))
    w1p = jnp.pad(w1, ((0, MLP_PAD - HYENA_POS_DIM), (0, pad_h)))
    b1p = jnp.pad(b1, (0, pad_h)).reshape(1, MLP_PAD)
    w2p = jnp.pad(w2, ((0, pad_h), (0, pad_h)))
    b2p = jnp.pad(b2, (0, pad_h)).reshape(1, MLP_PAD)
    w3p = jnp.pad(w3, ((0, pad_h), (0, 0)))
    ncol = w3.shape[1]
    b3r = b3.reshape(1, ncol)
    decr = decay.reshape(1, ncol)
    tc = HYENA_CHANNEL_TILE
    nct = D_HYENA // tc
    per_dir = HYENA_ORDER * nct
    col_f = lambda o, c: (0, o * nct + c)
    col_b = lambda o, c: (0, per_dir + o * nct + c)
    return pl.pallas_call(
        functools.partial(_filter_kernel, L=L),
        grid=(HYENA_ORDER, nct),
        in_specs=[
            _const_spec((L, MLP_PAD)),
            _const_spec((MLP_PAD, MLP_PAD)), _const_spec((1, MLP_PAD)),
            _const_spec((MLP_PAD, MLP_PAD)), _const_spec((1, MLP_PAD)),
            pl.BlockSpec((MLP_PAD, tc), col_f), pl.BlockSpec((1, tc), col_f),
            pl.BlockSpec((MLP_PAD, tc), col_b), pl.BlockSpec((1, tc), col_b),
            pl.BlockSpec((1, tc), col_f), pl.BlockSpec((1, tc), col_b),
            _const_spec((L, L // 2)), _const_spec((L, L // 2)),
        ],
        out_specs=pl.BlockSpec((None, 3, L, tc), lambda o, c: (o, 0, 0, c)),
        out_shape=jax.ShapeDtypeStruct((HYENA_ORDER, 3, L, D_HYENA), F32),
        scratch_shapes=[pltpu.VMEM((L, MLP_PAD), F32), pltpu.VMEM((tc // LANES, L, LANES), F32)],
        compiler_params=_params("arbitrary", "arbitrary"),
        name="hyena_spectra",
    )(z, w1p, b1p, w2p, b2p, w3p, b3r, w3p, b3r, decr, decr, fwd_even, fwd_odd)


def _seqmix_kernel(v_ref, x1_ref, x2_ref, p_ref, hb_ref, fwde_ref, fwdo_ref, inve_ref, invo_ref, ya_ref,
                   spec_sc, par_sc):
    n_req = v_ref.shape[0]
    mult_refs = (x1_ref, x2_ref)

    def chain(r):
        z = v_ref[r].astype(F32)
        z_even, z_odd = _split_parity(par_sc.at[r], z)
        yield
        for o in range(HYENA_ORDER):
            te = _dot(fwde_ref[...], z_even.astype(BF16))
            to = _dot(fwdo_ref[...], z_odd.astype(BF16))
            yield
            a, b = _fold_spectrum(te, to)
            p2 = p_ref[o, 1]
            even, odd = _unfold_spectrum(a * p_ref[o, 0] + b * p2, b * p_ref[o, 2] - a * p2)
            spec_sc[r, 0] = even.astype(BF16)
            spec_sc[r, 1] = odd.astype(BF16)
            yield
            y_even = _dot(inve_ref[...], spec_sc[r, 0])
            y_odd = _dot(invo_ref[...], spec_sc[r, 1])
            yield
            z = mult_refs[o][r].astype(F32) * (_join_parity(par_sc.at[r], y_even, y_odd) + hb_ref[o:o + 1, :] * z)
            if o < HYENA_ORDER - 1:
                z_even, z_odd = _split_parity(par_sc.at[r], z)
            else:
                ya_ref[r] = z.astype(ya_ref.dtype)
            yield

    chains = [chain(r) for r in range(n_req)]
    n_stages = 1 + 4 * HYENA_ORDER
    for tick in range(n_stages + n_req - 1):
        for r, c in enumerate(chains):
            if 0 <= tick - r < n_stages:
                next(c)


def _seqmix(hyc3d, spectra, hy_bias, tabs):
    B, L, _ = hyc3d.shape
    tc = HYENA_CHANNEL_TILE
    nct = D_HYENA // tc
    nr = max(2, STEP_TOKENS // L)
    act = lambda off: pl.BlockSpec((nr, L, tc), lambda c, b, off=off: (b, 0, off * nct + c))
    return pl.pallas_call(
        _seqmix_kernel,
        grid=(nct, B // nr),
        in_specs=[
            act(0), act(1), act(2),
            pl.BlockSpec((HYENA_ORDER, 3, L, tc), lambda c, b: (0, 0, 0, c), pipeline_mode=pl.Buffered(1)),
            pl.BlockSpec((HYENA_ORDER, tc), lambda c, b: (0, c)),
            _const_spec((L, L // 2)), _const_spec((L, L // 2)), _const_spec((L // 2, L)), _const_spec((L // 2, L)),
        ],
        out_specs=pl.BlockSpec((nr, L, tc), lambda c, b: (b, 0, c)),
        out_shape=jax.ShapeDtypeStruct((B, L, D_HYENA), BF16),
        scratch_shapes=[pltpu.VMEM((nr, 2, L, tc), BF16), pltpu.VMEM((nr, tc // LANES, L, LANES), F32)],
        compiler_params=_params("arbitrary", "arbitrary"),
        name="seqmix",
    )(hyc3d, hyc3d, hyc3d, spectra, hy_bias, *tabs)


def _head_rms(x, ones_ref, w):
    sq = x * x
    hi = sq.astype(BF16)
    lo = (sq - hi.astype(F32)).astype(BF16)
    ones = ones_ref[...]
    width = x.shape[1]
    ss = _dot(hi, ones[:width, :width]) + _dot(lo, ones[:width, :width])
    return x * lax.rsqrt(ss * (1.0 / HEAD_DIM) + NORM_EPS) * w


def _rope(x, cos, sin):
    width = x.shape[1]
    lane = lax.broadcasted_iota(jnp.int32, x.shape, 1)
    half = HEAD_DIM // 4
    low = (lane % (2 * half)) < half
    rot = jnp.where(low, -pltpu.roll(x, width - half, 1), pltpu.roll(x, half, 1))
    return x * cos + rot * sin


def _attn_kernel(*refs, rope):
    if rope:
        (q_ref, k_ref, v_ref, ck_ref, cv_ref, qw_ref, kw_ref, ones_ref, rep_ref, rept_ref,
         cos_ref, sin_ref, yb_ref) = refs
    else:
        q_ref, k_ref, v_ref, qw_ref, kw_ref, ones_ref, rep_ref, rept_ref, yb_ref, ko_ref, vo_ref = refs
    n_req, L, _ = q_ref.shape
    gw = HEADS_PER_KV * HEAD_DIM
    lane = lax.broadcasted_iota(jnp.int32, (L, gw), 1)
    for r in range(n_req):
        kn = _head_rms(k_ref[r], ones_ref, kw_ref[...])
        v = v_ref[r]
        q = _head_rms(q_ref[r], ones_ref, qw_ref[...])
        if rope:
            keys = _rope(kn, cos_ref[:, :ATT_KV], sin_ref[:, :ATT_KV])
            keys = jnp.concatenate([ck_ref[r], keys], axis=0)
            vals = jnp.concatenate([cv_ref[r], v], axis=0)
            q = _rope(q, cos_ref[...], sin_ref[...])
        else:
            ko_ref[r] = kn
            vo_ref[r] = v
            keys, vals = kn, v
        keys = keys.astype(BF16)
        vals = vals.astype(BF16)
        q = q * (HEAD_DIM ** -0.5)
        for g in range(N_KV_HEADS):
            k4 = lax.dot_general(rept_ref[g], keys, (((1,), (1,)), ((), ())), preferred_element_type=F32).astype(BF16)
            v4 = _dot(vals, rep_ref[g]).astype(BF16)
            qg = q[:, g * gw:(g + 1) * gw]
            acc = jnp.zeros((L, gw), F32)
            for h in range(HEADS_PER_KV):
                mine = (lane // HEAD_DIM) == h
                qm = jnp.where(mine, qg, 0.0).astype(BF16)
                s = _dot(qm, k4)
                p = jnp.exp(s - jnp.max(s, axis=-1, keepdims=True))
                denom = jnp.sum(p, axis=-1, keepdims=True)
                o4 = _dot(p.astype(BF16), v4)
                acc = jnp.where(mine, o4 * (1.0 / denom), acc)
            yb_ref[r, :, g * gw:(g + 1) * gw] = acc.astype(yb_ref.dtype)


def _attention(qkv3d, q_norm_w, k_norm_w, ctx_k=None, ctx_v=None, layer=0):
    B, L, _ = qkv3d.shape
    rope = ctx_k is not None
    nr = _requests_per_step(L)
    ones_np, rep_np = _head_tables()
    ones = jnp.asarray(ones_np).astype(BF16)
    rep = jnp.asarray(rep_np).astype(BF16)
    rept = jnp.asarray(np.swapaxes(rep_np, 1, 2)).astype(BF16)
    qw = jnp.tile(q_norm_w, N_HEADS).reshape(1, ATT_Q)
    kw = jnp.tile(k_norm_w, N_KV_HEADS).reshape(1, ATT_KV)
    kblk = ATT_Q // ATT_KV
    in_specs = [
        pl.BlockSpec((nr, L, ATT_Q), lambda b: (b, 0, 0)),
        pl.BlockSpec((nr, L, ATT_KV), lambda b: (b, 0, kblk)),
        pl.BlockSpec((nr, L, ATT_KV), lambda b: (b, 0, kblk + 1)),
    ]
    args = [qkv3d, qkv3d, qkv3d]
    if rope:
        n_ctx = ctx_k.shape[2]
        in_specs += [pl.BlockSpec((nr, None, n_ctx, ATT_KV), lambda b: (b, layer, 0, 0))] * 2
        args += [ctx_k, ctx_v]
    in_specs += [_const_spec((1, ATT_Q)), _const_spec((1, ATT_KV)), _const_spec((ATT_Q, ATT_Q)),
                 _const_spec((N_KV_HEADS, ATT_KV, HEADS_PER_KV * HEAD_DIM)),
                 _const_spec((N_KV_HEADS, HEADS_PER_KV * HEAD_DIM, ATT_KV))]
    args += [qw, kw, ones, rep, rept]
    yb_shape = jax.ShapeDtypeStruct((B, L, ATT_Q), BF16)
    yb_spec = pl.BlockSpec((nr, L, ATT_Q), lambda b: (b, 0, 0))
    if rope:
        cos_np, sin_np = _rope_tables(L)
        in_specs += [_const_spec((L, ATT_Q))] * 2
        args += [jnp.asarray(cos_np), jnp.asarray(sin_np)]
        out_specs = yb_spec
        out_shape = yb_shape
    else:
        kv_spec = pl.BlockSpec((nr, L, ATT_KV), lambda b: (b, 0, 0))
        kv_shape = jax.ShapeDtypeStruct((B, L, ATT_KV), F32)
        out_specs = [yb_spec, kv_spec, kv_spec]
        out_shape = [yb_shape, kv_shape, kv_shape]
    return pl.pallas_call(
        functools.partial(_attn_kernel, rope=rope),
        grid=(B // nr,),
        in_specs=in_specs,
        out_specs=out_specs,
        out_shape=out_shape,
        compiler_params=_params("parallel"),
        name="attention",
    )(*args)


def _merge_kernel(ya_ref, yb_ref, yc_ref, x_ref, mod_ref, nw1_ref, nw2_ref, *rest):
    gate_w_refs = rest[:GATE_COLS // ATT_COLS]
    wa_ref, wb_ref, wc_ref, wo_ref, xo_ref, h2_ref = rest[GATE_COLS // ATT_COLS:]
    tm = x_ref.shape[0]
    n_chain = 2
    rows_per = tm // n_chain

    def chain(c):
        rows = pl.ds(c * rows_per, rows_per)
        x = x_ref[rows, :]
        h = _rms_modulate(x, nw1_ref[...], mod_ref[0:1, :], mod_ref[1:2, :]).astype(BF16)
        yield
        gates = jnp.concatenate([_dot(h, w_ref[...]) for w_ref in gate_w_refs], axis=1)
        da = _dot(ya_ref[rows, :], wa_ref[...])
        db = _dot(yb_ref[rows, :], wb_ref[...])
        dc = _dot(yc_ref[rows, :], wc_ref[...])
        yield
        merged = (_sigmoid(gates[:, 0:D_MODEL]) * da + _sigmoid(gates[:, D_MODEL:2 * D_MODEL]) * db
                  + _sigmoid(gates[:, 2 * D_MODEL:]) * dc).astype(BF16)
        yield
        proj = _dot(merged, wo_ref[...])
        yield
        x = x + mod_ref[2:3, :] * proj
        xo_ref[rows, :] = x
        h2_ref[rows, :] = _rms_modulate(x, nw2_ref[...], mod_ref[3:4, :], mod_ref[4:5, :]).astype(BF16)
        yield

    chains = [chain(c) for c in range(n_chain)]
    n_stages = 5
    for tick in range(n_stages + n_chain - 1):
        for c, ch in enumerate(chains):
            if 0 <= tick - c < n_stages:
                next(ch)


def _merge(ya, yb, yc, x2d, mod, norm1_w, norm2_w, w_in_bf, wa, wb, wc, wo, layer, rows_per_req, tm=MERGE_ROWS):
    T = x2d.shape[0]
    n_req = mod.shape[0]
    req = (lambda i: (i * tm) // rows_per_req) if n_req > 1 else (lambda i: 0)
    br = pl.BlockSpec((tm, D_HYENA), lambda i: (i, 0))
    row = pl.BlockSpec((tm, D_MODEL), lambda i: (i, 0))
    gate_blk0 = MIX_COLS // ATT_COLS
    gate_w = [pl.BlockSpec((None, D_MODEL, ATT_COLS), lambda i, k=k: (layer, 0, gate_blk0 + k),
                           pipeline_mode=pl.Buffered(1)) for k in range(GATE_COLS // ATT_COLS)]
    return pl.pallas_call(
        _merge_kernel,
        grid=(T // tm,),
        in_specs=[
            br, br, br, row,
            pl.BlockSpec((None, 6, D_MODEL), lambda i: (req(i), 0, 0)),
            _const_spec((1, D_MODEL)), _const_spec((1, D_MODEL)),
            *gate_w,
            _layer_spec((D_HYENA, D_MODEL), layer), _layer_spec((ATT_Q, D_MODEL), layer),
            _layer_spec((D_SCONV, D_MODEL), layer), _layer_spec((D_MODEL, D_MODEL), layer),
        ],
        out_specs=[row, row],
        out_shape=[jax.ShapeDtypeStruct((T, D_MODEL), F32), jax.ShapeDtypeStruct((T, D_MODEL), BF16)],
        compiler_params=_params("parallel"),
        name="merge",
    )(ya, yb, yc, x2d, mod, norm1_w, norm2_w, *([w_in_bf] * len(gate_w)), wa, wb, wc, wo)


def _route_kernel(h_ref, rw_ref, tri_ref, aff_ref, slot_ref, *, cap):
    b = pl.program_id(0)
    n_req = h_ref.shape[0]
    for r in range(n_req):
        logits = lax.dot_general(rw_ref[...], h_ref[r], (((1,), (1,)), ((), ())), preferred_element_type=F32)
        ex = jnp.exp(logits - jnp.max(logits, axis=0, keepdims=True))
        first_row = pl.multiple_of((b * n_req + r) * N_EXPERTS, N_EXPERTS)
        aff_ref[pl.ds(first_row, N_EXPERTS), :] = ex / jnp.sum(ex, axis=0, keepdims=True)

    @pl.when(b == pl.num_programs(0) - 1)
    def _():
        aff = aff_ref[...]

        def count(mask):
            return jnp.sum(jnp.where(mask, 1.0, 0.0), axis=1, keepdims=True)

        kth = jnp.zeros((aff.shape[0], 1), jnp.int32)
        for bit in range(30, -1, -1):
            trial = kth | (1 << bit)
            enough = count(aff >= lax.bitcast_convert_type(trial, F32)) >= cap
            kth = jnp.where(enough, trial, kth)
        next_up = lax.bitcast_convert_type(kth + 1, F32)
        above = aff >= next_up
        tied = (aff >= lax.bitcast_convert_type(kth, F32)) & (aff < next_up)
        tri = tri_ref[...]
        tied_before = _dot(jnp.where(tied, 1.0, 0.0).astype(BF16), tri)
        chosen = above | (tied & (tied_before < (cap - count(above))))
        slot = _dot(jnp.where(chosen, 1.0, 0.0).astype(BF16), tri)
        slot_ref[...] = jnp.where(chosen, slot, -1.0)


def _route(h3d, router_wt):
    B, n_tok, _ = h3d.shape
    cap = CAPACITY_FACTOR * n_tok // N_EXPERTS
    tri = jnp.asarray(_prefix_table(n_tok)).astype(BF16)
    whole = pl.BlockSpec((B * N_EXPERTS, n_tok), lambda b: (0, 0))
    shape = jax.ShapeDtypeStruct((B * N_EXPERTS, n_tok), F32)
    nr = _requests_per_step(n_tok)
    return pl.pallas_call(
        functools.partial(_route_kernel, cap=cap),
        grid=(B // nr,),
        in_specs=[
            pl.BlockSpec((nr, n_tok, D_MODEL), lambda b: (b, 0, 0)),
            _const_spec((N_EXPERTS, D_MODEL)),
            _const_spec((n_tok, n_tok)),
        ],
        out_specs=[whole, whole],
        out_shape=[shape, shape],
        compiler_params=_params("arbitrary"),
        name="moe_route",
    )(h3d, router_wt, tri)


def _dispatch_kernel(h_ref, aff_ref, slot_ref, xs_ref, d_ref, g_ref, *, n_tok, cap):
    row = lax.broadcasted_iota(jnp.int32, (cap, n_tok), 0).astype(F32)
    n_part = 2
    per_part = N_EXPERTS // n_part
    for r in range(h_ref.shape[0]):
        for part in range(n_part):
            for e in range(part * per_part, (part + 1) * per_part):
                idx = r * N_EXPERTS + e
                hit = row == slot_ref[idx:idx + 1, :]
                d_ref[r, e * cap:(e + 1) * cap, :] = jnp.where(hit, 1.0, 0.0).astype(BF16)
                gate = jnp.sum(jnp.where(hit, aff_ref[idx:idx + 1, :], 0.0), axis=1, keepdims=True)
                g_ref[r, e * cap:(e + 1) * cap, :] = jnp.broadcast_to(gate, (cap, LANES))
            rows = slice(part * per_part * cap, (part + 1) * per_part * cap)
            xs_ref[r, rows, :] = _dot(d_ref[r, rows, :], h_ref[r]).astype(BF16)


def _requests_per_step(n_tok):
    return max(1, STEP_TOKENS // n_tok)


def _dispatch(h3d, router_wt):
    B, n_tok, _ = h3d.shape
    cap = CAPACITY_FACTOR * n_tok // N_EXPERTS
    rows = N_EXPERTS * cap
    aff, slot = _route(h3d, router_wt)
    nr = _requests_per_step(n_tok)
    per_req = pl.BlockSpec((nr * N_EXPERTS, n_tok), lambda b: (b, 0))
    blk = lambda w: pl.BlockSpec((nr, rows, w), lambda b: (b, 0, 0))
    return pl.pallas_call(
        functools.partial(_dispatch_kernel, n_tok=n_tok, cap=cap),
        grid=(B // nr,),
        in_specs=[pl.BlockSpec((nr, n_tok, D_MODEL), lambda b: (b, 0, 0)), per_req, per_req],
        out_specs=[blk(D_MODEL), blk(n_tok), blk(LANES)],
        out_shape=[jax.ShapeDtypeStruct((B, rows, D_MODEL), BF16),
                   jax.ShapeDtypeStruct((B, rows, n_tok), BF16),
                   jax.ShapeDtypeStruct((B, rows, LANES), F32)],
        compiler_params=_params("parallel"),
        name="moe_dispatch",
    )(h3d, aff, slot)


def _ffn_kernel(xp_ref, gp_ref, xs_ref, gs_ref, wg_ref, wu_ref, wd_ref, yp_ref, ys_ref):
    wg = wg_ref[...].astype(BF16)
    wu = wu_ref[...].astype(BF16)
    wd = wd_ref[...].astype(BF16)
    for x_ref, g_ref, y_ref in ((xp_ref, gp_ref, yp_ref), (xs_ref, gs_ref, ys_ref)):
        nb, cap, _ = x_ref.shape
        x = x_ref[...].reshape(nb * cap, D_MODEL)
        gate = g_ref[...].reshape(nb * cap, LANES)
        gate = jnp.concatenate([gate] * (EXPERT_FF // LANES), axis=1)
        act = _silu(_dot(x, wg)) * _dot(x, wu) * gate
        y = _dot(act.astype(BF16), wd)
        y_ref[...] = y.reshape(nb, cap, D_MODEL).astype(y_ref.dtype)


def _expert_ffn(xp, gp, xs, gs, w_gate, w_up, w_down, layer):
    def act_spec(a):
        nb, _, cap, w = a.shape
        return pl.BlockSpec((nb, None, cap, w), lambda e: (0, e, 0, 0))

    def w_spec(a):
        return pl.BlockSpec((None, None) + a.shape[2:], lambda e: (layer, e, 0, 0))

    return pl.pallas_call(
        _ffn_kernel,
        grid=(N_EXPERTS,),
        in_specs=[act_spec(xp), act_spec(gp), act_spec(xs), act_spec(gs),
                  w_spec(w_gate), w_spec(w_up), w_spec(w_down)],
        out_specs=[act_spec(xp), act_spec(xs)],
        out_shape=[jax.ShapeDtypeStruct(xp.shape, BF16), jax.ShapeDtypeStruct(xs.shape, BF16)],
        compiler_params=_params("parallel"),
        name="expert_ffn",
    )(xp, gp, xs, gs, w_gate, w_up, w_down)


def _combine_kernel(d_ref, y_ref, x_ref, mod_ref, o_ref):
    for r in range(d_ref.shape[0]):
        moe = lax.dot_general(d_ref[r], y_ref[r], (((0,), (0,)), ((), ())), preferred_element_type=F32)
        o_ref[r] = x_ref[r] + mod_ref[min(r, mod_ref.shape[0] - 1)] * moe


def _combine(dmat, y3d, x3d, mod_g2):
    B, n_tok, _ = x3d.shape
    rows = dmat.shape[1]
    nr = _requests_per_step(n_tok)
    shared_mod = mod_g2.shape[0] == 1
    mod_spec = (pl.BlockSpec((1, 1, D_MODEL), lambda b: (0, 0, 0)) if shared_mod
                else pl.BlockSpec((nr, 1, D_MODEL), lambda b: (b, 0, 0)))
    return pl.pallas_call(
        _combine_kernel,
        grid=(B // nr,),
        in_specs=[
            pl.BlockSpec((nr, rows, n_tok), lambda b: (b, 0, 0)),
            pl.BlockSpec((nr, rows, D_MODEL), lambda b: (b, 0, 0)),
            pl.BlockSpec((nr, n_tok, D_MODEL), lambda b: (b, 0, 0)),
            mod_spec,
        ],
        out_specs=pl.BlockSpec((nr, n_tok, D_MODEL), lambda b: (b, 0, 0)),
        out_shape=jax.ShapeDtypeStruct(x3d.shape, F32),
        compiler_params=_params("parallel"),
        name="moe_combine",
    )(dmat, y3d, x3d, mod_g2)


def _token_mixers(x3d, mod, p, tabs, layer, ctx=None):
    B, L, _ = x3d.shape
    x2d = x3d.reshape(B * L, D_MODEL)
    hyc, yc, qkv = _inproj(x2d, mod, p['norm1_w'], p['w_in'], p['hy_short_w'], p['hy_short_b'], p['sc_w'],
                           p['sc_b'], layer, L)
    spectra = _hyena_spectra(L, p['hy_w1'], p['hy_b1'], p['hy_w2'], p['hy_b2'], p['hy_w3'], p['hy_b3'],
                             p['hy_decay'], tabs[0], tabs[1])
    ya = _seqmix(hyc.reshape(B, L, HY_COLS), spectra, p['hy_bias'], tabs)
    qkv3d = qkv.reshape(B, L, ATT_COLS)
    if ctx is None:
        yb, k, v = _attention(qkv3d, p['q_norm_w'], p['k_norm_w'])
    else:
        yb = _attention(qkv3d, p['q_norm_w'], p['k_norm_w'], ctx[0], ctx[1], layer)
        k = v = None
    x_mid, h2 = _merge(ya.reshape(B * L, D_HYENA), yb.reshape(B * L, ATT_Q), yc,
                       x2d, mod, p['norm1_w'], p['norm2_w'], p['w_in'], p['w_br_a'], p['w_br_b'], p['w_br_c'],
                       p['w_o'], layer, L)
    return x_mid.reshape(B, L, D_MODEL), h2.reshape(B, L, D_MODEL), k, v


def _moe_split(a, n_exp):
    B, rows, w = a.shape
    return a.reshape(B, n_exp, rows // n_exp, w)


def kernel(x_prompt, x_sample, cache_k, cache_v, c, c_ctx, mod_w, mod_b, norm1_w, norm2_w, w_in, hy_short_w, hy_short_b, hy_w1, hy_b1, hy_w2, hy_b2, hy_w3, hy_b3, hy_decay, hy_bias, q_norm_w, k_norm_w, sc_w, sc_b, w_br_a, w_br_b, w_br_c, w_o, router_w, exp_w_gate, exp_w_up, exp_w_down):
    n_dec = x_sample.shape[0]
    n_ctx = cache_k.shape[2]
    lp = x_prompt.shape[1]
    ls = x_sample.shape[1]

    cond_rows = COND_ROWS
    cond = jnp.concatenate([c_ctx[None, :], c, jnp.zeros((cond_rows - 1 - n_dec, D_MODEL), F32)], axis=0)
    mod = _modulation(cond, mod_w, mod_b).reshape(DEPTH, cond_rows, 6, D_MODEL)

    tabs_p = tuple(jnp.asarray(t).astype(BF16) for t in _split_dft_tables(lp))
    tabs_s = tuple(jnp.asarray(t).astype(BF16) for t in _split_dft_tables(ls))
    ctx_k = cache_k.reshape(n_dec, DEPTH, n_ctx, ATT_KV)
    ctx_v = cache_v.reshape(n_dec, DEPTH, n_ctx, ATT_KV)
    dense = {'w_in': w_in.astype(BF16), 'w_br_a': w_br_a.astype(BF16), 'w_br_b': w_br_b.astype(BF16),
             'w_br_c': w_br_c.astype(BF16), 'w_o': w_o.astype(BF16)}

    xp, xs = x_prompt, x_sample
    ks_new, vs_new = [], []
    for l in range(DEPTH):
        p = dict(dense)
        p.update({
            'norm1_w': norm1_w[l].reshape(1, D_MODEL), 'norm2_w': norm2_w[l].reshape(1, D_MODEL),
            'hy_short_w': hy_short_w[l], 'hy_short_b': hy_short_b[l],
            'hy_w1': hy_w1[l], 'hy_b1': hy_b1[l], 'hy_w2': hy_w2[l], 'hy_b2': hy_b2[l],
            'hy_w3': hy_w3[l], 'hy_b3': hy_b3[l], 'hy_decay': hy_decay[l], 'hy_bias': hy_bias[l],
            'q_norm_w': q_norm_w[l], 'k_norm_w': k_norm_w[l], 'sc_w': sc_w[l], 'sc_b': sc_b[l],
        })
        mod_p = mod[l, 0:1]
        mod_s = mod[l, 1:1 + n_dec]
        xp_mid, h2p, k_l, v_l = _token_mixers(xp, mod_p, p, tabs_p, l)
        xs_mid, h2s, _, _ = _token_mixers(xs, mod_s, p, tabs_s, l, (ctx_k, ctx_v))
        ks_new.append(k_l.reshape(k_l.shape[0], lp, N_KV_HEADS, HEAD_DIM))
        vs_new.append(v_l.reshape(v_l.shape[0], lp, N_KV_HEADS, HEAD_DIM))

        router_wt = router_w[l].T.astype(BF16)
        gp_x, gp_d, gp_g = _dispatch(h2p, router_wt)
        gs_x, gs_d, gs_g = _dispatch(h2s, router_wt)
        yp, ys = _expert_ffn(_moe_split(gp_x, N_EXPERTS), _moe_split(gp_g, N_EXPERTS),
                             _moe_split(gs_x, N_EXPERTS), _moe_split(gs_g, N_EXPERTS),
                             exp_w_gate, exp_w_up, exp_w_down, l)
        xp = _combine(gp_d, yp.reshape(gp_x.shape), xp_mid, mod_p[:, 5:6])
        xs = _combine(gs_d, ys.reshape(gs_x.shape), xs_mid, mod_s[:, 5:6])

    return (xp, xs, jnp.stack(ks_new, axis=1), jnp.stack(vs_new, axis=1))
```

```python
import functools
import math

import numpy as np
import jax
import jax.numpy as jnp
from jax import lax
from jax.experimental import pallas as pl
from jax.experimental.pallas import tpu as pltpu

D_MODEL = 1024
DEPTH = 2
GRID_W = 64
D_HYENA = 512
HYENA_ORDER = 2
HYENA_POS_BANDS = 16
HYENA_POS_DIM = 1 + 2 * HYENA_POS_BANDS
HYENA_FILTER_HIDDEN = 64
HYENA_WINDOW_SHIFT = 0.05
N_HEADS = 8
N_KV_HEADS = 2
HEAD_DIM = 64
HEADS_PER_KV = N_HEADS // N_KV_HEADS
ATT_Q = N_HEADS * HEAD_DIM
ATT_KV = N_KV_HEADS * HEAD_DIM
ROPE_THETA = 10000.0
D_SCONV = 512
N_EXPERTS = 16
EXPERT_FF = 512
CAPACITY_FACTOR = 2
NORM_EPS = 1e-6
HY_COLS = (HYENA_ORDER + 1) * D_HYENA
ATT_COLS = ATT_Q + 2 * ATT_KV
SC_COLS = 3 * D_SCONV
GATE_COLS = 3 * D_MODEL
MIX_COLS = HY_COLS + ATT_COLS + SC_COLS
D_IN = MIX_COLS + GATE_COLS

F32 = jnp.float32
BF16 = jnp.bfloat16

V7X_VMEM_BYTES = 64 * 1024 * 1024
VMEM_LIMIT = V7X_VMEM_BYTES - 8 * 1024 * 1024
LANES = 128
MLP_PAD = 128
STEP_TOKENS = 1024
INPROJ_ROWS = 1024
MERGE_ROWS = 1024
HYENA_CHANNEL_TILE = 256
MOD_COL_TILE = 2048
COND_ROWS = 16


def _params(*sem):
    return pltpu.CompilerParams(dimension_semantics=sem, vmem_limit_bytes=VMEM_LIMIT)


def _const_spec(shape):
    nd = len(shape)
    return pl.BlockSpec(shape, lambda *_: (0,) * nd, pipeline_mode=pl.Buffered(1))


def _layer_spec(shape, layer):
    nd = len(shape)
    return pl.BlockSpec((None,) + tuple(shape), lambda *_: (layer,) + (0,) * nd, pipeline_mode=pl.Buffered(1))


def _dot(a, b):
    return jnp.dot(a, b, preferred_element_type=F32)


def _silu(x):
    return x * (1.0 / (1.0 + jnp.exp(-x)))


def _sigmoid(x):
    return 1.0 / (1.0 + jnp.exp(-x))


@functools.lru_cache(maxsize=None)
def _split_dft_tables(L):
    h = L // 2
    k = np.arange(h, dtype=np.int64)
    alt = 1.0 - 2.0 * (k % 2)
    fwd, inv = [], []
    for odd in (0, 1):
        idx = (k[:, None] * (2 * k[None, :] + odd)) % (2 * L)
        ang = idx.astype(np.float64) * (np.pi / L)
        c = np.cos(ang)
        s = np.sin(ang)
        s[0, :] = alt
        fwd.append(np.concatenate([c, s], axis=0).astype(np.float32))
        inv.append(np.concatenate([c.T, s.T], axis=1).astype(np.float32))
    return fwd[0], fwd[1], inv[0], inv[1]


@functools.lru_cache(maxsize=None)
def _pos_features(L):
    n = np.arange(L, dtype=np.float64)
    t = n / max(L - 1, 1)
    bands = np.linspace(1e-4, HYENA_POS_BANDS - 1, HYENA_POS_BANDS)
    ang = 2.0 * math.pi * n[:, None] * bands[None, :] / L
    z = np.concatenate([t[:, None], np.cos(ang), np.sin(ang)], axis=-1)
    zp = np.zeros((L, MLP_PAD), np.float32)
    zp[:, :HYENA_POS_DIM] = z
    return zp


@functools.lru_cache(maxsize=None)
def _rope_tables(L):
    rows = L // GRID_W
    row = np.repeat(np.arange(rows, dtype=np.float64), GRID_W)
    col = np.tile(np.arange(GRID_W, dtype=np.float64), rows)
    axis_dim = HEAD_DIM // 2
    inv_freq = ROPE_THETA ** (-np.arange(0, axis_dim, 2, dtype=np.float64) / axis_dim)
    ar = row[:, None] * inv_freq[None, :]
    ac = col[:, None] * inv_freq[None, :]
    ang = np.concatenate([ar, ar, ac, ac], axis=-1)
    cos = np.cos(ang).astype(np.float32)
    sin = np.sin(ang).astype(np.float32)
    return np.tile(cos, (1, N_HEADS)), np.tile(sin, (1, N_HEADS))


@functools.lru_cache(maxsize=None)
def _head_tables():
    lane = np.arange(ATT_Q)
    block_ones = (lane[:, None] // HEAD_DIM == lane[None, :] // HEAD_DIM).astype(np.float32)
    src = np.arange(ATT_KV)
    dst = np.arange(HEADS_PER_KV * HEAD_DIM)
    rep = np.stack([(src[:, None] == g * HEAD_DIM + dst[None, :] % HEAD_DIM) for g in range(N_KV_HEADS)])
    return block_ones, rep.astype(np.float32)


@functools.lru_cache(maxsize=None)
def _prefix_table(n):
    i = np.arange(n)
    return (i[:, None] < i[None, :]).astype(np.float32)


def _mod_kernel(cond_ref, w_ref, b_ref, o_ref):
    a = _silu(cond_ref[...]).astype(BF16)
    o_ref[...] = _dot(a, w_ref[...].astype(BF16)) + b_ref[...]


def _modulation(cond, mod_w, mod_b):
    rows = cond.shape[0]
    tn = MOD_COL_TILE
    ncols = mod_w.shape[-1]
    return pl.pallas_call(
        _mod_kernel,
        grid=(DEPTH, ncols // tn),
        in_specs=[
            pl.BlockSpec((rows, D_MODEL), lambda l, j: (0, 0)),
            pl.BlockSpec((None, D_MODEL, tn), lambda l, j: (l, 0, j)),
            pl.BlockSpec((None, 1, tn), lambda l, j: (l, 0, j)),
        ],
        out_specs=pl.BlockSpec((None, rows, tn), lambda l, j: (l, 0, j)),
        out_shape=jax.ShapeDtypeStruct((DEPTH, rows, ncols), F32),
        compiler_params=_params("parallel", "parallel"),
        name="modulation",
    )(cond, mod_w, mod_b.reshape(DEPTH, 1, ncols))


def _rms_modulate(x, norm_w, shift, scale):
    y = x * lax.rsqrt(jnp.mean(x * x, axis=-1, keepdims=True) + NORM_EPS)
    return (y * norm_w) * (1.0 + scale) + shift


def _conv3(x, w, b, first, last):
    n = x.shape[0]
    prev = jnp.where(first, 0.0, pltpu.roll(x, 1, 0))
    nxt = jnp.where(last, 0.0, pltpu.roll(x, n - 1, 0))
    return prev * w[0:1, :] + x * w[1:2, :] + nxt * w[2:3, :] + b


def _inproj_kernel(x_ref, mod_ref, nw_ref, w_ref, hyw_ref, hyb_ref, scw_ref, scb_ref,
                   hyc_ref, yc_ref, qkv_ref, h_sc, *, L):
    h = _rms_modulate(x_ref[...], nw_ref[...], mod_ref[0:1, :], mod_ref[1:2, :])
    h_sc[...] = h.astype(BF16)
    tm = x_ref.shape[0]

    def edges(cols):
        pos = lax.broadcasted_iota(jnp.int32, (tm, cols), 0) % L
        return pos == 0, pos == L - 1

    sc0 = HY_COLS + ATT_COLS
    u, bgate, cgate = (_dot(h_sc[...], w_ref[:, sc0 + k * D_SCONV:sc0 + (k + 1) * D_SCONV]) for k in range(3))
    hy_chunks = [slice(lo, lo + ATT_COLS) for lo in range(0, HY_COLS, ATT_COLS)]
    y_next = _dot(h_sc[...], w_ref[:, hy_chunks[0]])
    first, last = edges(D_SCONV)
    yc_ref[...] = (bgate * _conv3(cgate * u, scw_ref[...], scb_ref[...], first, last)).astype(BF16)
    first, last = edges(ATT_COLS)
    for i, cols in enumerate(hy_chunks):
        y = y_next
        if i + 1 < len(hy_chunks):
            y_next = _dot(h_sc[...], w_ref[:, hy_chunks[i + 1]])
        else:
            qkv_ref[...] = _dot(h_sc[...], w_ref[:, HY_COLS:HY_COLS + ATT_COLS])
        hyc_ref[:, cols] = _conv3(y, hyw_ref[:, cols], hyb_ref[:, cols], first, last).astype(BF16)


def _inproj(x2d, mod, norm_w, w_in_bf, hy_short_w, hy_short_b, sc_w, sc_b, layer, rows_per_req, tm=INPROJ_ROWS):
    T = x2d.shape[0]
    n_req = mod.shape[0]
    req = (lambda i: (i * tm) // rows_per_req) if n_req > 1 else (lambda i: 0)
    out = lambda w: pl.BlockSpec((tm, w), lambda i: (i, 0))
    return pl.pallas_call(
        functools.partial(_inproj_kernel, L=rows_per_req),
        grid=(T // tm,),
        in_specs=[
            pl.BlockSpec((tm, D_MODEL), lambda i: (i, 0)),
            pl.BlockSpec((None, 6, D_MODEL), lambda i: (req(i), 0, 0)),
            _const_spec((1, D_MODEL)),
            _layer_spec((D_MODEL, MIX_COLS), layer),
            _const_spec((3, HY_COLS)), _const_spec((1, HY_COLS)),
            _const_spec((3, D_SCONV)), _const_spec((1, D_SCONV)),
        ],
        out_specs=[out(HY_COLS), out(D_SCONV), out(ATT_COLS)],
        out_shape=[jax.ShapeDtypeStruct((T, HY_COLS), BF16), jax.ShapeDtypeStruct((T, D_SCONV), BF16),
                   jax.ShapeDtypeStruct((T, ATT_COLS), F32)],
        scratch_shapes=[pltpu.VMEM((tm, D_MODEL), BF16)],
        compiler_params=_params("parallel"),
        name="inproj",
    )(x2d, mod, norm_w, w_in_bf, hy_short_w, hy_short_b.reshape(1, HY_COLS), sc_w, sc_b.reshape(1, D_SCONV))


def _split_parity(par_sc, z):
    half = z.shape[0] // 2
    n_lt = z.shape[1] // LANES
    for c in range(n_lt):
        par_sc[c] = z[:, c * LANES:(c + 1) * LANES]
    return tuple(jnp.concatenate([par_sc[c, pl.ds(par, half, stride=2), :] for c in range(n_lt)], axis=1)
                 for par in (0, 1))


def _join_parity(par_sc, even, odd):
    half = even.shape[0]
    n_lt = even.shape[1] // LANES
    for c in range(n_lt):
        par_sc[c, pl.ds(0, half, stride=2), :] = even[:, c * LANES:(c + 1) * LANES]
        par_sc[c, pl.ds(1, half, stride=2), :] = odd[:, c * LANES:(c + 1) * LANES]
    return jnp.concatenate([par_sc[c] for c in range(n_lt)], axis=1)


def _fold_spectrum(te, to):
    half = te.shape[0] // 2
    row0 = lax.broadcasted_iota(jnp.int32, (half, te.shape[1]), 0) == 0
    ce, se, co, so = te[:half], te[half:], to[:half], to[half:]
    a = jnp.concatenate([ce + co, jnp.where(row0, se, ce - co)], axis=0)
    b = jnp.concatenate([jnp.where(row0, ce - co, se + so), jnp.where(row0, so, so - se)], axis=0)
    return a, b


def _unfold_spectrum(yr, yi):
    half = yr.shape[0] // 2
    row0 = lax.broadcasted_iota(jnp.int32, (half, yr.shape[1]), 0) == 0
    yr_lo, yr_up, yi_lo, yi_up = yr[:half], yr[half:], yi[:half], yi[half:]
    even = jnp.concatenate([jnp.where(row0, yr_lo + yi_lo, yr_lo + yr_up),
                            jnp.where(row0, yr_up, yi_lo - yi_up)], axis=0)
    odd = jnp.concatenate([jnp.where(row0, yr_lo - yi_lo, yr_lo - yr_up),
                           jnp.where(row0, yi_up, yi_lo + yi_up)], axis=0)
    return even, odd


def _filter_kernel(z_ref, w1_ref, b1_ref, w2_ref, b2_ref, w3f_ref, b3f_ref, w3b_ref, b3b_ref,
                   decf_ref, decb_ref, fwde_ref, fwdo_ref, p_ref, h_sc, par_sc, *, L):
    hi = lax.Precision.HIGHEST

    @pl.when((pl.program_id(1) == 0) & (pl.program_id(2) == 0))
    def _():
        h1 = jnp.sin(jnp.dot(z_ref[...], w1_ref[...], precision=hi, preferred_element_type=F32) + b1_ref[...])
        h_sc[...] = jnp.sin(jnp.dot(h1, w2_ref[...], precision=hi, preferred_element_type=F32) + b2_ref[...])

    h = h_sc[...]
    tc = w3f_ref.shape[1]
    pos = lax.broadcasted_iota(jnp.int32, (L, tc), 0)
    t = pos.astype(F32) / float(max(L - 1, 1))

    def taps(w3_ref, b3_ref, dec_ref):
        g = jnp.dot(h, w3_ref[...], precision=hi, preferred_element_type=F32) + b3_ref[...]
        return g * (jnp.exp(-t * jnp.abs(dec_ref[...])) + HYENA_WINDOW_SHIFT)

    hf = taps(w3f_ref, b3f_ref, decf_ref)
    hb = jnp.where(pos == 0, 0.0, taps(w3b_ref, b3b_ref, decb_ref))

    def transform(taps_lc):
        even, odd = _split_parity(par_sc, taps_lc)
        return _fold_spectrum(_dot(fwde_ref[...], even.astype(BF16)), _dot(fwdo_ref[...], odd.astype(BF16)))

    fa, fb = transform(hf)
    ba, bb = transform(hb)
    k_re = fa + ba
    k_im = bb - fb
    k_ny = fb + bb
    first = pos == 0
    inv_n = 1.0 / (2 * L)
    p_ref[0] = jnp.where(first, k_re * inv_n, k_re * (2.0 * inv_n))
    p_ref[1] = jnp.where(first, 0.0, k_im * (2.0 * inv_n))
    p_ref[2] = jnp.where(first, k_ny * inv_n, k_re * (2.0 * inv_n))


def _hyena_spectra(L, w1, b1, w2, b2, w3, b3, decay, fwd_even, fwd_odd):
    pad_h = MLP_PAD - HYENA_FILTER_HIDDEN
    z = jnp.asarray(_pos_features(L))
    w1p = jnp.pad(w1, ((0, 0), (0, MLP_PAD - HYENA_POS_DIM), (0, pad_h)))
    b1p = jnp.pad(b1, ((0, 0), (0, pad_h))).reshape(DEPTH, 1, MLP_PAD)
    w2p = jnp.pad(w2, ((0, 0), (0, pad_h), (0, pad_h)))
    b2p = jnp.pad(b2, ((0, 0), (0, pad_h))).reshape(DEPTH, 1, MLP_PAD)
    w3p = jnp.pad(w3, ((0, 0), (0, pad_h), (0, 0)))
    ncol = w3.shape[-1]
    b3r = b3.reshape(DEPTH, 1, ncol)
    decr = decay.reshape(DEPTH, 1, ncol)
    tc = HYENA_CHANNEL_TILE
    nct = D_HYENA // tc
    per_dir = HYENA_ORDER * nct
    col_f = lambda l, o, c: (l, 0, o * nct + c)
    col_b = lambda l, o, c: (l, 0, per_dir + o * nct + c)
    per_layer = lambda shape: pl.BlockSpec((None,) + shape, lambda l, o, c: (l, 0, 0))
    return pl.pallas_call(
        functools.partial(_filter_kernel, L=L),
        grid=(DEPTH, HYENA_ORDER, nct),
        in_specs=[
            _const_spec((L, MLP_PAD)),
            per_layer((MLP_PAD, MLP_PAD)), per_layer((1, MLP_PAD)),
            per_layer((MLP_PAD, MLP_PAD)), per_layer((1, MLP_PAD)),
            pl.BlockSpec((None, MLP_PAD, tc), col_f), pl.BlockSpec((None, 1, tc), col_f),
            pl.BlockSpec((None, MLP_PAD, tc), col_b), pl.BlockSpec((None, 1, tc), col_b),
            pl.BlockSpec((None, 1, tc), col_f), pl.BlockSpec((None, 1, tc), col_b),
            _const_spec((L, L // 2)), _const_spec((L, L // 2)),
        ],
        out_specs=pl.BlockSpec((None, None, 3, L, tc), lambda l, o, c: (l, o, 0, 0, c)),
        out_shape=jax.ShapeDtypeStruct((DEPTH, HYENA_ORDER, 3, L, D_HYENA), F32),
        scratch_shapes=[pltpu.VMEM((L, MLP_PAD), F32), pltpu.VMEM((tc // LANES, L, LANES), F32)],
        compiler_params=_params("arbitrary", "arbitrary", "arbitrary"),
        name="hyena_spectra",
    )(z, w1p, b1p, w2p, b2p, w3p, b3r, w3p, b3r, decr, decr, fwd_even, fwd_odd)


def _seqmix_kernel(v_ref, x1_ref, x2_ref, p_ref, hb_ref, fwde_ref, fwdo_ref, inve_ref, invo_ref, ya_ref,
                   spec_sc, par_sc):
    n_req = v_ref.shape[0]
    mult_refs = (x1_ref, x2_ref)

    def chain(r):
        z = v_ref[r].astype(F32)
        z_even, z_odd = _split_parity(par_sc.at[r], z)
        yield
        for o in range(HYENA_ORDER):
            te = _dot(fwde_ref[...], z_even.astype(BF16))
            to = _dot(fwdo_ref[...], z_odd.astype(BF16))
            yield
            a, b = _fold_spectrum(te, to)
            p2 = p_ref[o, 1]
            even, odd = _unfold_spectrum(a * p_ref[o, 0] + b * p2, b * p_ref[o, 2] - a * p2)
            spec_sc[r, 0] = even.astype(BF16)
            spec_sc[r, 1] = odd.astype(BF16)
            yield
            y_even = _dot(inve_ref[...], spec_sc[r, 0])
            y_odd = _dot(invo_ref[...], spec_sc[r, 1])
            yield
            z = mult_refs[o][r].astype(F32) * (_join_parity(par_sc.at[r], y_even, y_odd) + hb_ref[o:o + 1, :] * z)
            if o < HYENA_ORDER - 1:
                z_even, z_odd = _split_parity(par_sc.at[r], z)
            else:
                ya_ref[r] = z.astype(ya_ref.dtype)
            yield

    chains = [chain(r) for r in range(n_req)]
    n_stages = 1 + 4 * HYENA_ORDER
    for tick in range(n_stages + n_req - 1):
        for r, c in enumerate(chains):
            if 0 <= tick - r < n_stages:
                next(c)


def _seqmix(hyc3d, spectra, hy_bias, tabs, layer):
    B, L, _ = hyc3d.shape
    tc = HYENA_CHANNEL_TILE
    nct = D_HYENA // tc
    nr = max(2, STEP_TOKENS // L)
    act = lambda off: pl.BlockSpec((nr, L, tc), lambda c, b, off=off: (b, 0, off * nct + c))
    return pl.pallas_call(
        _seqmix_kernel,
        grid=(nct, B // nr),
        in_specs=[
            act(0), act(1), act(2),
            pl.BlockSpec((None, HYENA_ORDER, 3, L, tc), lambda c, b: (layer, 0, 0, 0, c),
                         pipeline_mode=pl.Buffered(1)),
            pl.BlockSpec((None, HYENA_ORDER, tc), lambda c, b: (layer, 0, c)),
            _const_spec((L, L // 2)), _const_spec((L, L // 2)), _const_spec((L // 2, L)), _const_spec((L // 2, L)),
        ],
        out_specs=pl.BlockSpec((nr, L, tc), lambda c, b: (b, 0, c)),
        out_shape=jax.ShapeDtypeStruct((B, L, D_HYENA), BF16),
        scratch_shapes=[pltpu.VMEM((nr, 2, L, tc), BF16), pltpu.VMEM((nr, tc // LANES, L, LANES), F32)],
        compiler_params=_params("arbitrary", "arbitrary"),
        name="seqmix",
    )(hyc3d, hyc3d, hyc3d, spectra, hy_bias, *tabs)


def _head_rms(x, ones_ref, w):
    sq = x * x
    hi = sq.astype(BF16)
    lo = (sq - hi.astype(F32)).astype(BF16)
    ones = ones_ref[...]
    width = x.shape[1]
    ss = _dot(hi, ones[:width, :width]) + _dot(lo, ones[:width, :width])
    return x * lax.rsqrt(ss * (1.0 / HEAD_DIM) + NORM_EPS) * w


def _rope(x, cos, sin):
    width = x.shape[1]
    lane = lax.broadcasted_iota(jnp.int32, x.shape, 1)
    half = HEAD_DIM // 4
    low = (lane % (2 * half)) < half
    rot = jnp.where(low, -pltpu.roll(x, width - half, 1), pltpu.roll(x, half, 1))
    return x * cos + rot * sin


def _attn_kernel(*refs, rope):
    if rope:
        (q_ref, k_ref, v_ref, ck_ref, cv_ref, qw_ref, kw_ref, ones_ref, rep_ref, rept_ref,
         cos_ref, sin_ref, yb_ref) = refs
    else:
        q_ref, k_ref, v_ref, qw_ref, kw_ref, ones_ref, rep_ref, rept_ref, yb_ref, ko_ref, vo_ref = refs
    n_req, L, _ = q_ref.shape
    gw = HEADS_PER_KV * HEAD_DIM
    lane = lax.broadcasted_iota(jnp.int32, (L, gw), 1)
    for r in range(n_req):
        kn = _head_rms(k_ref[r], ones_ref, kw_ref[...])
        v = v_ref[r]
        q = _head_rms(q_ref[r], ones_ref, qw_ref[...])
        if rope:
            keys = _rope(kn, cos_ref[:, :ATT_KV], sin_ref[:, :ATT_KV])
            keys = jnp.concatenate([ck_ref[r], keys], axis=0)
            vals = jnp.concatenate([cv_ref[r], v], axis=0)
            q = _rope(q, cos_ref[...], sin_ref[...])
        else:
            ko_ref[r] = kn
            vo_ref[r] = v
            keys, vals = kn, v
        keys = keys.astype(BF16)
        vals = vals.astype(BF16)
        q = q * (HEAD_DIM ** -0.5)
        for g in range(N_KV_HEADS):
            k4 = lax.dot_general(rept_ref[g], keys, (((1,), (1,)), ((), ())), preferred_element_type=F32).astype(BF16)
            v4 = _dot(vals, rep_ref[g]).astype(BF16)
            qg = q[:, g * gw:(g + 1) * gw]
            acc = jnp.zeros((L, gw), F32)
            for h in range(HEADS_PER_KV):
                mine = (lane // HEAD_DIM) == h
                qm = jnp.where(mine, qg, 0.0).astype(BF16)
                s = _dot(qm, k4)
                p = jnp.exp(s - jnp.max(s, axis=-1, keepdims=True))
                denom = jnp.sum(p, axis=-1, keepdims=True)
                o4 = _dot(p.astype(BF16), v4)
                acc = jnp.where(mine, o4 * (1.0 / denom), acc)
            yb_ref[r, :, g * gw:(g + 1) * gw] = acc.astype(yb_ref.dtype)


def _attention(qkv3d, q_norm_w, k_norm_w, ctx_k=None, ctx_v=None, layer=0):
    B, L, _ = qkv3d.shape
    rope = ctx_k is not None
    nr = _requests_per_step(L)
    ones_np, rep_np = _head_tables()
    ones = jnp.asarray(ones_np).astype(BF16)
    rep = jnp.asarray(rep_np).astype(BF16)
    rept = jnp.asarray(np.swapaxes(rep_np, 1, 2)).astype(BF16)
    qw = jnp.tile(q_norm_w, N_HEADS).reshape(1, ATT_Q)
    kw = jnp.tile(k_norm_w, N_KV_HEADS).reshape(1, ATT_KV)
    kblk = ATT_Q // ATT_KV
    in_specs = [
        pl.BlockSpec((nr, L, ATT_Q), lambda b: (b, 0, 0)),
        pl.BlockSpec((nr, L, ATT_KV), lambda b: (b, 0, kblk)),
        pl.BlockSpec((nr, L, ATT_KV), lambda b: (b, 0, kblk + 1)),
    ]
    args = [qkv3d, qkv3d, qkv3d]
    if rope:
        n_ctx = ctx_k.shape[2]
        in_specs += [pl.BlockSpec((nr, None, n_ctx, ATT_KV), lambda b: (b, layer, 0, 0))] * 2
        args += [ctx_k, ctx_v]
    in_specs += [_const_spec((1, ATT_Q)), _const_spec((1, ATT_KV)), _const_spec((ATT_Q, ATT_Q)),
                 _const_spec((N_KV_HEADS, ATT_KV, HEADS_PER_KV * HEAD_DIM)),
                 _const_spec((N_KV_HEADS, HEADS_PER_KV * HEAD_DIM, ATT_KV))]
    args += [qw, kw, ones, rep, rept]
    yb_shape = jax.ShapeDtypeStruct((B, L, ATT_Q), BF16)
    yb_spec = pl.BlockSpec((nr, L, ATT_Q), lambda b: (b, 0, 0))
    if rope:
        cos_np, sin_np = _rope_tables(L)
        in_specs += [_const_spec((L, ATT_Q))] * 2
        args += [jnp.asarray(cos_np), jnp.asarray(sin_np)]
        out_specs = yb_spec
        out_shape = yb_shape
    else:
        kv_spec = pl.BlockSpec((nr, L, ATT_KV), lambda b: (b, 0, 0))
        kv_shape = jax.ShapeDtypeStruct((B, L, ATT_KV), F32)
        out_specs = [yb_spec, kv_spec, kv_spec]
        out_shape = [yb_shape, kv_shape, kv_shape]
    return pl.pallas_call(
        functools.partial(_attn_kernel, rope=rope),
        grid=(B // nr,),
        in_specs=in_specs,
        out_specs=out_specs,
        out_shape=out_shape,
        compiler_params=_params("parallel"),
        name="attention",
    )(*args)


def _merge_kernel(ya_ref, yb_ref, yc_ref, x_ref, mod_ref, nw1_ref, nw2_ref, *rest):
    gate_w_refs = rest[:GATE_COLS // ATT_COLS]
    wa_ref, wb_ref, wc_ref, wo_ref, xo_ref, h2_ref = rest[GATE_COLS // ATT_COLS:]
    tm = x_ref.shape[0]
    n_chain = 2
    rows_per = tm // n_chain

    def chain(c):
        rows = pl.ds(c * rows_per, rows_per)
        x = x_ref[rows, :]
        h = _rms_modulate(x, nw1_ref[...], mod_ref[0:1, :], mod_ref[1:2, :]).astype(BF16)
        yield
        gates = jnp.concatenate([_dot(h, w_ref[...]) for w_ref in gate_w_refs], axis=1)
        da = _dot(ya_ref[rows, :], wa_ref[...])
        db = _dot(yb_ref[rows, :], wb_ref[...])
        dc = _dot(yc_ref[rows, :], wc_ref[...])
        yield
        merged = (_sigmoid(gates[:, 0:D_MODEL]) * da + _sigmoid(gates[:, D_MODEL:2 * D_MODEL]) * db
                  + _sigmoid(gates[:, 2 * D_MODEL:]) * dc).astype(BF16)
        yield
        proj = _dot(merged, wo_ref[...])
        yield
        x = x + mod_ref[2:3, :] * proj
        xo_ref[rows, :] = x
        h2_ref[rows, :] = _rms_modulate(x, nw2_ref[...], mod_ref[3:4, :], mod_ref[4:5, :]).astype(BF16)
        yield

    chains = [chain(c) for c in range(n_chain)]
    n_stages = 5
    for tick in range(n_stages + n_chain - 1):
        for c, ch in enumerate(chains):
            if 0 <= tick - c < n_stages:
                next(ch)


def _merge(ya, yb, yc, x2d, mod, norm1_w, norm2_w, w_in_bf, wa, wb, wc, wo, layer, rows_per_req, tm=MERGE_ROWS):
    T = x2d.shape[0]
    n_req = mod.shape[0]
    req = (lambda i: (i * tm) // rows_per_req) if n_req > 1 else (lambda i: 0)
    br = pl.BlockSpec((tm, D_HYENA), lambda i: (i, 0))
    row = pl.BlockSpec((tm, D_MODEL), lambda i: (i, 0))
    gate_blk0 = MIX_COLS // ATT_COLS
    gate_w = [pl.BlockSpec((None, D_MODEL, ATT_COLS), lambda i, k=k: (layer, 0, gate_blk0 + k),
                           pipeline_mode=pl.Buffered(1)) for k in range(GATE_COLS // ATT_COLS)]
    return pl.pallas_call(
        _merge_kernel,
        grid=(T // tm,),
        in_specs=[
            br, br, br, row,
            pl.BlockSpec((None, 6, D_MODEL), lambda i: (req(i), 0, 0)),
            _const_spec((1, D_MODEL)), _const_spec((1, D_MODEL)),
            *gate_w,
            _layer_spec((D_HYENA, D_MODEL), layer), _layer_spec((ATT_Q, D_MODEL), layer),
            _layer_spec((D_SCONV, D_MODEL), layer), _layer_spec((D_MODEL, D_MODEL), layer),
        ],
        out_specs=[row, row],
        out_shape=[jax.ShapeDtypeStruct((T, D_MODEL), F32), jax.ShapeDtypeStruct((T, D_MODEL), BF16)],
        compiler_params=_params("parallel"),
        name="merge",
    )(ya, yb, yc, x2d, mod, norm1_w, norm2_w, *([w_in_bf] * len(gate_w)), wa, wb, wc, wo)


def _route_kernel(h_ref, rw_ref, tri_ref, aff_ref, slot_ref, *, cap):
    b = pl.program_id(0)
    n_req = h_ref.shape[0]
    for r in range(n_req):
        logits = lax.dot_general(rw_ref[...], h_ref[r], (((1,), (1,)), ((), ())), preferred_element_type=F32)
        ex = jnp.exp(logits - jnp.max(logits, axis=0, keepdims=True))
        first_row = pl.multiple_of((b * n_req + r) * N_EXPERTS, N_EXPERTS)
        aff_ref[pl.ds(first_row, N_EXPERTS), :] = ex / jnp.sum(ex, axis=0, keepdims=True)

    @pl.when(b == pl.num_programs(0) - 1)
    def _():
        aff = aff_ref[...]

        def count(mask):
            return jnp.sum(jnp.where(mask, 1.0, 0.0), axis=1, keepdims=True)

        kth = jnp.zeros((aff.shape[0], 1), jnp.int32)
        for bit in range(30, -1, -1):
            trial = kth | (1 << bit)
            enough = count(aff >= lax.bitcast_convert_type(trial, F32)) >= cap
            kth = jnp.where(enough, trial, kth)
        next_up = lax.bitcast_convert_type(kth + 1, F32)
        above = aff >= next_up
        tied = (aff >= lax.bitcast_convert_type(kth, F32)) & (aff < next_up)
        tri = tri_ref[...]
        tied_before = _dot(jnp.where(tied, 1.0, 0.0).astype(BF16), tri)
        chosen = above | (tied & (tied_before < (cap - count(above))))
        slot = _dot(jnp.where(chosen, 1.0, 0.0).astype(BF16), tri)
        slot_ref[...] = jnp.where(chosen, slot, -1.0)


def _route(h3d, router_wt):
    B, n_tok, _ = h3d.shape
    cap = CAPACITY_FACTOR * n_tok // N_EXPERTS
    tri = jnp.asarray(_prefix_table(n_tok)).astype(BF16)
    whole = pl.BlockSpec((B * N_EXPERTS, n_tok), lambda b: (0, 0))
    shape = jax.ShapeDtypeStruct((B * N_EXPERTS, n_tok), F32)
    nr = _requests_per_step(n_tok)
    return pl.pallas_call(
        functools.partial(_route_kernel, cap=cap),
        grid=(B // nr,),
        in_specs=[
            pl.BlockSpec((nr, n_tok, D_MODEL), lambda b: (b, 0, 0)),
            _const_spec((N_EXPERTS, D_MODEL)),
            _const_spec((n_tok, n_tok)),
        ],
        out_specs=[whole, whole],
        out_shape=[shape, shape],
        compiler_params=_params("arbitrary"),
        name="moe_route",
    )(h3d, router_wt, tri)


def _dispatch_kernel(h_ref, aff_ref, slot_ref, xs_ref, d_ref, g_ref, *, n_tok, cap):
    row = lax.broadcasted_iota(jnp.int32, (cap, n_tok), 0).astype(F32)
    n_part = 2
    per_part = N_EXPERTS // n_part
    for r in range(h_ref.shape[0]):
        for part in range(n_part):
            for e in range(part * per_part, (part + 1) * per_part):
                idx = r * N_EXPERTS + e
                hit = row == slot_ref[idx:idx + 1, :]
                d_ref[r, e * cap:(e + 1) * cap, :] = jnp.where(hit, 1.0, 0.0).astype(BF16)
                gate = jnp.sum(jnp.where(hit, aff_ref[idx:idx + 1, :], 0.0), axis=1, keepdims=True)
                g_ref[r, e * cap:(e + 1) * cap, :] = jnp.broadcast_to(gate, (cap, LANES))
            rows = slice(part * per_part * cap, (part + 1) * per_part * cap)
            xs_ref[r, rows, :] = _dot(d_ref[r, rows, :], h_ref[r]).astype(BF16)


def _requests_per_step(n_tok):
    return max(1, STEP_TOKENS // n_tok)


def _dispatch(h3d, router_wt):
    B, n_tok, _ = h3d.shape
    cap = CAPACITY_FACTOR * n_tok // N_EXPERTS
    rows = N_EXPERTS * cap
    aff, slot = _route(h3d, router_wt)
    nr = _requests_per_step(n_tok)
    per_req = pl.BlockSpec((nr * N_EXPERTS, n_tok), lambda b: (b, 0))
    blk = lambda w: pl.BlockSpec((nr, rows, w), lambda b: (b, 0, 0))
    return pl.pallas_call(
        functools.partial(_dispatch_kernel, n_tok=n_tok, cap=cap),
        grid=(B // nr,),
        in_specs=[pl.BlockSpec((nr, n_tok, D_MODEL), lambda b: (b, 0, 0)), per_req, per_req],
        out_specs=[blk(D_MODEL), blk(n_tok), blk(LANES)],
        out_shape=[jax.ShapeDtypeStruct((B, rows, D_MODEL), BF16),
                   jax.ShapeDtypeStruct((B, rows, n_tok), BF16),
                   jax.ShapeDtypeStruct((B, rows, LANES), F32)],
        compiler_params=_params("parallel"),
        name="moe_dispatch",
    )(h3d, aff, slot)


def _ffn_kernel(xp_ref, gp_ref, xs_ref, gs_ref, wg_ref, wu_ref, wd_ref, yp_ref, ys_ref):
    wg = wg_ref[...].astype(BF16)
    wu = wu_ref[...].astype(BF16)
    wd = wd_ref[...].astype(BF16)
    for x_ref, g_ref, y_ref in ((xp_ref, gp_ref, yp_ref), (xs_ref, gs_ref, ys_ref)):
        nb, cap, _ = x_ref.shape
        x = x_ref[...].reshape(nb * cap, D_MODEL)
        gate = g_ref[...].reshape(nb * cap, LANES)
        gate = jnp.concatenate([gate] * (EXPERT_FF // LANES), axis=1)
        act = _silu(_dot(x, wg)) * _dot(x, wu) * gate
        y = _dot(act.astype(BF16), wd)
        y_ref[...] = y.reshape(nb, cap, D_MODEL).astype(y_ref.dtype)


def _expert_ffn(xp, gp, xs, gs, w_gate, w_up, w_down, layer):
    def act_spec(a):
        nb, _, cap, w = a.shape
        return pl.BlockSpec((nb, None, cap, w), lambda e: (0, e, 0, 0))

    def w_spec(a):
        return pl.BlockSpec((None, None) + a.shape[2:], lambda e: (layer, e, 0, 0))

    return pl.pallas_call(
        _ffn_kernel,
        grid=(N_EXPERTS,),
        in_specs=[act_spec(xp), act_spec(gp), act_spec(xs), act_spec(gs),
                  w_spec(w_gate), w_spec(w_up), w_spec(w_down)],
        out_specs=[act_spec(xp), act_spec(xs)],
        out_shape=[jax.ShapeDtypeStruct(xp.shape, BF16), jax.ShapeDtypeStruct(xs.shape, BF16)],
        compiler_params=_params("parallel"),
        name="expert_ffn",
    )(xp, gp, xs, gs, w_gate, w_up, w_down)


def _combine_kernel(d_ref, y_ref, x_ref, mod_ref, o_ref):
    for r in range(d_ref.shape[0]):
        moe = lax.dot_general(d_ref[r], y_ref[r], (((0,), (0,)), ((), ())), preferred_element_type=F32)
        o_ref[r] = x_ref[r] + mod_ref[min(r, mod_ref.shape[0] - 1)] * moe


def _combine(dmat, y3d, x3d, mod_g2):
    B, n_tok, _ = x3d.shape
    rows = dmat.shape[1]
    nr = _requests_per_step(n_tok)
    shared_mod = mod_g2.shape[0] == 1
    mod_spec = (pl.BlockSpec((1, 1, D_MODEL), lambda b: (0, 0, 0)) if shared_mod
                else pl.BlockSpec((nr, 1, D_MODEL), lambda b: (b, 0, 0)))
    return pl.pallas_call(
        _combine_kernel,
        grid=(B // nr,),
        in_specs=[
            pl.BlockSpec((nr, rows, n_tok), lambda b: (b, 0, 0)),
            pl.BlockSpec((nr, rows, D_MODEL), lambda b: (b, 0, 0)),
            pl.BlockSpec((nr, n_tok, D_MODEL), lambda b: (b, 0, 0)),
            mod_spec,
        ],
        out_specs=pl.BlockSpec((nr, n_tok, D_MODEL), lambda b: (b, 0, 0)),
        out_shape=jax.ShapeDtypeStruct(x3d.shape, F32),
        compiler_params=_params("parallel"),
        name="moe_combine",
    )(dmat, y3d, x3d, mod_g2)


def _token_mixers(x3d, mod, p, tabs, spectra, layer, ctx=None):
    B, L, _ = x3d.shape
    x2d = x3d.reshape(B * L, D_MODEL)
    hyc, yc, qkv = _inproj(x2d, mod, p['norm1_w'], p['w_in'], p['hy_short_w'], p['hy_short_b'], p['sc_w'],
                           p['sc_b'], layer, L)
    ya = _seqmix(hyc.reshape(B, L, HY_COLS), spectra, p['hy_bias'], tabs, layer)
    qkv3d = qkv.reshape(B, L, ATT_COLS)
    if ctx is None:
        yb, k, v = _attention(qkv3d, p['q_norm_w'], p['k_norm_w'])
    else:
        yb = _attention(qkv3d, p['q_norm_w'], p['k_norm_w'], ctx[0], ctx[1], layer)
        k = v = None
    x_mid, h2 = _merge(ya.reshape(B * L, D_HYENA), yb.reshape(B * L, ATT_Q), yc,
                       x2d, mod, p['norm1_w'], p['norm2_w'], p['w_in'], p['w_br_a'], p['w_br_b'], p['w_br_c'],
                       p['w_o'], layer, L)
    return x_mid.reshape(B, L, D_MODEL), h2.reshape(B, L, D_MODEL), k, v


def _moe_split(a, n_exp):
    B, rows, w = a.shape
    return a.reshape(B, n_exp, rows // n_exp, w)


def kernel(x_prompt, x_sample, cache_k, cache_v, c, c_ctx, mod_w, mod_b, norm1_w, norm2_w, w_in, hy_short_w, hy_short_b, hy_w1, hy_b1, hy_w2, hy_b2, hy_w3, hy_b3, hy_decay, hy_bias, q_norm_w, k_norm_w, sc_w, sc_b, w_br_a, w_br_b, w_br_c, w_o, router_w, exp_w_gate, exp_w_up, exp_w_down):
    n_dec = x_sample.shape[0]
    n_ctx = cache_k.shape[2]
    lp = x_prompt.shape[1]
    ls = x_sample.shape[1]

    cond_rows = COND_ROWS
    cond = jnp.concatenate([c_ctx[None, :], c, jnp.zeros((cond_rows - 1 - n_dec, D_MODEL), F32)], axis=0)
    mod = _modulation(cond, mod_w, mod_b).reshape(DEPTH, cond_rows, 6, D_MODEL)

    tabs_p = tuple(jnp.asarray(t).astype(BF16) for t in _split_dft_tables(lp))
    tabs_s = tuple(jnp.asarray(t).astype(BF16) for t in _split_dft_tables(ls))
    ctx_k = cache_k.reshape(n_dec, DEPTH, n_ctx, ATT_KV)
    ctx_v = cache_v.reshape(n_dec, DEPTH, n_ctx, ATT_KV)
    dense = {'w_in': w_in.astype(BF16), 'w_br_a': w_br_a.astype(BF16), 'w_br_b': w_br_b.astype(BF16),
             'w_br_c': w_br_c.astype(BF16), 'w_o': w_o.astype(BF16)}

    filt = (hy_w1, hy_b1, hy_w2, hy_b2, hy_w3, hy_b3, hy_decay)
    spectra_p = _hyena_spectra(lp, *filt, tabs_p[0], tabs_p[1])
    spectra_s = _hyena_spectra(ls, *filt, tabs_s[0], tabs_s[1])

    xp, xs = x_prompt, x_sample
    ks_new, vs_new = [], []
    for l in range(DEPTH):
        p = dict(dense)
        p.update({
            'norm1_w': norm1_w[l].reshape(1, D_MODEL), 'norm2_w': norm2_w[l].reshape(1, D_MODEL),
            'hy_short_w': hy_short_w[l], 'hy_short_b': hy_short_b[l],
            'hy_bias': hy_bias,
            'q_norm_w': q_norm_w[l], 'k_norm_w': k_norm_w[l], 'sc_w': sc_w[l], 'sc_b': sc_b[l],
        })
        mod_p = mod[l, 0:1]
        mod_s = mod[l, 1:1 + n_dec]
        xp_mid, h2p, k_l, v_l = _token_mixers(xp, mod_p, p, tabs_p, spectra_p, l)
        xs_mid, h2s, _, _ = _token_mixers(xs, mod_s, p, tabs_s, spectra_s, l, (ctx_k, ctx_v))
        ks_new.append(k_l.reshape(k_l.shape[0], lp, N_KV_HEADS, HEAD_DIM))
        vs_new.append(v_l.reshape(v_l.shape[0], lp, N_KV_HEADS, HEAD_DIM))

        router_wt = router_w[l].T.astype(BF16)
        gp_x, gp_d, gp_g = _dispatch(h2p, router_wt)
        gs_x, gs_d, gs_g = _dispatch(h2s, router_wt)
        yp, ys = _expert_ffn(_moe_split(gp_x, N_EXPERTS), _moe_split(gp_g, N_EXPERTS),
                             _moe_split(gs_x, N_EXPERTS), _moe_split(gs_g, N_EXPERTS),
                             exp_w_gate, exp_w_up, exp_w_down, l)
        xp = _combine(gp_d, yp.reshape(gp_x.shape), xp_mid, mod_p[:, 5:6])
        xs = _combine(gs_d, ys.reshape(gs_x.shape), xs_mid, mod_s[:, 5:6])

    return (xp, xs, jnp.stack(ks_new, axis=1), jnp.stack(vs_new, axis=1))
```

```python
import functools
import math

import numpy as np
import jax
import jax.numpy as jnp
from jax import lax
from jax.experimental import pallas as pl
from jax.experimental.pallas import tpu as pltpu

D_MODEL = 1024
DEPTH = 2
GRID_W = 64
D_HYENA = 512
HYENA_ORDER = 2
HYENA_POS_BANDS = 16
HYENA_POS_DIM = 1 + 2 * HYENA_POS_BANDS
HYENA_FILTER_HIDDEN = 64
HYENA_WINDOW_SHIFT = 0.05
N_HEADS = 8
N_KV_HEADS = 2
HEAD_DIM = 64
HEADS_PER_KV = N_HEADS // N_KV_HEADS
ATT_Q = N_HEADS * HEAD_DIM
ATT_KV = N_KV_HEADS * HEAD_DIM
ROPE_THETA = 10000.0
D_SCONV = 512
N_EXPERTS = 16
EXPERT_FF = 512
CAPACITY_FACTOR = 2
NORM_EPS = 1e-6
HY_COLS = (HYENA_ORDER + 1) * D_HYENA
ATT_COLS = ATT_Q + 2 * ATT_KV
SC_COLS = 3 * D_SCONV
GATE_COLS = 3 * D_MODEL
MIX_COLS = HY_COLS + ATT_COLS + SC_COLS
D_IN = MIX_COLS + GATE_COLS

F32 = jnp.float32
BF16 = jnp.bfloat16

V7X_VMEM_BYTES = 64 * 1024 * 1024
VMEM_LIMIT = V7X_VMEM_BYTES - 8 * 1024 * 1024
LANES = 128
MLP_PAD = 128
STEP_TOKENS = 1024
INPROJ_ROWS = 1024
MERGE_ROWS = 1024
HYENA_CHANNEL_TILE = 256
MOD_COL_TILE = 2048
COND_ROWS = 16


def _params(*sem):
    return pltpu.CompilerParams(dimension_semantics=sem, vmem_limit_bytes=VMEM_LIMIT)


def _const_spec(shape):
    nd = len(shape)
    return pl.BlockSpec(shape, lambda *_: (0,) * nd, pipeline_mode=pl.Buffered(1))


def _layer_spec(shape, layer):
    nd = len(shape)
    return pl.BlockSpec((None,) + tuple(shape), lambda *_: (layer,) + (0,) * nd, pipeline_mode=pl.Buffered(1))


def _dot(a, b):
    return jnp.dot(a, b, preferred_element_type=F32)


def _silu(x):
    return x * (1.0 / (1.0 + jnp.exp(-x)))


def _sigmoid(x):
    return 1.0 / (1.0 + jnp.exp(-x))


@functools.lru_cache(maxsize=None)
def _split_dft_tables(L):
    h = L // 2
    k = np.arange(h, dtype=np.int64)
    alt = 1.0 - 2.0 * (k % 2)
    fwd, inv = [], []
    for odd in (0, 1):
        idx = (k[:, None] * (2 * k[None, :] + odd)) % (2 * L)
        ang = idx.astype(np.float64) * (np.pi / L)
        c = np.cos(ang)
        s = np.sin(ang)
        s[0, :] = alt
        fwd.append(np.concatenate([c, s], axis=0).astype(np.float32))
        inv.append(np.concatenate([c.T, s.T], axis=1).astype(np.float32))
    return fwd[0], fwd[1], inv[0], inv[1]


@functools.lru_cache(maxsize=None)
def _pos_features(L):
    n = np.arange(L, dtype=np.float64)
    t = n / max(L - 1, 1)
    bands = np.linspace(1e-4, HYENA_POS_BANDS - 1, HYENA_POS_BANDS)
    ang = 2.0 * math.pi * n[:, None] * bands[None, :] / L
    z = np.concatenate([t[:, None], np.cos(ang), np.sin(ang)], axis=-1)
    zp = np.zeros((L, MLP_PAD), np.float32)
    zp[:, :HYENA_POS_DIM] = z
    return zp


@functools.lru_cache(maxsize=None)
def _rope_tables(L):
    rows = L // GRID_W
    row = np.repeat(np.arange(rows, dtype=np.float64), GRID_W)
    col = np.tile(np.arange(GRID_W, dtype=np.float64), rows)
    axis_dim = HEAD_DIM // 2
    inv_freq = ROPE_THETA ** (-np.arange(0, axis_dim, 2, dtype=np.float64) / axis_dim)
    ar = row[:, None] * inv_freq[None, :]
    ac = col[:, None] * inv_freq[None, :]
    ang = np.concatenate([ar, ar, ac, ac], axis=-1)
    cos = np.cos(ang).astype(np.float32)
    sin = np.sin(ang).astype(np.float32)
    return np.tile(cos, (1, N_HEADS)), np.tile(sin, (1, N_HEADS))


@functools.lru_cache(maxsize=None)
def _head_tables():
    lane = np.arange(ATT_Q)
    block_ones = (lane[:, None] // HEAD_DIM == lane[None, :] // HEAD_DIM).astype(np.float32)
    src = np.arange(ATT_KV)
    dst = np.arange(HEADS_PER_KV * HEAD_DIM)
    rep = np.stack([(src[:, None] == g * HEAD_DIM + dst[None, :] % HEAD_DIM) for g in range(N_KV_HEADS)])
    return block_ones, rep.astype(np.float32)


@functools.lru_cache(maxsize=None)
def _prefix_table(n):
    i = np.arange(n)
    return (i[:, None] < i[None, :]).astype(np.float32)


def _mod_kernel(cond_ref, w_ref, b_ref, o_ref):
    a = _silu(cond_ref[...]).astype(BF16)
    o_ref[...] = _dot(a, w_ref[...].astype(BF16)) + b_ref[...]


def _modulation(cond, mod_w, mod_b):
    rows = cond.shape[0]
    tn = MOD_COL_TILE
    ncols = mod_w.shape[-1]
    return pl.pallas_call(
        _mod_kernel,
        grid=(DEPTH, ncols // tn),
        in_specs=[
            pl.BlockSpec((rows, D_MODEL), lambda l, j: (0, 0)),
            pl.BlockSpec((None, D_MODEL, tn), lambda l, j: (l, 0, j)),
            pl.BlockSpec((None, 1, tn), lambda l, j: (l, 0, j)),
        ],
        out_specs=pl.BlockSpec((None, rows, tn), lambda l, j: (l, 0, j)),
        out_shape=jax.ShapeDtypeStruct((DEPTH, rows, ncols), F32),
        compiler_params=_params("parallel", "parallel"),
        name="modulation",
    )(cond, mod_w, mod_b.reshape(DEPTH, 1, ncols))


def _rms_modulate(x, norm_w, shift, scale):
    y = x * lax.rsqrt(jnp.mean(x * x, axis=-1, keepdims=True) + NORM_EPS)
    return (y * norm_w) * (1.0 + scale) + shift


def _conv3(x, w, b, first, last):
    n = x.shape[0]
    prev = jnp.where(first, 0.0, pltpu.roll(x, 1, 0))
    nxt = jnp.where(last, 0.0, pltpu.roll(x, n - 1, 0))
    return prev * w[0:1, :] + x * w[1:2, :] + nxt * w[2:3, :] + b


def _inproj_kernel(x_ref, mod_ref, nw_ref, w_ref, hyw_ref, hyb_ref, scw_ref, scb_ref,
                   hyc_ref, yc_ref, qkv_ref, h_sc, *, L):
    h = _rms_modulate(x_ref[...], nw_ref[...], mod_ref[0:1, :], mod_ref[1:2, :])
    h_sc[...] = h.astype(BF16)
    tm = x_ref.shape[0]

    def edges(cols):
        pos = lax.broadcasted_iota(jnp.int32, (tm, cols), 0) % L
        return pos == 0, pos == L - 1

    sc0 = HY_COLS + ATT_COLS
    u, bgate, cgate = (_dot(h_sc[...], w_ref[:, sc0 + k * D_SCONV:sc0 + (k + 1) * D_SCONV]) for k in range(3))
    hy_chunks = [slice(lo, lo + ATT_COLS) for lo in range(0, HY_COLS, ATT_COLS)]
    y_next = _dot(h_sc[...], w_ref[:, hy_chunks[0]])
    first, last = edges(D_SCONV)
    yc_ref[...] = (bgate * _conv3(cgate * u, scw_ref[...], scb_ref[...], first, last)).astype(BF16)
    first, last = edges(ATT_COLS)
    for i, cols in enumerate(hy_chunks):
        y = y_next
        if i + 1 < len(hy_chunks):
            y_next = _dot(h_sc[...], w_ref[:, hy_chunks[i + 1]])
        else:
            qkv_ref[...] = _dot(h_sc[...], w_ref[:, HY_COLS:HY_COLS + ATT_COLS])
        hyc_ref[:, cols] = _conv3(y, hyw_ref[:, cols], hyb_ref[:, cols], first, last).astype(BF16)


def _inproj(x2d, mod, norm_w, w_in_bf, hy_short_w, hy_short_b, sc_w, sc_b, layer, rows_per_req, tm=INPROJ_ROWS):
    T = x2d.shape[0]
    n_req = mod.shape[0]
    req = (lambda i: (i * tm) // rows_per_req) if n_req > 1 else (lambda i: 0)
    out = lambda w: pl.BlockSpec((tm, w), lambda i: (i, 0))
    return pl.pallas_call(
        functools.partial(_inproj_kernel, L=rows_per_req),
        grid=(T // tm,),
        in_specs=[
            pl.BlockSpec((tm, D_MODEL), lambda i: (i, 0)),
            pl.BlockSpec((None, 6, D_MODEL), lambda i: (req(i), 0, 0)),
            _const_spec((1, D_MODEL)),
            _layer_spec((D_MODEL, MIX_COLS), layer),
            _const_spec((3, HY_COLS)), _const_spec((1, HY_COLS)),
            _const_spec((3, D_SCONV)), _const_spec((1, D_SCONV)),
        ],
        out_specs=[out(HY_COLS), out(D_SCONV), out(ATT_COLS)],
        out_shape=[jax.ShapeDtypeStruct((T, HY_COLS), BF16), jax.ShapeDtypeStruct((T, D_SCONV), BF16),
                   jax.ShapeDtypeStruct((T, ATT_COLS), F32)],
        scratch_shapes=[pltpu.VMEM((tm, D_MODEL), BF16)],
        compiler_params=_params("parallel"),
        name="inproj",
    )(x2d, mod, norm_w, w_in_bf, hy_short_w, hy_short_b.reshape(1, HY_COLS), sc_w, sc_b.reshape(1, D_SCONV))


def _split_parity(par_sc, z):
    half = z.shape[0] // 2
    n_lt = z.shape[1] // LANES
    for c in range(n_lt):
        par_sc[c] = z[:, c * LANES:(c + 1) * LANES]
    return tuple(jnp.concatenate([par_sc[c, pl.ds(par, half, stride=2), :] for c in range(n_lt)], axis=1)
                 for par in (0, 1))


def _join_parity(par_sc, even, odd):
    half = even.shape[0]
    n_lt = even.shape[1] // LANES
    for c in range(n_lt):
        par_sc[c, pl.ds(0, half, stride=2), :] = even[:, c * LANES:(c + 1) * LANES]
        par_sc[c, pl.ds(1, half, stride=2), :] = odd[:, c * LANES:(c + 1) * LANES]
    return jnp.concatenate([par_sc[c] for c in range(n_lt)], axis=1)


def _fold_spectrum(te, to):
    half = te.shape[0] // 2
    row0 = lax.broadcasted_iota(jnp.int32, (half, te.shape[1]), 0) == 0
    ce, se, co, so = te[:half], te[half:], to[:half], to[half:]
    a = jnp.concatenate([ce + co, jnp.where(row0, se, ce - co)], axis=0)
    b = jnp.concatenate([jnp.where(row0, ce - co, se + so), jnp.where(row0, so, so - se)], axis=0)
    return a, b


def _unfold_spectrum(yr, yi):
    half = yr.shape[0] // 2
    row0 = lax.broadcasted_iota(jnp.int32, (half, yr.shape[1]), 0) == 0
    yr_lo, yr_up, yi_lo, yi_up = yr[:half], yr[half:], yi[:half], yi[half:]
    even = jnp.concatenate([jnp.where(row0, yr_lo + yi_lo, yr_lo + yr_up),
                            jnp.where(row0, yr_up, yi_lo - yi_up)], axis=0)
    odd = jnp.concatenate([jnp.where(row0, yr_lo - yi_lo, yr_lo - yr_up),
                           jnp.where(row0, yi_up, yi_lo + yi_up)], axis=0)
    return even, odd


def _filter_kernel(z_ref, w1_ref, b1_ref, w2_ref, b2_ref, w3f_ref, b3f_ref, w3b_ref, b3b_ref,
                   decf_ref, decb_ref, fwde_ref, fwdo_ref, p_ref, h_sc, par_sc, *, L):
    hi = lax.Precision.HIGHEST

    @pl.when((pl.program_id(1) == 0) & (pl.program_id(2) == 0))
    def _():
        h1 = jnp.sin(jnp.dot(z_ref[...], w1_ref[...], precision=hi, preferred_element_type=F32) + b1_ref[...])
        h_sc[...] = jnp.sin(jnp.dot(h1, w2_ref[...], precision=hi, preferred_element_type=F32) + b2_ref[...])

    h = h_sc[...]
    tc = w3f_ref.shape[1]
    pos = lax.broadcasted_iota(jnp.int32, (L, tc), 0)
    t = pos.astype(F32) / float(max(L - 1, 1))

    def taps(w3_ref, b3_ref, dec_ref):
        g = jnp.dot(h, w3_ref[...], precision=hi, preferred_element_type=F32) + b3_ref[...]
        return g * (jnp.exp(-t * jnp.abs(dec_ref[...])) + HYENA_WINDOW_SHIFT)

    hf = taps(w3f_ref, b3f_ref, decf_ref)
    hb = jnp.where(pos == 0, 0.0, taps(w3b_ref, b3b_ref, decb_ref))

    def transform(taps_lc):
        even, odd = _split_parity(par_sc, taps_lc)
        return _fold_spectrum(_dot(fwde_ref[...], even.astype(BF16)), _dot(fwdo_ref[...], odd.astype(BF16)))

    fa, fb = transform(hf)
    ba, bb = transform(hb)
    k_re = fa + ba
    k_im = bb - fb
    k_ny = fb + bb
    first = pos == 0
    inv_n = 1.0 / (2 * L)
    p_ref[0] = jnp.where(first, k_re * inv_n, k_re * (2.0 * inv_n))
    p_ref[1] = jnp.where(first, 0.0, k_im * (2.0 * inv_n))
    p_ref[2] = jnp.where(first, k_ny * inv_n, k_re * (2.0 * inv_n))


def _hyena_spectra(L, w1, b1, w2, b2, w3, b3, decay, fwd_even, fwd_odd):
    pad_h = MLP_PAD - HYENA_FILTER_HIDDEN
    z = jnp.asarray(_pos_features(L))
    w1p = jnp.pad(w1, ((0, 0), (0, MLP_PAD - HYENA_POS_DIM), (0, pad_h)))
    b1p = jnp.pad(b1, ((0, 0), (0, pad_h))).reshape(DEPTH, 1, MLP_PAD)
    w2p = jnp.pad(w2, ((0, 0), (0, pad_h), (0, pad_h)))
    b2p = jnp.pad(b2, ((0, 0), (0, pad_h))).reshape(DEPTH, 1, MLP_PAD)
    w3p = jnp.pad(w3, ((0, 0), (0, pad_h), (0, 0)))
    ncol = w3.shape[-1]
    b3r = b3.reshape(DEPTH, 1, ncol)
    decr = decay.reshape(DEPTH, 1, ncol)
    tc = HYENA_CHANNEL_TILE
    nct = D_HYENA // tc
    per_dir = HYENA_ORDER * nct
    col_f = lambda l, o, c: (l, 0, o * nct + c)
    col_b = lambda l, o, c: (l, 0, per_dir + o * nct + c)
    per_layer = lambda shape: pl.BlockSpec((None,) + shape, lambda l, o, c: (l, 0, 0))
    return pl.pallas_call(
        functools.partial(_filter_kernel, L=L),
        grid=(DEPTH, HYENA_ORDER, nct),
        in_specs=[
            _const_spec((L, MLP_PAD)),
            per_layer((MLP_PAD, MLP_PAD)), per_layer((1, MLP_PAD)),
            per_layer((MLP_PAD, MLP_PAD)), per_layer((1, MLP_PAD)),
            pl.BlockSpec((None, MLP_PAD, tc), col_f), pl.BlockSpec((None, 1, tc), col_f),
            pl.BlockSpec((None, MLP_PAD, tc), col_b), pl.BlockSpec((None, 1, tc), col_b),
            pl.BlockSpec((None, 1, tc), col_f), pl.BlockSpec((None, 1, tc), col_b),
            _const_spec((L, L // 2)), _const_spec((L, L // 2)),
        ],
        out_specs=pl.BlockSpec((None, None, 3, L, tc), lambda l, o, c: (l, o, 0, 0, c)),
        out_shape=jax.ShapeDtypeStruct((DEPTH, HYENA_ORDER, 3, L, D_HYENA), F32),
        scratch_shapes=[pltpu.VMEM((L, MLP_PAD), F32), pltpu.VMEM((tc // LANES, L, LANES), F32)],
        compiler_params=_params("arbitrary", "arbitrary", "arbitrary"),
        name="hyena_spectra",
    )(z, w1p, b1p, w2p, b2p, w3p, b3r, w3p, b3r, decr, decr, fwd_even, fwd_odd)


def _seqmix_kernel(v_ref, x1_ref, x2_ref, p_ref, hb_ref, fwde_ref, fwdo_ref, inve_ref, invo_ref, ya_ref,
                   spec_sc, par_sc):
    n_req = v_ref.shape[0]
    mult_refs = (x1_ref, x2_ref)

    def chain(r):
        z = v_ref[r].astype(F32)
        z_even, z_odd = _split_parity(par_sc.at[r], z)
        yield
        for o in range(HYENA_ORDER):
            te = _dot(fwde_ref[...], z_even.astype(BF16))
            to = _dot(fwdo_ref[...], z_odd.astype(BF16))
            yield
            a, b = _fold_spectrum(te, to)
            p2 = p_ref[o, 1]
            even, odd = _unfold_spectrum(a * p_ref[o, 0] + b * p2, b * p_ref[o, 2] - a * p2)
            spec_sc[r, 0] = even.astype(BF16)
            spec_sc[r, 1] = odd.astype(BF16)
            yield
            y_even = _dot(inve_ref[...], spec_sc[r, 0])
            y_odd = _dot(invo_ref[...], spec_sc[r, 1])
            yield
            z = mult_refs[o][r].astype(F32) * (_join_parity(par_sc.at[r], y_even, y_odd) + hb_ref[o:o + 1, :] * z)
            if o < HYENA_ORDER - 1:
                z_even, z_odd = _split_parity(par_sc.at[r], z)
            else:
                ya_ref[r] = z.astype(ya_ref.dtype)
            yield

    chains = [chain(r) for r in range(n_req)]
    n_stages = 1 + 4 * HYENA_ORDER
    for tick in range(n_stages + n_req - 1):
        for r, c in enumerate(chains):
            if 0 <= tick - r < n_stages:
                next(c)


def _seqmix(hyc3d, spectra, hy_bias, tabs, layer):
    B, L, _ = hyc3d.shape
    tc = HYENA_CHANNEL_TILE
    nct = D_HYENA // tc
    nr = max(2, STEP_TOKENS // L)
    act = lambda off: pl.BlockSpec((nr, L, tc), lambda c, b, off=off: (b, 0, off * nct + c))
    return pl.pallas_call(
        _seqmix_kernel,
        grid=(nct, B // nr),
        in_specs=[
            act(0), act(1), act(2),
            pl.BlockSpec((None, HYENA_ORDER, 3, L, tc), lambda c, b: (layer, 0, 0, 0, c),
                         pipeline_mode=pl.Buffered(1)),
            pl.BlockSpec((None, HYENA_ORDER, tc), lambda c, b: (layer, 0, c)),
            _const_spec((L, L // 2)), _const_spec((L, L // 2)), _const_spec((L // 2, L)), _const_spec((L // 2, L)),
        ],
        out_specs=pl.BlockSpec((nr, L, tc), lambda c, b: (b, 0, c)),
        out_shape=jax.ShapeDtypeStruct((B, L, D_HYENA), BF16),
        scratch_shapes=[pltpu.VMEM((nr, 2, L, tc), BF16), pltpu.VMEM((nr, tc // LANES, L, LANES), F32)],
        compiler_params=_params("arbitrary", "arbitrary"),
        name="seqmix",
    )(hyc3d, hyc3d, hyc3d, spectra, hy_bias, *tabs)


def _head_rms(x, ones_ref, w):
    sq = x * x
    hi = sq.astype(BF16)
    lo = (sq - hi.astype(F32)).astype(BF16)
    ones = ones_ref[...]
    width = x.shape[1]
    ss = _dot(hi, ones[:width, :width]) + _dot(lo, ones[:width, :width])
    return x * lax.rsqrt(ss * (1.0 / HEAD_DIM) + NORM_EPS) * w


def _rope(x, cos, sin):
    width = x.shape[1]
    lane = lax.broadcasted_iota(jnp.int32, x.shape, 1)
    half = HEAD_DIM // 4
    low = (lane % (2 * half)) < half
    rot = jnp.where(low, -pltpu.roll(x, width - half, 1), pltpu.roll(x, half, 1))
    return x * cos + rot * sin


def _attn_kernel(*refs, rope):
    if rope:
        (q_ref, k_ref, v_ref, ck_ref, cv_ref, qw_ref, kw_ref, ones_ref, rep_ref, rept_ref,
         cos_ref, sin_ref, yb_ref) = refs
    else:
        q_ref, k_ref, v_ref, qw_ref, kw_ref, ones_ref, rep_ref, rept_ref, yb_ref, ko_ref, vo_ref = refs
    n_req, L, _ = q_ref.shape
    gw = HEADS_PER_KV * HEAD_DIM
    lane = lax.broadcasted_iota(jnp.int32, (L, gw), 1)
    for r in range(n_req):
        kn = _head_rms(k_ref[r], ones_ref, kw_ref[...])
        v = v_ref[r]
        q = _head_rms(q_ref[r], ones_ref, qw_ref[...])
        if rope:
            keys = _rope(kn, cos_ref[:, :ATT_KV], sin_ref[:, :ATT_KV])
            keys = jnp.concatenate([ck_ref[r], keys], axis=0)
            vals = jnp.concatenate([cv_ref[r], v], axis=0)
            q = _rope(q, cos_ref[...], sin_ref[...])
        else:
            ko_ref[r] = kn
            vo_ref[r] = v
            keys, vals = kn, v
        keys = keys.astype(BF16)
        vals = vals.astype(BF16)
        q = q * (HEAD_DIM ** -0.5)
        for g in range(N_KV_HEADS):
            k4 = lax.dot_general(rept_ref[g], keys, (((1,), (1,)), ((), ())), preferred_element_type=F32).astype(BF16)
            v4 = _dot(vals, rep_ref[g]).astype(BF16)
            qg = q[:, g * gw:(g + 1) * gw]
            acc = jnp.zeros((L, gw), F32)
            for h in range(HEADS_PER_KV):
                mine = (lane // HEAD_DIM) == h
                qm = jnp.where(mine, qg, 0.0).astype(BF16)
                s = _dot(qm, k4)
                p = jnp.exp(s - jnp.max(s, axis=-1, keepdims=True))
                denom = jnp.sum(p, axis=-1, keepdims=True)
                o4 = _dot(p.astype(BF16), v4)
                acc = jnp.where(mine, o4 * (1.0 / denom), acc)
            yb_ref[r, :, g * gw:(g + 1) * gw] = acc.astype(yb_ref.dtype)


def _attention(qkv3d, q_norm_w, k_norm_w, ctx_k=None, ctx_v=None, layer=0):
    B, L, _ = qkv3d.shape
    rope = ctx_k is not None
    nr = _requests_per_step(L)
    ones_np, rep_np = _head_tables()
    ones = jnp.asarray(ones_np).astype(BF16)
    rep = jnp.asarray(rep_np).astype(BF16)
    rept = jnp.asarray(np.swapaxes(rep_np, 1, 2)).astype(BF16)
    qw = jnp.tile(q_norm_w, N_HEADS).reshape(1, ATT_Q)
    kw = jnp.tile(k_norm_w, N_KV_HEADS).reshape(1, ATT_KV)
    kblk = ATT_Q // ATT_KV
    in_specs = [
        pl.BlockSpec((nr, L, ATT_Q), lambda b: (b, 0, 0)),
        pl.BlockSpec((nr, L, ATT_KV), lambda b: (b, 0, kblk)),
        pl.BlockSpec((nr, L, ATT_KV), lambda b: (b, 0, kblk + 1)),
    ]
    args = [qkv3d, qkv3d, qkv3d]
    if rope:
        n_ctx = ctx_k.shape[2]
        in_specs += [pl.BlockSpec((nr, None, n_ctx, ATT_KV), lambda b: (b, layer, 0, 0))] * 2
        args += [ctx_k, ctx_v]
    in_specs += [_const_spec((1, ATT_Q)), _const_spec((1, ATT_KV)), _const_spec((ATT_Q, ATT_Q)),
                 _const_spec((N_KV_HEADS, ATT_KV, HEADS_PER_KV * HEAD_DIM)),
                 _const_spec((N_KV_HEADS, HEADS_PER_KV * HEAD_DIM, ATT_KV))]
    args += [qw, kw, ones, rep, rept]
    yb_shape = jax.ShapeDtypeStruct((B, L, ATT_Q), BF16)
    yb_spec = pl.BlockSpec((nr, L, ATT_Q), lambda b: (b, 0, 0))
    if rope:
        cos_np, sin_np = _rope_tables(L)
        in_specs += [_const_spec((L, ATT_Q))] * 2
        args += [jnp.asarray(cos_np), jnp.asarray(sin_np)]
        out_specs = yb_spec
        out_shape = yb_shape
    else:
        kv_spec = pl.BlockSpec((nr, L, ATT_KV), lambda b: (b, 0, 0))
        kv_shape = jax.ShapeDtypeStruct((B, L, ATT_KV), F32)
        out_specs = [yb_spec, kv_spec, kv_spec]
        out_shape = [yb_shape, kv_shape, kv_shape]
    return pl.pallas_call(
        functools.partial(_attn_kernel, rope=rope),
        grid=(B // nr,),
        in_specs=in_specs,
        out_specs=out_specs,
        out_shape=out_shape,
        compiler_params=_params("parallel"),
        name="attention",
    )(*args)


def _merge_kernel(ya_ref, yb_ref, yc_ref, x_ref, mod_ref, nw1_ref, nw2_ref, *rest):
    gate_w_refs = rest[:GATE_COLS // ATT_COLS]
    wa_ref, wb_ref, wc_ref, wo_ref, xo_ref, h2_ref = rest[GATE_COLS // ATT_COLS:]
    tm = x_ref.shape[0]
    n_chain = 2
    rows_per = tm // n_chain

    def chain(c):
        rows = pl.ds(c * rows_per, rows_per)
        x = x_ref[rows, :]
        h = _rms_modulate(x, nw1_ref[...], mod_ref[0:1, :], mod_ref[1:2, :]).astype(BF16)
        yield
        gates = jnp.concatenate([_dot(h, w_ref[...]) for w_ref in gate_w_refs], axis=1)
        da = _dot(ya_ref[rows, :], wa_ref[...])
        db = _dot(yb_ref[rows, :], wb_ref[...])
        dc = _dot(yc_ref[rows, :], wc_ref[...])
        yield
        merged = (_sigmoid(gates[:, 0:D_MODEL]) * da + _sigmoid(gates[:, D_MODEL:2 * D_MODEL]) * db
                  + _sigmoid(gates[:, 2 * D_MODEL:]) * dc).astype(BF16)
        yield
        proj = _dot(merged, wo_ref[...])
        yield
        x = x + mod_ref[2:3, :] * proj
        xo_ref[rows, :] = x
        h2_ref[rows, :] = _rms_modulate(x, nw2_ref[...], mod_ref[3:4, :], mod_ref[4:5, :]).astype(BF16)
        yield

    chains = [chain(c) for c in range(n_chain)]
    n_stages = 5
    for tick in range(n_stages + n_chain - 1):
        for c, ch in enumerate(chains):
            if 0 <= tick - c < n_stages:
                next(ch)


def _merge(ya, yb, yc, x2d, mod, norm1_w, norm2_w, w_in_bf, wa, wb, wc, wo, layer, rows_per_req, tm=MERGE_ROWS):
    T = x2d.shape[0]
    n_req = mod.shape[0]
    req = (lambda i: (i * tm) // rows_per_req) if n_req > 1 else (lambda i: 0)
    br = pl.BlockSpec((tm, D_HYENA), lambda i: (i, 0))
    row = pl.BlockSpec((tm, D_MODEL), lambda i: (i, 0))
    gate_blk0 = MIX_COLS // ATT_COLS
    gate_w = [pl.BlockSpec((None, D_MODEL, ATT_COLS), lambda i, k=k: (layer, 0, gate_blk0 + k),
                           pipeline_mode=pl.Buffered(1)) for k in range(GATE_COLS // ATT_COLS)]
    return pl.pallas_call(
        _merge_kernel,
        grid=(T // tm,),
        in_specs=[
            br, br, br, row,
            pl.BlockSpec((None, 6, D_MODEL), lambda i: (req(i), 0, 0)),
            _const_spec((1, D_MODEL)), _const_spec((1, D_MODEL)),
            *gate_w,
            _layer_spec((D_HYENA, D_MODEL), layer), _layer_spec((ATT_Q, D_MODEL), layer),
            _layer_spec((D_SCONV, D_MODEL), layer), _layer_spec((D_MODEL, D_MODEL), layer),
        ],
        out_specs=[row, row],
        out_shape=[jax.ShapeDtypeStruct((T, D_MODEL), F32), jax.ShapeDtypeStruct((T, D_MODEL), BF16)],
        compiler_params=_params("parallel"),
        name="merge",
    )(ya, yb, yc, x2d, mod, norm1_w, norm2_w, *([w_in_bf] * len(gate_w)), wa, wb, wc, wo)


def _route_kernel(h_ref, rw_ref, tri_ref, aff_ref, slot_ref, *, cap):
    b = pl.program_id(0)
    n_req = h_ref.shape[0]
    for r in range(n_req):
        logits = lax.dot_general(rw_ref[...], h_ref[r], (((1,), (1,)), ((), ())), preferred_element_type=F32)
        ex = jnp.exp(logits - jnp.max(logits, axis=0, keepdims=True))
        first_row = pl.multiple_of((b * n_req + r) * N_EXPERTS, N_EXPERTS)
        aff_ref[pl.ds(first_row, N_EXPERTS), :] = ex / jnp.sum(ex, axis=0, keepdims=True)

    @pl.when(b == pl.num_programs(0) - 1)
    def _():
        aff = aff_ref[...]

        def count(mask):
            return jnp.sum(jnp.where(mask, 1.0, 0.0), axis=1, keepdims=True)

        kth = jnp.zeros((aff.shape[0], 1), jnp.int32)
        for bit in range(30, -1, -1):
            trial = kth | (1 << bit)
            enough = count(aff >= lax.bitcast_convert_type(trial, F32)) >= cap
            kth = jnp.where(enough, trial, kth)
        next_up = lax.bitcast_convert_type(kth + 1, F32)
        above = aff >= next_up
        tied = (aff >= lax.bitcast_convert_type(kth, F32)) & (aff < next_up)
        tri = tri_ref[...]
        tied_before = _dot(jnp.where(tied, 1.0, 0.0).astype(BF16), tri)
        chosen = above | (tied & (tied_before < (cap - count(above))))
        slot = _dot(jnp.where(chosen, 1.0, 0.0).astype(BF16), tri)
        slot_ref[...] = jnp.where(chosen, slot, -1.0)


def _route(h3d, router_wt):
    B, n_tok, _ = h3d.shape
    cap = CAPACITY_FACTOR * n_tok // N_EXPERTS
    tri = jnp.asarray(_prefix_table(n_tok)).astype(BF16)
    whole = pl.BlockSpec((B * N_EXPERTS, n_tok), lambda b: (0, 0))
    shape = jax.ShapeDtypeStruct((B * N_EXPERTS, n_tok), F32)
    nr = _requests_per_step(n_tok)
    return pl.pallas_call(
        functools.partial(_route_kernel, cap=cap),
        grid=(B // nr,),
        in_specs=[
            pl.BlockSpec((nr, n_tok, D_MODEL), lambda b: (b, 0, 0)),
            _const_spec((N_EXPERTS, D_MODEL)),
            _const_spec((n_tok, n_tok)),
        ],
        out_specs=[whole, whole],
        out_shape=[shape, shape],
        compiler_params=_params("arbitrary"),
        name="moe_route",
    )(h3d, router_wt, tri)


MOE_PARTS = 2


def _dispatch_kernel(h_ref, aff_ref, slot_ref, xs_ref, g_ref, d_sc, *, n_tok, cap):
    row = lax.broadcasted_iota(jnp.int32, (cap, n_tok), 0).astype(F32)
    per_part = N_EXPERTS // MOE_PARTS
    for r in range(h_ref.shape[0]):
        for part in range(MOE_PARTS):
            for e in range(part * per_part, (part + 1) * per_part):
                idx = r * N_EXPERTS + e
                hit = row == slot_ref[idx:idx + 1, :]
                d_sc[r, e * cap:(e + 1) * cap, :] = jnp.where(hit, 1.0, 0.0).astype(BF16)
                gate = jnp.sum(jnp.where(hit, aff_ref[idx:idx + 1, :], 0.0), axis=1, keepdims=True)
                g_ref[r, e * cap:(e + 1) * cap, :] = jnp.broadcast_to(gate, (cap, LANES))
            rows = slice(part * per_part * cap, (part + 1) * per_part * cap)
            xs_ref[r, rows, :] = _dot(d_sc[r, rows, :], h_ref[r]).astype(BF16)


def _requests_per_step(n_tok):
    return max(1, STEP_TOKENS // n_tok)


def _dispatch(h3d, router_wt):
    B, n_tok, _ = h3d.shape
    cap = CAPACITY_FACTOR * n_tok // N_EXPERTS
    rows = N_EXPERTS * cap
    aff, slot = _route(h3d, router_wt)
    nr = _requests_per_step(n_tok)
    per_req = pl.BlockSpec((nr * N_EXPERTS, n_tok), lambda b: (b, 0))
    blk = lambda w: pl.BlockSpec((nr, rows, w), lambda b: (b, 0, 0))
    xs, gates = pl.pallas_call(
        functools.partial(_dispatch_kernel, n_tok=n_tok, cap=cap),
        grid=(B // nr,),
        in_specs=[pl.BlockSpec((nr, n_tok, D_MODEL), lambda b: (b, 0, 0)), per_req, per_req],
        out_specs=[blk(D_MODEL), blk(LANES)],
        out_shape=[jax.ShapeDtypeStruct((B, rows, D_MODEL), BF16),
                   jax.ShapeDtypeStruct((B, rows, LANES), F32)],
        scratch_shapes=[pltpu.VMEM((nr, rows, n_tok), BF16)],
        compiler_params=_params("parallel"),
        name="moe_dispatch",
    )(h3d, aff, slot)
    return xs, gates, slot


def _ffn_kernel(xp_ref, gp_ref, xs_ref, gs_ref, wg_ref, wu_ref, wd_ref, yp_ref, ys_ref):
    wg = wg_ref[...].astype(BF16)
    wu = wu_ref[...].astype(BF16)
    wd = wd_ref[...].astype(BF16)
    for x_ref, g_ref, y_ref in ((xp_ref, gp_ref, yp_ref), (xs_ref, gs_ref, ys_ref)):
        nb, cap, _ = x_ref.shape
        x = x_ref[...].reshape(nb * cap, D_MODEL)
        gate = g_ref[...].reshape(nb * cap, LANES)
        gate = jnp.concatenate([gate] * (EXPERT_FF // LANES), axis=1)
        act = _silu(_dot(x, wg)) * _dot(x, wu) * gate
        y = _dot(act.astype(BF16), wd)
        y_ref[...] = y.reshape(nb, cap, D_MODEL).astype(y_ref.dtype)


def _expert_ffn(xp, gp, xs, gs, w_gate, w_up, w_down, layer):
    def act_spec(a):
        nb, _, cap, w = a.shape
        return pl.BlockSpec((nb, None, cap, w), lambda e: (0, e, 0, 0))

    def w_spec(a):
        return pl.BlockSpec((None, None) + a.shape[2:], lambda e: (layer, e, 0, 0))

    return pl.pallas_call(
        _ffn_kernel,
        grid=(N_EXPERTS,),
        in_specs=[act_spec(xp), act_spec(gp), act_spec(xs), act_spec(gs),
                  w_spec(w_gate), w_spec(w_up), w_spec(w_down)],
        out_specs=[act_spec(xp), act_spec(xs)],
        out_shape=[jax.ShapeDtypeStruct(xp.shape, BF16), jax.ShapeDtypeStruct(xs.shape, BF16)],
        compiler_params=_params("parallel"),
        name="expert_ffn",
    )(xp, gp, xs, gs, w_gate, w_up, w_down)


def _combine_kernel(slot_ref, y_ref, x_ref, mod_ref, o_ref, d_sc, *, n_tok, cap):
    row = lax.broadcasted_iota(jnp.int32, (cap, n_tok), 0).astype(F32)
    per_part = N_EXPERTS // MOE_PARTS
    for r in range(y_ref.shape[0]):
        moe = None
        for part in range(MOE_PARTS):
            for e in range(part * per_part, (part + 1) * per_part):
                idx = r * N_EXPERTS + e
                hit = row == slot_ref[idx:idx + 1, :]
                d_sc[r, e * cap:(e + 1) * cap, :] = jnp.where(hit, 1.0, 0.0).astype(BF16)
            rows = slice(part * per_part * cap, (part + 1) * per_part * cap)
            scattered = lax.dot_general(d_sc[r, rows, :], y_ref[r, rows, :], (((0,), (0,)), ((), ())),
                                        preferred_element_type=F32)
            moe = scattered if moe is None else moe + scattered
        o_ref[r] = x_ref[r] + mod_ref[min(r, mod_ref.shape[0] - 1)] * moe


def _combine(slot, y3d, x3d, mod_g2):
    B, n_tok, _ = x3d.shape
    rows = y3d.shape[1]
    cap = rows // N_EXPERTS
    nr = _requests_per_step(n_tok)
    shared_mod = mod_g2.shape[0] == 1
    mod_spec = (pl.BlockSpec((1, 1, D_MODEL), lambda b: (0, 0, 0)) if shared_mod
                else pl.BlockSpec((nr, 1, D_MODEL), lambda b: (b, 0, 0)))
    return pl.pallas_call(
        functools.partial(_combine_kernel, n_tok=n_tok, cap=cap),
        grid=(B // nr,),
        in_specs=[
            pl.BlockSpec((nr * N_EXPERTS, n_tok), lambda b: (b, 0)),
            pl.BlockSpec((nr, rows, D_MODEL), lambda b: (b, 0, 0)),
            pl.BlockSpec((nr, n_tok, D_MODEL), lambda b: (b, 0, 0)),
            mod_spec,
        ],
        out_specs=pl.BlockSpec((nr, n_tok, D_MODEL), lambda b: (b, 0, 0)),
        out_shape=jax.ShapeDtypeStruct(x3d.shape, F32),
        scratch_shapes=[pltpu.VMEM((nr, rows, n_tok), BF16)],
        compiler_params=_params("parallel"),
        name="moe_combine",
    )(slot, y3d, x3d, mod_g2)


def _token_mixers(x3d, mod, p, tabs, spectra, layer, ctx=None):
    B, L, _ = x3d.shape
    x2d = x3d.reshape(B * L, D_MODEL)
    hyc, yc, qkv = _inproj(x2d, mod, p['norm1_w'], p['w_in'], p['hy_short_w'], p['hy_short_b'], p['sc_w'],
                           p['sc_b'], layer, L)
    ya = _seqmix(hyc.reshape(B, L, HY_COLS), spectra, p['hy_bias'], tabs, layer)
    qkv3d = qkv.reshape(B, L, ATT_COLS)
    if ctx is None:
        yb, k, v = _attention(qkv3d, p['q_norm_w'], p['k_norm_w'])
    else:
        yb = _attention(qkv3d, p['q_norm_w'], p['k_norm_w'], ctx[0], ctx[1], layer)
        k = v = None
    x_mid, h2 = _merge(ya.reshape(B * L, D_HYENA), yb.reshape(B * L, ATT_Q), yc,
                       x2d, mod, p['norm1_w'], p['norm2_w'], p['w_in'], p['w_br_a'], p['w_br_b'], p['w_br_c'],
                       p['w_o'], layer, L)
    return x_mid.reshape(B, L, D_MODEL), h2.reshape(B, L, D_MODEL), k, v


def _moe_split(a, n_exp):
    B, rows, w = a.shape
    return a.reshape(B, n_exp, rows // n_exp, w)


def kernel(x_prompt, x_sample, cache_k, cache_v, c, c_ctx, mod_w, mod_b, norm1_w, norm2_w, w_in, hy_short_w, hy_short_b, hy_w1, hy_b1, hy_w2, hy_b2, hy_w3, hy_b3, hy_decay, hy_bias, q_norm_w, k_norm_w, sc_w, sc_b, w_br_a, w_br_b, w_br_c, w_o, router_w, exp_w_gate, exp_w_up, exp_w_down):
    n_dec = x_sample.shape[0]
    n_ctx = cache_k.shape[2]
    lp = x_prompt.shape[1]
    ls = x_sample.shape[1]

    cond_rows = COND_ROWS
    cond = jnp.concatenate([c_ctx[None, :], c, jnp.zeros((cond_rows - 1 - n_dec, D_MODEL), F32)], axis=0)
    mod = _modulation(cond, mod_w, mod_b).reshape(DEPTH, cond_rows, 6, D_MODEL)

    tabs_p = tuple(jnp.asarray(t).astype(BF16) for t in _split_dft_tables(lp))
    tabs_s = tuple(jnp.asarray(t).astype(BF16) for t in _split_dft_tables(ls))
    ctx_k = cache_k.reshape(n_dec, DEPTH, n_ctx, ATT_KV)
    ctx_v = cache_v.reshape(n_dec, DEPTH, n_ctx, ATT_KV)
    dense = {'w_in': w_in.astype(BF16), 'w_br_a': w_br_a.astype(BF16), 'w_br_b': w_br_b.astype(BF16),
             'w_br_c': w_br_c.astype(BF16), 'w_o': w_o.astype(BF16)}

    filt = (hy_w1, hy_b1, hy_w2, hy_b2, hy_w3, hy_b3, hy_decay)
    spectra_p = _hyena_spectra(lp, *filt, tabs_p[0], tabs_p[1])
    spectra_s = _hyena_spectra(ls, *filt, tabs_s[0], tabs_s[1])

    xp, xs = x_prompt, x_sample
    ks_new, vs_new = [], []
    for l in range(DEPTH):
        p = dict(dense)
        p.update({
            'norm1_w': norm1_w[l].reshape(1, D_MODEL), 'norm2_w': norm2_w[l].reshape(1, D_MODEL),
            'hy_short_w': hy_short_w[l], 'hy_short_b': hy_short_b[l],
            'hy_bias': hy_bias,
            'q_norm_w': q_norm_w[l], 'k_norm_w': k_norm_w[l], 'sc_w': sc_w[l], 'sc_b': sc_b[l],
        })
        mod_p = mod[l, 0:1]
        mod_s = mod[l, 1:1 + n_dec]
        xp_mid, h2p, k_l, v_l = _token_mixers(xp, mod_p, p, tabs_p, spectra_p, l)
        xs_mid, h2s, _, _ = _token_mixers(xs, mod_s, p, tabs_s, spectra_s, l, (ctx_k, ctx_v))
        ks_new.append(k_l.reshape(k_l.shape[0], lp, N_KV_HEADS, HEAD_DIM))
        vs_new.append(v_l.reshape(v_l.shape[0], lp, N_KV_HEADS, HEAD_DIM))

        router_wt = router_w[l].T.astype(BF16)
        gp_x, gp_g, gp_slot = _dispatch(h2p, router_wt)
        gs_x, gs_g, gs_slot = _dispatch(h2s, router_wt)
        yp, ys = _expert_ffn(_moe_split(gp_x, N_EXPERTS), _moe_split(gp_g, N_EXPERTS),
                             _moe_split(gs_x, N_EXPERTS), _moe_split(gs_g, N_EXPERTS),
                             exp_w_gate, exp_w_up, exp_w_down, l)
        xp = _combine(gp_slot, yp.reshape(gp_x.shape), xp_mid, mod_p[:, 5:6])
        xs = _combine(gs_slot, ys.reshape(gs_x.shape), xs_mid, mod_s[:, 5:6])

    return (xp, xs, jnp.stack(ks_new, axis=1), jnp.stack(vs_new, axis=1))
```

```python
import functools
import math

import numpy as np
import jax
import jax.numpy as jnp
from jax import lax
from jax.experimental import pallas as pl
from jax.experimental.pallas import tpu as pltpu

D_MODEL = 1024
DEPTH = 2
GRID_W = 64
D_HYENA = 512
HYENA_ORDER = 2
HYENA_POS_BANDS = 16
HYENA_POS_DIM = 1 + 2 * HYENA_POS_BANDS
HYENA_FILTER_HIDDEN = 64
HYENA_WINDOW_SHIFT = 0.05
N_HEADS = 8
N_KV_HEADS = 2
HEAD_DIM = 64
HEADS_PER_KV = N_HEADS // N_KV_HEADS
ATT_Q = N_HEADS * HEAD_DIM
ATT_KV = N_KV_HEADS * HEAD_DIM
ROPE_THETA = 10000.0
D_SCONV = 512
N_EXPERTS = 16
EXPERT_FF = 512
CAPACITY_FACTOR = 2
NORM_EPS = 1e-6
HY_COLS = (HYENA_ORDER + 1) * D_HYENA
ATT_COLS = ATT_Q + 2 * ATT_KV
SC_COLS = 3 * D_SCONV
GATE_COLS = 3 * D_MODEL
MIX_COLS = HY_COLS + ATT_COLS + SC_COLS
D_IN = MIX_COLS + GATE_COLS

F32 = jnp.float32
BF16 = jnp.bfloat16

V7X_VMEM_BYTES = 64 * 1024 * 1024
VMEM_LIMIT = V7X_VMEM_BYTES - 8 * 1024 * 1024
LANES = 128
MLP_PAD = 128
STEP_TOKENS = 1024
INPROJ_ROWS = 1024
MERGE_ROWS = 512
HYENA_CHANNEL_TILE = 256
MOD_COL_TILE = 2048
COND_ROWS = 16


def _params(*sem):
    return pltpu.CompilerParams(dimension_semantics=sem, vmem_limit_bytes=VMEM_LIMIT)


def _const_spec(shape):
    nd = len(shape)
    return pl.BlockSpec(shape, lambda *_: (0,) * nd, pipeline_mode=pl.Buffered(1))


def _layer_spec(shape, layer):
    nd = len(shape)
    return pl.BlockSpec((None,) + tuple(shape), lambda *_: (layer,) + (0,) * nd, pipeline_mode=pl.Buffered(1))


def _dot(a, b):
    return jnp.dot(a, b, preferred_element_type=F32)


def _silu(x):
    return x * (1.0 / (1.0 + jnp.exp(-x)))


def _sigmoid(x):
    return 1.0 / (1.0 + jnp.exp(-x))


@functools.lru_cache(maxsize=None)
def _split_dft_tables(L):
    h = L // 2
    k = np.arange(h, dtype=np.int64)
    alt = 1.0 - 2.0 * (k % 2)
    fwd, inv = [], []
    for odd in (0, 1):
        idx = (k[:, None] * (2 * k[None, :] + odd)) % (2 * L)
        ang = idx.astype(np.float64) * (np.pi / L)
        c = np.cos(ang)
        s = np.sin(ang)
        s[0, :] = alt
        fwd.append(np.concatenate([c, s], axis=0).astype(np.float32))
        inv.append(np.concatenate([c.T, s.T], axis=1).astype(np.float32))
    return fwd[0], fwd[1], inv[0], inv[1]


@functools.lru_cache(maxsize=None)
def _pos_features(L):
    n = np.arange(L, dtype=np.float64)
    t = n / max(L - 1, 1)
    bands = np.linspace(1e-4, HYENA_POS_BANDS - 1, HYENA_POS_BANDS)
    ang = 2.0 * math.pi * n[:, None] * bands[None, :] / L
    z = np.concatenate([t[:, None], np.cos(ang), np.sin(ang)], axis=-1)
    zp = np.zeros((L, MLP_PAD), np.float32)
    zp[:, :HYENA_POS_DIM] = z
    return zp


@functools.lru_cache(maxsize=None)
def _rope_tables(L):
    rows = L // GRID_W
    row = np.repeat(np.arange(rows, dtype=np.float64), GRID_W)
    col = np.tile(np.arange(GRID_W, dtype=np.float64), rows)
    axis_dim = HEAD_DIM // 2
    inv_freq = ROPE_THETA ** (-np.arange(0, axis_dim, 2, dtype=np.float64) / axis_dim)
    ar = row[:, None] * inv_freq[None, :]
    ac = col[:, None] * inv_freq[None, :]
    ang = np.concatenate([ar, ar, ac, ac], axis=-1)
    cos = np.cos(ang).astype(np.float32)
    sin = np.sin(ang).astype(np.float32)
    return np.tile(cos, (1, N_HEADS)), np.tile(sin, (1, N_HEADS))


@functools.lru_cache(maxsize=None)
def _head_tables():
    lane = np.arange(ATT_Q)
    block_ones = (lane[:, None] // HEAD_DIM == lane[None, :] // HEAD_DIM).astype(np.float32)
    src = np.arange(ATT_KV)
    dst = np.arange(HEADS_PER_KV * HEAD_DIM)
    rep = np.stack([(src[:, None] == g * HEAD_DIM + dst[None, :] % HEAD_DIM) for g in range(N_KV_HEADS)])
    return block_ones, rep.astype(np.float32)


@functools.lru_cache(maxsize=None)
def _prefix_table(n):
    i = np.arange(n)
    return (i[:, None] < i[None, :]).astype(np.float32)


def _mod_kernel(cond_ref, w_ref, b_ref, o_ref):
    a = _silu(cond_ref[...]).astype(BF16)
    o_ref[...] = _dot(a, w_ref[...].astype(BF16)) + b_ref[...]


def _modulation(cond, mod_w, mod_b):
    rows = cond.shape[0]
    tn = MOD_COL_TILE
    ncols = mod_w.shape[-1]
    return pl.pallas_call(
        _mod_kernel,
        grid=(DEPTH, ncols // tn),
        in_specs=[
            pl.BlockSpec((rows, D_MODEL), lambda l, j: (0, 0)),
            pl.BlockSpec((None, D_MODEL, tn), lambda l, j: (l, 0, j)),
            pl.BlockSpec((None, 1, tn), lambda l, j: (l, 0, j)),
        ],
        out_specs=pl.BlockSpec((None, rows, tn), lambda l, j: (l, 0, j)),
        out_shape=jax.ShapeDtypeStruct((DEPTH, rows, ncols), F32),
        compiler_params=_params("parallel", "parallel"),
        name="modulation",
    )(cond, mod_w, mod_b.reshape(DEPTH, 1, ncols))


def _rms_modulate(x, norm_w, shift, scale):
    y = x * lax.rsqrt(jnp.mean(x * x, axis=-1, keepdims=True) + NORM_EPS)
    return (y * norm_w) * (1.0 + scale) + shift


def _conv3(x, w, b, first, last):
    n = x.shape[0]
    prev = jnp.where(first, 0.0, pltpu.roll(x, 1, 0))
    nxt = jnp.where(last, 0.0, pltpu.roll(x, n - 1, 0))
    return prev * w[0:1, :] + x * w[1:2, :] + nxt * w[2:3, :] + b


def _inproj_kernel(x_ref, mod_ref, nw_ref, w_ref, hyw_ref, hyb_ref, scw_ref, scb_ref,
                   hyc_ref, yc_ref, qkv_ref, h_sc, *, L):
    h = _rms_modulate(x_ref[...], nw_ref[...], mod_ref[0:1, :], mod_ref[1:2, :])
    h_sc[...] = h.astype(BF16)
    tm = x_ref.shape[0]

    def edges(cols):
        pos = lax.broadcasted_iota(jnp.int32, (tm, cols), 0) % L
        return pos == 0, pos == L - 1

    sc0 = HY_COLS + ATT_COLS
    u, bgate, cgate = (_dot(h_sc[...], w_ref[:, sc0 + k * D_SCONV:sc0 + (k + 1) * D_SCONV]) for k in range(3))
    hy_chunks = [slice(lo, lo + ATT_COLS) for lo in range(0, HY_COLS, ATT_COLS)]
    y_next = _dot(h_sc[...], w_ref[:, hy_chunks[0]])
    first, last = edges(D_SCONV)
    yc_ref[...] = (bgate * _conv3(cgate * u, scw_ref[...], scb_ref[...], first, last)).astype(BF16)
    first, last = edges(ATT_COLS)
    for i, cols in enumerate(hy_chunks):
        y = y_next
        if i + 1 < len(hy_chunks):
            y_next = _dot(h_sc[...], w_ref[:, hy_chunks[i + 1]])
        else:
            qkv_ref[...] = _dot(h_sc[...], w_ref[:, HY_COLS:HY_COLS + ATT_COLS])
        hyc_ref[:, cols] = _conv3(y, hyw_ref[:, cols], hyb_ref[:, cols], first, last).astype(BF16)


def _two_group_kernel(tile_fn, n_group_in, n_shared, n_group_out, *refs, n_first, lens):
    a_in, refs = refs[:n_group_in], refs[n_group_in:]
    b_in, refs = refs[:n_group_in], refs[n_group_in:]
    shared, refs = refs[:n_shared], refs[n_shared:]
    a_out, refs = refs[:n_group_out], refs[n_group_out:]
    b_out, scratch = refs[:n_group_out], refs[n_group_out:]
    step = pl.program_id(0)

    @pl.when(step < n_first)
    def _():
        tile_fn(*a_in, *shared, *a_out, *scratch, lens[0])

    @pl.when(step >= n_first)
    def _():
        tile_fn(*b_in, *shared, *b_out, *scratch, lens[1])


def _group_specs(block, n_first):
    return (pl.BlockSpec(block, lambda i: (jnp.minimum(i, n_first - 1), 0)),
            pl.BlockSpec(block, lambda i: (jnp.maximum(i - n_first, 0), 0)))


def _group_mod_spec(n_first, tm, second_len):
    return pl.BlockSpec((None, 6, D_MODEL),
                        lambda i: (jnp.where(i < n_first, 0, 1 + (jnp.maximum(i - n_first, 0) * tm) // second_len), 0, 0))


def _inproj(x2d, mod, norm_w, w_in_bf, hy_short_w, hy_short_b, sc_w, sc_b, layer, rows_per_req, tm=INPROJ_ROWS):
    T = x2d.shape[0]
    n_req = mod.shape[0]
    req = (lambda i: (i * tm) // rows_per_req) if n_req > 1 else (lambda i: 0)
    out = lambda w: pl.BlockSpec((tm, w), lambda i: (i, 0))
    return pl.pallas_call(
        functools.partial(_inproj_kernel, L=rows_per_req),
        grid=(T // tm,),
        in_specs=[
            pl.BlockSpec((tm, D_MODEL), lambda i: (i, 0)),
            pl.BlockSpec((None, 6, D_MODEL), lambda i: (req(i), 0, 0)),
            _const_spec((1, D_MODEL)),
            _layer_spec((D_MODEL, MIX_COLS), layer),
            _const_spec((3, HY_COLS)), _const_spec((1, HY_COLS)),
            _const_spec((3, D_SCONV)), _const_spec((1, D_SCONV)),
        ],
        out_specs=[out(HY_COLS), out(D_SCONV), out(ATT_COLS)],
        out_shape=[jax.ShapeDtypeStruct((T, HY_COLS), BF16), jax.ShapeDtypeStruct((T, D_SCONV), BF16),
                   jax.ShapeDtypeStruct((T, ATT_COLS), F32)],
        scratch_shapes=[pltpu.VMEM((tm, D_MODEL), BF16)],
        compiler_params=_params("parallel"),
        name="inproj",
    )(x2d, mod, norm_w, w_in_bf, hy_short_w, hy_short_b.reshape(1, HY_COLS), sc_w, sc_b.reshape(1, D_SCONV))


def _split_parity(par_sc, z):
    half = z.shape[0] // 2
    n_lt = z.shape[1] // LANES
    for c in range(n_lt):
        par_sc[c] = z[:, c * LANES:(c + 1) * LANES]
    return tuple(jnp.concatenate([par_sc[c, pl.ds(par, half, stride=2), :] for c in range(n_lt)], axis=1)
                 for par in (0, 1))


def _join_parity(par_sc, even, odd):
    half = even.shape[0]
    n_lt = even.shape[1] // LANES
    for c in range(n_lt):
        par_sc[c, pl.ds(0, half, stride=2), :] = even[:, c * LANES:(c + 1) * LANES]
        par_sc[c, pl.ds(1, half, stride=2), :] = odd[:, c * LANES:(c + 1) * LANES]
    return jnp.concatenate([par_sc[c] for c in range(n_lt)], axis=1)


def _fold_spectrum(te, to):
    half = te.shape[0] // 2
    row0 = lax.broadcasted_iota(jnp.int32, (half, te.shape[1]), 0) == 0
    ce, se, co, so = te[:half], te[half:], to[:half], to[half:]
    a = jnp.concatenate([ce + co, jnp.where(row0, se, ce - co)], axis=0)
    b = jnp.concatenate([jnp.where(row0, ce - co, se + so), jnp.where(row0, so, so - se)], axis=0)
    return a, b


def _unfold_spectrum(yr, yi):
    half = yr.shape[0] // 2
    row0 = lax.broadcasted_iota(jnp.int32, (half, yr.shape[1]), 0) == 0
    yr_lo, yr_up, yi_lo, yi_up = yr[:half], yr[half:], yi[:half], yi[half:]
    even = jnp.concatenate([jnp.where(row0, yr_lo + yi_lo, yr_lo + yr_up),
                            jnp.where(row0, yr_up, yi_lo - yi_up)], axis=0)
    odd = jnp.concatenate([jnp.where(row0, yr_lo - yi_lo, yr_lo - yr_up),
                           jnp.where(row0, yi_up, yi_lo + yi_up)], axis=0)
    return even, odd


def _filter_kernel(z_ref, w1_ref, b1_ref, w2_ref, b2_ref, w3f_ref, b3f_ref, w3b_ref, b3b_ref,
                   decf_ref, decb_ref, fwde_ref, fwdo_ref, p_ref, h_sc, par_sc, *, L):
    hi = lax.Precision.HIGHEST

    @pl.when((pl.program_id(1) == 0) & (pl.program_id(2) == 0))
    def _():
        h1 = jnp.sin(jnp.dot(z_ref[...], w1_ref[...], precision=hi, preferred_element_type=F32) + b1_ref[...])
        h_sc[...] = jnp.sin(jnp.dot(h1, w2_ref[...], precision=hi, preferred_element_type=F32) + b2_ref[...])

    h = h_sc[...]
    tc = w3f_ref.shape[1]
    pos = lax.broadcasted_iota(jnp.int32, (L, tc), 0)
    t = pos.astype(F32) / float(max(L - 1, 1))

    def taps(w3_ref, b3_ref, dec_ref):
        g = jnp.dot(h, w3_ref[...], precision=hi, preferred_element_type=F32) + b3_ref[...]
        return g * (jnp.exp(-t * jnp.abs(dec_ref[...])) + HYENA_WINDOW_SHIFT)

    hf = taps(w3f_ref, b3f_ref, decf_ref)
    hb = jnp.where(pos == 0, 0.0, taps(w3b_ref, b3b_ref, decb_ref))

    def transform(taps_lc):
        even, odd = _split_parity(par_sc, taps_lc)
        return _fold_spectrum(_dot(fwde_ref[...], even.astype(BF16)), _dot(fwdo_ref[...], odd.astype(BF16)))

    fa, fb = transform(hf)
    ba, bb = transform(hb)
    k_re = fa + ba
    k_im = bb - fb
    k_ny = fb + bb
    first = pos == 0
    inv_n = 1.0 / (2 * L)
    p_ref[0] = jnp.where(first, k_re * inv_n, k_re * (2.0 * inv_n))
    p_ref[1] = jnp.where(first, 0.0, k_im * (2.0 * inv_n))
    p_ref[2] = jnp.where(first, k_ny * inv_n, k_re * (2.0 * inv_n))


def _hyena_spectra(L, w1, b1, w2, b2, w3, b3, decay, fwd_even, fwd_odd):
    pad_h = MLP_PAD - HYENA_FILTER_HIDDEN
    z = jnp.asarray(_pos_features(L))
    w1p = jnp.pad(w1, ((0, 0), (0, MLP_PAD - HYENA_POS_DIM), (0, pad_h)))
    b1p = jnp.pad(b1, ((0, 0), (0, pad_h))).reshape(DEPTH, 1, MLP_PAD)
    w2p = jnp.pad(w2, ((0, 0), (0, pad_h), (0, pad_h)))
    b2p = jnp.pad(b2, ((0, 0), (0, pad_h))).reshape(DEPTH, 1, MLP_PAD)
    w3p = jnp.pad(w3, ((0, 0), (0, pad_h), (0, 0)))
    ncol = w3.shape[-1]
    b3r = b3.reshape(DEPTH, 1, ncol)
    decr = decay.reshape(DEPTH, 1, ncol)
    tc = HYENA_CHANNEL_TILE
    nct = D_HYENA // tc
    per_dir = HYENA_ORDER * nct
    col_f = lambda l, o, c: (l, 0, o * nct + c)
    col_b = lambda l, o, c: (l, 0, per_dir + o * nct + c)
    per_layer = lambda shape: pl.BlockSpec((None,) + shape, lambda l, o, c: (l, 0, 0))
    return pl.pallas_call(
        functools.partial(_filter_kernel, L=L),
        grid=(DEPTH, HYENA_ORDER, nct),
        in_specs=[
            _const_spec((L, MLP_PAD)),
            per_layer((MLP_PAD, MLP_PAD)), per_layer((1, MLP_PAD)),
            per_layer((MLP_PAD, MLP_PAD)), per_layer((1, MLP_PAD)),
            pl.BlockSpec((None, MLP_PAD, tc), col_f), pl.BlockSpec((None, 1, tc), col_f),
            pl.BlockSpec((None, MLP_PAD, tc), col_b), pl.BlockSpec((None, 1, tc), col_b),
            pl.BlockSpec((None, 1, tc), col_f), pl.BlockSpec((None, 1, tc), col_b),
            _const_spec((L, L // 2)), _const_spec((L, L // 2)),
        ],
        out_specs=pl.BlockSpec((None, None, 3, L, tc), lambda l, o, c: (l, o, 0, 0, c)),
        out_shape=jax.ShapeDtypeStruct((DEPTH, HYENA_ORDER, 3, L, D_HYENA), F32),
        scratch_shapes=[pltpu.VMEM((L, MLP_PAD), F32), pltpu.VMEM((tc // LANES, L, LANES), F32)],
        compiler_params=_params("arbitrary", "arbitrary", "arbitrary"),
        name="hyena_spectra",
    )(z, w1p, b1p, w2p, b2p, w3p, b3r, w3p, b3r, decr, decr, fwd_even, fwd_odd)


def _seqmix_kernel(v_ref, x1_ref, x2_ref, p_ref, hb_ref, fwde_ref, fwdo_ref, inve_ref, invo_ref, ya_ref,
                   spec_sc, par_sc):
    n_req = v_ref.shape[0]
    mult_refs = (x1_ref, x2_ref)

    def chain(r):
        z = v_ref[r].astype(F32)
        z_even, z_odd = _split_parity(par_sc.at[r], z)
        yield
        for o in range(HYENA_ORDER):
            te = _dot(fwde_ref[...], z_even.astype(BF16))
            to = _dot(fwdo_ref[...], z_odd.astype(BF16))
            yield
            a, b = _fold_spectrum(te, to)
            p2 = p_ref[o, 1]
            even, odd = _unfold_spectrum(a * p_ref[o, 0] + b * p2, b * p_ref[o, 2] - a * p2)
            spec_sc[r, 0] = even.astype(BF16)
            spec_sc[r, 1] = odd.astype(BF16)
            yield
            y_even = _dot(inve_ref[...], spec_sc[r, 0])
            y_odd = _dot(invo_ref[...], spec_sc[r, 1])
            yield
            z = mult_refs[o][r].astype(F32) * (_join_parity(par_sc.at[r], y_even, y_odd) + hb_ref[o:o + 1, :] * z)
            if o < HYENA_ORDER - 1:
                z_even, z_odd = _split_parity(par_sc.at[r], z)
            else:
                ya_ref[r] = z.astype(ya_ref.dtype)
            yield

    chains = [chain(r) for r in range(n_req)]
    n_stages = 1 + 4 * HYENA_ORDER
    for tick in range(n_stages + n_req - 1):
        for r, c in enumerate(chains):
            if 0 <= tick - r < n_stages:
                next(c)


def _seqmix(hyc3d, spectra, hy_bias, tabs, layer):
    B, L, _ = hyc3d.shape
    tc = HYENA_CHANNEL_TILE
    nct = D_HYENA // tc
    nr = max(2, STEP_TOKENS // L)
    act = lambda off: pl.BlockSpec((nr, L, tc), lambda c, b, off=off: (b, 0, off * nct + c))
    return pl.pallas_call(
        _seqmix_kernel,
        grid=(nct, B // nr),
        in_specs=[
            act(0), act(1), act(2),
            pl.BlockSpec((None, HYENA_ORDER, 3, L, tc), lambda c, b: (layer, 0, 0, 0, c),
                         pipeline_mode=pl.Buffered(1)),
            pl.BlockSpec((None, HYENA_ORDER, tc), lambda c, b: (layer, 0, c)),
            _const_spec((L, L // 2)), _const_spec((L, L // 2)), _const_spec((L // 2, L)), _const_spec((L // 2, L)),
        ],
        out_specs=pl.BlockSpec((nr, L, tc), lambda c, b: (b, 0, c)),
        out_shape=jax.ShapeDtypeStruct((B, L, D_HYENA), BF16),
        scratch_shapes=[pltpu.VMEM((nr, 2, L, tc), BF16), pltpu.VMEM((nr, tc // LANES, L, LANES), F32)],
        compiler_params=_params("arbitrary", "arbitrary"),
        name="seqmix",
    )(hyc3d, hyc3d, hyc3d, spectra, hy_bias, *tabs)


def _head_rms(x, ones_ref, w):
    sq = x * x
    hi = sq.astype(BF16)
    lo = (sq - hi.astype(F32)).astype(BF16)
    ones = ones_ref[...]
    width = x.shape[1]
    ss = _dot(hi, ones[:width, :width]) + _dot(lo, ones[:width, :width])
    return x * lax.rsqrt(ss * (1.0 / HEAD_DIM) + NORM_EPS) * w


def _rope(x, cos, sin):
    width = x.shape[1]
    lane = lax.broadcasted_iota(jnp.int32, x.shape, 1)
    half = HEAD_DIM // 4
    low = (lane % (2 * half)) < half
    rot = jnp.where(low, -pltpu.roll(x, width - half, 1), pltpu.roll(x, half, 1))
    return x * cos + rot * sin


def _attn_kernel(*refs, rope):
    if rope:
        (q_ref, k_ref, v_ref, ck_ref, cv_ref, qw_ref, kw_ref, ones_ref, rep_ref, rept_ref,
         cos_ref, sin_ref, yb_ref) = refs
    else:
        q_ref, k_ref, v_ref, qw_ref, kw_ref, ones_ref, rep_ref, rept_ref, yb_ref, ko_ref, vo_ref = refs
    n_req, L, _ = q_ref.shape
    gw = HEADS_PER_KV * HEAD_DIM
    lane = lax.broadcasted_iota(jnp.int32, (L, gw), 1)
    for r in range(n_req):
        kn = _head_rms(k_ref[r], ones_ref, kw_ref[...])
        v = v_ref[r]
        q = _head_rms(q_ref[r], ones_ref, qw_ref[...])
        if rope:
            keys = _rope(kn, cos_ref[:, :ATT_KV], sin_ref[:, :ATT_KV])
            keys = jnp.concatenate([ck_ref[r], keys], axis=0)
            vals = jnp.concatenate([cv_ref[r], v], axis=0)
            q = _rope(q, cos_ref[...], sin_ref[...])
        else:
            ko_ref[r] = kn
            vo_ref[r] = v
            keys, vals = kn, v
        keys = keys.astype(BF16)
        vals = vals.astype(BF16)
        q = q * (HEAD_DIM ** -0.5)
        for g in range(N_KV_HEADS):
            k4 = lax.dot_general(rept_ref[g], keys, (((1,), (1,)), ((), ())), preferred_element_type=F32).astype(BF16)
            v4 = _dot(vals, rep_ref[g]).astype(BF16)
            qg = q[:, g * gw:(g + 1) * gw]
            acc = jnp.zeros((L, gw), F32)
            for h in range(HEADS_PER_KV):
                mine = (lane // HEAD_DIM) == h
                qm = jnp.where(mine, qg, 0.0).astype(BF16)
                s = _dot(qm, k4)
                p = jnp.exp(s - jnp.max(s, axis=-1, keepdims=True))
                denom = jnp.sum(p, axis=-1, keepdims=True)
                o4 = _dot(p.astype(BF16), v4)
                acc = jnp.where(mine, o4 * (1.0 / denom), acc)
            yb_ref[r, :, g * gw:(g + 1) * gw] = acc.astype(yb_ref.dtype)


def _attention(qkv3d, q_norm_w, k_norm_w, ctx_k=None, ctx_v=None, layer=0):
    B, L, _ = qkv3d.shape
    rope = ctx_k is not None
    nr = _requests_per_step(L)
    ones_np, rep_np = _head_tables()
    ones = jnp.asarray(ones_np).astype(BF16)
    rep = jnp.asarray(rep_np).astype(BF16)
    rept = jnp.asarray(np.swapaxes(rep_np, 1, 2)).astype(BF16)
    qw = jnp.tile(q_norm_w, N_HEADS).reshape(1, ATT_Q)
    kw = jnp.tile(k_norm_w, N_KV_HEADS).reshape(1, ATT_KV)
    kblk = ATT_Q // ATT_KV
    in_specs = [
        pl.BlockSpec((nr, L, ATT_Q), lambda b: (b, 0, 0)),
        pl.BlockSpec((nr, L, ATT_KV), lambda b: (b, 0, kblk)),
        pl.BlockSpec((nr, L, ATT_KV), lambda b: (b, 0, kblk + 1)),
    ]
    args = [qkv3d, qkv3d, qkv3d]
    if rope:
        n_ctx = ctx_k.shape[2]
        in_specs += [pl.BlockSpec((nr, None, n_ctx, ATT_KV), lambda b: (b, layer, 0, 0))] * 2
        args += [ctx_k, ctx_v]
    in_specs += [_const_spec((1, ATT_Q)), _const_spec((1, ATT_KV)), _const_spec((ATT_Q, ATT_Q)),
                 _const_spec((N_KV_HEADS, ATT_KV, HEADS_PER_KV * HEAD_DIM)),
                 _const_spec((N_KV_HEADS, HEADS_PER_KV * HEAD_DIM, ATT_KV))]
    args += [qw, kw, ones, rep, rept]
    yb_shape = jax.ShapeDtypeStruct((B, L, ATT_Q), BF16)
    yb_spec = pl.BlockSpec((nr, L, ATT_Q), lambda b: (b, 0, 0))
    if rope:
        cos_np, sin_np = _rope_tables(L)
        in_specs += [_const_spec((L, ATT_Q))] * 2
        args += [jnp.asarray(cos_np), jnp.asarray(sin_np)]
        out_specs = yb_spec
        out_shape = yb_shape
    else:
        kv_spec = pl.BlockSpec((nr, L, ATT_KV), lambda b: (b, 0, 0))
        kv_shape = jax.ShapeDtypeStruct((B, L, ATT_KV), F32)
        out_specs = [yb_spec, kv_spec, kv_spec]
        out_shape = [yb_shape, kv_shape, kv_shape]
    return pl.pallas_call(
        functools.partial(_attn_kernel, rope=rope),
        grid=(B // nr,),
        in_specs=in_specs,
        out_specs=out_specs,
        out_shape=out_shape,
        compiler_params=_params("parallel"),
        name="attention",
    )(*args)


def _merge_tile(ya_ref, yb_ref, yc_ref, x_ref, mod_ref, nw1_ref, nw2_ref, *rest):
    gate_w_refs = rest[:GATE_COLS // ATT_COLS]
    wa_ref, wb_ref, wc_ref, wo_ref, xo_ref, h2_ref, _ = rest[GATE_COLS // ATT_COLS:]
    tm = x_ref.shape[0]
    n_chain = 2
    rows_per = tm // n_chain

    def chain(c):
        rows = pl.ds(c * rows_per, rows_per)
        x = x_ref[rows, :]
        h = _rms_modulate(x, nw1_ref[...], mod_ref[0:1, :], mod_ref[1:2, :]).astype(BF16)
        yield
        gates = jnp.concatenate([_dot(h, w_ref[...]) for w_ref in gate_w_refs], axis=1)
        da = _dot(ya_ref[rows, :], wa_ref[...])
        db = _dot(yb_ref[rows, :], wb_ref[...])
        dc = _dot(yc_ref[rows, :], wc_ref[...])
        yield
        merged = (_sigmoid(gates[:, 0:D_MODEL]) * da + _sigmoid(gates[:, D_MODEL:2 * D_MODEL]) * db
                  + _sigmoid(gates[:, 2 * D_MODEL:]) * dc).astype(BF16)
        yield
        proj = _dot(merged, wo_ref[...])
        yield
        x = x + mod_ref[2:3, :] * proj
        xo_ref[rows, :] = x
        h2_ref[rows, :] = _rms_modulate(x, nw2_ref[...], mod_ref[3:4, :], mod_ref[4:5, :]).astype(BF16)
        yield

    chains = [chain(c) for c in range(n_chain)]
    n_stages = 5
    for tick in range(n_stages + n_chain - 1):
        for c, ch in enumerate(chains):
            if 0 <= tick - c < n_stages:
                next(ch)


def _merge(branches_p, xp2d, branches_s, xs2d, mod_l, norm1_w, norm2_w, w_in_bf, wa, wb, wc, wo, layer, lens,
           tm=MERGE_ROWS):
    n_p, n_s = xp2d.shape[0] // tm, xs2d.shape[0] // tm
    br_p, br_s = _group_specs((tm, D_HYENA), n_p)
    row_p, row_s = _group_specs((tm, D_MODEL), n_p)
    n_gate = GATE_COLS // ATT_COLS
    gate_blk0 = MIX_COLS // ATT_COLS
    gate_w = [pl.BlockSpec((None, D_MODEL, ATT_COLS), lambda i, k=k: (layer, 0, gate_blk0 + k),
                           pipeline_mode=pl.Buffered(1)) for k in range(n_gate)]
    shapes = lambda t: [jax.ShapeDtypeStruct((t, D_MODEL), F32), jax.ShapeDtypeStruct((t, D_MODEL), BF16)]
    res = pl.pallas_call(
        functools.partial(_two_group_kernel, _merge_tile, 4, 7 + n_gate, 2, n_first=n_p, lens=lens),
        grid=(n_p + n_s,),
        in_specs=[
            br_p, br_p, br_p, row_p, br_s, br_s, br_s, row_s,
            _group_mod_spec(n_p, tm, lens[1]),
            _const_spec((1, D_MODEL)), _const_spec((1, D_MODEL)),
            *gate_w,
            _layer_spec((D_HYENA, D_MODEL), layer), _layer_spec((ATT_Q, D_MODEL), layer),
            _layer_spec((D_SCONV, D_MODEL), layer), _layer_spec((D_MODEL, D_MODEL), layer),
        ],
        out_specs=[row_p, row_p, row_s, row_s],
        out_shape=shapes(xp2d.shape[0]) + shapes(xs2d.shape[0]),
        compiler_params=_params("arbitrary"),
        name="merge",
    )(*branches_p, xp2d, *branches_s, xs2d, mod_l, norm1_w, norm2_w, *([w_in_bf] * n_gate), wa, wb, wc, wo)
    return res[:2], res[2:]


def _route_kernel(h_ref, rw_ref, tri_ref, aff_ref, slot_ref, *, cap):
    b = pl.program_id(0)
    n_req = h_ref.shape[0]
    for r in range(n_req):
        logits = lax.dot_general(rw_ref[...], h_ref[r], (((1,), (1,)), ((), ())), preferred_element_type=F32)
        ex = jnp.exp(logits - jnp.max(logits, axis=0, keepdims=True))
        first_row = pl.multiple_of((b * n_req + r) * N_EXPERTS, N_EXPERTS)
        aff_ref[pl.ds(first_row, N_EXPERTS), :] = ex / jnp.sum(ex, axis=0, keepdims=True)

    @pl.when(b == pl.num_programs(0) - 1)
    def _():
        aff = aff_ref[...]

        def count(mask):
            return jnp.sum(jnp.where(mask, 1.0, 0.0), axis=1, keepdims=True)

        kth = jnp.zeros((aff.shape[0], 1), jnp.int32)
        for bit in range(30, -1, -1):
            trial = kth | (1 << bit)
            enough = count(aff >= lax.bitcast_convert_type(trial, F32)) >= cap
            kth = jnp.where(enough, trial, kth)
        next_up = lax.bitcast_convert_type(kth + 1, F32)
        above = aff >= next_up
        tied = (aff >= lax.bitcast_convert_type(kth, F32)) & (aff < next_up)
        tri = tri_ref[...]
        tied_before = _dot(jnp.where(tied, 1.0, 0.0).astype(BF16), tri)
        chosen = above | (tied & (tied_before < (cap - count(above))))
        slot = _dot(jnp.where(chosen, 1.0, 0.0).astype(BF16), tri)
        slot_ref[...] = jnp.where(chosen, slot, -1.0)


def _route(h3d, router_wt):
    B, n_tok, _ = h3d.shape
    cap = CAPACITY_FACTOR * n_tok // N_EXPERTS
    tri = jnp.asarray(_prefix_table(n_tok)).astype(BF16)
    whole = pl.BlockSpec((B * N_EXPERTS, n_tok), lambda b: (0, 0))
    shape = jax.ShapeDtypeStruct((B * N_EXPERTS, n_tok), F32)
    nr = _requests_per_step(n_tok)
    return pl.pallas_call(
        functools.partial(_route_kernel, cap=cap),
        grid=(B // nr,),
        in_specs=[
            pl.BlockSpec((nr, n_tok, D_MODEL), lambda b: (b, 0, 0)),
            _const_spec((N_EXPERTS, D_MODEL)),
            _const_spec((n_tok, n_tok)),
        ],
        out_specs=[whole, whole],
        out_shape=[shape, shape],
        compiler_params=_params("arbitrary"),
        name="moe_route",
    )(h3d, router_wt, tri)


MOE_PARTS = 2


def _dispatch_kernel(h_ref, aff_ref, slot_ref, xs_ref, g_ref, d_sc, *, n_tok, cap):
    row = lax.broadcasted_iota(jnp.int32, (cap, n_tok), 0).astype(F32)
    per_part = N_EXPERTS // MOE_PARTS
    for r in range(h_ref.shape[0]):
        for part in range(MOE_PARTS):
            for e in range(part * per_part, (part + 1) * per_part):
                idx = r * N_EXPERTS + e
                hit = row == slot_ref[idx:idx + 1, :]
                d_sc[r, e * cap:(e + 1) * cap, :] = jnp.where(hit, 1.0, 0.0).astype(BF16)
                gate = jnp.sum(jnp.where(hit, aff_ref[idx:idx + 1, :], 0.0), axis=1, keepdims=True)
                g_ref[r, e * cap:(e + 1) * cap, :] = jnp.broadcast_to(gate, (cap, LANES))
            rows = slice(part * per_part * cap, (part + 1) * per_part * cap)
            xs_ref[r, rows, :] = _dot(d_sc[r, rows, :], h_ref[r]).astype(BF16)


def _requests_per_step(n_tok):
    return max(1, STEP_TOKENS // n_tok)


def _dispatch(h3d, router_wt):
    B, n_tok, _ = h3d.shape
    cap = CAPACITY_FACTOR * n_tok // N_EXPERTS
    rows = N_EXPERTS * cap
    aff, slot = _route(h3d, router_wt)
    nr = _requests_per_step(n_tok)
    per_req = pl.BlockSpec((nr * N_EXPERTS, n_tok), lambda b: (b, 0))
    blk = lambda w: pl.BlockSpec((nr, rows, w), lambda b: (b, 0, 0))
    xs, gates = pl.pallas_call(
        functools.partial(_dispatch_kernel, n_tok=n_tok, cap=cap),
        grid=(B // nr,),
        in_specs=[pl.BlockSpec((nr, n_tok, D_MODEL), lambda b: (b, 0, 0)), per_req, per_req],
        out_specs=[blk(D_MODEL), blk(LANES)],
        out_shape=[jax.ShapeDtypeStruct((B, rows, D_MODEL), BF16),
                   jax.ShapeDtypeStruct((B, rows, LANES), F32)],
        scratch_shapes=[pltpu.VMEM((nr, rows, n_tok), BF16)],
        compiler_params=_params("parallel"),
        name="moe_dispatch",
    )(h3d, aff, slot)
    return xs, gates, slot


def _ffn_kernel(xp_ref, gp_ref, xs_ref, gs_ref, wg_ref, wu_ref, wd_ref, yp_ref, ys_ref):
    wg = wg_ref[...].astype(BF16)
    wu = wu_ref[...].astype(BF16)
    wd = wd_ref[...].astype(BF16)
    for x_ref, g_ref, y_ref in ((xp_ref, gp_ref, yp_ref), (xs_ref, gs_ref, ys_ref)):
        nb, cap, _ = x_ref.shape
        x = x_ref[...].reshape(nb * cap, D_MODEL)
        gate = g_ref[...].reshape(nb * cap, LANES)
        gate = jnp.concatenate([gate] * (EXPERT_FF // LANES), axis=1)
        act = _silu(_dot(x, wg)) * _dot(x, wu) * gate
        y = _dot(act.astype(BF16), wd)
        y_ref[...] = y.reshape(nb, cap, D_MODEL).astype(y_ref.dtype)


def _expert_ffn(xp, gp, xs, gs, w_gate, w_up, w_down, layer):
    def act_spec(a):
        nb, _, cap, w = a.shape
        return pl.BlockSpec((nb, None, cap, w), lambda e: (0, e, 0, 0))

    def w_spec(a):
        return pl.BlockSpec((None, None) + a.shape[2:], lambda e: (layer, e, 0, 0))

    return pl.pallas_call(
        _ffn_kernel,
        grid=(N_EXPERTS,),
        in_specs=[act_spec(xp), act_spec(gp), act_spec(xs), act_spec(gs),
                  w_spec(w_gate), w_spec(w_up), w_spec(w_down)],
        out_specs=[act_spec(xp), act_spec(xs)],
        out_shape=[jax.ShapeDtypeStruct(xp.shape, BF16), jax.ShapeDtypeStruct(xs.shape, BF16)],
        compiler_params=_params("parallel"),
        name="expert_ffn",
    )(xp, gp, xs, gs, w_gate, w_up, w_down)


def _combine_kernel(slot_ref, y_ref, x_ref, mod_ref, o_ref, d_sc, *, n_tok, cap):
    row = lax.broadcasted_iota(jnp.int32, (cap, n_tok), 0).astype(F32)
    per_part = N_EXPERTS // MOE_PARTS
    for r in range(y_ref.shape[0]):
        moe = None
        for part in range(MOE_PARTS):
            for e in range(part * per_part, (part + 1) * per_part):
                idx = r * N_EXPERTS + e
                hit = row == slot_ref[idx:idx + 1, :]
                d_sc[r, e * cap:(e + 1) * cap, :] = jnp.where(hit, 1.0, 0.0).astype(BF16)
            rows = slice(part * per_part * cap, (part + 1) * per_part * cap)
            scattered = lax.dot_general(d_sc[r, rows, :], y_ref[r, rows, :], (((0,), (0,)), ((), ())),
                                        preferred_element_type=F32)
            moe = scattered if moe is None else moe + scattered
        o_ref[r] = x_ref[r] + mod_ref[min(r, mod_ref.shape[0] - 1)] * moe


def _combine(slot, y3d, x3d, mod_g2):
    B, n_tok, _ = x3d.shape
    rows = y3d.shape[1]
    cap = rows // N_EXPERTS
    nr = _requests_per_step(n_tok)
    shared_mod = mod_g2.shape[0] == 1
    mod_spec = (pl.BlockSpec((1, 1, D_MODEL), lambda b: (0, 0, 0)) if shared_mod
                else pl.BlockSpec((nr, 1, D_MODEL), lambda b: (b, 0, 0)))
    return pl.pallas_call(
        functools.partial(_combine_kernel, n_tok=n_tok, cap=cap),
        grid=(B // nr,),
        in_specs=[
            pl.BlockSpec((nr * N_EXPERTS, n_tok), lambda b: (b, 0)),
            pl.BlockSpec((nr, rows, D_MODEL), lambda b: (b, 0, 0)),
            pl.BlockSpec((nr, n_tok, D_MODEL), lambda b: (b, 0, 0)),
            mod_spec,
        ],
        out_specs=pl.BlockSpec((nr, n_tok, D_MODEL), lambda b: (b, 0, 0)),
        out_shape=jax.ShapeDtypeStruct(x3d.shape, F32),
        scratch_shapes=[pltpu.VMEM((nr, rows, n_tok), BF16)],
        compiler_params=_params("parallel"),
        name="moe_combine",
    )(slot, y3d, x3d, mod_g2)


def _token_mixers(xp, xs, mod_l, p, tabs, spectra, layer, ctx):
    (bp, lp, _), (bs, ls, _) = xp.shape, xs.shape
    xp2d, xs2d = xp.reshape(bp * lp, D_MODEL), xs.reshape(bs * ls, D_MODEL)
    conv = (p['hy_short_w'], p['hy_short_b'], p['sc_w'], p['sc_b'])
    hyc_p, yc_p, qkv_p = _inproj(xp2d, mod_l[0:1], p['norm1_w'], p['w_in'], *conv, layer, lp)
    hyc_s, yc_s, qkv_s = _inproj(xs2d, mod_l[1:1 + bs], p['norm1_w'], p['w_in'], *conv, layer, ls)
    ya_p = _seqmix(hyc_p.reshape(bp, lp, HY_COLS), spectra[0], p['hy_bias'], tabs[0], layer)
    ya_s = _seqmix(hyc_s.reshape(bs, ls, HY_COLS), spectra[1], p['hy_bias'], tabs[1], layer)
    yb_p, k, v = _attention(qkv_p.reshape(bp, lp, ATT_COLS), p['q_norm_w'], p['k_norm_w'])
    yb_s = _attention(qkv_s.reshape(bs, ls, ATT_COLS), p['q_norm_w'], p['k_norm_w'], ctx[0], ctx[1], layer)
    (xp_mid, h2p), (xs_mid, h2s) = _merge(
        (ya_p.reshape(bp * lp, D_HYENA), yb_p.reshape(bp * lp, ATT_Q), yc_p), xp2d,
        (ya_s.reshape(bs * ls, D_HYENA), yb_s.reshape(bs * ls, ATT_Q), yc_s), xs2d,
        mod_l, p['norm1_w'], p['norm2_w'], p['w_in'], p['w_br_a'], p['w_br_b'], p['w_br_c'], p['w_o'],
        layer, (lp, ls))
    return ((xp_mid.reshape(xp.shape), h2p.reshape(xp.shape)), (xs_mid.reshape(xs.shape), h2s.reshape(xs.shape)),
            k, v)


def _moe_split(a, n_exp):
    B, rows, w = a.shape
    return a.reshape(B, n_exp, rows // n_exp, w)


def kernel(x_prompt, x_sample, cache_k, cache_v, c, c_ctx, mod_w, mod_b, norm1_w, norm2_w, w_in, hy_short_w, hy_short_b, hy_w1, hy_b1, hy_w2, hy_b2, hy_w3, hy_b3, hy_decay, hy_bias, q_norm_w, k_norm_w, sc_w, sc_b, w_br_a, w_br_b, w_br_c, w_o, router_w, exp_w_gate, exp_w_up, exp_w_down):
    n_dec = x_sample.shape[0]
    n_ctx = cache_k.shape[2]
    lp = x_prompt.shape[1]
    ls = x_sample.shape[1]

    cond_rows = COND_ROWS
    cond = jnp.concatenate([c_ctx[None, :], c, jnp.zeros((cond_rows - 1 - n_dec, D_MODEL), F32)], axis=0)
    mod = _modulation(cond, mod_w, mod_b).reshape(DEPTH, cond_rows, 6, D_MODEL)

    tabs_p = tuple(jnp.asarray(t).astype(BF16) for t in _split_dft_tables(lp))
    tabs_s = tuple(jnp.asarray(t).astype(BF16) for t in _split_dft_tables(ls))
    ctx_k = cache_k.reshape(n_dec, DEPTH, n_ctx, ATT_KV)
    ctx_v = cache_v.reshape(n_dec, DEPTH, n_ctx, ATT_KV)
    dense = {'w_in': w_in.astype(BF16), 'w_br_a': w_br_a.astype(BF16), 'w_br_b': w_br_b.astype(BF16),
             'w_br_c': w_br_c.astype(BF16), 'w_o': w_o.astype(BF16)}

    filt = (hy_w1, hy_b1, hy_w2, hy_b2, hy_w3, hy_b3, hy_decay)
    spectra_p = _hyena_spectra(lp, *filt, tabs_p[0], tabs_p[1])
    spectra_s = _hyena_spectra(ls, *filt, tabs_s[0], tabs_s[1])

    xp, xs = x_prompt, x_sample
    ks_new, vs_new = [], []
    for l in range(DEPTH):
        p = dict(dense)
        p.update({
            'norm1_w': norm1_w[l].reshape(1, D_MODEL), 'norm2_w': norm2_w[l].reshape(1, D_MODEL),
            'hy_short_w': hy_short_w[l], 'hy_short_b': hy_short_b[l],
            'hy_bias': hy_bias,
            'q_norm_w': q_norm_w[l], 'k_norm_w': k_norm_w[l], 'sc_w': sc_w[l], 'sc_b': sc_b[l],
        })
        mod_p = mod[l, 0:1]
        mod_s = mod[l, 1:1 + n_dec]
        (xp_mid, h2p), (xs_mid, h2s), k_l, v_l = _token_mixers(
            xp, xs, mod[l], p, (tabs_p, tabs_s), (spectra_p, spectra_s), l, (ctx_k, ctx_v))
        ks_new.append(k_l.reshape(k_l.shape[0], lp, N_KV_HEADS, HEAD_DIM))
        vs_new.append(v_l.reshape(v_l.shape[0], lp, N_KV_HEADS, HEAD_DIM))

        router_wt = router_w[l].T.astype(BF16)
        gp_x, gp_g, gp_slot = _dispatch(h2p, router_wt)
        gs_x, gs_g, gs_slot = _dispatch(h2s, router_wt)
        yp, ys = _expert_ffn(_moe_split(gp_x, N_EXPERTS), _moe_split(gp_g, N_EXPERTS),
                             _moe_split(gs_x, N_EXPERTS), _moe_split(gs_g, N_EXPERTS),
                             exp_w_gate, exp_w_up, exp_w_down, l)
        xp = _combine(gp_slot, yp.reshape(gp_x.shape), xp_mid, mod_p[:, 5:6])
        xs = _combine(gs_slot, ys.reshape(gs_x.shape), xs_mid, mod_s[:, 5:6])

    return (xp, xs, jnp.stack(ks_new, axis=1), jnp.stack(vs_new, axis=1))
```

```python
import functools
import math

import numpy as np
import jax
import jax.numpy as jnp
from jax import lax
from jax.experimental import pallas as pl
from jax.experimental.pallas import tpu as pltpu

D_MODEL = 1024
DEPTH = 2
GRID_W = 64
D_HYENA = 512
HYENA_ORDER = 2
HYENA_POS_BANDS = 16
HYENA_POS_DIM = 1 + 2 * HYENA_POS_BANDS
HYENA_FILTER_HIDDEN = 64
HYENA_WINDOW_SHIFT = 0.05
N_HEADS = 8
N_KV_HEADS = 2
HEAD_DIM = 64
HEADS_PER_KV = N_HEADS // N_KV_HEADS
ATT_Q = N_HEADS * HEAD_DIM
ATT_KV = N_KV_HEADS * HEAD_DIM
ROPE_THETA = 10000.0
D_SCONV = 512
N_EXPERTS = 16
EXPERT_FF = 512
CAPACITY_FACTOR = 2
NORM_EPS = 1e-6
HY_COLS = (HYENA_ORDER + 1) * D_HYENA
ATT_COLS = ATT_Q + 2 * ATT_KV
SC_COLS = 3 * D_SCONV
GATE_COLS = 3 * D_MODEL
MIX_COLS = HY_COLS + ATT_COLS + SC_COLS
D_IN = MIX_COLS + GATE_COLS

F32 = jnp.float32
BF16 = jnp.bfloat16

V7X_VMEM_BYTES = 64 * 1024 * 1024
VMEM_LIMIT = V7X_VMEM_BYTES - 8 * 1024 * 1024
LANES = 128
MLP_PAD = 128
STEP_TOKENS = 1024
INPROJ_ROWS = 1024
MERGE_ROWS = 512
HYENA_CHANNEL_TILE = 256
MOD_COL_TILE = 2048
COND_ROWS = 16


def _params(*sem):
    return pltpu.CompilerParams(dimension_semantics=sem, vmem_limit_bytes=VMEM_LIMIT)


def _const_spec(shape):
    nd = len(shape)
    return pl.BlockSpec(shape, lambda *_: (0,) * nd, pipeline_mode=pl.Buffered(1))


def _layer_spec(shape, layer):
    nd = len(shape)
    return pl.BlockSpec((None,) + tuple(shape), lambda *_: (layer,) + (0,) * nd, pipeline_mode=pl.Buffered(1))


def _dot(a, b):
    return jnp.dot(a, b, preferred_element_type=F32)


def _silu(x):
    return x * (1.0 / (1.0 + jnp.exp(-x)))


def _sigmoid(x):
    return 1.0 / (1.0 + jnp.exp(-x))


@functools.lru_cache(maxsize=None)
def _split_dft_tables(L):
    h = L // 2
    k = np.arange(h, dtype=np.int64)
    alt = 1.0 - 2.0 * (k % 2)
    fwd, inv = [], []
    for odd in (0, 1):
        idx = (k[:, None] * (2 * k[None, :] + odd)) % (2 * L)
        ang = idx.astype(np.float64) * (np.pi / L)
        c = np.cos(ang)
        s = np.sin(ang)
        s[0, :] = alt
        fwd.append(np.concatenate([c, s], axis=0).astype(np.float32))
        inv.append(np.concatenate([c.T, s.T], axis=1).astype(np.float32))
    return fwd[0], fwd[1], inv[0], inv[1]


@functools.lru_cache(maxsize=None)
def _pos_features(L):
    n = np.arange(L, dtype=np.float64)
    t = n / max(L - 1, 1)
    bands = np.linspace(1e-4, HYENA_POS_BANDS - 1, HYENA_POS_BANDS)
    ang = 2.0 * math.pi * n[:, None] * bands[None, :] / L
    z = np.concatenate([t[:, None], np.cos(ang), np.sin(ang)], axis=-1)
    zp = np.zeros((L, MLP_PAD), np.float32)
    zp[:, :HYENA_POS_DIM] = z
    return zp


@functools.lru_cache(maxsize=None)
def _rope_tables(L):
    rows = L // GRID_W
    row = np.repeat(np.arange(rows, dtype=np.float64), GRID_W)
    col = np.tile(np.arange(GRID_W, dtype=np.float64), rows)
    axis_dim = HEAD_DIM // 2
    inv_freq = ROPE_THETA ** (-np.arange(0, axis_dim, 2, dtype=np.float64) / axis_dim)
    ar = row[:, None] * inv_freq[None, :]
    ac = col[:, None] * inv_freq[None, :]
    ang = np.concatenate([ar, ar, ac, ac], axis=-1)
    cos = np.cos(ang).astype(np.float32)
    sin = np.sin(ang).astype(np.float32)
    return np.tile(cos, (1, N_HEADS)), np.tile(sin, (1, N_HEADS))


@functools.lru_cache(maxsize=None)
def _head_tables():
    lane = np.arange(ATT_Q)
    block_ones = (lane[:, None] // HEAD_DIM == lane[None, :] // HEAD_DIM).astype(np.float32)
    src = np.arange(ATT_KV)
    dst = np.arange(HEADS_PER_KV * HEAD_DIM)
    rep = np.stack([(src[:, None] == g * HEAD_DIM + dst[None, :] % HEAD_DIM) for g in range(N_KV_HEADS)])
    return block_ones, rep.astype(np.float32)


@functools.lru_cache(maxsize=None)
def _prefix_table(n):
    i = np.arange(n)
    return (i[:, None] < i[None, :]).astype(np.float32)


def _mod_kernel(cond_ref, w_ref, b_ref, o_ref):
    a = _silu(cond_ref[...]).astype(BF16)
    o_ref[...] = _dot(a, w_ref[...].astype(BF16)) + b_ref[...]


def _modulation(cond, mod_w, mod_b):
    rows = cond.shape[0]
    tn = MOD_COL_TILE
    ncols = mod_w.shape[-1]
    return pl.pallas_call(
        _mod_kernel,
        grid=(DEPTH, ncols // tn),
        in_specs=[
            pl.BlockSpec((rows, D_MODEL), lambda l, j: (0, 0)),
            pl.BlockSpec((None, D_MODEL, tn), lambda l, j: (l, 0, j)),
            pl.BlockSpec((None, 1, tn), lambda l, j: (l, 0, j)),
        ],
        out_specs=pl.BlockSpec((None, rows, tn), lambda l, j: (l, 0, j)),
        out_shape=jax.ShapeDtypeStruct((DEPTH, rows, ncols), F32),
        compiler_params=_params("parallel", "parallel"),
        name="modulation",
    )(cond, mod_w, mod_b.reshape(DEPTH, 1, ncols))


def _rms_modulate(x, norm_w, shift, scale):
    y = x * lax.rsqrt(jnp.mean(x * x, axis=-1, keepdims=True) + NORM_EPS)
    return (y * norm_w) * (1.0 + scale) + shift


def _conv3(x, w, b, first, last):
    n = x.shape[0]
    prev = jnp.where(first, 0.0, pltpu.roll(x, 1, 0))
    nxt = jnp.where(last, 0.0, pltpu.roll(x, n - 1, 0))
    return prev * w[0:1, :] + x * w[1:2, :] + nxt * w[2:3, :] + b


def _inproj_kernel(x_ref, mod_ref, nw_ref, w_ref, hyw_ref, hyb_ref, scw_ref, scb_ref,
                   hyc_ref, yc_ref, qkv_ref, h_sc, *, L):
    h = _rms_modulate(x_ref[...], nw_ref[...], mod_ref[0:1, :], mod_ref[1:2, :])
    h_sc[...] = h.astype(BF16)
    tm = x_ref.shape[0]

    def edges(cols):
        pos = lax.broadcasted_iota(jnp.int32, (tm, cols), 0) % L
        return pos == 0, pos == L - 1

    sc0 = HY_COLS + ATT_COLS
    u, bgate, cgate = (_dot(h_sc[...], w_ref[:, sc0 + k * D_SCONV:sc0 + (k + 1) * D_SCONV]) for k in range(3))
    hy_chunks = [slice(lo, lo + ATT_COLS) for lo in range(0, HY_COLS, ATT_COLS)]
    y_next = _dot(h_sc[...], w_ref[:, hy_chunks[0]])
    first, last = edges(D_SCONV)
    yc_ref[...] = (bgate * _conv3(cgate * u, scw_ref[...], scb_ref[...], first, last)).astype(BF16)
    first, last = edges(ATT_COLS)
    for i, cols in enumerate(hy_chunks):
        y = y_next
        if i + 1 < len(hy_chunks):
            y_next = _dot(h_sc[...], w_ref[:, hy_chunks[i + 1]])
        else:
            qkv_ref[...] = _dot(h_sc[...], w_ref[:, HY_COLS:HY_COLS + ATT_COLS])
        hyc_ref[:, cols] = _conv3(y, hyw_ref[:, cols], hyb_ref[:, cols], first, last).astype(BF16)


def _two_group_kernel(tile_fn, n_group_in, n_shared, n_group_out, *refs, n_first, lens):
    a_in, refs = refs[:n_group_in], refs[n_group_in:]
    b_in, refs = refs[:n_group_in], refs[n_group_in:]
    shared, refs = refs[:n_shared], refs[n_shared:]
    a_out, refs = refs[:n_group_out], refs[n_group_out:]
    b_out, scratch = refs[:n_group_out], refs[n_group_out:]
    step = pl.program_id(0)

    @pl.when(step < n_first)
    def _():
        tile_fn(*a_in, *shared, *a_out, *scratch, lens[0])

    @pl.when(step >= n_first)
    def _():
        tile_fn(*b_in, *shared, *b_out, *scratch, lens[1])


def _group_specs(block, n_first):
    return (pl.BlockSpec(block, lambda i: (jnp.minimum(i, n_first - 1), 0)),
            pl.BlockSpec(block, lambda i: (jnp.maximum(i - n_first, 0), 0)))


def _group_mod_spec(n_first, tm, second_len):
    return pl.BlockSpec((None, 6, D_MODEL),
                        lambda i: (jnp.where(i < n_first, 0, 1 + (jnp.maximum(i - n_first, 0) * tm) // second_len), 0, 0))


def _inproj(x2d, mod, norm_w, w_in_bf, hy_short_w, hy_short_b, sc_w, sc_b, layer, rows_per_req, tm=INPROJ_ROWS):
    T = x2d.shape[0]
    n_req = mod.shape[0]
    req = (lambda i: (i * tm) // rows_per_req) if n_req > 1 else (lambda i: 0)
    out = lambda w: pl.BlockSpec((tm, w), lambda i: (i, 0))
    return pl.pallas_call(
        functools.partial(_inproj_kernel, L=rows_per_req),
        grid=(T // tm,),
        in_specs=[
            pl.BlockSpec((tm, D_MODEL), lambda i: (i, 0)),
            pl.BlockSpec((None, 6, D_MODEL), lambda i: (req(i), 0, 0)),
            _const_spec((1, D_MODEL)),
            _layer_spec((D_MODEL, MIX_COLS), layer),
            _const_spec((3, HY_COLS)), _const_spec((1, HY_COLS)),
            _const_spec((3, D_SCONV)), _const_spec((1, D_SCONV)),
        ],
        out_specs=[out(HY_COLS), out(D_SCONV), out(ATT_COLS)],
        out_shape=[jax.ShapeDtypeStruct((T, HY_COLS), BF16), jax.ShapeDtypeStruct((T, D_SCONV), BF16),
                   jax.ShapeDtypeStruct((T, ATT_COLS), F32)],
        scratch_shapes=[pltpu.VMEM((tm, D_MODEL), BF16)],
        compiler_params=_params("parallel"),
        name="inproj",
    )(x2d, mod, norm_w, w_in_bf, hy_short_w, hy_short_b.reshape(1, HY_COLS), sc_w, sc_b.reshape(1, D_SCONV))


def _split_parity(par_sc, z):
    half = z.shape[0] // 2
    n_lt = z.shape[1] // LANES
    for c in range(n_lt):
        par_sc[c] = z[:, c * LANES:(c + 1) * LANES]
    return tuple(jnp.concatenate([par_sc[c, pl.ds(par, half, stride=2), :] for c in range(n_lt)], axis=1)
                 for par in (0, 1))


def _join_parity(par_sc, even, odd):
    half = even.shape[0]
    n_lt = even.shape[1] // LANES
    for c in range(n_lt):
        par_sc[c, pl.ds(0, half, stride=2), :] = even[:, c * LANES:(c + 1) * LANES]
        par_sc[c, pl.ds(1, half, stride=2), :] = odd[:, c * LANES:(c + 1) * LANES]
    return jnp.concatenate([par_sc[c] for c in range(n_lt)], axis=1)


def _fold_spectrum(te, to):
    half = te.shape[0] // 2
    row0 = lax.broadcasted_iota(jnp.int32, (half, te.shape[1]), 0) == 0
    ce, se, co, so = te[:half], te[half:], to[:half], to[half:]
    a = jnp.concatenate([ce + co, jnp.where(row0, se, ce - co)], axis=0)
    b = jnp.concatenate([jnp.where(row0, ce - co, se + so), jnp.where(row0, so, so - se)], axis=0)
    return a, b


def _unfold_spectrum(yr, yi):
    half = yr.shape[0] // 2
    row0 = lax.broadcasted_iota(jnp.int32, (half, yr.shape[1]), 0) == 0
    yr_lo, yr_up, yi_lo, yi_up = yr[:half], yr[half:], yi[:half], yi[half:]
    even = jnp.concatenate([jnp.where(row0, yr_lo + yi_lo, yr_lo + yr_up),
                            jnp.where(row0, yr_up, yi_lo - yi_up)], axis=0)
    odd = jnp.concatenate([jnp.where(row0, yr_lo - yi_lo, yr_lo - yr_up),
                           jnp.where(row0, yi_up, yi_lo + yi_up)], axis=0)
    return even, odd


def _filter_kernel(z_ref, w1_ref, b1_ref, w2_ref, b2_ref, w3f_ref, b3f_ref, w3b_ref, b3b_ref,
                   decf_ref, decb_ref, fwde_ref, fwdo_ref, p_ref, h_sc, par_sc, *, L):
    hi = lax.Precision.HIGHEST

    @pl.when((pl.program_id(1) == 0) & (pl.program_id(2) == 0))
    def _():
        h1 = jnp.sin(jnp.dot(z_ref[...], w1_ref[...], precision=hi, preferred_element_type=F32) + b1_ref[...])
        h_sc[...] = jnp.sin(jnp.dot(h1, w2_ref[...], precision=hi, preferred_element_type=F32) + b2_ref[...])

    h = h_sc[...]
    tc = w3f_ref.shape[1]
    pos = lax.broadcasted_iota(jnp.int32, (L, tc), 0)
    t = pos.astype(F32) / float(max(L - 1, 1))

    def taps(w3_ref, b3_ref, dec_ref):
        g = jnp.dot(h, w3_ref[...], precision=hi, preferred_element_type=F32) + b3_ref[...]
        return g * (jnp.exp(-t * jnp.abs(dec_ref[...])) + HYENA_WINDOW_SHIFT)

    hf = taps(w3f_ref, b3f_ref, decf_ref)
    hb = jnp.where(pos == 0, 0.0, taps(w3b_ref, b3b_ref, decb_ref))

    def transform(taps_lc):
        even, odd = _split_parity(par_sc, taps_lc)
        return _fold_spectrum(_dot(fwde_ref[...], even.astype(BF16)), _dot(fwdo_ref[...], odd.astype(BF16)))

    fa, fb = transform(hf)
    ba, bb = transform(hb)
    k_re = fa + ba
    k_im = bb - fb
    k_ny = fb + bb
    first = pos == 0
    inv_n = 1.0 / (2 * L)
    p_ref[0] = jnp.where(first, k_re * inv_n, k_re * (2.0 * inv_n))
    p_ref[1] = jnp.where(first, 0.0, k_im * (2.0 * inv_n))
    p_ref[2] = jnp.where(first, k_ny * inv_n, k_re * (2.0 * inv_n))


def _hyena_spectra(L, w1, b1, w2, b2, w3, b3, decay, fwd_even, fwd_odd):
    pad_h = MLP_PAD - HYENA_FILTER_HIDDEN
    z = jnp.asarray(_pos_features(L))
    w1p = jnp.pad(w1, ((0, 0), (0, MLP_PAD - HYENA_POS_DIM), (0, pad_h)))
    b1p = jnp.pad(b1, ((0, 0), (0, pad_h))).reshape(DEPTH, 1, MLP_PAD)
    w2p = jnp.pad(w2, ((0, 0), (0, pad_h), (0, pad_h)))
    b2p = jnp.pad(b2, ((0, 0), (0, pad_h))).reshape(DEPTH, 1, MLP_PAD)
    w3p = jnp.pad(w3, ((0, 0), (0, pad_h), (0, 0)))
    ncol = w3.shape[-1]
    b3r = b3.reshape(DEPTH, 1, ncol)
    decr = decay.reshape(DEPTH, 1, ncol)
    tc = HYENA_CHANNEL_TILE
    nct = D_HYENA // tc
    per_dir = HYENA_ORDER * nct
    col_f = lambda l, o, c: (l, 0, o * nct + c)
    col_b = lambda l, o, c: (l, 0, per_dir + o * nct + c)
    per_layer = lambda shape: pl.BlockSpec((None,) + shape, lambda l, o, c: (l, 0, 0))
    return pl.pallas_call(
        functools.partial(_filter_kernel, L=L),
        grid=(DEPTH, HYENA_ORDER, nct),
        in_specs=[
            _const_spec((L, MLP_PAD)),
            per_layer((MLP_PAD, MLP_PAD)), per_layer((1, MLP_PAD)),
            per_layer((MLP_PAD, MLP_PAD)), per_layer((1, MLP_PAD)),
            pl.BlockSpec((None, MLP_PAD, tc), col_f), pl.BlockSpec((None, 1, tc), col_f),
            pl.BlockSpec((None, MLP_PAD, tc), col_b), pl.BlockSpec((None, 1, tc), col_b),
            pl.BlockSpec((None, 1, tc), col_f), pl.BlockSpec((None, 1, tc), col_b),
            _const_spec((L, L // 2)), _const_spec((L, L // 2)),
        ],
        out_specs=pl.BlockSpec((None, None, 3, L, tc), lambda l, o, c: (l, o, 0, 0, c)),
        out_shape=jax.ShapeDtypeStruct((DEPTH, HYENA_ORDER, 3, L, D_HYENA), F32),
        scratch_shapes=[pltpu.VMEM((L, MLP_PAD), F32), pltpu.VMEM((tc // LANES, L, LANES), F32)],
        compiler_params=_params("arbitrary", "arbitrary", "arbitrary"),
        name="hyena_spectra",
    )(z, w1p, b1p, w2p, b2p, w3p, b3r, w3p, b3r, decr, decr, fwd_even, fwd_odd)


def _seqmix_kernel(v_ref, x1_ref, x2_ref, p_ref, hb_ref, fwde_ref, fwdo_ref, inve_ref, invo_ref, ya_ref,
                   spec_sc, par_sc):
    n_req = v_ref.shape[0]
    mult_refs = (x1_ref, x2_ref)

    def chain(r):
        z = v_ref[r].astype(F32)
        z_even, z_odd = _split_parity(par_sc.at[r], z)
        yield
        for o in range(HYENA_ORDER):
            te = _dot(fwde_ref[...], z_even.astype(BF16))
            to = _dot(fwdo_ref[...], z_odd.astype(BF16))
            yield
            a, b = _fold_spectrum(te, to)
            p2 = p_ref[o, 1]
            even, odd = _unfold_spectrum(a * p_ref[o, 0] + b * p2, b * p_ref[o, 2] - a * p2)
            spec_sc[r, 0] = even.astype(BF16)
            spec_sc[r, 1] = odd.astype(BF16)
            yield
            y_even = _dot(inve_ref[...], spec_sc[r, 0])
            y_odd = _dot(invo_ref[...], spec_sc[r, 1])
            yield
            z = mult_refs[o][r].astype(F32) * (_join_parity(par_sc.at[r], y_even, y_odd) + hb_ref[o:o + 1, :] * z)
            if o < HYENA_ORDER - 1:
                z_even, z_odd = _split_parity(par_sc.at[r], z)
            else:
                ya_ref[r] = z.astype(ya_ref.dtype)
            yield

    chains = [chain(r) for r in range(n_req)]
    n_stages = 1 + 4 * HYENA_ORDER
    for tick in range(n_stages + n_req - 1):
        for r, c in enumerate(chains):
            if 0 <= tick - r < n_stages:
                next(c)


def _seqmix(hyc3d, spectra, hy_bias, tabs, layer):
    B, L, _ = hyc3d.shape
    tc = HYENA_CHANNEL_TILE
    nct = D_HYENA // tc
    nr = max(2, STEP_TOKENS // L)
    act = lambda off: pl.BlockSpec((nr, L, tc), lambda c, b, off=off: (b, 0, off * nct + c))
    return pl.pallas_call(
        _seqmix_kernel,
        grid=(nct, B // nr),
        in_specs=[
            act(0), act(1), act(2),
            pl.BlockSpec((None, HYENA_ORDER, 3, L, tc), lambda c, b: (layer, 0, 0, 0, c)),
            pl.BlockSpec((None, HYENA_ORDER, tc), lambda c, b: (layer, 0, c)),
            _const_spec((L, L // 2)), _const_spec((L, L // 2)), _const_spec((L // 2, L)), _const_spec((L // 2, L)),
        ],
        out_specs=pl.BlockSpec((nr, L, tc), lambda c, b: (b, 0, c)),
        out_shape=jax.ShapeDtypeStruct((B, L, D_HYENA), BF16),
        scratch_shapes=[pltpu.VMEM((nr, 2, L, tc), BF16), pltpu.VMEM((nr, tc // LANES, L, LANES), F32)],
        compiler_params=_params("arbitrary", "arbitrary"),
        name="seqmix",
    )(hyc3d, hyc3d, hyc3d, spectra, hy_bias, *tabs)


def _head_rms(x, ones_ref, w):
    sq = x * x
    hi = sq.astype(BF16)
    lo = (sq - hi.astype(F32)).astype(BF16)
    ones = ones_ref[...]
    width = x.shape[1]
    ss = _dot(hi, ones[:width, :width]) + _dot(lo, ones[:width, :width])
    return x * lax.rsqrt(ss * (1.0 / HEAD_DIM) + NORM_EPS) * w


def _rope(x, cos, sin):
    width = x.shape[1]
    lane = lax.broadcasted_iota(jnp.int32, x.shape, 1)
    half = HEAD_DIM // 4
    low = (lane % (2 * half)) < half
    rot = jnp.where(low, -pltpu.roll(x, width - half, 1), pltpu.roll(x, half, 1))
    return x * cos + rot * sin


def _attn_kernel(*refs, rope):
    if rope:
        (q_ref, k_ref, v_ref, ck_ref, cv_ref, qw_ref, kw_ref, ones_ref, rep_ref, rept_ref,
         cos_ref, sin_ref, yb_ref) = refs
    else:
        q_ref, k_ref, v_ref, qw_ref, kw_ref, ones_ref, rep_ref, rept_ref, yb_ref, ko_ref, vo_ref = refs
    n_req, L, _ = q_ref.shape
    gw = HEADS_PER_KV * HEAD_DIM
    lane = lax.broadcasted_iota(jnp.int32, (L, gw), 1)
    for r in range(n_req):
        kn = _head_rms(k_ref[r], ones_ref, kw_ref[...])
        v = v_ref[r]
        q = _head_rms(q_ref[r], ones_ref, qw_ref[...])
        if rope:
            keys = _rope(kn, cos_ref[:, :ATT_KV], sin_ref[:, :ATT_KV])
            keys = jnp.concatenate([ck_ref[r], keys], axis=0)
            vals = jnp.concatenate([cv_ref[r], v], axis=0)
            q = _rope(q, cos_ref[...], sin_ref[...])
        else:
            ko_ref[r] = kn
            vo_ref[r] = v
            keys, vals = kn, v
        keys = keys.astype(BF16)
        vals = vals.astype(BF16)
        q = q * (HEAD_DIM ** -0.5)
        for g in range(N_KV_HEADS):
            k4 = lax.dot_general(rept_ref[g], keys, (((1,), (1,)), ((), ())), preferred_element_type=F32).astype(BF16)
            v4 = _dot(vals, rep_ref[g]).astype(BF16)
            qg = q[:, g * gw:(g + 1) * gw]
            acc = jnp.zeros((L, gw), F32)
            for h in range(HEADS_PER_KV):
                mine = (lane // HEAD_DIM) == h
                qm = jnp.where(mine, qg, 0.0).astype(BF16)
                s = _dot(qm, k4)
                p = jnp.exp(s - jnp.max(s, axis=-1, keepdims=True))
                denom = jnp.sum(p, axis=-1, keepdims=True)
                o4 = _dot(p.astype(BF16), v4)
                acc = jnp.where(mine, o4 * (1.0 / denom), acc)
            yb_ref[r, :, g * gw:(g + 1) * gw] = acc.astype(yb_ref.dtype)


def _attention(qkv3d, q_norm_w, k_norm_w, ctx_k=None, ctx_v=None, layer=0):
    B, L, _ = qkv3d.shape
    rope = ctx_k is not None
    nr = _requests_per_step(L)
    ones_np, rep_np = _head_tables()
    ones = jnp.asarray(ones_np).astype(BF16)
    rep = jnp.asarray(rep_np).astype(BF16)
    rept = jnp.asarray(np.swapaxes(rep_np, 1, 2)).astype(BF16)
    qw = jnp.tile(q_norm_w, N_HEADS).reshape(1, ATT_Q)
    kw = jnp.tile(k_norm_w, N_KV_HEADS).reshape(1, ATT_KV)
    kblk = ATT_Q // ATT_KV
    in_specs = [
        pl.BlockSpec((nr, L, ATT_Q), lambda b: (b, 0, 0)),
        pl.BlockSpec((nr, L, ATT_KV), lambda b: (b, 0, kblk)),
        pl.BlockSpec((nr, L, ATT_KV), lambda b: (b, 0, kblk + 1)),
    ]
    args = [qkv3d, qkv3d, qkv3d]
    if rope:
        n_ctx = ctx_k.shape[2]
        in_specs += [pl.BlockSpec((nr, None, n_ctx, ATT_KV), lambda b: (b, layer, 0, 0))] * 2
        args += [ctx_k, ctx_v]
    in_specs += [_const_spec((1, ATT_Q)), _const_spec((1, ATT_KV)), _const_spec((ATT_Q, ATT_Q)),
                 _const_spec((N_KV_HEADS, ATT_KV, HEADS_PER_KV * HEAD_DIM)),
                 _const_spec((N_KV_HEADS, HEADS_PER_KV * HEAD_DIM, ATT_KV))]
    args += [qw, kw, ones, rep, rept]
    yb_shape = jax.ShapeDtypeStruct((B, L, ATT_Q), BF16)
    yb_spec = pl.BlockSpec((nr, L, ATT_Q), lambda b: (b, 0, 0))
    if rope:
        cos_np, sin_np = _rope_tables(L)
        in_specs += [_const_spec((L, ATT_Q))] * 2
        args += [jnp.asarray(cos_np), jnp.asarray(sin_np)]
        out_specs = yb_spec
        out_shape = yb_shape
    else:
        kv_spec = pl.BlockSpec((nr, L, ATT_KV), lambda b: (b, 0, 0))
        kv_shape = jax.ShapeDtypeStruct((B, L, ATT_KV), F32)
        out_specs = [yb_spec, kv_spec, kv_spec]
        out_shape = [yb_shape, kv_shape, kv_shape]
    return pl.pallas_call(
        functools.partial(_attn_kernel, rope=rope),
        grid=(B // nr,),
        in_specs=in_specs,
        out_specs=out_specs,
        out_shape=out_shape,
        compiler_params=_params("parallel"),
        name="attention",
    )(*args)


def _merge_tile(ya_ref, yb_ref, yc_ref, x_ref, mod_ref, nw1_ref, nw2_ref, *rest):
    gate_w_refs = rest[:GATE_COLS // ATT_COLS]
    wa_ref, wb_ref, wc_ref, wo_ref, xo_ref, h2_ref, _ = rest[GATE_COLS // ATT_COLS:]
    tm = x_ref.shape[0]
    n_chain = 2
    rows_per = tm // n_chain

    def chain(c):
        rows = pl.ds(c * rows_per, rows_per)
        x = x_ref[rows, :]
        h = _rms_modulate(x, nw1_ref[...], mod_ref[0:1, :], mod_ref[1:2, :]).astype(BF16)
        yield
        gates = jnp.concatenate([_dot(h, w_ref[...]) for w_ref in gate_w_refs], axis=1)
        da = _dot(ya_ref[rows, :], wa_ref[...])
        db = _dot(yb_ref[rows, :], wb_ref[...])
        dc = _dot(yc_ref[rows, :], wc_ref[...])
        yield
        merged = (_sigmoid(gates[:, 0:D_MODEL]) * da + _sigmoid(gates[:, D_MODEL:2 * D_MODEL]) * db
                  + _sigmoid(gates[:, 2 * D_MODEL:]) * dc).astype(BF16)
        yield
        proj = _dot(merged, wo_ref[...])
        yield
        x = x + mod_ref[2:3, :] * proj
        xo_ref[rows, :] = x
        h2_ref[rows, :] = _rms_modulate(x, nw2_ref[...], mod_ref[3:4, :], mod_ref[4:5, :]).astype(BF16)
        yield

    chains = [chain(c) for c in range(n_chain)]
    n_stages = 5
    for tick in range(n_stages + n_chain - 1):
        for c, ch in enumerate(chains):
            if 0 <= tick - c < n_stages:
                next(ch)


def _merge(branches_p, xp2d, branches_s, xs2d, mod_l, norm1_w, norm2_w, w_in_bf, wa, wb, wc, wo, layer, lens,
           tm=MERGE_ROWS):
    n_p, n_s = xp2d.shape[0] // tm, xs2d.shape[0] // tm
    br_p, br_s = _group_specs((tm, D_HYENA), n_p)
    row_p, row_s = _group_specs((tm, D_MODEL), n_p)
    n_gate = GATE_COLS // ATT_COLS
    gate_blk0 = MIX_COLS // ATT_COLS
    gate_w = [pl.BlockSpec((None, D_MODEL, ATT_COLS), lambda i, k=k: (layer, 0, gate_blk0 + k),
                           pipeline_mode=pl.Buffered(1)) for k in range(n_gate)]
    shapes = lambda t: [jax.ShapeDtypeStruct((t, D_MODEL), F32), jax.ShapeDtypeStruct((t, D_MODEL), BF16)]
    res = pl.pallas_call(
        functools.partial(_two_group_kernel, _merge_tile, 4, 7 + n_gate, 2, n_first=n_p, lens=lens),
        grid=(n_p + n_s,),
        in_specs=[
            br_p, br_p, br_p, row_p, br_s, br_s, br_s, row_s,
            _group_mod_spec(n_p, tm, lens[1]),
            _const_spec((1, D_MODEL)), _const_spec((1, D_MODEL)),
            *gate_w,
            _layer_spec((D_HYENA, D_MODEL), layer), _layer_spec((ATT_Q, D_MODEL), layer),
            _layer_spec((D_SCONV, D_MODEL), layer), _layer_spec((D_MODEL, D_MODEL), layer),
        ],
        out_specs=[row_p, row_p, row_s, row_s],
        out_shape=shapes(xp2d.shape[0]) + shapes(xs2d.shape[0]),
        compiler_params=_params("arbitrary"),
        name="merge",
    )(*branches_p, xp2d, *branches_s, xs2d, mod_l, norm1_w, norm2_w, *([w_in_bf] * n_gate), wa, wb, wc, wo)
    return res[:2], res[2:]


def _route_kernel(h_ref, rw_ref, tri_ref, aff_ref, slot_ref, *, cap):
    b = pl.program_id(0)
    n_req = h_ref.shape[0]
    for r in range(n_req):
        logits = lax.dot_general(rw_ref[...], h_ref[r], (((1,), (1,)), ((), ())), preferred_element_type=F32)
        ex = jnp.exp(logits - jnp.max(logits, axis=0, keepdims=True))
        first_row = pl.multiple_of((b * n_req + r) * N_EXPERTS, N_EXPERTS)
        aff_ref[pl.ds(first_row, N_EXPERTS), :] = ex / jnp.sum(ex, axis=0, keepdims=True)

    @pl.when(b == pl.num_programs(0) - 1)
    def _():
        aff = aff_ref[...]

        def count(mask):
            return jnp.sum(jnp.where(mask, 1.0, 0.0), axis=1, keepdims=True)

        kth = jnp.zeros((aff.shape[0], 1), jnp.int32)
        for bit in range(30, -1, -1):
            trial = kth | (1 << bit)
            enough = count(aff >= lax.bitcast_convert_type(trial, F32)) >= cap
            kth = jnp.where(enough, trial, kth)
        next_up = lax.bitcast_convert_type(kth + 1, F32)
        above = aff >= next_up
        tied = (aff >= lax.bitcast_convert_type(kth, F32)) & (aff < next_up)
        tri = tri_ref[...]
        tied_before = _dot(jnp.where(tied, 1.0, 0.0).astype(BF16), tri)
        chosen = above | (tied & (tied_before < (cap - count(above))))
        slot = _dot(jnp.where(chosen, 1.0, 0.0).astype(BF16), tri)
        slot_ref[...] = jnp.where(chosen, slot, -1.0)


def _route(h3d, router_wt):
    B, n_tok, _ = h3d.shape
    cap = CAPACITY_FACTOR * n_tok // N_EXPERTS
    tri = jnp.asarray(_prefix_table(n_tok)).astype(BF16)
    whole = pl.BlockSpec((B * N_EXPERTS, n_tok), lambda b: (0, 0))
    shape = jax.ShapeDtypeStruct((B * N_EXPERTS, n_tok), F32)
    nr = _requests_per_step(n_tok)
    return pl.pallas_call(
        functools.partial(_route_kernel, cap=cap),
        grid=(B // nr,),
        in_specs=[
            pl.BlockSpec((nr, n_tok, D_MODEL), lambda b: (b, 0, 0)),
            _const_spec((N_EXPERTS, D_MODEL)),
            _const_spec((n_tok, n_tok)),
        ],
        out_specs=[whole, whole],
        out_shape=[shape, shape],
        compiler_params=_params("arbitrary"),
        name="moe_route",
    )(h3d, router_wt, tri)


MOE_PARTS = 2


def _dispatch_kernel(h_ref, aff_ref, slot_ref, xs_ref, g_ref, d_sc, *, n_tok, cap):
    row = lax.broadcasted_iota(jnp.int32, (cap, n_tok), 0).astype(F32)
    per_part = N_EXPERTS // MOE_PARTS
    for r in range(h_ref.shape[0]):
        for part in range(MOE_PARTS):
            for e in range(part * per_part, (part + 1) * per_part):
                idx = r * N_EXPERTS + e
                hit = row == slot_ref[idx:idx + 1, :]
                d_sc[r, e * cap:(e + 1) * cap, :] = jnp.where(hit, 1.0, 0.0).astype(BF16)
                gate = jnp.sum(jnp.where(hit, aff_ref[idx:idx + 1, :], 0.0), axis=1, keepdims=True)
                g_ref[r, e * cap:(e + 1) * cap, :] = jnp.broadcast_to(gate, (cap, LANES))
            rows = slice(part * per_part * cap, (part + 1) * per_part * cap)
            xs_ref[r, rows, :] = _dot(d_sc[r, rows, :], h_ref[r]).astype(BF16)


def _requests_per_step(n_tok):
    return max(1, STEP_TOKENS // n_tok)


def _dispatch(h3d, router_wt):
    B, n_tok, _ = h3d.shape
    cap = CAPACITY_FACTOR * n_tok // N_EXPERTS
    rows = N_EXPERTS * cap
    aff, slot = _route(h3d, router_wt)
    nr = _requests_per_step(n_tok)
    per_req = pl.BlockSpec((nr * N_EXPERTS, n_tok), lambda b: (b, 0))
    blk = lambda w: pl.BlockSpec((nr, rows, w), lambda b: (b, 0, 0))
    xs, gates = pl.pallas_call(
        functools.partial(_dispatch_kernel, n_tok=n_tok, cap=cap),
        grid=(B // nr,),
        in_specs=[pl.BlockSpec((nr, n_tok, D_MODEL), lambda b: (b, 0, 0)), per_req, per_req],
        out_specs=[blk(D_MODEL), blk(LANES)],
        out_shape=[jax.ShapeDtypeStruct((B, rows, D_MODEL), BF16),
                   jax.ShapeDtypeStruct((B, rows, LANES), F32)],
        scratch_shapes=[pltpu.VMEM((nr, rows, n_tok), BF16)],
        compiler_params=_params("parallel"),
        name="moe_dispatch",
    )(h3d, aff, slot)
    return xs, gates, slot


def _ffn_kernel(xp_ref, gp_ref, xs_ref, gs_ref, wg_ref, wu_ref, wd_ref, yp_ref, ys_ref):
    wg = wg_ref[...].astype(BF16)
    wu = wu_ref[...].astype(BF16)
    wd = wd_ref[...].astype(BF16)
    for x_ref, g_ref, y_ref in ((xp_ref, gp_ref, yp_ref), (xs_ref, gs_ref, ys_ref)):
        nb, cap, _ = x_ref.shape
        x = x_ref[...].reshape(nb * cap, D_MODEL)
        gate = g_ref[...].reshape(nb * cap, LANES)
        gate = jnp.concatenate([gate] * (EXPERT_FF // LANES), axis=1)
        act = _silu(_dot(x, wg)) * _dot(x, wu) * gate
        y = _dot(act.astype(BF16), wd)
        y_ref[...] = y.reshape(nb, cap, D_MODEL).astype(y_ref.dtype)


def _expert_ffn(xp, gp, xs, gs, w_gate, w_up, w_down, layer):
    def act_spec(a):
        nb, _, cap, w = a.shape
        return pl.BlockSpec((nb, None, cap, w), lambda e: (0, e, 0, 0))

    def w_spec(a):
        return pl.BlockSpec((None, None) + a.shape[2:], lambda e: (layer, e, 0, 0))

    return pl.pallas_call(
        _ffn_kernel,
        grid=(N_EXPERTS,),
        in_specs=[act_spec(xp), act_spec(gp), act_spec(xs), act_spec(gs),
                  w_spec(w_gate), w_spec(w_up), w_spec(w_down)],
        out_specs=[act_spec(xp), act_spec(xs)],
        out_shape=[jax.ShapeDtypeStruct(xp.shape, BF16), jax.ShapeDtypeStruct(xs.shape, BF16)],
        compiler_params=_params("parallel"),
        name="expert_ffn",
    )(xp, gp, xs, gs, w_gate, w_up, w_down)


def _combine_kernel(slot_ref, y_ref, x_ref, mod_ref, o_ref, d_sc, *, n_tok, cap):
    row = lax.broadcasted_iota(jnp.int32, (cap, n_tok), 0).astype(F32)
    per_part = N_EXPERTS // MOE_PARTS
    for r in range(y_ref.shape[0]):
        moe = None
        for part in range(MOE_PARTS):
            for e in range(part * per_part, (part + 1) * per_part):
                idx = r * N_EXPERTS + e
                hit = row == slot_ref[idx:idx + 1, :]
                d_sc[r, e * cap:(e + 1) * cap, :] = jnp.where(hit, 1.0, 0.0).astype(BF16)
            rows = slice(part * per_part * cap, (part + 1) * per_part * cap)
            scattered = lax.dot_general(d_sc[r, rows, :], y_ref[r, rows, :], (((0,), (0,)), ((), ())),
                                        preferred_element_type=F32)
            moe = scattered if moe is None else moe + scattered
        o_ref[r] = x_ref[r] + mod_ref[min(r, mod_ref.shape[0] - 1)] * moe


def _combine(slot, y3d, x3d, mod_g2):
    B, n_tok, _ = x3d.shape
    rows = y3d.shape[1]
    cap = rows // N_EXPERTS
    nr = _requests_per_step(n_tok)
    shared_mod = mod_g2.shape[0] == 1
    mod_spec = (pl.BlockSpec((1, 1, D_MODEL), lambda b: (0, 0, 0)) if shared_mod
                else pl.BlockSpec((nr, 1, D_MODEL), lambda b: (b, 0, 0)))
    return pl.pallas_call(
        functools.partial(_combine_kernel, n_tok=n_tok, cap=cap),
        grid=(B // nr,),
        in_specs=[
            pl.BlockSpec((nr * N_EXPERTS, n_tok), lambda b: (b, 0)),
            pl.BlockSpec((nr, rows, D_MODEL), lambda b: (b, 0, 0)),
            pl.BlockSpec((nr, n_tok, D_MODEL), lambda b: (b, 0, 0)),
            mod_spec,
        ],
        out_specs=pl.BlockSpec((nr, n_tok, D_MODEL), lambda b: (b, 0, 0)),
        out_shape=jax.ShapeDtypeStruct(x3d.shape, F32),
        scratch_shapes=[pltpu.VMEM((nr, rows, n_tok), BF16)],
        compiler_params=_params("parallel"),
        name="moe_combine",
    )(slot, y3d, x3d, mod_g2)


def _token_mixers(xp, xs, mod_l, p, tabs, spectra, layer, ctx):
    (bp, lp, _), (bs, ls, _) = xp.shape, xs.shape
    xp2d, xs2d = xp.reshape(bp * lp, D_MODEL), xs.reshape(bs * ls, D_MODEL)
    conv = (p['hy_short_w'], p['hy_short_b'], p['sc_w'], p['sc_b'])
    hyc_p, yc_p, qkv_p = _inproj(xp2d, mod_l[0:1], p['norm1_w'], p['w_in'], *conv, layer, lp)
    hyc_s, yc_s, qkv_s = _inproj(xs2d, mod_l[1:1 + bs], p['norm1_w'], p['w_in'], *conv, layer, ls)
    ya_p = _seqmix(hyc_p.reshape(bp, lp, HY_COLS), spectra[0], p['hy_bias'], tabs[0], layer)
    ya_s = _seqmix(hyc_s.reshape(bs, ls, HY_COLS), spectra[1], p['hy_bias'], tabs[1], layer)
    yb_p, k, v = _attention(qkv_p.reshape(bp, lp, ATT_COLS), p['q_norm_w'], p['k_norm_w'])
    yb_s = _attention(qkv_s.reshape(bs, ls, ATT_COLS), p['q_norm_w'], p['k_norm_w'], ctx[0], ctx[1], layer)
    (xp_mid, h2p), (xs_mid, h2s) = _merge(
        (ya_p.reshape(bp * lp, D_HYENA), yb_p.reshape(bp * lp, ATT_Q), yc_p), xp2d,
        (ya_s.reshape(bs * ls, D_HYENA), yb_s.reshape(bs * ls, ATT_Q), yc_s), xs2d,
        mod_l, p['norm1_w'], p['norm2_w'], p['w_in'], p['w_br_a'], p['w_br_b'], p['w_br_c'], p['w_o'],
        layer, (lp, ls))
    return ((xp_mid.reshape(xp.shape), h2p.reshape(xp.shape)), (xs_mid.reshape(xs.shape), h2s.reshape(xs.shape)),
            k, v)


def _moe_split(a, n_exp):
    B, rows, w = a.shape
    return a.reshape(B, n_exp, rows // n_exp, w)


def kernel(x_prompt, x_sample, cache_k, cache_v, c, c_ctx, mod_w, mod_b, norm1_w, norm2_w, w_in, hy_short_w, hy_short_b, hy_w1, hy_b1, hy_w2, hy_b2, hy_w3, hy_b3, hy_decay, hy_bias, q_norm_w, k_norm_w, sc_w, sc_b, w_br_a, w_br_b, w_br_c, w_o, router_w, exp_w_gate, exp_w_up, exp_w_down):
    n_dec = x_sample.shape[0]
    n_ctx = cache_k.shape[2]
    lp = x_prompt.shape[1]
    ls = x_sample.shape[1]

    cond_rows = COND_ROWS
    cond = jnp.concatenate([c_ctx[None, :], c, jnp.zeros((cond_rows - 1 - n_dec, D_MODEL), F32)], axis=0)
    mod = _modulation(cond, mod_w, mod_b).reshape(DEPTH, cond_rows, 6, D_MODEL)

    tabs_p = tuple(jnp.asarray(t).astype(BF16) for t in _split_dft_tables(lp))
    tabs_s = tuple(jnp.asarray(t).astype(BF16) for t in _split_dft_tables(ls))
    ctx_k = cache_k.reshape(n_dec, DEPTH, n_ctx, ATT_KV)
    ctx_v = cache_v.reshape(n_dec, DEPTH, n_ctx, ATT_KV)
    dense = {'w_in': w_in.astype(BF16), 'w_br_a': w_br_a.astype(BF16), 'w_br_b': w_br_b.astype(BF16),
             'w_br_c': w_br_c.astype(BF16), 'w_o': w_o.astype(BF16)}

    filt = (hy_w1, hy_b1, hy_w2, hy_b2, hy_w3, hy_b3, hy_decay)
    spectra_p = _hyena_spectra(lp, *filt, tabs_p[0], tabs_p[1])
    spectra_s = _hyena_spectra(ls, *filt, tabs_s[0], tabs_s[1])

    xp, xs = x_prompt, x_sample
    ks_new, vs_new = [], []
    for l in range(DEPTH):
        p = dict(dense)
        p.update({
            'norm1_w': norm1_w[l].reshape(1, D_MODEL), 'norm2_w': norm2_w[l].reshape(1, D_MODEL),
            'hy_short_w': hy_short_w[l], 'hy_short_b': hy_short_b[l],
            'hy_bias': hy_bias,
            'q_norm_w': q_norm_w[l], 'k_norm_w': k_norm_w[l], 'sc_w': sc_w[l], 'sc_b': sc_b[l],
        })
        mod_p = mod[l, 0:1]
        mod_s = mod[l, 1:1 + n_dec]
        (xp_mid, h2p), (xs_mid, h2s), k_l, v_l = _token_mixers(
            xp, xs, mod[l], p, (tabs_p, tabs_s), (spectra_p, spectra_s), l, (ctx_k, ctx_v))
        ks_new.append(k_l.reshape(k_l.shape[0], lp, N_KV_HEADS, HEAD_DIM))
        vs_new.append(v_l.reshape(v_l.shape[0], lp, N_KV_HEADS, HEAD_DIM))

        router_wt = router_w[l].T.astype(BF16)
        gp_x, gp_g, gp_slot = _dispatch(h2p, router_wt)
        gs_x, gs_g, gs_slot = _dispatch(h2s, router_wt)
        yp, ys = _expert_ffn(_moe_split(gp_x, N_EXPERTS), _moe_split(gp_g, N_EXPERTS),
                             _moe_split(gs_x, N_EXPERTS), _moe_split(gs_g, N_EXPERTS),
                             exp_w_gate, exp_w_up, exp_w_down, l)
        xp = _combine(gp_slot, yp.reshape(gp_x.shape), xp_mid, mod_p[:, 5:6])
        xs = _combine(gs_slot, ys.reshape(gs_x.shape), xs_mid, mod_s[:, 5:6])

    return (xp, xs, jnp.stack(ks_new, axis=1), jnp.stack(vs_new, axis=1))
```

```python
import functools
import math

import numpy as np
import jax
import jax.numpy as jnp
from jax import lax
from jax.experimental import pallas as pl
from jax.experimental.pallas import tpu as pltpu

D_MODEL = 1024
DEPTH = 2
GRID_W = 64
D_HYENA = 512
HYENA_ORDER = 2
HYENA_POS_BANDS = 16
HYENA_POS_DIM = 1 + 2 * HYENA_POS_BANDS
HYENA_FILTER_HIDDEN = 64
HYENA_WINDOW_SHIFT = 0.05
N_HEADS = 8
N_KV_HEADS = 2
HEAD_DIM = 64
HEADS_PER_KV = N_HEADS // N_KV_HEADS
ATT_Q = N_HEADS * HEAD_DIM
ATT_KV = N_KV_HEADS * HEAD_DIM
ROPE_THETA = 10000.0
D_SCONV = 512
N_EXPERTS = 16
EXPERT_FF = 512
CAPACITY_FACTOR = 2
NORM_EPS = 1e-6
HY_COLS = (HYENA_ORDER + 1) * D_HYENA
ATT_COLS = ATT_Q + 2 * ATT_KV
SC_COLS = 3 * D_SCONV
GATE_COLS = 3 * D_MODEL
MIX_COLS = HY_COLS + ATT_COLS + SC_COLS
D_IN = MIX_COLS + GATE_COLS

F32 = jnp.float32
BF16 = jnp.bfloat16

V7X_VMEM_BYTES = 64 * 1024 * 1024
VMEM_LIMIT = V7X_VMEM_BYTES - 8 * 1024 * 1024
LANES = 128
MLP_PAD = 128
STEP_TOKENS = 1024
INPROJ_ROWS = 1024
MERGE_ROWS = 512
HYENA_CHANNEL_TILE = 256
MOD_COL_TILE = 2048
KEY_PART = 768
COND_ROWS = 16


def _params(*sem):
    return pltpu.CompilerParams(dimension_semantics=sem, vmem_limit_bytes=VMEM_LIMIT)


def _const_spec(shape):
    nd = len(shape)
    return pl.BlockSpec(shape, lambda *_: (0,) * nd, pipeline_mode=pl.Buffered(1))


def _layer_spec(shape, layer):
    nd = len(shape)
    return pl.BlockSpec((None,) + tuple(shape), lambda *_: (layer,) + (0,) * nd, pipeline_mode=pl.Buffered(1))


def _dot(a, b):
    return jnp.dot(a, b, preferred_element_type=F32)


def _silu(x):
    return x * (1.0 / (1.0 + jnp.exp(-x)))


def _sigmoid(x):
    return 1.0 / (1.0 + jnp.exp(-x))


@functools.lru_cache(maxsize=None)
def _split_dft_tables(L):
    h = L // 2
    k = np.arange(h, dtype=np.int64)
    alt = 1.0 - 2.0 * (k % 2)
    fwd, inv = [], []
    for odd in (0, 1):
        idx = (k[:, None] * (2 * k[None, :] + odd)) % (2 * L)
        ang = idx.astype(np.float64) * (np.pi / L)
        c = np.cos(ang)
        s = np.sin(ang)
        s[0, :] = alt
        fwd.append(np.concatenate([c, s], axis=0).astype(np.float32))
        inv.append(np.concatenate([c.T, s.T], axis=1).astype(np.float32))
    return fwd[0], fwd[1], inv[0], inv[1]


@functools.lru_cache(maxsize=None)
def _pos_features(L):
    n = np.arange(L, dtype=np.float64)
    t = n / max(L - 1, 1)
    bands = np.linspace(1e-4, HYENA_POS_BANDS - 1, HYENA_POS_BANDS)
    ang = 2.0 * math.pi * n[:, None] * bands[None, :] / L
    z = np.concatenate([t[:, None], np.cos(ang), np.sin(ang)], axis=-1)
    zp = np.zeros((L, MLP_PAD), np.float32)
    zp[:, :HYENA_POS_DIM] = z
    return zp


@functools.lru_cache(maxsize=None)
def _rope_tables(L):
    rows = L // GRID_W
    row = np.repeat(np.arange(rows, dtype=np.float64), GRID_W)
    col = np.tile(np.arange(GRID_W, dtype=np.float64), rows)
    axis_dim = HEAD_DIM // 2
    inv_freq = ROPE_THETA ** (-np.arange(0, axis_dim, 2, dtype=np.float64) / axis_dim)
    ar = row[:, None] * inv_freq[None, :]
    ac = col[:, None] * inv_freq[None, :]
    ang = np.concatenate([ar, ar, ac, ac], axis=-1)
    cos = np.cos(ang).astype(np.float32)
    sin = np.sin(ang).astype(np.float32)
    return np.tile(cos, (1, N_HEADS)), np.tile(sin, (1, N_HEADS))


@functools.lru_cache(maxsize=None)
def _head_tables():
    lane = np.arange(ATT_Q)
    block_ones = (lane[:, None] // HEAD_DIM == lane[None, :] // HEAD_DIM).astype(np.float32)
    src = np.arange(ATT_KV)
    dst = np.arange(HEADS_PER_KV * HEAD_DIM)
    rep = np.stack([(src[:, None] == g * HEAD_DIM + dst[None, :] % HEAD_DIM) for g in range(N_KV_HEADS)])
    return block_ones, rep.astype(np.float32)


@functools.lru_cache(maxsize=None)
def _prefix_table(n):
    i = np.arange(n)
    return (i[:, None] < i[None, :]).astype(np.float32)


def _mod_kernel(cond_ref, w_ref, b_ref, o_ref):
    a = _silu(cond_ref[...]).astype(BF16)
    o_ref[...] = _dot(a, w_ref[...].astype(BF16)) + b_ref[...]


def _modulation(cond, mod_w, mod_b):
    rows = cond.shape[0]
    tn = MOD_COL_TILE
    ncols = mod_w.shape[-1]
    return pl.pallas_call(
        _mod_kernel,
        grid=(DEPTH, ncols // tn),
        in_specs=[
            pl.BlockSpec((rows, D_MODEL), lambda l, j: (0, 0)),
            pl.BlockSpec((None, D_MODEL, tn), lambda l, j: (l, 0, j)),
            pl.BlockSpec((None, 1, tn), lambda l, j: (l, 0, j)),
        ],
        out_specs=pl.BlockSpec((None, rows, tn), lambda l, j: (l, 0, j)),
        out_shape=jax.ShapeDtypeStruct((DEPTH, rows, ncols), F32),
        compiler_params=_params("parallel", "parallel"),
        name="modulation",
    )(cond, mod_w, mod_b.reshape(DEPTH, 1, ncols))


def _rms_modulate(x, norm_w, shift, scale):
    y = x * lax.rsqrt(jnp.mean(x * x, axis=-1, keepdims=True) + NORM_EPS)
    return (y * norm_w) * (1.0 + scale) + shift


def _conv3(x, w, b, first, last):
    n = x.shape[0]
    prev = jnp.where(first, 0.0, pltpu.roll(x, 1, 0))
    nxt = jnp.where(last, 0.0, pltpu.roll(x, n - 1, 0))
    return prev * w[0:1, :] + x * w[1:2, :] + nxt * w[2:3, :] + b


def _inproj_kernel(x_ref, mod_ref, nw_ref, w_ref, hyw_ref, hyb_ref, scw_ref, scb_ref,
                   hyc_ref, yc_ref, qkv_ref, h_sc, *, L):
    h = _rms_modulate(x_ref[...], nw_ref[...], mod_ref[0:1, :], mod_ref[1:2, :])
    h_sc[...] = h.astype(BF16)
    tm = x_ref.shape[0]

    def edges(cols):
        pos = lax.broadcasted_iota(jnp.int32, (tm, cols), 0) % L
        return pos == 0, pos == L - 1

    sc0 = HY_COLS + ATT_COLS
    u, bgate, cgate = (_dot(h_sc[...], w_ref[:, sc0 + k * D_SCONV:sc0 + (k + 1) * D_SCONV]) for k in range(3))
    hy_chunks = [slice(lo, lo + ATT_COLS) for lo in range(0, HY_COLS, ATT_COLS)]
    y_next = _dot(h_sc[...], w_ref[:, hy_chunks[0]])
    first, last = edges(D_SCONV)
    yc_ref[...] = (bgate * _conv3(cgate * u, scw_ref[...], scb_ref[...], first, last)).astype(BF16)
    first, last = edges(ATT_COLS)
    for i, cols in enumerate(hy_chunks):
        y = y_next
        if i + 1 < len(hy_chunks):
            y_next = _dot(h_sc[...], w_ref[:, hy_chunks[i + 1]])
        else:
            qkv_ref[...] = _dot(h_sc[...], w_ref[:, HY_COLS:HY_COLS + ATT_COLS])
        hyc_ref[:, cols] = _conv3(y, hyw_ref[:, cols], hyb_ref[:, cols], first, last).astype(BF16)


def _two_group_kernel(tile_fn, n_group_in, n_shared, n_group_out, *refs, n_first, lens):
    a_in, refs = refs[:n_group_in], refs[n_group_in:]
    b_in, refs = refs[:n_group_in], refs[n_group_in:]
    shared, refs = refs[:n_shared], refs[n_shared:]
    a_out, refs = refs[:n_group_out], refs[n_group_out:]
    b_out, scratch = refs[:n_group_out], refs[n_group_out:]
    step = pl.program_id(0)

    @pl.when(step < n_first)
    def _():
        tile_fn(*a_in, *shared, *a_out, *scratch, lens[0])

    @pl.when(step >= n_first)
    def _():
        tile_fn(*b_in, *shared, *b_out, *scratch, lens[1])


def _group_specs(block, n_first):
    return (pl.BlockSpec(block, lambda i: (jnp.minimum(i, n_first - 1), 0)),
            pl.BlockSpec(block, lambda i: (jnp.maximum(i - n_first, 0), 0)))


def _group_mod_spec(n_first, tm, second_len):
    return pl.BlockSpec((None, 6, D_MODEL),
                        lambda i: (jnp.where(i < n_first, 0, 1 + (jnp.maximum(i - n_first, 0) * tm) // second_len), 0, 0))


def _inproj(x2d, mod, norm_w, w_in_bf, hy_short_w, hy_short_b, sc_w, sc_b, layer, rows_per_req, tm=INPROJ_ROWS):
    T = x2d.shape[0]
    n_req = mod.shape[0]
    req = (lambda i: (i * tm) // rows_per_req) if n_req > 1 else (lambda i: 0)
    out = lambda w: pl.BlockSpec((tm, w), lambda i: (i, 0))
    return pl.pallas_call(
        functools.partial(_inproj_kernel, L=rows_per_req),
        grid=(T // tm,),
        in_specs=[
            pl.BlockSpec((tm, D_MODEL), lambda i: (i, 0)),
            pl.BlockSpec((None, 6, D_MODEL), lambda i: (req(i), 0, 0)),
            _const_spec((1, D_MODEL)),
            _layer_spec((D_MODEL, MIX_COLS), layer),
            _const_spec((3, HY_COLS)), _const_spec((1, HY_COLS)),
            _const_spec((3, D_SCONV)), _const_spec((1, D_SCONV)),
        ],
        out_specs=[out(HY_COLS), out(D_SCONV), out(ATT_COLS)],
        out_shape=[jax.ShapeDtypeStruct((T, HY_COLS), BF16), jax.ShapeDtypeStruct((T, D_SCONV), BF16),
                   jax.ShapeDtypeStruct((T, ATT_COLS), F32)],
        scratch_shapes=[pltpu.VMEM((tm, D_MODEL), BF16)],
        compiler_params=_params("parallel"),
        name="inproj",
    )(x2d, mod, norm_w, w_in_bf, hy_short_w, hy_short_b.reshape(1, HY_COLS), sc_w, sc_b.reshape(1, D_SCONV))


def _split_parity(par_sc, z):
    half = z.shape[0] // 2
    n_lt = z.shape[1] // LANES
    for c in range(n_lt):
        par_sc[c] = z[:, c * LANES:(c + 1) * LANES]
    return tuple(jnp.concatenate([par_sc[c, pl.ds(par, half, stride=2), :] for c in range(n_lt)], axis=1)
                 for par in (0, 1))


def _join_parity(par_sc, even, odd):
    half = even.shape[0]
    n_lt = even.shape[1] // LANES
    for c in range(n_lt):
        par_sc[c, pl.ds(0, half, stride=2), :] = even[:, c * LANES:(c + 1) * LANES]
        par_sc[c, pl.ds(1, half, stride=2), :] = odd[:, c * LANES:(c + 1) * LANES]
    return jnp.concatenate([par_sc[c] for c in range(n_lt)], axis=1)


def _fold_spectrum(te, to):
    half = te.shape[0] // 2
    row0 = lax.broadcasted_iota(jnp.int32, (half, te.shape[1]), 0) == 0
    ce, se, co, so = te[:half], te[half:], to[:half], to[half:]
    a = jnp.concatenate([ce + co, jnp.where(row0, se, ce - co)], axis=0)
    b = jnp.concatenate([jnp.where(row0, ce - co, se + so), jnp.where(row0, so, so - se)], axis=0)
    return a, b


def _unfold_spectrum(yr, yi):
    half = yr.shape[0] // 2
    row0 = lax.broadcasted_iota(jnp.int32, (half, yr.shape[1]), 0) == 0
    yr_lo, yr_up, yi_lo, yi_up = yr[:half], yr[half:], yi[:half], yi[half:]
    even = jnp.concatenate([jnp.where(row0, yr_lo + yi_lo, yr_lo + yr_up),
                            jnp.where(row0, yr_up, yi_lo - yi_up)], axis=0)
    odd = jnp.concatenate([jnp.where(row0, yr_lo - yi_lo, yr_lo - yr_up),
                           jnp.where(row0, yi_up, yi_lo + yi_up)], axis=0)
    return even, odd


def _filter_kernel(z_ref, w1_ref, b1_ref, w2_ref, b2_ref, w3f_ref, b3f_ref, w3b_ref, b3b_ref,
                   decf_ref, decb_ref, fwde_ref, fwdo_ref, p_ref, h_sc, par_sc, *, L):
    hi = lax.Precision.HIGHEST

    @pl.when((pl.program_id(1) == 0) & (pl.program_id(2) == 0))
    def _():
        h1 = jnp.sin(jnp.dot(z_ref[...], w1_ref[...], precision=hi, preferred_element_type=F32) + b1_ref[...])
        h_sc[...] = jnp.sin(jnp.dot(h1, w2_ref[...], precision=hi, preferred_element_type=F32) + b2_ref[...])

    h = h_sc[...]
    tc = w3f_ref.shape[1]
    pos = lax.broadcasted_iota(jnp.int32, (L, tc), 0)
    t = pos.astype(F32) / float(max(L - 1, 1))

    def taps(w3_ref, b3_ref, dec_ref):
        g = jnp.dot(h, w3_ref[...], precision=hi, preferred_element_type=F32) + b3_ref[...]
        return g * (jnp.exp(-t * jnp.abs(dec_ref[...])) + HYENA_WINDOW_SHIFT)

    hf = taps(w3f_ref, b3f_ref, decf_ref)
    hb = jnp.where(pos == 0, 0.0, taps(w3b_ref, b3b_ref, decb_ref))

    def transform(taps_lc):
        even, odd = _split_parity(par_sc, taps_lc)
        return _fold_spectrum(_dot(fwde_ref[...], even.astype(BF16)), _dot(fwdo_ref[...], odd.astype(BF16)))

    fa, fb = transform(hf)
    ba, bb = transform(hb)
    k_re = fa + ba
    k_im = bb - fb
    k_ny = fb + bb
    first = pos == 0
    inv_n = 1.0 / (2 * L)
    p_ref[0] = jnp.where(first, k_re * inv_n, k_re * (2.0 * inv_n))
    p_ref[1] = jnp.where(first, 0.0, k_im * (2.0 * inv_n))
    p_ref[2] = jnp.where(first, k_ny * inv_n, k_re * (2.0 * inv_n))


def _hyena_spectra(L, w1, b1, w2, b2, w3, b3, decay, fwd_even, fwd_odd):
    pad_h = MLP_PAD - HYENA_FILTER_HIDDEN
    z = jnp.asarray(_pos_features(L))
    w1p = jnp.pad(w1, ((0, 0), (0, MLP_PAD - HYENA_POS_DIM), (0, pad_h)))
    b1p = jnp.pad(b1, ((0, 0), (0, pad_h))).reshape(DEPTH, 1, MLP_PAD)
    w2p = jnp.pad(w2, ((0, 0), (0, pad_h), (0, pad_h)))
    b2p = jnp.pad(b2, ((0, 0), (0, pad_h))).reshape(DEPTH, 1, MLP_PAD)
    w3p = jnp.pad(w3, ((0, 0), (0, pad_h), (0, 0)))
    ncol = w3.shape[-1]
    b3r = b3.reshape(DEPTH, 1, ncol)
    decr = decay.reshape(DEPTH, 1, ncol)
    tc = HYENA_CHANNEL_TILE
    nct = D_HYENA // tc
    per_dir = HYENA_ORDER * nct
    col_f = lambda l, o, c: (l, 0, o * nct + c)
    col_b = lambda l, o, c: (l, 0, per_dir + o * nct + c)
    per_layer = lambda shape: pl.BlockSpec((None,) + shape, lambda l, o, c: (l, 0, 0))
    return pl.pallas_call(
        functools.partial(_filter_kernel, L=L),
        grid=(DEPTH, HYENA_ORDER, nct),
        in_specs=[
            _const_spec((L, MLP_PAD)),
            per_layer((MLP_PAD, MLP_PAD)), per_layer((1, MLP_PAD)),
            per_layer((MLP_PAD, MLP_PAD)), per_layer((1, MLP_PAD)),
            pl.BlockSpec((None, MLP_PAD, tc), col_f), pl.BlockSpec((None, 1, tc), col_f),
            pl.BlockSpec((None, MLP_PAD, tc), col_b), pl.BlockSpec((None, 1, tc), col_b),
            pl.BlockSpec((None, 1, tc), col_f), pl.BlockSpec((None, 1, tc), col_b),
            _const_spec((L, L // 2)), _const_spec((L, L // 2)),
        ],
        out_specs=pl.BlockSpec((None, None, 3, L, tc), lambda l, o, c: (l, o, 0, 0, c)),
        out_shape=jax.ShapeDtypeStruct((DEPTH, HYENA_ORDER, 3, L, D_HYENA), F32),
        scratch_shapes=[pltpu.VMEM((L, MLP_PAD), F32), pltpu.VMEM((tc // LANES, L, LANES), F32)],
        compiler_params=_params("arbitrary", "arbitrary", "arbitrary"),
        name="hyena_spectra",
    )(z, w1p, b1p, w2p, b2p, w3p, b3r, w3p, b3r, decr, decr, fwd_even, fwd_odd)


def _seqmix_kernel(v_ref, x1_ref, x2_ref, p_ref, hb_ref, fwde_ref, fwdo_ref, inve_ref, invo_ref, ya_ref,
                   spec_sc, par_sc):
    n_req = v_ref.shape[0]
    mult_refs = (x1_ref, x2_ref)

    def chain(r):
        z = v_ref[r].astype(F32)
        z_even, z_odd = _split_parity(par_sc.at[r], z)
        yield
        for o in range(HYENA_ORDER):
            te = _dot(fwde_ref[...], z_even.astype(BF16))
            to = _dot(fwdo_ref[...], z_odd.astype(BF16))
            yield
            a, b = _fold_spectrum(te, to)
            p2 = p_ref[o, 1]
            even, odd = _unfold_spectrum(a * p_ref[o, 0] + b * p2, b * p_ref[o, 2] - a * p2)
            spec_sc[r, 0] = even.astype(BF16)
            spec_sc[r, 1] = odd.astype(BF16)
            yield
            y_even = _dot(inve_ref[...], spec_sc[r, 0])
            y_odd = _dot(invo_ref[...], spec_sc[r, 1])
            yield
            z = mult_refs[o][r].astype(F32) * (_join_parity(par_sc.at[r], y_even, y_odd) + hb_ref[o:o + 1, :] * z)
            if o < HYENA_ORDER - 1:
                z_even, z_odd = _split_parity(par_sc.at[r], z)
            else:
                ya_ref[r] = z.astype(ya_ref.dtype)
            yield

    chains = [chain(r) for r in range(n_req)]
    n_stages = 1 + 4 * HYENA_ORDER
    for tick in range(n_stages + n_req - 1):
        for r, c in enumerate(chains):
            if 0 <= tick - r < n_stages:
                next(c)


def _seqmix(hyc3d, spectra, hy_bias, tabs, layer):
    B, L, _ = hyc3d.shape
    tc = HYENA_CHANNEL_TILE
    nct = D_HYENA // tc
    nr = max(2, STEP_TOKENS // L)
    act = lambda off: pl.BlockSpec((nr, L, tc), lambda c, b, off=off: (b, 0, off * nct + c))
    return pl.pallas_call(
        _seqmix_kernel,
        grid=(nct, B // nr),
        in_specs=[
            act(0), act(1), act(2),
            pl.BlockSpec((None, HYENA_ORDER, 3, L, tc), lambda c, b: (layer, 0, 0, 0, c)),
            pl.BlockSpec((None, HYENA_ORDER, tc), lambda c, b: (layer, 0, c)),
            _const_spec((L, L // 2)), _const_spec((L, L // 2)), _const_spec((L // 2, L)), _const_spec((L // 2, L)),
        ],
        out_specs=pl.BlockSpec((nr, L, tc), lambda c, b: (b, 0, c)),
        out_shape=jax.ShapeDtypeStruct((B, L, D_HYENA), BF16),
        scratch_shapes=[pltpu.VMEM((nr, 2, L, tc), BF16), pltpu.VMEM((nr, tc // LANES, L, LANES), F32)],
        compiler_params=_params("arbitrary", "arbitrary"),
        name="seqmix",
    )(hyc3d, hyc3d, hyc3d, spectra, hy_bias, *tabs)


def _head_rms(x, ones_ref, w):
    sq = x * x
    hi = sq.astype(BF16)
    lo = (sq - hi.astype(F32)).astype(BF16)
    ones = ones_ref[...]
    width = x.shape[1]
    ss = _dot(hi, ones[:width, :width]) + _dot(lo, ones[:width, :width])
    return x * lax.rsqrt(ss * (1.0 / HEAD_DIM) + NORM_EPS) * w


def _rope(x, cos, sin):
    width = x.shape[1]
    lane = lax.broadcasted_iota(jnp.int32, x.shape, 1)
    half = HEAD_DIM // 4
    low = (lane % (2 * half)) < half
    rot = jnp.where(low, -pltpu.roll(x, width - half, 1), pltpu.roll(x, half, 1))
    return x * cos + rot * sin


def _attn_kernel(*refs, rope):
    if rope:
        (q_ref, k_ref, v_ref, ck_ref, cv_ref, qw_ref, kw_ref, ones_ref, rep_ref, rept_ref,
         cos_ref, sin_ref, yb_ref) = refs
    else:
        q_ref, k_ref, v_ref, qw_ref, kw_ref, ones_ref, rep_ref, rept_ref, yb_ref, ko_ref, vo_ref = refs
    n_req, L, _ = q_ref.shape
    gw = HEADS_PER_KV * HEAD_DIM
    lane = lax.broadcasted_iota(jnp.int32, (L, gw), 1)
    for r in range(n_req):
        kn = _head_rms(k_ref[r], ones_ref, kw_ref[...])
        v = v_ref[r]
        q = _head_rms(q_ref[r], ones_ref, qw_ref[...])
        if rope:
            keys = _rope(kn, cos_ref[:, :ATT_KV], sin_ref[:, :ATT_KV])
            keys = jnp.concatenate([ck_ref[r], keys], axis=0)
            vals = jnp.concatenate([cv_ref[r], v], axis=0)
            q = _rope(q, cos_ref[...], sin_ref[...])
        else:
            ko_ref[r] = kn
            vo_ref[r] = v
            keys, vals = kn, v
        keys = keys.astype(BF16)
        vals = vals.astype(BF16)
        q = q * (HEAD_DIM ** -0.5)
        for g in range(N_KV_HEADS):
            k4 = lax.dot_general(rept_ref[g], keys, (((1,), (1,)), ((), ())), preferred_element_type=F32).astype(BF16)
            v4 = _dot(vals, rep_ref[g]).astype(BF16)
            qg = q[:, g * gw:(g + 1) * gw]
            acc = jnp.zeros((L, gw), F32)
            for h in range(HEADS_PER_KV):
                mine = (lane // HEAD_DIM) == h
                qm = jnp.where(mine, qg, 0.0).astype(BF16)
                lk = k4.shape[1]
                o4 = denom = top = None
                for k0 in range(0, lk, KEY_PART):
                    k1 = min(k0 + KEY_PART, lk)
                    s = _dot(qm, k4[:, k0:k1])
                    m = jnp.max(s, axis=-1, keepdims=True)
                    p = jnp.exp(s - m)
                    l = jnp.sum(p, axis=-1, keepdims=True)
                    o = _dot(p.astype(BF16), v4[k0:k1, :])
                    if top is None:
                        o4, denom, top = o, l, m
                    else:
                        new_top = jnp.maximum(top, m)
                        keep, add = jnp.exp(top - new_top), jnp.exp(m - new_top)
                        o4, denom, top = o4 * keep + o * add, denom * keep + l * add, new_top
                acc = jnp.where(mine, o4 * (1.0 / denom), acc)
            yb_ref[r, :, g * gw:(g + 1) * gw] = acc.astype(yb_ref.dtype)


def _attention(qkv3d, q_norm_w, k_norm_w, ctx_k=None, ctx_v=None, layer=0):
    B, L, _ = qkv3d.shape
    rope = ctx_k is not None
    nr = _requests_per_step(L)
    ones_np, rep_np = _head_tables()
    ones = jnp.asarray(ones_np).astype(BF16)
    rep = jnp.asarray(rep_np).astype(BF16)
    rept = jnp.asarray(np.swapaxes(rep_np, 1, 2)).astype(BF16)
    qw = jnp.tile(q_norm_w, N_HEADS).reshape(1, ATT_Q)
    kw = jnp.tile(k_norm_w, N_KV_HEADS).reshape(1, ATT_KV)
    kblk = ATT_Q // ATT_KV
    in_specs = [
        pl.BlockSpec((nr, L, ATT_Q), lambda b: (b, 0, 0)),
        pl.BlockSpec((nr, L, ATT_KV), lambda b: (b, 0, kblk)),
        pl.BlockSpec((nr, L, ATT_KV), lambda b: (b, 0, kblk + 1)),
    ]
    args = [qkv3d, qkv3d, qkv3d]
    if rope:
        n_ctx = ctx_k.shape[2]
        in_specs += [pl.BlockSpec((nr, None, n_ctx, ATT_KV), lambda b: (b, layer, 0, 0))] * 2
        args += [ctx_k, ctx_v]
    in_specs += [_const_spec((1, ATT_Q)), _const_spec((1, ATT_KV)), _const_spec((ATT_Q, ATT_Q)),
                 _const_spec((N_KV_HEADS, ATT_KV, HEADS_PER_KV * HEAD_DIM)),
                 _const_spec((N_KV_HEADS, HEADS_PER_KV * HEAD_DIM, ATT_KV))]
    args += [qw, kw, ones, rep, rept]
    yb_shape = jax.ShapeDtypeStruct((B, L, ATT_Q), BF16)
    yb_spec = pl.BlockSpec((nr, L, ATT_Q), lambda b: (b, 0, 0))
    if rope:
        cos_np, sin_np = _rope_tables(L)
        in_specs += [_const_spec((L, ATT_Q))] * 2
        args += [jnp.asarray(cos_np), jnp.asarray(sin_np)]
        out_specs = yb_spec
        out_shape = yb_shape
    else:
        kv_spec = pl.BlockSpec((nr, L, ATT_KV), lambda b: (b, 0, 0))
        kv_shape = jax.ShapeDtypeStruct((B, L, ATT_KV), F32)
        out_specs = [yb_spec, kv_spec, kv_spec]
        out_shape = [yb_shape, kv_shape, kv_shape]
    return pl.pallas_call(
        functools.partial(_attn_kernel, rope=rope),
        grid=(B // nr,),
        in_specs=in_specs,
        out_specs=out_specs,
        out_shape=out_shape,
        compiler_params=_params("parallel"),
        name="attention",
    )(*args)


def _merge_tile(ya_ref, yb_ref, yc_ref, x_ref, mod_ref, nw1_ref, nw2_ref, *rest):
    gate_w_refs = rest[:GATE_COLS // ATT_COLS]
    wa_ref, wb_ref, wc_ref, wo_ref, xo_ref, h2_ref, _ = rest[GATE_COLS // ATT_COLS:]
    tm = x_ref.shape[0]
    n_chain = 2
    rows_per = tm // n_chain

    def chain(c):
        rows = pl.ds(c * rows_per, rows_per)
        x = x_ref[rows, :]
        h = _rms_modulate(x, nw1_ref[...], mod_ref[0:1, :], mod_ref[1:2, :]).astype(BF16)
        yield
        gates = jnp.concatenate([_dot(h, w_ref[...]) for w_ref in gate_w_refs], axis=1)
        da = _dot(ya_ref[rows, :], wa_ref[...])
        db = _dot(yb_ref[rows, :], wb_ref[...])
        dc = _dot(yc_ref[rows, :], wc_ref[...])
        yield
        merged = (_sigmoid(gates[:, 0:D_MODEL]) * da + _sigmoid(gates[:, D_MODEL:2 * D_MODEL]) * db
                  + _sigmoid(gates[:, 2 * D_MODEL:]) * dc).astype(BF16)
        yield
        proj = _dot(merged, wo_ref[...])
        yield
        x = x + mod_ref[2:3, :] * proj
        xo_ref[rows, :] = x
        h2_ref[rows, :] = _rms_modulate(x, nw2_ref[...], mod_ref[3:4, :], mod_ref[4:5, :]).astype(BF16)
        yield

    chains = [chain(c) for c in range(n_chain)]
    n_stages = 5
    for tick in range(n_stages + n_chain - 1):
        for c, ch in enumerate(chains):
            if 0 <= tick - c < n_stages:
                next(ch)


def _merge(branches_p, xp2d, branches_s, xs2d, mod_l, norm1_w, norm2_w, w_in_bf, wa, wb, wc, wo, layer, lens,
           tm=MERGE_ROWS):
    n_p, n_s = xp2d.shape[0] // tm, xs2d.shape[0] // tm
    br_p, br_s = _group_specs((tm, D_HYENA), n_p)
    row_p, row_s = _group_specs((tm, D_MODEL), n_p)
    n_gate = GATE_COLS // ATT_COLS
    gate_blk0 = MIX_COLS // ATT_COLS
    gate_w = [pl.BlockSpec((None, D_MODEL, ATT_COLS), lambda i, k=k: (layer, 0, gate_blk0 + k),
                           pipeline_mode=pl.Buffered(1)) for k in range(n_gate)]
    shapes = lambda t: [jax.ShapeDtypeStruct((t, D_MODEL), F32), jax.ShapeDtypeStruct((t, D_MODEL), BF16)]
    res = pl.pallas_call(
        functools.partial(_two_group_kernel, _merge_tile, 4, 7 + n_gate, 2, n_first=n_p, lens=lens),
        grid=(n_p + n_s,),
        in_specs=[
            br_p, br_p, br_p, row_p, br_s, br_s, br_s, row_s,
            _group_mod_spec(n_p, tm, lens[1]),
            _const_spec((1, D_MODEL)), _const_spec((1, D_MODEL)),
            *gate_w,
            _layer_spec((D_HYENA, D_MODEL), layer), _layer_spec((ATT_Q, D_MODEL), layer),
            _layer_spec((D_SCONV, D_MODEL), layer), _layer_spec((D_MODEL, D_MODEL), layer),
        ],
        out_specs=[row_p, row_p, row_s, row_s],
        out_shape=shapes(xp2d.shape[0]) + shapes(xs2d.shape[0]),
        compiler_params=_params("arbitrary"),
        name="merge",
    )(*branches_p, xp2d, *branches_s, xs2d, mod_l, norm1_w, norm2_w, *([w_in_bf] * n_gate), wa, wb, wc, wo)
    return res[:2], res[2:]


def _route_kernel(h_ref, rw_ref, tri_ref, aff_ref, slot_ref, *, cap):
    b = pl.program_id(0)
    n_req = h_ref.shape[0]
    for r in range(n_req):
        logits = lax.dot_general(rw_ref[...], h_ref[r], (((1,), (1,)), ((), ())), preferred_element_type=F32)
        ex = jnp.exp(logits - jnp.max(logits, axis=0, keepdims=True))
        first_row = pl.multiple_of((b * n_req + r) * N_EXPERTS, N_EXPERTS)
        aff_ref[pl.ds(first_row, N_EXPERTS), :] = ex / jnp.sum(ex, axis=0, keepdims=True)

    @pl.when(b == pl.num_programs(0) - 1)
    def _():
        aff = aff_ref[...]

        def count(mask):
            return jnp.sum(jnp.where(mask, 1.0, 0.0), axis=1, keepdims=True)

        kth = jnp.zeros((aff.shape[0], 1), jnp.int32)
        for bit in range(30, -1, -1):
            trial = kth | (1 << bit)
            enough = count(aff >= lax.bitcast_convert_type(trial, F32)) >= cap
            kth = jnp.where(enough, trial, kth)
        next_up = lax.bitcast_convert_type(kth + 1, F32)
        above = aff >= next_up
        tied = (aff >= lax.bitcast_convert_type(kth, F32)) & (aff < next_up)
        tri = tri_ref[...]
        tied_before = _dot(jnp.where(tied, 1.0, 0.0).astype(BF16), tri)
        chosen = above | (tied & (tied_before < (cap - count(above))))
        slot = _dot(jnp.where(chosen, 1.0, 0.0).astype(BF16), tri)
        slot_ref[...] = jnp.where(chosen, slot, -1.0)


def _route(h3d, router_wt):
    B, n_tok, _ = h3d.shape
    cap = CAPACITY_FACTOR * n_tok // N_EXPERTS
    tri = jnp.asarray(_prefix_table(n_tok)).astype(BF16)
    whole = pl.BlockSpec((B * N_EXPERTS, n_tok), lambda b: (0, 0))
    shape = jax.ShapeDtypeStruct((B * N_EXPERTS, n_tok), F32)
    nr = _requests_per_step(n_tok)
    return pl.pallas_call(
        functools.partial(_route_kernel, cap=cap),
        grid=(B // nr,),
        in_specs=[
            pl.BlockSpec((nr, n_tok, D_MODEL), lambda b: (b, 0, 0)),
            _const_spec((N_EXPERTS, D_MODEL)),
            _const_spec((n_tok, n_tok)),
        ],
        out_specs=[whole, whole],
        out_shape=[shape, shape],
        compiler_params=_params("arbitrary"),
        name="moe_route",
    )(h3d, router_wt, tri)


MOE_PARTS = 2


def _dispatch_kernel(h_ref, aff_ref, slot_ref, xs_ref, g_ref, d_sc, *, n_tok, cap):
    row = lax.broadcasted_iota(jnp.int32, (cap, n_tok), 0).astype(F32)
    per_part = N_EXPERTS // MOE_PARTS
    for r in range(h_ref.shape[0]):
        for part in range(MOE_PARTS):
            for e in range(part * per_part, (part + 1) * per_part):
                idx = r * N_EXPERTS + e
                hit = row == slot_ref[idx:idx + 1, :]
                d_sc[r, e * cap:(e + 1) * cap, :] = jnp.where(hit, 1.0, 0.0).astype(BF16)
                gate = jnp.sum(jnp.where(hit, aff_ref[idx:idx + 1, :], 0.0), axis=1, keepdims=True)
                g_ref[r, e * cap:(e + 1) * cap, :] = jnp.broadcast_to(gate, (cap, LANES))
            rows = slice(part * per_part * cap, (part + 1) * per_part * cap)
            xs_ref[r, rows, :] = _dot(d_sc[r, rows, :], h_ref[r]).astype(BF16)


def _requests_per_step(n_tok):
    return max(1, STEP_TOKENS // n_tok)


def _dispatch(h3d, router_wt):
    B, n_tok, _ = h3d.shape
    cap = CAPACITY_FACTOR * n_tok // N_EXPERTS
    rows = N_EXPERTS * cap
    aff, slot = _route(h3d, router_wt)
    nr = _requests_per_step(n_tok)
    per_req = pl.BlockSpec((nr * N_EXPERTS, n_tok), lambda b: (b, 0))
    blk = lambda w: pl.BlockSpec((nr, rows, w), lambda b: (b, 0, 0))
    xs, gates = pl.pallas_call(
        functools.partial(_dispatch_kernel, n_tok=n_tok, cap=cap),
        grid=(B // nr,),
        in_specs=[pl.BlockSpec((nr, n_tok, D_MODEL), lambda b: (b, 0, 0)), per_req, per_req],
        out_specs=[blk(D_MODEL), blk(LANES)],
        out_shape=[jax.ShapeDtypeStruct((B, rows, D_MODEL), BF16),
                   jax.ShapeDtypeStruct((B, rows, LANES), F32)],
        scratch_shapes=[pltpu.VMEM((nr, rows, n_tok), BF16)],
        compiler_params=_params("parallel"),
        name="moe_dispatch",
    )(h3d, aff, slot)
    return xs, gates, slot


def _ffn_kernel(xp_ref, gp_ref, xs_ref, gs_ref, wg_ref, wu_ref, wd_ref, yp_ref, ys_ref):
    wg = wg_ref[...].astype(BF16)
    wu = wu_ref[...].astype(BF16)
    wd = wd_ref[...].astype(BF16)
    for x_ref, g_ref, y_ref in ((xp_ref, gp_ref, yp_ref), (xs_ref, gs_ref, ys_ref)):
        nb, cap, _ = x_ref.shape
        x = x_ref[...].reshape(nb * cap, D_MODEL)
        gate = g_ref[...].reshape(nb * cap, LANES)
        gate = jnp.concatenate([gate] * (EXPERT_FF // LANES), axis=1)
        act = _silu(_dot(x, wg)) * _dot(x, wu) * gate
        y = _dot(act.astype(BF16), wd)
        y_ref[...] = y.reshape(nb, cap, D_MODEL).astype(y_ref.dtype)


def _expert_ffn(xp, gp, xs, gs, w_gate, w_up, w_down, layer):
    def act_spec(a):
        nb, _, cap, w = a.shape
        return pl.BlockSpec((nb, None, cap, w), lambda e: (0, e, 0, 0))

    def w_spec(a):
        return pl.BlockSpec((None, None) + a.shape[2:], lambda e: (layer, e, 0, 0))

    return pl.pallas_call(
        _ffn_kernel,
        grid=(N_EXPERTS,),
        in_specs=[act_spec(xp), act_spec(gp), act_spec(xs), act_spec(gs),
                  w_spec(w_gate), w_spec(w_up), w_spec(w_down)],
        out_specs=[act_spec(xp), act_spec(xs)],
        out_shape=[jax.ShapeDtypeStruct(xp.shape, BF16), jax.ShapeDtypeStruct(xs.shape, BF16)],
        compiler_params=_params("parallel"),
        name="expert_ffn",
    )(xp, gp, xs, gs, w_gate, w_up, w_down)


def _combine_kernel(slot_ref, y_ref, x_ref, mod_ref, o_ref, d_sc, *, n_tok, cap):
    row = lax.broadcasted_iota(jnp.int32, (cap, n_tok), 0).astype(F32)
    per_part = N_EXPERTS // MOE_PARTS
    for r in range(y_ref.shape[0]):
        moe = None
        for part in range(MOE_PARTS):
            for e in range(part * per_part, (part + 1) * per_part):
                idx = r * N_EXPERTS + e
                hit = row == slot_ref[idx:idx + 1, :]
                d_sc[r, e * cap:(e + 1) * cap, :] = jnp.where(hit, 1.0, 0.0).astype(BF16)
            rows = slice(part * per_part * cap, (part + 1) * per_part * cap)
            scattered = lax.dot_general(d_sc[r, rows, :], y_ref[r, rows, :], (((0,), (0,)), ((), ())),
                                        preferred_element_type=F32)
            moe = scattered if moe is None else moe + scattered
        o_ref[r] = x_ref[r] + mod_ref[min(r, mod_ref.shape[0] - 1)] * moe


def _combine(slot, y3d, x3d, mod_g2):
    B, n_tok, _ = x3d.shape
    rows = y3d.shape[1]
    cap = rows // N_EXPERTS
    nr = _requests_per_step(n_tok)
    shared_mod = mod_g2.shape[0] == 1
    mod_spec = (pl.BlockSpec((1, 1, D_MODEL), lambda b: (0, 0, 0)) if shared_mod
                else pl.BlockSpec((nr, 1, D_MODEL), lambda b: (b, 0, 0)))
    return pl.pallas_call(
        functools.partial(_combine_kernel, n_tok=n_tok, cap=cap),
        grid=(B // nr,),
        in_specs=[
            pl.BlockSpec((nr * N_EXPERTS, n_tok), lambda b: (b, 0)),
            pl.BlockSpec((nr, rows, D_MODEL), lambda b: (b, 0, 0)),
            pl.BlockSpec((nr, n_tok, D_MODEL), lambda b: (b, 0, 0)),
            mod_spec,
        ],
        out_specs=pl.BlockSpec((nr, n_tok, D_MODEL), lambda b: (b, 0, 0)),
        out_shape=jax.ShapeDtypeStruct(x3d.shape, F32),
        scratch_shapes=[pltpu.VMEM((nr, rows, n_tok), BF16)],
        compiler_params=_params("parallel"),
        name="moe_combine",
    )(slot, y3d, x3d, mod_g2)


def _token_mixers(xp, xs, mod_l, p, tabs, spectra, layer, ctx):
    (bp, lp, _), (bs, ls, _) = xp.shape, xs.shape
    xp2d, xs2d = xp.reshape(bp * lp, D_MODEL), xs.reshape(bs * ls, D_MODEL)
    conv = (p['hy_short_w'], p['hy_short_b'], p['sc_w'], p['sc_b'])
    hyc_p, yc_p, qkv_p = _inproj(xp2d, mod_l[0:1], p['norm1_w'], p['w_in'], *conv, layer, lp)
    hyc_s, yc_s, qkv_s = _inproj(xs2d, mod_l[1:1 + bs], p['norm1_w'], p['w_in'], *conv, layer, ls)
    ya_p = _seqmix(hyc_p.reshape(bp, lp, HY_COLS), spectra[0], p['hy_bias'], tabs[0], layer)
    ya_s = _seqmix(hyc_s.reshape(bs, ls, HY_COLS), spectra[1], p['hy_bias'], tabs[1], layer)
    yb_p, k, v = _attention(qkv_p.reshape(bp, lp, ATT_COLS), p['q_norm_w'], p['k_norm_w'])
    yb_s = _attention(qkv_s.reshape(bs, ls, ATT_COLS), p['q_norm_w'], p['k_norm_w'], ctx[0], ctx[1], layer)
    (xp_mid, h2p), (xs_mid, h2s) = _merge(
        (ya_p.reshape(bp * lp, D_HYENA), yb_p.reshape(bp * lp, ATT_Q), yc_p), xp2d,
        (ya_s.reshape(bs * ls, D_HYENA), yb_s.reshape(bs * ls, ATT_Q), yc_s), xs2d,
        mod_l, p['norm1_w'], p['norm2_w'], p['w_in'], p['w_br_a'], p['w_br_b'], p['w_br_c'], p['w_o'],
        layer, (lp, ls))
    return ((xp_mid.reshape(xp.shape), h2p.reshape(xp.shape)), (xs_mid.reshape(xs.shape), h2s.reshape(xs.shape)),
            k, v)


def _moe_split(a, n_exp):
    B, rows, w = a.shape
    return a.reshape(B, n_exp, rows // n_exp, w)


def kernel(x_prompt, x_sample, cache_k, cache_v, c, c_ctx, mod_w, mod_b, norm1_w, norm2_w, w_in, hy_short_w, hy_short_b, hy_w1, hy_b1, hy_w2, hy_b2, hy_w3, hy_b3, hy_decay, hy_bias, q_norm_w, k_norm_w, sc_w, sc_b, w_br_a, w_br_b, w_br_c, w_o, router_w, exp_w_gate, exp_w_up, exp_w_down):
    n_dec = x_sample.shape[0]
    n_ctx = cache_k.shape[2]
    lp = x_prompt.shape[1]
    ls = x_sample.shape[1]

    cond_rows = COND_ROWS
    cond = jnp.concatenate([c_ctx[None, :], c, jnp.zeros((cond_rows - 1 - n_dec, D_MODEL), F32)], axis=0)
    mod = _modulation(cond, mod_w, mod_b).reshape(DEPTH, cond_rows, 6, D_MODEL)

    tabs_p = tuple(jnp.asarray(t).astype(BF16) for t in _split_dft_tables(lp))
    tabs_s = tuple(jnp.asarray(t).astype(BF16) for t in _split_dft_tables(ls))
    ctx_k = cache_k.reshape(n_dec, DEPTH, n_ctx, ATT_KV)
    ctx_v = cache_v.reshape(n_dec, DEPTH, n_ctx, ATT_KV)
    dense = {'w_in': w_in.astype(BF16), 'w_br_a': w_br_a.astype(BF16), 'w_br_b': w_br_b.astype(BF16),
             'w_br_c': w_br_c.astype(BF16), 'w_o': w_o.astype(BF16)}

    filt = (hy_w1, hy_b1, hy_w2, hy_b2, hy_w3, hy_b3, hy_decay)
    spectra_p = _hyena_spectra(lp, *filt, tabs_p[0], tabs_p[1])
    spectra_s = _hyena_spectra(ls, *filt, tabs_s[0], tabs_s[1])

    xp, xs = x_prompt, x_sample
    ks_new, vs_new = [], []
    for l in range(DEPTH):
        p = dict(dense)
        p.update({
            'norm1_w': norm1_w[l].reshape(1, D_MODEL), 'norm2_w': norm2_w[l].reshape(1, D_MODEL),
            'hy_short_w': hy_short_w[l], 'hy_short_b': hy_short_b[l],
            'hy_bias': hy_bias,
            'q_norm_w': q_norm_w[l], 'k_norm_w': k_norm_w[l], 'sc_w': sc_w[l], 'sc_b': sc_b[l],
        })
        mod_p = mod[l, 0:1]
        mod_s = mod[l, 1:1 + n_dec]
        (xp_mid, h2p), (xs_mid, h2s), k_l, v_l = _token_mixers(
            xp, xs, mod[l], p, (tabs_p, tabs_s), (spectra_p, spectra_s), l, (ctx_k, ctx_v))
        ks_new.append(k_l.reshape(k_l.shape[0], lp, N_KV_HEADS, HEAD_DIM))
        vs_new.append(v_l.reshape(v_l.shape[0], lp, N_KV_HEADS, HEAD_DIM))

        router_wt = router_w[l].T.astype(BF16)
        gp_x, gp_g, gp_slot = _dispatch(h2p, router_wt)
        gs_x, gs_g, gs_slot = _dispatch(h2s, router_wt)
        yp, ys = _expert_ffn(_moe_split(gp_x, N_EXPERTS), _moe_split(gp_g, N_EXPERTS),
                             _moe_split(gs_x, N_EXPERTS), _moe_split(gs_g, N_EXPERTS),
                             exp_w_gate, exp_w_up, exp_w_down, l)
        xp = _combine(gp_slot, yp.reshape(gp_x.shape), xp_mid, mod_p[:, 5:6])
        xs = _combine(gs_slot, ys.reshape(gs_x.shape), xs_mid, mod_s[:, 5:6])

    return (xp, xs, jnp.stack(ks_new, axis=1), jnp.stack(vs_new, axis=1))
```
